```python
import jax, jax.numpy as jnp
from jax import lax
import numpy as np

D_MODEL = 1024
BATCH = 8
SEQ = 8192
DEPTH = 1
DEC_BATCH = 16
DEC_SEQ = 4096
PAST_LEN = 128

HGRN_HEADS = 8
HGRN_DK = 128
HGRN_DV = 128
HGRN_CHUNK = 64
MLA_HEADS = 8
MLA_Q_LORA = 384
MLA_KV_LORA = 256
MLA_NOPE = 128
MLA_ROPE = 64
MLA_V = 128
ROPE_THETA = 10000.0
Q_BLOCK = 128
MEM_TOKENS = 256
MEM_HEADS = 4
MEM_HEAD_DIM = D_MODEL // MEM_HEADS
N_EXPERTS = 32
TOP_K = 4
D_FF = D_MODEL
SWIGLU_LIMIT = 7.0
SWIGLU_ALPHA = 1.702
MOE_BLOCK = 256
DN_ALPHA = (2.0 * DEPTH) ** 0.25
DN_BETA = (8.0 * DEPTH) ** -0.25
LN_EPS = 1e-5
RMS_EPS = 1e-6

HG = HGRN_HEADS * HGRN_DK
HV = HGRN_HEADS * HGRN_DV
MEM_Q = MEM_HEADS * MEM_HEAD_DIM
IN_SIZES = (HG, HG, HG, HV, HV, MLA_Q_LORA, MLA_KV_LORA + MLA_ROPE, MEM_Q, 3 * D_MODEL)
IN_COLS = HG * 3 + HV * 2 + MLA_Q_LORA + MLA_KV_LORA + MLA_ROPE + MEM_Q + 3 * D_MODEL

kernel_name = 'hybrid_hgrn2_mla_moe_encoder'


def _split_points():
    return [int(c) for c in np.cumsum(np.array(IN_SIZES))[:-1]]


def _rmsnorm(x, w):
    xf = x.astype(jnp.float32)
    y = xf * lax.rsqrt(jnp.mean(xf * xf, axis=-1, keepdims=True) + RMS_EPS)
    return (y * w.astype(jnp.float32)).astype(x.dtype)


def _layernorm(x, w, b):
    xf = x.astype(jnp.float32)
    mu = jnp.mean(xf, axis=-1, keepdims=True)
    var = jnp.mean(jnp.square(xf - mu), axis=-1, keepdims=True)
    y = (xf - mu) * lax.rsqrt(var + LN_EPS) * w.astype(jnp.float32) + b.astype(jnp.float32)
    return y.astype(x.dtype)


def _rope(x):
    S, R = x.shape[1], x.shape[-1]
    inv_freq = ROPE_THETA ** (-jnp.arange(0, R, 2, dtype=jnp.float32) / R)
    ang = jnp.arange(S, dtype=jnp.float32)[:, None] * inv_freq[None, :]
    ang = ang.reshape((S,) + (1,) * (x.ndim - 3) + (R // 2,))
    cos, sin = jnp.cos(ang), jnp.sin(ang)
    xf = x.astype(jnp.float32)
    x1, x2 = xf[..., : R // 2], xf[..., R // 2:]
    return jnp.concatenate([x1 * cos - x2 * sin, x2 * cos + x1 * sin], axis=-1).astype(x.dtype)


def _hgrn_scan(q, k, v, log_f):
    B, S, H, DK = q.shape
    DV = v.shape[-1]
    C = HGRN_CHUNK
    N = S // C

    def to_chunks(t):
        return t.astype(jnp.float32).reshape(B, N, C, H, t.shape[-1]).transpose(1, 0, 3, 2, 4)

    mask = jnp.tril(jnp.ones((C, C), dtype=bool))[:, :, None]

    def step(state, inp):
        qc, kc, vc, gc = inp
        b = jnp.cumsum(gc, axis=2)
        o_inter = jnp.einsum('bhtk,bhkv->bhtv', qc * jnp.exp(b), state)
        decay = jnp.exp(jnp.where(mask, b[:, :, :, None, :] - b[:, :, None, :, :], -jnp.inf))
        scores = jnp.einsum('bhtk,bhsk,bhtsk->bhts', qc, kc, decay)
        o_intra = jnp.einsum('bhts,bhsv->bhtv', scores, vc)
        b_last = b[:, :, -1:, :]
        new_state = jnp.exp(b_last[:, :, 0, :])[..., None] * state + jnp.einsum(
            'bhsk,bhsv->bhkv', kc * jnp.exp(b_last - b), vc)
        return new_state, o_inter + o_intra

    init = jnp.zeros((B, H, DK, DV), jnp.float32)
    _, o = lax.scan(step, init, (to_chunks(q), to_chunks(k), to_chunks(v), to_chunks(log_f)))
    return o.transpose(1, 0, 3, 2, 4).reshape(B, S, H, DV)


def _hgrn_branch(x_q, x_ff, x_fb, x_i, x_g, lb_fwd, lb_bwd, norm_w):
    B, S, _ = x_q.shape

    def heads(t, d):
        return t.reshape(B, S, HGRN_HEADS, d)

    q = heads(x_q, HGRN_DK)
    v = heads(x_i, HGRN_DV)

    def gates(z, lb):
        zf = z.astype(jnp.float32)
        f = lb + (1.0 - lb) * jax.nn.sigmoid(zf)
        k = (1.0 - lb) * jax.nn.sigmoid(-zf)
        return heads(k, HGRN_DK), heads(jnp.log(f), HGRN_DK)

    k_f, lf_f = gates(x_ff, lb_fwd)
    k_b, lf_b = gates(x_fb, lb_bwd)
    o_f = _hgrn_scan(q, k_f, v, lf_f)
    rev = lambda t: jnp.flip(t, axis=1)
    o_b = rev(_hgrn_scan(rev(q), rev(k_b), rev(v), rev(lf_b)))
    o = _rmsnorm(o_f + o_b, norm_w) * jax.nn.silu(heads(x_g, HGRN_DV).astype(jnp.float32))
    return o.reshape(B, S, HV).astype(x_q.dtype)


def _mla_branch(x_dq, x_dkv, q_norm_w, w_uq, kv_norm_w, w_uk, w_uv):
    B, S, _ = x_dq.shape
    c_q = _rmsnorm(x_dq, q_norm_w)
    q = (c_q @ w_uq).reshape(B, S, MLA_HEADS, MLA_NOPE + MLA_ROPE)
    q_nope, q_rope = q[..., :MLA_NOPE], _rope(q[..., MLA_NOPE:])
    c_kv = _rmsnorm(x_dkv[..., :MLA_KV_LORA], kv_norm_w)
    k_rope = _rope(x_dkv[..., MLA_KV_LORA:])
    q_lat = jnp.einsum('bshn,chn->bshc', q_nope, w_uk.reshape(MLA_KV_LORA, MLA_HEADS, MLA_NOPE))
    nq = S // Q_BLOCK
    scale = (MLA_NOPE + MLA_ROPE) ** -0.5

    def blocks(t):
        return t.reshape((B, nq, Q_BLOCK) + t.shape[2:]).swapaxes(0, 1)

    def attend(qs):
        ql, qr = qs
        s = jnp.einsum('bqhc,bkc->bhqk', ql, c_kv) + jnp.einsum('bqhr,bkr->bhqk', qr, k_rope)
        p = jax.nn.softmax(s.astype(jnp.float32) * scale, axis=-1).astype(c_kv.dtype)
        return jnp.einsum('bhqk,bkc->bqhc', p, c_kv)

    o_lat = lax.map(attend, (blocks(q_lat), blocks(q_rope)))
    o_lat = o_lat.swapaxes(0, 1).reshape(B, S, MLA_HEADS, MLA_KV_LORA)
    o = jnp.einsum('bshc,chv->bshv', o_lat, w_uv.reshape(MLA_KV_LORA, MLA_HEADS, MLA_V))
    return o.reshape(B, S, MLA_HEADS * MLA_V)


def _memory_branch(x_mq, mem, w_mem_kv):
    B, S, _ = x_mq.shape
    M = mem.shape[1]
    q = x_mq.reshape(B, S, MEM_HEADS, MEM_HEAD_DIM)
    kv = (mem @ w_mem_kv).reshape(B, M, 2, MEM_HEADS, MEM_HEAD_DIM)
    s = jnp.einsum('bshd,bmhd->bhsm', q, kv[:, :, 0])
    p = jax.nn.softmax(s.astype(jnp.float32) * MEM_HEAD_DIM ** -0.5, axis=-1).astype(x_mq.dtype)
    o = jnp.einsum('bhsm,bmhd->bshd', p, kv[:, :, 1])
    return o.reshape(B, S, MEM_Q)


def _moe(x, router_w, router_b, w_gu, b_gu, w_down, b_down):
    B, S, D = x.shape
    T = B * S
    xf = x.reshape(T, D)
    logits = xf.astype(jnp.float32) @ router_w.astype(jnp.float32) + router_b.astype(jnp.float32)
    top_logit, top_idx = lax.top_k(logits, TOP_K)
    gate = jax.nn.softmax(top_logit, axis=-1).astype(x.dtype)
    A = T * TOP_K
    flat_e = top_idx.reshape(A)
    order = jnp.argsort(flat_e, stable=True)
    e_sorted = flat_e[order]
    tok_sorted = (order // TOP_K).astype(jnp.int32)
    w_sorted = gate.reshape(A)[order]
    counts = jnp.bincount(flat_e, length=N_EXPERTS)
    padded = (counts + MOE_BLOCK - 1) // MOE_BLOCK * MOE_BLOCK
    start = jnp.cumsum(counts) - counts
    pad_end = jnp.cumsum(padded)
    pad_start = pad_end - padded
    dest = pad_start[e_sorted] + jnp.arange(A, dtype=jnp.int32) - start[e_sorted]
    n_blocks = -(-A // MOE_BLOCK) + N_EXPERTS
    P = n_blocks * MOE_BLOCK
    slot_tok = jnp.zeros((P,), jnp.int32).at[dest].set(tok_sorted)
    slot_w = jnp.zeros((P,), x.dtype).at[dest].set(w_sorted)
    block_e = jnp.minimum(
        jnp.searchsorted(pad_end, jnp.arange(n_blocks, dtype=jnp.int32) * MOE_BLOCK, side='right'),
        N_EXPERTS - 1)

    def expert_block(y, blk):
        tok, w, e = blk
        h = xf[tok] @ w_gu[e] + b_gu[e]
        g = jnp.minimum(h[:, :D_FF], SWIGLU_LIMIT)
        u = jnp.clip(h[:, D_FF:], -SWIGLU_LIMIT, SWIGLU_LIMIT)
        act = (u + 1.0) * g * jax.nn.sigmoid(SWIGLU_ALPHA * g)
        out = act @ w_down[e] + b_down[e]
        return y.at[tok].add(out * w[:, None]), None

    y, _ = lax.scan(expert_block, jnp.zeros_like(xf),
                    (slot_tok.reshape(n_blocks, MOE_BLOCK), slot_w.reshape(n_blocks, MOE_BLOCK), block_e))
    return y.reshape(B, S, D)


def _encoder_layer(x, mem, lb_fwd, lb_bwd, w_in, hgrn_norm_w, mla_q_norm_w, mla_w_uq, mla_kv_norm_w,
                   mla_w_uk, mla_w_uv, mem_w_kv, w_out, ln1_w, ln1_b, router_w, router_b,
                   exp_w_gu, exp_b_gu, exp_w_down, exp_b_down, ln2_w, ln2_b):
    proj = x @ w_in
    a_q, a_ff, a_fb, a_i, a_g, b_dq, b_dkv, m_q, gate_logits = jnp.split(proj, _split_points(), axis=-1)
    o_a = _hgrn_branch(a_q, a_ff, a_fb, a_i, a_g, lb_fwd, lb_bwd, hgrn_norm_w)
    o_b = _mla_branch(b_dq, b_dkv, mla_q_norm_w, mla_w_uq, mla_kv_norm_w, mla_w_uk, mla_w_uv)
    o_m = _memory_branch(m_q, mem, mem_w_kv)
    g_a, g_b, g_m = jnp.split(jax.nn.sigmoid(gate_logits), 3, axis=-1)
    merged = g_a * o_a + g_b * o_b + g_m * o_m
    x = _layernorm(DN_ALPHA * x + merged @ w_out, ln1_w, ln1_b)
    x = _layernorm(DN_ALPHA * x + _moe(x, router_w, router_b, exp_w_gu, exp_b_gu, exp_w_down, exp_b_down),
                   ln2_w, ln2_b)
    return x


def setup_inputs(seed: int = 0) -> dict:
    key = jax.random.key(seed)
    keys = jax.random.split(key, 24)
    f32 = jnp.float32

    def nrm(i, shape, scale):
        return jax.random.normal(keys[i], shape, f32) * scale

    L = DEPTH
    i_start = 3 * HG
    col_scale = jnp.ones((IN_COLS,), f32).at[i_start:i_start + HV].set(DN_BETA)
    kv_scale = jnp.concatenate([jnp.ones((MEM_Q,), f32), jnp.full((MEM_Q,), DN_BETA, f32)])
    return {
        'x_prompt': nrm(0, (BATCH, SEQ, D_MODEL), 1.0),
        'x_sample': nrm(1, (DEC_BATCH, DEC_SEQ, D_MODEL), 1.0),
        'mem_prompt': nrm(2, (BATCH, MEM_TOKENS, D_MODEL), 1.0),
        'mem_sample': nrm(3, (DEC_BATCH, MEM_TOKENS, D_MODEL), 1.0),
        'w_in': nrm(4, (L, D_MODEL, IN_COLS), D_MODEL ** -0.5) * col_scale,
        'hgrn_lb_logits': nrm(5, (2, L + 1, HG), 1.0),
        'hgrn_norm_w': 1.0 + nrm(6, (L, HGRN_DV), 0.05),
        'mla_q_norm_w': 1.0 + nrm(7, (L, MLA_Q_LORA), 0.05),
        'mla_w_uq': nrm(8, (L, MLA_Q_LORA, MLA_HEADS * (MLA_NOPE + MLA_ROPE)), MLA_Q_LORA ** -0.5),
        'mla_kv_norm_w': 1.0 + nrm(9, (L, MLA_KV_LORA), 0.05),
        'mla_w_uk': nrm(10, (L, MLA_KV_LORA, MLA_HEADS * MLA_NOPE), MLA_KV_LORA ** -0.5),
        'mla_w_uv': nrm(11, (L, MLA_KV_LORA, MLA_HEADS * MLA_V), DN_BETA * MLA_KV_LORA ** -0.5),
        'mem_w_kv': nrm(12, (L, D_MODEL, 2 * MEM_Q), D_MODEL ** -0.5) * kv_scale,
        'w_out': nrm(13, (L, D_MODEL, D_MODEL), DN_BETA * D_MODEL ** -0.5),
        'ln1_w': 1.0 + nrm(14, (L, D_MODEL), 0.05),
        'ln1_b': nrm(15, (L, D_MODEL), 0.01),
        'router_w': nrm(16, (L, D_MODEL, N_EXPERTS), D_MODEL ** -0.5),
        'router_b': nrm(17, (L, N_EXPERTS), 0.01),
        'exp_w_gu': nrm(18, (L, N_EXPERTS, D_MODEL, 2 * D_FF), D_MODEL ** -0.5),
        'exp_b_gu': nrm(19, (L, N_EXPERTS, 2 * D_FF), 0.01),
        'exp_w_down': nrm(20, (L, N_EXPERTS, D_FF, D_MODEL), DN_BETA * D_FF ** -0.5),
        'exp_b_down': nrm(21, (L, N_EXPERTS, D_MODEL), 0.01),
        'ln2_w': 1.0 + nrm(22, (L, D_MODEL), 0.05),
        'ln2_b': nrm(23, (L, D_MODEL), 0.01),
    }


def reference(x_prompt, x_sample, mem_prompt, mem_sample, w_in, hgrn_lb_logits, hgrn_norm_w,
              mla_q_norm_w, mla_w_uq, mla_kv_norm_w, mla_w_uk, mla_w_uv, mem_w_kv, w_out,
              ln1_w, ln1_b, router_w, router_b, exp_w_gu, exp_b_gu, exp_w_down, exp_b_down,
              ln2_w, ln2_b):
    gamma = jax.nn.softmax(hgrn_lb_logits.astype(jnp.float32), axis=1)
    cum = jnp.cumsum(gamma, axis=1)
    lb = cum[:, 1:] - cum[:, :1]
    y_prompt, y_sample = x_prompt, x_sample
    for l in range(DEPTH):
        lw = (w_in[l], hgrn_norm_w[l], mla_q_norm_w[l], mla_w_uq[l], mla_kv_norm_w[l], mla_w_uk[l],
              mla_w_uv[l], mem_w_kv[l], w_out[l], ln1_w[l], ln1_b[l], router_w[l], router_b[l],
              exp_w_gu[l], exp_b_gu[l], exp_w_down[l], exp_b_down[l], ln2_w[l], ln2_b[l])
        y_prompt = _encoder_layer(y_prompt, mem_prompt, lb[0, l], lb[1, l], *lw)
        y_sample = _encoder_layer(y_sample, mem_sample, lb[0, l], lb[1, l], *lw)
    return (y_prompt, y_sample)
```

```python
import functools

import numpy as np
import jax
import jax.numpy as jnp
from jax import lax
from jax.experimental import pallas as pl
from jax.experimental.pallas import tpu as pltpu

F32 = jnp.float32
BF16 = jnp.bfloat16

D_MODEL = 1024
HGRN_HEADS = 8
HGRN_DK = 128
MLA_HEADS = 8
MLA_Q_LORA = 384
MLA_KV_LORA = 256
MLA_NOPE = 128
MLA_ROPE = 64
MLA_V = 128
ROPE_THETA = 10000.0
MEM_HEADS = 4
MEM_HEAD_DIM = D_MODEL // MEM_HEADS
N_EXPERTS = 32
TOP_K = 4
D_FF = D_MODEL
SWIGLU_LIMIT = 7.0
SWIGLU_ALPHA = 1.702
DN_ALPHA = 2.0 ** 0.25
LN_EPS = 1e-5
RMS_EPS = 1e-6

LANES = 128
SUBLANES = 8
QK_PAD = 256
HGRN_CHUNK = 64
HGRN_SAFE_RANGE = 160.0
EXPERT_BLOCK = 512
VMEM_LIMIT = 56 * 1024 * 1024

COL_GA, COL_GB, COL_GM, COL_Q, COL_ZF, COL_ZB, COL_I, COL_G, COL_MQ = range(9)
COL_DQ, COL_DKV = 18, 19
IN_COLS_PAD = 10240

NT_DIMS = (((1,), (1,)), ((), ()))
TN_DIMS = (((0,), (0,)), ((), ()))


def _params(sem, vmem=VMEM_LIMIT):
    return pltpu.CompilerParams(dimension_semantics=sem, vmem_limit_bytes=vmem)


def _mm_kernel(x_ref, w_ref, o_ref, xb_ref):
    @pl.when(pl.program_id(1) == 0)
    def _():
        xb_ref[...] = x_ref[...].astype(BF16)

    o_ref[...] = jnp.dot(xb_ref[...], w_ref[...], preferred_element_type=F32).astype(o_ref.dtype)


def _matmul(x, w, out_dtype, tm, tn, name):
    m, k = x.shape
    n = w.shape[1]
    return pl.pallas_call(
        _mm_kernel,
        out_shape=jax.ShapeDtypeStruct((m, n), out_dtype),
        grid=(m // tm, n // tn),
        in_specs=[pl.BlockSpec((tm, k), lambda i, j: (i, 0)),
                  pl.BlockSpec((k, tn), lambda i, j: (0, j))],
        out_specs=pl.BlockSpec((tm, tn), lambda i, j: (i, j)),
        scratch_shapes=[pltpu.VMEM((tm, k), BF16)],
        compiler_params=_params(("arbitrary", "arbitrary")),
        name=name,
    )(x, w)


def _hgrn_kernel(*refs, reverse, epilogue, sc):
    if epilogue:
        lb_ref, q_ref, z_ref, v_ref, of_ref, g_ref, nw_ref, o_ref, st_ref, kk_s, b_s, vf_s, os_s = refs
    else:
        lb_ref, q_ref, z_ref, v_ref, o_ref, st_ref, kk_s, b_s, vf_s = refs
    C = HGRN_CHUNK
    nch = sc // C

    @pl.when(pl.program_id(1) == 0)
    def _():
        st_ref[...] = jnp.zeros_like(st_ref)

    lb = lb_ref[...]
    row = lax.broadcasted_iota(jnp.int32, (C, C), 0)
    col = lax.broadcasted_iota(jnp.int32, (C, C), 1)
    tri = (row <= col) if reverse else (row >= col)
    trib = jnp.where(tri, 1.0, 0.0).astype(BF16)

    minb = None
    for c in range(nch):
        rows = slice(c * C, (c + 1) * C)
        z = z_ref[rows, :].astype(F32)
        f = lb + (1.0 - lb) * jax.nn.sigmoid(z)
        lf = jnp.log(f)
        kk_s[rows, :] = (1.0 - lb) * jax.nn.sigmoid(-z)
        hi = lf.astype(BF16)
        r1 = lf - hi.astype(F32)
        mid = r1.astype(BF16)
        lo = (r1 - mid.astype(F32)).astype(BF16)
        b = (jnp.dot(trib, hi, preferred_element_type=F32)
             + jnp.dot(trib, mid, preferred_element_type=F32)
             + jnp.dot(trib, lo, preferred_element_type=F32))
        b_s[rows, :] = b
        mb = jnp.min(b)
        minb = mb if minb is None else jnp.minimum(minb, mb)

    rid = lax.broadcasted_iota(jnp.int32, (C, 1), 0)

    def chunk(i, carry, fast):
        c = (nch - 1 - i) if reverse else i
        r0 = pl.multiple_of(c * C, C)
        rows = pl.ds(r0, C)
        for h in range(HGRN_HEADS):
            cols = slice(h * HGRN_DK, (h + 1) * HGRN_DK)
            q = q_ref[rows, cols].astype(F32)
            kk = kk_s[rows, cols]
            b = b_s[rows, cols]
            v = v_ref[rows, cols]
            bl = b[0:1, :] if reverse else b[C - 1:C, :]
            if fast:
                bm = 0.5 * bl
                qd = (q * jnp.exp(b - bm)).astype(BF16)
                kd = (kk * jnp.exp(bm - b)).astype(BF16)
                s = lax.dot_general(qd, kd, NT_DIMS, preferred_element_type=F32)
                s = jnp.where(tri, s, 0.0).astype(BF16)
                o = jnp.dot(s, v, preferred_element_type=F32)
            else:
                def sbody(g_i, o_acc):
                    grp = pl.ds(pl.multiple_of(r0 + g_i * SUBLANES, SUBLANES), SUBLANES)
                    b8 = b_s[grp, cols]
                    k8 = kk_s[grp, cols]
                    v8 = vf_s[grp, cols]
                    for jj in range(SUBLANES):
                        s_i = g_i * SUBLANES + jj
                        w = q * k8[jj:jj + 1, :] * jnp.exp(jnp.minimum(b - b8[jj:jj + 1, :], 0.0))
                        scol = jnp.sum(w, axis=-1, keepdims=True)
                        keep = (rid <= s_i) if reverse else (rid >= s_i)
                        o_acc = o_acc + jnp.where(keep, scol, 0.0) * v8[jj:jj + 1, :]
                    return o_acc
                o = lax.fori_loop(0, C // SUBLANES, sbody, jnp.zeros((C, HGRN_DK), F32))
            st = st_ref[h]
            qi = (q * jnp.exp(b)).astype(BF16)
            o = o + lax.dot_general(qi, st.astype(BF16), NT_DIMS, preferred_element_type=F32)
            ke = (kk * jnp.exp(bl - b)).astype(BF16)
            upd = lax.dot_general(v, ke, TN_DIMS, preferred_element_type=F32)
            st_ref[h] = st * jnp.exp(bl) + upd
            if epilogue:
                os_s[rows, cols] = o + of_ref[rows, cols].astype(F32)
            else:
                o_ref[rows, cols] = o.astype(o_ref.dtype)
        return carry

    safe = minb >= -HGRN_SAFE_RANGE

    @pl.when(safe)
    def _():
        lax.fori_loop(0, nch, functools.partial(chunk, fast=True), 0)

    @pl.when(jnp.logical_not(safe))
    def _():
        vf_s[...] = v_ref[...].astype(F32)
        lax.fori_loop(0, nch, functools.partial(chunk, fast=False), 0)

    if epilogue:
        nw = nw_ref[...]
        for h in range(HGRN_HEADS):
            cols = slice(h * HGRN_DK, (h + 1) * HGRN_DK)
            os = os_s[:, cols]
            ms = jnp.mean(os * os, axis=-1, keepdims=True)
            y = os * lax.rsqrt(ms + RMS_EPS) * nw
            g = g_ref[:, cols].astype(F32)
            o_ref[:, cols] = (y * (g * jax.nn.sigmoid(g))).astype(o_ref.dtype)


def _hgrn(proj, lb, norm_w, batch, seq, sc):
    t = batch * seq
    ns = seq // sc
    blk = (sc, D_MODEL)

    def spec(colblk, reverse):
        if reverse:
            return pl.BlockSpec(blk, lambda b, n: (b * ns + ns - 1 - n, colblk))
        return pl.BlockSpec(blk, lambda b, n: (b * ns + n, colblk))

    def row_spec(reverse):
        if reverse:
            return pl.BlockSpec(blk, lambda b, n: (b * ns + ns - 1 - n, 0))
        return pl.BlockSpec(blk, lambda b, n: (b * ns + n, 0))

    lb_spec = pl.BlockSpec((1, D_MODEL), lambda b, n: (0, 0))
    common_scratch = [pltpu.VMEM((HGRN_HEADS, HGRN_DK, HGRN_DK), F32),
                      pltpu.VMEM(blk, F32), pltpu.VMEM(blk, F32), pltpu.VMEM(blk, F32)]
    o_f = pl.pallas_call(
        functools.partial(_hgrn_kernel, reverse=False, epilogue=False, sc=sc),
        out_shape=jax.ShapeDtypeStruct((t, D_MODEL), BF16),
        grid=(batch, ns),
        in_specs=[lb_spec, spec(COL_Q, False), spec(COL_ZF, False), spec(COL_I, False)],
        out_specs=row_spec(False),
        scratch_shapes=common_scratch,
        compiler_params=_params(("arbitrary", "arbitrary")),
        name="hgrn_fwd",
    )(lb[0:1], proj, proj, proj)
    o_a = pl.pallas_call(
        functools.partial(_hgrn_kernel, reverse=True, epilogue=True, sc=sc),
        out_shape=jax.ShapeDtypeStruct((t, D_MODEL), BF16),
        grid=(batch, ns),
        in_specs=[lb_spec, spec(COL_Q, True), spec(COL_ZB, True), spec(COL_I, True),
                  row_spec(True), spec(COL_G, True),
                  pl.BlockSpec((1, HGRN_DK), lambda b, n: (0, 0))],
        out_specs=row_spec(True),
        scratch_shapes=common_scratch + [pltpu.VMEM(blk, F32)],
        compiler_params=_params(("arbitrary", "arbitrary")),
        name="hgrn_bwd",
    )(lb[1:2], proj, proj, proj, o_f, proj, norm_w.reshape(1, HGRN_DK))
    return o_a


def _mla_prep_kernel(dq_ref, dkv_ref, qnw_ref, kvnw_ref, wq_ref, wk_ref, wv_ref, cos_ref, sin_ref,
                     q_ref, k_ref, v_ref):
    scale = (MLA_NOPE + MLA_ROPE) ** -0.5
    cos = cos_ref[...]
    sin = sin_ref[...]
    dq = dq_ref[...].astype(F32)
    ms = jnp.sum(dq * dq, axis=-1, keepdims=True) * (1.0 / MLA_Q_LORA)
    cq = (dq * lax.rsqrt(ms + RMS_EPS) * qnw_ref[...]).astype(BF16)
    qa = jnp.dot(cq, wq_ref[...], preferred_element_type=F32)
    for h in range(MLA_HEADS):
        base = h * QK_PAD
        q_ref[:, base:base + MLA_NOPE] = (qa[:, base:base + MLA_NOPE] * scale).astype(BF16)
        rp = qa[:, base + MLA_NOPE:base + QK_PAD]
        sw = qa[:, MLA_HEADS * QK_PAD + h * LANES:MLA_HEADS * QK_PAD + (h + 1) * LANES]
        q_ref[:, base + MLA_NOPE:base + QK_PAD] = ((rp * cos + sw * sin) * scale).astype(BF16)
    dkv = dkv_ref[...].astype(F32)
    ckv = dkv[:, :MLA_KV_LORA]
    msk = jnp.mean(ckv * ckv, axis=-1, keepdims=True)
    cn = (ckv * lax.rsqrt(msk + RMS_EPS) * kvnw_ref[...]).astype(BF16)
    kn = jnp.dot(cn, wk_ref[...], preferred_element_type=F32)
    v_ref[...] = jnp.dot(cn, wv_ref[...], preferred_element_type=F32).astype(BF16)
    kr = (dkv[:, MLA_KV_LORA:MLA_KV_LORA + LANES] * cos
          + dkv[:, MLA_KV_LORA + LANES:MLA_KV_LORA + 2 * LANES] * sin).astype(BF16)
    for h in range(MLA_HEADS):
        base = h * QK_PAD
        k_ref[:, base:base + MLA_NOPE] = kn[:, h * MLA_NOPE:(h + 1) * MLA_NOPE].astype(BF16)
        k_ref[:, base + MLA_NOPE:base + QK_PAD] = kr


def _mla_prep(proj, qnw, kvnw, wq, wk, wv, cos, sin, batch, seq, tm):
    t = batch * seq
    npos = seq // tm
    full = lambda shape: pl.BlockSpec(shape, lambda i: (0, 0))
    return pl.pallas_call(
        _mla_prep_kernel,
        out_shape=(jax.ShapeDtypeStruct((t, MLA_HEADS * QK_PAD), BF16),
                   jax.ShapeDtypeStruct((t, MLA_HEADS * QK_PAD), BF16),
                   jax.ShapeDtypeStruct((t, MLA_HEADS * MLA_V), BF16)),
        grid=(t // tm,),
        in_specs=[pl.BlockSpec((tm, 512), lambda i: (i, COL_DQ)),
                  pl.BlockSpec((tm, 512), lambda i: (i, COL_DKV)),
                  full((1, 512)), full((1, MLA_KV_LORA)),
                  full(wq.shape), full(wk.shape), full(wv.shape),
                  pl.BlockSpec((tm, LANES), lambda i: (i % npos, 0)),
                  pl.BlockSpec((tm, LANES), lambda i: (i % npos, 0))],
        out_specs=(pl.BlockSpec((tm, MLA_HEADS * QK_PAD), lambda i: (i, 0)),
                   pl.BlockSpec((tm, MLA_HEADS * QK_PAD), lambda i: (i, 0)),
                   pl.BlockSpec((tm, MLA_HEADS * MLA_V), lambda i: (i, 0))),
        compiler_params=_params(("arbitrary",)),
        name="mla_prep",
    )(proj, proj, qnw, kvnw, wq, wk, wv, cos, sin)


def _flash_kernel(q_ref, k_ref, v_ref, o_ref, m_s, l_s, acc_s, *, bk, nk):
    q = q_ref[...]
    m_s[...] = jnp.full(m_s.shape, -jnp.inf, F32)
    l_s[...] = jnp.zeros(l_s.shape, F32)
    acc_s[...] = jnp.zeros(acc_s.shape, F32)

    def body(j, carry):
        rows = pl.ds(pl.multiple_of(j * bk, bk), bk)
        s = lax.dot_general(q, k_ref[rows, :], NT_DIMS, preferred_element_type=F32)
        m_prev = m_s[...]
        m_new = jnp.maximum(m_prev, jnp.max(s, axis=-1, keepdims=True))
        a = jnp.exp(m_prev - m_new)
        p = jnp.exp(s - m_new)
        l_s[...] = a * l_s[...] + jnp.sum(p, axis=-1, keepdims=True)
        acc_s[...] = a * acc_s[...] + jnp.dot(p.astype(BF16), v_ref[rows, :], preferred_element_type=F32)
        m_s[...] = m_new
        return carry

    lax.fori_loop(0, nk, body, 0)
    o_ref[...] = (acc_s[...] / l_s[...]).astype(o_ref.dtype)


def _flash(q, k, v, batch, seq, bq, bk):
    t = batch * seq
    nq = seq // bq
    return pl.pallas_call(
        functools.partial(_flash_kernel, bk=bk, nk=seq // bk),
        out_shape=jax.ShapeDtypeStruct((t, MLA_HEADS * MLA_V), BF16),
        grid=(batch, MLA_HEADS, nq),
        in_specs=[pl.BlockSpec((bq, QK_PAD), lambda b, h, i: (b * nq + i, h)),
                  pl.BlockSpec((seq, QK_PAD), lambda b, h, i: (b, h)),
                  pl.BlockSpec((seq, MLA_V), lambda b, h, i: (b, h))],
        out_specs=pl.BlockSpec((bq, MLA_V), lambda b, h, i: (b * nq + i, h)),
        scratch_shapes=[pltpu.VMEM((bq, 1), F32), pltpu.VMEM((bq, 1), F32), pltpu.VMEM((bq, MLA_V), F32)],
        compiler_params=_params(("arbitrary", "arbitrary", "arbitrary")),
        name="mla_flash",
    )(q, k, v)


def _layernorm(y, w, b):
    mu = jnp.mean(y, axis=-1, keepdims=True)
    yc = y - mu
    var = jnp.mean(yc * yc, axis=-1, keepdims=True)
    return yc * lax.rsqrt(var + LN_EPS) * w + b


def _merge_kernel(x_ref, ga_ref, gb_ref, gm_ref, mq_ref, oa_ref, ob_ref, kvm_ref, wout_ref,
                  l1w_ref, l1b_ref, rw_ref, rb_ref,
                  x1_ref, ri_ref, rg_ref, cnt_ref, carry_s, *, tm):
    @pl.when(pl.program_id(0) == 0)
    def _():
        carry_s[...] = jnp.zeros_like(carry_s)

    parts = []
    for h in range(MEM_HEADS):
        cols = slice(h * MEM_HEAD_DIM, (h + 1) * MEM_HEAD_DIM)
        kh = kvm_ref[:, cols]
        vh = kvm_ref[:, D_MODEL + h * MEM_HEAD_DIM:D_MODEL + (h + 1) * MEM_HEAD_DIM]
        s = lax.dot_general(mq_ref[:, cols], kh, NT_DIMS, preferred_element_type=F32) * (MEM_HEAD_DIM ** -0.5)
        s = s - jnp.max(s, axis=-1, keepdims=True)
        p = jnp.exp(s)
        p = p / jnp.sum(p, axis=-1, keepdims=True)
        parts.append(jnp.dot(p.astype(BF16), vh, preferred_element_type=F32))
    om = jnp.concatenate(parts, axis=1)

    merged = (jax.nn.sigmoid(ga_ref[...].astype(F32)) * oa_ref[...].astype(F32)
              + jax.nn.sigmoid(gb_ref[...].astype(F32)) * ob_ref[...].astype(F32)
              + jax.nn.sigmoid(gm_ref[...].astype(F32)) * om)
    y = DN_ALPHA * x_ref[...] + jnp.dot(merged.astype(BF16), wout_ref[...], preferred_element_type=F32)
    x1 = _layernorm(y, l1w_ref[...], l1b_ref[...])
    x1_ref[...] = x1

    logits = jnp.dot(x1, rw_ref[...], preferred_element_type=F32, precision=lax.Precision.HIGHEST) + rb_ref[...]
    lane_i = lax.broadcasted_iota(jnp.int32, (tm, LANES), 1)
    lane = lane_i.astype(F32)
    work = logits
    idx, val = [], []
    for _ in range(TOP_K):
        mx = jnp.max(work, axis=-1, keepdims=True)
        ix = jnp.min(jnp.where(work == mx, lane, float(LANES)), axis=-1, keepdims=True)
        idx.append(ix)
        val.append(mx)
        work = jnp.where(lane == ix, -jnp.inf, work)
    ex = [jnp.exp(v - val[0]) for v in val]
    tot = ex[0] + ex[1] + ex[2] + ex[3]
    hot = [jnp.where(lane == ix, 1.0, 0.0) for ix in idx]
    multi = hot[0] + hot[1] + hot[2] + hot[3]
    r = lax.broadcasted_iota(jnp.int32, (tm, tm), 0)
    c = lax.broadcasted_iota(jnp.int32, (tm, tm), 1)
    lower = jnp.where(r > c, 1.0, 0.0).astype(BF16)
    before = jnp.dot(lower, multi.astype(BF16), preferred_element_type=F32) + carry_s[0:1, :]
    ri = jnp.zeros((tm, LANES), F32)
    rg = jnp.zeros((tm, LANES), F32)
    for j in range(TOP_K):
        rank = jnp.sum(before * hot[j], axis=-1, keepdims=True)
        ri = ri + jnp.where(lane == float(j), idx[j], 0.0) + jnp.where(lane == float(TOP_K + j), rank, 0.0)
        rg = rg + jnp.where(lane == float(j), ex[j] / tot, 0.0)
    ri_ref[...] = ri.astype(jnp.int32)
    rg_ref[...] = rg
    carry_s[...] = carry_s[...] + jnp.sum(multi, axis=0, keepdims=True)
    cnt_ref[...] = carry_s[...]


def _merge(x2d, proj, o_a, o_b, kvm, wout, l1w, l1b, rw, rb, batch, seq, tm):
    t = batch * seq
    per_b = seq // tm
    nmem = kvm.shape[0] // batch
    tile = lambda colblk: pl.BlockSpec((tm, D_MODEL), lambda i: (i, colblk))
    full = lambda shape: pl.BlockSpec(shape, lambda i: (0, 0))
    return pl.pallas_call(
        functools.partial(_merge_kernel, tm=tm),
        out_shape=(jax.ShapeDtypeStruct((t, D_MODEL), F32),
                   jax.ShapeDtypeStruct((t, LANES), jnp.int32),
                   jax.ShapeDtypeStruct((t, LANES), F32),
                   jax.ShapeDtypeStruct((8, LANES), F32)),
        grid=(t // tm,),
        in_specs=[tile(0), tile(COL_GA), tile(COL_GB), tile(COL_GM), tile(COL_MQ), tile(0), tile(0),
                  pl.BlockSpec((nmem, 2 * D_MODEL), lambda i: (i // per_b, 0)),
                  full((D_MODEL, D_MODEL)), full((1, D_MODEL)), full((1, D_MODEL)),
                  full((D_MODEL, LANES)), full((1, LANES))],
        out_specs=(tile(0), pl.BlockSpec((tm, LANES), lambda i: (i, 0)),
                   pl.BlockSpec((tm, LANES), lambda i: (i, 0)), full((8, LANES))),
        scratch_shapes=[pltpu.VMEM((8, LANES), F32)],
        compiler_params=_params(("arbitrary",)),
        name="merge_router",
    )(x2d, proj, proj, proj, proj, o_a, o_b, kvm, wout, l1w, l1b, rw, rb)


def _dispatch_kernel(dest_hbm, x_hbm, xs_in, xs_hbm, idx_s, sem_i, sem_r, *, tm):
    del xs_in
    i = pl.program_id(0)
    n = TOP_K * tm
    cp = pltpu.make_async_copy(dest_hbm.at[pl.ds(pl.multiple_of(i * n, n), n)], idx_s, sem_i)
    cp.start()
    cp.wait()
    base = i * tm

    def body(r, carry):
        src = x_hbm.at[pl.ds(base + r, 1)]
        for j in range(TOP_K):
            d = idx_s[r * TOP_K + j]
            pltpu.make_async_copy(src, xs_hbm.at[pl.ds(d, 1)], sem_r).start()
        return carry

    lax.fori_loop(0, tm, body, 0)
    pltpu.make_async_copy(x_hbm.at[pl.ds(0, n)], xs_hbm.at[pl.ds(0, n)], sem_r).wait()


def _dispatch(dest, x1, slots, tm):
    t = x1.shape[0]
    zeros = jnp.zeros((slots, D_MODEL), F32)
    return pl.pallas_call(
        functools.partial(_dispatch_kernel, tm=tm),
        out_shape=jax.ShapeDtypeStruct((slots, D_MODEL), F32),
        grid=(t // tm,),
        in_specs=[pl.BlockSpec(memory_space=pl.ANY)] * 3,
        out_specs=pl.BlockSpec(memory_space=pl.ANY),
        scratch_shapes=[pltpu.SMEM((TOP_K * tm,), jnp.int32),
                        pltpu.SemaphoreType.DMA, pltpu.SemaphoreType.DMA],
        input_output_aliases={2: 0},
        compiler_params=_params(("arbitrary",)),
        name="moe_dispatch",
    )(dest, x1, zeros)


def _expert_kernel(be_ref, nu_ref, xs_ref, wgu_ref, bgu_ref, wd_ref, bd_ref, ys_ref):
    del be_ref
    i = pl.program_id(0)

    @pl.when(i < nu_ref[0])
    def _():
        h = jnp.dot(xs_ref[...].astype(BF16), wgu_ref[0], preferred_element_type=F32) + bgu_ref[0]
        g = jnp.minimum(h[:, :D_FF], SWIGLU_LIMIT)
        u = jnp.clip(h[:, D_FF:], -SWIGLU_LIMIT, SWIGLU_LIMIT)
        act = (u + 1.0) * g * jax.nn.sigmoid(SWIGLU_ALPHA * g)
        ys_ref[...] = jnp.dot(act.astype(BF16), wd_ref[0], preferred_element_type=F32) + bd_ref[0]

    @pl.when(i >= nu_ref[0])
    def _():
        ys_ref[...] = jnp.zeros_like(ys_ref)


def _experts(block_e, n_used, xs, wgu, bgu, wd, bd):
    slots = xs.shape[0]
    nb = slots // EXPERT_BLOCK
    grid_spec = pltpu.PrefetchScalarGridSpec(
        num_scalar_prefetch=2,
        grid=(nb,),
        in_specs=[pl.BlockSpec((EXPERT_BLOCK, D_MODEL), lambda i, be, nu: (i, 0)),
                  pl.BlockSpec((1, D_MODEL, 2 * D_FF), lambda i, be, nu: (be[i], 0, 0)),
                  pl.BlockSpec((1, 1, 2 * D_FF), lambda i, be, nu: (be[i], 0, 0)),
                  pl.BlockSpec((1, D_FF, D_MODEL), lambda i, be, nu: (be[i], 0, 0)),
                  pl.BlockSpec((1, 1, D_MODEL), lambda i, be, nu: (be[i], 0, 0))],
        out_specs=pl.BlockSpec((EXPERT_BLOCK, D_MODEL), lambda i, be, nu: (i, 0)),
    )
    return pl.pallas_call(
        _expert_kernel,
        out_shape=jax.ShapeDtypeStruct((slots, D_MODEL), F32),
        grid_spec=grid_spec,
        compiler_params=_params(("arbitrary",)),
        name="moe_experts",
    )(block_e, n_used, xs, wgu, bgu, wd, bd)


def _combine_kernel(dest_hbm, x1_ref, rg_ref, ys_hbm, l2w_ref, l2b_ref, o_ref, idx_s, buf, sem_i, sem_r, *, tm):
    i = pl.program_id(0)
    nsteps = pl.num_programs(0)
    n = TOP_K * tm

    def issue(step, slot):
        cp = pltpu.make_async_copy(dest_hbm.at[pl.ds(pl.multiple_of(step * n, n), n)], idx_s, sem_i)
        cp.start()
        cp.wait()

        def body(r, carry):
            for j in range(TOP_K):
                d = idx_s[r * TOP_K + j]
                pltpu.make_async_copy(ys_hbm.at[pl.ds(d, 1)], buf.at[slot, pl.ds(j * tm + r, 1)],
                                      sem_r.at[slot]).start()
            return carry

        lax.fori_loop(0, tm, body, 0)

    @pl.when(i == 0)
    def _():
        issue(0, 0)

    @pl.when(i + 1 < nsteps)
    def _():
        issue(i + 1, (i + 1) % 2)

    slot = i % 2
    pltpu.make_async_copy(ys_hbm.at[pl.ds(0, n)], buf.at[slot], sem_r.at[slot]).wait()
    rg = rg_ref[...]
    moe = rg[:, 0:1] * buf[slot, pl.ds(0, tm), :]
    for j in range(1, TOP_K):
        moe = moe + rg[:, j:j + 1] * buf[slot, pl.ds(j * tm, tm), :]
    o_ref[...] = _layernorm(DN_ALPHA * x1_ref[...] + moe, l2w_ref[...], l2b_ref[...])


def _combine(dest, x1, rg, ys, l2w, l2b, tm):
    t = x1.shape[0]
    full = lambda shape: pl.BlockSpec(shape, lambda i: (0, 0))
    return pl.pallas_call(
        functools.partial(_combine_kernel, tm=tm),
        out_shape=jax.ShapeDtypeStruct((t, D_MODEL), F32),
        grid=(t // tm,),
        in_specs=[pl.BlockSpec(memory_space=pl.ANY),
                  pl.BlockSpec((tm, D_MODEL), lambda i: (i, 0)),
                  pl.BlockSpec((tm, LANES), lambda i: (i, 0)),
                  pl.BlockSpec(memory_space=pl.ANY),
                  full((1, D_MODEL)), full((1, D_MODEL))],
        out_specs=pl.BlockSpec((tm, D_MODEL), lambda i: (i, 0)),
        scratch_shapes=[pltpu.SMEM((TOP_K * tm,), jnp.int32),
                        pltpu.VMEM((2, TOP_K * tm, D_MODEL), F32),
                        pltpu.SemaphoreType.DMA, pltpu.SemaphoreType.DMA((2,))],
        compiler_params=_params(("arbitrary",)),
        name="moe_combine",
    )(dest, x1, rg, ys, l2w, l2b)


def _prep_w_in(w):
    zeros = lambda n: jnp.zeros((D_MODEL, n), w.dtype)
    kr = w[:, 5760:5824]
    kr_sw = jnp.concatenate([kr[:, MLA_ROPE // 2:], kr[:, :MLA_ROPE // 2]], axis=1)
    parts = [w[:, 6848:9920], w[:, 0:5120], w[:, 5824:6848],
             w[:, 5120:5504], zeros(128),
             w[:, 5504:5760], kr, zeros(64), kr_sw, zeros(64)]
    return jnp.concatenate(parts, axis=1).astype(BF16)


def _prep_w_uq(w):
    w3 = w.reshape(MLA_Q_LORA, MLA_HEADS, MLA_NOPE + MLA_ROPE)
    rope = w3[:, :, MLA_NOPE:]
    rope_sw = jnp.concatenate([rope[:, :, MLA_ROPE // 2:], rope[:, :, :MLA_ROPE // 2]], axis=-1)
    pad = jnp.zeros((MLA_Q_LORA, MLA_HEADS, QK_PAD - MLA_NOPE - MLA_ROPE), w.dtype)
    main = jnp.concatenate([w3, pad], axis=-1).reshape(MLA_Q_LORA, MLA_HEADS * QK_PAD)
    swp = jnp.concatenate([rope_sw, pad], axis=-1).reshape(MLA_Q_LORA, MLA_HEADS * LANES)
    both = jnp.concatenate([main, swp], axis=1)
    return jnp.pad(both, ((0, 512 - MLA_Q_LORA), (0, 0))).astype(BF16)


def _rope_tables(seq):
    inv_freq = ROPE_THETA ** (-jnp.arange(0, MLA_ROPE, 2, dtype=F32) / MLA_ROPE)
    ang = jnp.arange(seq, dtype=F32)[:, None] * inv_freq[None, :]
    cos, sin = jnp.cos(ang), jnp.sin(ang)
    pad = jnp.zeros((seq, LANES - MLA_ROPE), F32)
    return (jnp.concatenate([cos, cos, pad], axis=1), jnp.concatenate([-sin, sin, pad], axis=1))


def _tile(n, pref):
    return pref if n % pref == 0 else n


def _layer(x, mem, wts):
    batch, seq, _ = x.shape
    t = batch * seq
    x2d = x.reshape(t, D_MODEL)

    proj = _matmul(x2d, wts["w_in"], BF16, _tile(t, 1024), 1024, "in_proj")
    o_a = _hgrn(proj, wts["lb"], wts["hgrn_norm_w"], batch, seq, _tile(seq, 512))

    cos, sin = _rope_tables(seq)
    q, k, v = _mla_prep(proj, wts["qnw"], wts["kvnw"], wts["w_uq"], wts["w_uk"], wts["w_uv"],
                        cos, sin, batch, seq, _tile(seq, 512))
    o_b = _flash(q, k, v, batch, seq, _tile(seq, 512), _tile(seq, 512))

    nmem = mem.shape[1]
    kvm = _matmul(mem.reshape(batch * nmem, D_MODEL), wts["mem_w_kv"], BF16,
                  _tile(batch * nmem, 512), 1024, "mem_kv")

    tm = _tile(seq, 512)
    x1, ri, rg, cnt = _merge(x2d, proj, o_a, o_b, kvm, wts["w_out"], wts["ln1_w"], wts["ln1_b"],
                             wts["router_w"], wts["router_b"], batch, seq, tm)

    idx = ri[:, :TOP_K]
    rank = ri[:, TOP_K:2 * TOP_K]
    counts = cnt[0, :N_EXPERTS].astype(jnp.int32)
    padded = (counts + EXPERT_BLOCK - 1) // EXPERT_BLOCK * EXPERT_BLOCK
    pad_end = jnp.cumsum(padded)
    pad_start = pad_end - padded
    dest = (pad_start[idx] + rank).reshape(t * TOP_K).astype(jnp.int32)
    nb = t * TOP_K // EXPERT_BLOCK + N_EXPERTS
    block_e = jnp.minimum(
        jnp.searchsorted(pad_end, jnp.arange(nb, dtype=jnp.int32) * EXPERT_BLOCK, side="right"),
        N_EXPERTS - 1).astype(jnp.int32)
    n_used = (pad_end[-1:] // EXPERT_BLOCK).astype(jnp.int32)

    tmd = _tile(t, 512)
    xs = _dispatch(dest, x1, nb * EXPERT_BLOCK, tmd)
    ys = _experts(block_e, n_used, xs, wts["exp_w_gu"], wts["exp_b_gu"], wts["exp_w_down"], wts["exp_b_down"])
    y = _combine(dest, x1, rg, ys, wts["ln2_w"], wts["ln2_b"], _tile(t, 256))
    return y.reshape(batch, seq, D_MODEL)


def kernel(x_prompt, x_sample, mem_prompt, mem_sample, w_in, hgrn_lb_logits, hgrn_norm_w,
           mla_q_norm_w, mla_w_uq, mla_kv_norm_w, mla_w_uk, mla_w_uv, mem_w_kv, w_out,
           ln1_w, ln1_b, router_w, router_b, exp_w_gu, exp_b_gu, exp_w_down, exp_b_down,
           ln2_w, ln2_b):
    depth = w_in.shape[0]
    gamma = jax.nn.softmax(hgrn_lb_logits.astype(F32), axis=1)
    cum = jnp.cumsum(gamma, axis=1)
    lb_all = cum[:, 1:] - cum[:, :1]
    y_prompt, y_sample = x_prompt, x_sample
    for l in range(depth):
        row = lambda a: a[l].reshape(1, -1).astype(F32)
        wts = {
            "w_in": _prep_w_in(w_in[l]),
            "lb": lb_all[:, l],
            "hgrn_norm_w": hgrn_norm_w[l].astype(F32),
            "qnw": jnp.pad(row(mla_q_norm_w), ((0, 0), (0, 512 - MLA_Q_LORA))),
            "kvnw": row(mla_kv_norm_w),
            "w_uq": _prep_w_uq(mla_w_uq[l]),
            "w_uk": mla_w_uk[l].astype(BF16),
            "w_uv": mla_w_uv[l].astype(BF16),
            "mem_w_kv": mem_w_kv[l].astype(BF16),
            "w_out": w_out[l].astype(BF16),
            "ln1_w": row(ln1_w), "ln1_b": row(ln1_b),
            "router_w": jnp.pad(router_w[l].astype(F32), ((0, 0), (0, LANES - N_EXPERTS))),
            "router_b": jnp.pad(row(router_b), ((0, 0), (0, LANES - N_EXPERTS)), constant_values=-jnp.inf),
            "exp_w_gu": exp_w_gu[l].astype(BF16),
            "exp_b_gu": exp_b_gu[l].reshape(N_EXPERTS, 1, 2 * D_FF).astype(F32),
            "exp_w_down": exp_w_down[l].astype(BF16),
            "exp_b_down": exp_b_down[l].reshape(N_EXPERTS, 1, D_MODEL).astype(F32),
            "ln2_w": row(ln2_w), "ln2_b": row(ln2_b),
        }
        y_prompt = _layer(y_prompt, mem_prompt, wts)
        y_sample = _layer(y_sample, mem_sample, wts)
    return (y_prompt, y_sample)
```

```python
import functools

import numpy as np
import jax
import jax.numpy as jnp
from jax import lax
from jax.experimental import pallas as pl
from jax.experimental.pallas import tpu as pltpu

F32 = jnp.float32
BF16 = jnp.bfloat16

D_MODEL = 1024
HGRN_HEADS = 8
HGRN_DK = 128
MLA_HEADS = 8
MLA_Q_LORA = 384
MLA_KV_LORA = 256
MLA_NOPE = 128
MLA_ROPE = 64
MLA_V = 128
ROPE_THETA = 10000.0
MEM_HEADS = 4
MEM_HEAD_DIM = D_MODEL // MEM_HEADS
N_EXPERTS = 32
TOP_K = 4
D_FF = D_MODEL
SWIGLU_LIMIT = 7.0
SWIGLU_ALPHA = 1.702
DN_ALPHA = 2.0 ** 0.25
LN_EPS = 1e-5
RMS_EPS = 1e-6

LANES = 128
SUBLANES = 8
QK_PAD = 256
V_PAD = 256
LOG2E = 1.4426950408889634
HGRN_CHUNK = 64
HGRN_SAFE_RANGE = 160.0
EXPERT_BLOCK = 512
VMEM_LIMIT = 56 * 1024 * 1024

COL_GA, COL_GB, COL_GM, COL_Q, COL_ZF, COL_ZB, COL_I, COL_G, COL_MQ = range(9)
COL_DQ, COL_DKV = 18, 19
IN_COLS_PAD = 10240

NT_DIMS = (((1,), (1,)), ((), ()))
TN_DIMS = (((0,), (0,)), ((), ()))


def _params(sem, vmem=VMEM_LIMIT):
    return pltpu.CompilerParams(dimension_semantics=sem, vmem_limit_bytes=vmem)


def _mm_kernel(x_ref, w_ref, o_ref, xb_ref):
    @pl.when(pl.program_id(1) == 0)
    def _():
        xb_ref[...] = x_ref[...].astype(BF16)

    o_ref[...] = jnp.dot(xb_ref[...], w_ref[...], preferred_element_type=F32).astype(o_ref.dtype)


def _matmul(x, w, out_dtype, tm, tn, name):
    m, k = x.shape
    n = w.shape[1]
    return pl.pallas_call(
        _mm_kernel,
        out_shape=jax.ShapeDtypeStruct((m, n), out_dtype),
        grid=(m // tm, n // tn),
        in_specs=[pl.BlockSpec((tm, k), lambda i, j: (i, 0)),
                  pl.BlockSpec((k, tn), lambda i, j: (0, j))],
        out_specs=pl.BlockSpec((tm, tn), lambda i, j: (i, j)),
        scratch_shapes=[pltpu.VMEM((tm, k), BF16)],
        compiler_params=_params(("arbitrary", "arbitrary")),
        name=name,
    )(x, w)


def _hgrn_kernel(*refs, reverse, epilogue, sc):
    if epilogue:
        lb_ref, q_ref, z_ref, v_ref, of_ref, g_ref, nw_ref, o_ref, st_ref, kk_s, b_s, vf_s, os_s = refs
    else:
        lb_ref, q_ref, z_ref, v_ref, o_ref, st_ref, kk_s, b_s, vf_s = refs
    C = HGRN_CHUNK
    nch = sc // C

    @pl.when(pl.program_id(1) == 0)
    def _():
        st_ref[...] = jnp.zeros_like(st_ref)

    lb = lb_ref[...]
    row = lax.broadcasted_iota(jnp.int32, (C, C), 0)
    col = lax.broadcasted_iota(jnp.int32, (C, C), 1)
    tri = (row <= col) if reverse else (row >= col)
    trib = jnp.where(tri, 1.0, 0.0).astype(BF16)

    minb = None
    for c in range(nch):
        rows = slice(c * C, (c + 1) * C)
        z = z_ref[rows, :].astype(F32)
        f = lb + (1.0 - lb) * jax.nn.sigmoid(z)
        lf = jnp.log(f)
        kk_s[rows, :] = (1.0 - lb) * jax.nn.sigmoid(-z)
        hi = lf.astype(BF16)
        r1 = lf - hi.astype(F32)
        mid = r1.astype(BF16)
        lo = (r1 - mid.astype(F32)).astype(BF16)
        b = (jnp.dot(trib, hi, preferred_element_type=F32)
             + jnp.dot(trib, mid, preferred_element_type=F32)
             + jnp.dot(trib, lo, preferred_element_type=F32))
        b_s[rows, :] = b
        mb = jnp.min(b)
        minb = mb if minb is None else jnp.minimum(minb, mb)

    rid = lax.broadcasted_iota(jnp.int32, (C, 1), 0)

    def chunk(i, carry, fast):
        c = (nch - 1 - i) if reverse else i
        r0 = pl.multiple_of(c * C, C)
        rows = pl.ds(r0, C)
        for h in range(HGRN_HEADS):
            cols = slice(h * HGRN_DK, (h + 1) * HGRN_DK)
            q = q_ref[rows, cols].astype(F32)
            kk = kk_s[rows, cols]
            b = b_s[rows, cols]
            v = v_ref[rows, cols]
            bl = b[0:1, :] if reverse else b[C - 1:C, :]
            if fast:
                bm = 0.5 * bl
                qd = (q * jnp.exp(b - bm)).astype(BF16)
                kd = (kk * jnp.exp(bm - b)).astype(BF16)
                s = lax.dot_general(qd, kd, NT_DIMS, preferred_element_type=F32)
                s = jnp.where(tri, s, 0.0).astype(BF16)
                o = jnp.dot(s, v, preferred_element_type=F32)
            else:
                def sbody(g_i, o_acc):
                    grp = pl.ds(pl.multiple_of(r0 + g_i * SUBLANES, SUBLANES), SUBLANES)
                    b8 = b_s[grp, cols]
                    k8 = kk_s[grp, cols]
                    v8 = vf_s[grp, cols]
                    for jj in range(SUBLANES):
                        s_i = g_i * SUBLANES + jj
                        w = q * k8[jj:jj + 1, :] * jnp.exp(jnp.minimum(b - b8[jj:jj + 1, :], 0.0))
                        scol = jnp.sum(w, axis=-1, keepdims=True)
                        keep = (rid <= s_i) if reverse else (rid >= s_i)
                        o_acc = o_acc + jnp.where(keep, scol, 0.0) * v8[jj:jj + 1, :]
                    return o_acc
                o = lax.fori_loop(0, C // SUBLANES, sbody, jnp.zeros((C, HGRN_DK), F32))
            st = st_ref[h]
            qi = (q * jnp.exp(b)).astype(BF16)
            o = o + lax.dot_general(qi, st.astype(BF16), NT_DIMS, preferred_element_type=F32)
            ke = (kk * jnp.exp(bl - b)).astype(BF16)
            upd = lax.dot_general(v, ke, TN_DIMS, preferred_element_type=F32)
            st_ref[h] = st * jnp.exp(bl) + upd
            if epilogue:
                os_s[rows, cols] = o + of_ref[rows, cols].astype(F32)
            else:
                o_ref[rows, cols] = o.astype(o_ref.dtype)
        return carry

    safe = minb >= -HGRN_SAFE_RANGE

    @pl.when(safe)
    def _():
        lax.fori_loop(0, nch, functools.partial(chunk, fast=True), 0)

    @pl.when(jnp.logical_not(safe))
    def _():
        vf_s[...] = v_ref[...].astype(F32)
        lax.fori_loop(0, nch, functools.partial(chunk, fast=False), 0)

    if epilogue:
        nw = nw_ref[...]
        for h in range(HGRN_HEADS):
            cols = slice(h * HGRN_DK, (h + 1) * HGRN_DK)
            os = os_s[:, cols]
            ms = jnp.mean(os * os, axis=-1, keepdims=True)
            y = os * lax.rsqrt(ms + RMS_EPS) * nw
            g = g_ref[:, cols].astype(F32)
            o_ref[:, cols] = (y * (g * jax.nn.sigmoid(g))).astype(o_ref.dtype)


def _hgrn(proj, lb, norm_w, batch, seq, sc):
    t = batch * seq
    ns = seq // sc
    blk = (sc, D_MODEL)

    def spec(colblk, reverse):
        if reverse:
            return pl.BlockSpec(blk, lambda b, n: (b * ns + ns - 1 - n, colblk))
        return pl.BlockSpec(blk, lambda b, n: (b * ns + n, colblk))

    def row_spec(reverse):
        if reverse:
            return pl.BlockSpec(blk, lambda b, n: (b * ns + ns - 1 - n, 0))
        return pl.BlockSpec(blk, lambda b, n: (b * ns + n, 0))

    lb_spec = pl.BlockSpec((1, D_MODEL), lambda b, n: (0, 0))
    common_scratch = [pltpu.VMEM((HGRN_HEADS, HGRN_DK, HGRN_DK), F32),
                      pltpu.VMEM(blk, F32), pltpu.VMEM(blk, F32), pltpu.VMEM(blk, F32)]
    o_f = pl.pallas_call(
        functools.partial(_hgrn_kernel, reverse=False, epilogue=False, sc=sc),
        out_shape=jax.ShapeDtypeStruct((t, D_MODEL), BF16),
        grid=(batch, ns),
        in_specs=[lb_spec, spec(COL_Q, False), spec(COL_ZF, False), spec(COL_I, False)],
        out_specs=row_spec(False),
        scratch_shapes=common_scratch,
        compiler_params=_params(("arbitrary", "arbitrary")),
        name="hgrn_fwd",
    )(lb[0:1], proj, proj, proj)
    o_a = pl.pallas_call(
        functools.partial(_hgrn_kernel, reverse=True, epilogue=True, sc=sc),
        out_shape=jax.ShapeDtypeStruct((t, D_MODEL), BF16),
        grid=(batch, ns),
        in_specs=[lb_spec, spec(COL_Q, True), spec(COL_ZB, True), spec(COL_I, True),
                  row_spec(True), spec(COL_G, True),
                  pl.BlockSpec((1, HGRN_DK), lambda b, n: (0, 0))],
        out_specs=row_spec(True),
        scratch_shapes=common_scratch + [pltpu.VMEM(blk, F32)],
        compiler_params=_params(("arbitrary", "arbitrary")),
        name="hgrn_bwd",
    )(lb[1:2], proj, proj, proj, o_f, proj, norm_w.reshape(1, HGRN_DK))
    return o_a


def _mla_prep_kernel(dq_ref, dkv_ref, qnw_ref, kvnw_ref, wq_ref, wk_ref, wv_ref, cos_ref, sin_ref,
                     q_ref, k_ref, v_ref):
    scale = (MLA_NOPE + MLA_ROPE) ** -0.5 * LOG2E
    cos = cos_ref[...]
    sin = sin_ref[...]
    dq = dq_ref[...].astype(F32)
    ms = jnp.sum(dq * dq, axis=-1, keepdims=True) * (1.0 / MLA_Q_LORA)
    cq = (dq * lax.rsqrt(ms + RMS_EPS) * qnw_ref[...]).astype(BF16)
    qa = jnp.dot(cq, wq_ref[...], preferred_element_type=F32)
    for h in range(MLA_HEADS):
        base = h * QK_PAD
        q_ref[:, base:base + MLA_NOPE] = (qa[:, base:base + MLA_NOPE] * scale).astype(BF16)
        rp = qa[:, base + MLA_NOPE:base + QK_PAD]
        sw = qa[:, MLA_HEADS * QK_PAD + h * LANES:MLA_HEADS * QK_PAD + (h + 1) * LANES]
        q_ref[:, base + MLA_NOPE:base + QK_PAD] = ((rp * cos + sw * sin) * scale).astype(BF16)
    dkv = dkv_ref[...].astype(F32)
    ckv = dkv[:, :MLA_KV_LORA]
    msk = jnp.mean(ckv * ckv, axis=-1, keepdims=True)
    cn = (ckv * lax.rsqrt(msk + RMS_EPS) * kvnw_ref[...]).astype(BF16)
    kn = jnp.dot(cn, wk_ref[...], preferred_element_type=F32)
    vv = jnp.dot(cn, wv_ref[...], preferred_element_type=F32).astype(BF16)
    ones_col = jnp.where(lax.broadcasted_iota(jnp.int32, (vv.shape[0], LANES), 1) == 0, 1.0, 0.0).astype(BF16)
    for h in range(MLA_HEADS):
        v_ref[:, h * V_PAD:h * V_PAD + MLA_V] = vv[:, h * MLA_V:(h + 1) * MLA_V]
        v_ref[:, h * V_PAD + MLA_V:(h + 1) * V_PAD] = ones_col
    kr = (dkv[:, MLA_KV_LORA:MLA_KV_LORA + LANES] * cos
          + dkv[:, MLA_KV_LORA + LANES:MLA_KV_LORA + 2 * LANES] * sin).astype(BF16)
    for h in range(MLA_HEADS):
        base = h * QK_PAD
        k_ref[:, base:base + MLA_NOPE] = kn[:, h * MLA_NOPE:(h + 1) * MLA_NOPE].astype(BF16)
        k_ref[:, base + MLA_NOPE:base + QK_PAD] = kr


def _mla_prep(proj, qnw, kvnw, wq, wk, wv, cos, sin, batch, seq, tm):
    t = batch * seq
    npos = seq // tm
    full = lambda shape: pl.BlockSpec(shape, lambda i: (0, 0))
    return pl.pallas_call(
        _mla_prep_kernel,
        out_shape=(jax.ShapeDtypeStruct((t, MLA_HEADS * QK_PAD), BF16),
                   jax.ShapeDtypeStruct((t, MLA_HEADS * QK_PAD), BF16),
                   jax.ShapeDtypeStruct((t, MLA_HEADS * V_PAD), BF16)),
        grid=(t // tm,),
        in_specs=[pl.BlockSpec((tm, 512), lambda i: (i, COL_DQ)),
                  pl.BlockSpec((tm, 512), lambda i: (i, COL_DKV)),
                  full((1, 512)), full((1, MLA_KV_LORA)),
                  full(wq.shape), full(wk.shape), full(wv.shape),
                  pl.BlockSpec((tm, LANES), lambda i: (i % npos, 0)),
                  pl.BlockSpec((tm, LANES), lambda i: (i % npos, 0))],
        out_specs=(pl.BlockSpec((tm, MLA_HEADS * QK_PAD), lambda i: (i, 0)),
                   pl.BlockSpec((tm, MLA_HEADS * QK_PAD), lambda i: (i, 0)),
                   pl.BlockSpec((tm, MLA_HEADS * V_PAD), lambda i: (i, 0))),
        compiler_params=_params(("arbitrary",)),
        name="mla_prep",
    )(proj, proj, qnw, kvnw, wq, wk, wv, cos, sin)


def _flash_kernel(q_ref, k_ref, v_ref, o_ref, m_s, acc_s, *, bk, nk):
    m_s[...] = jnp.full(m_s.shape, -jnp.inf, F32)
    acc_s[...] = jnp.zeros(acc_s.shape, F32)
    nt = bk // LANES
    bq = q_ref.shape[0]
    nsplit = 2 if bq % 256 == 0 else 1
    hq = bq // nsplit

    def body(j, carry):
        rows = pl.ds(pl.multiple_of(j * bk, bk), bk)
        for hf in range(nsplit):
            qr = slice(hf * hq, (hf + 1) * hq)
            s = lax.dot_general(q_ref[qr, :], k_ref[rows, :], NT_DIMS, preferred_element_type=F32)
            tiles = [s[:, t * LANES:(t + 1) * LANES] for t in range(nt)]
            tmax = tiles[0]
            for t in range(1, nt):
                tmax = jnp.maximum(tmax, tiles[t])
            m_prev = m_s[qr, :]
            m_new = jnp.maximum(m_prev, jnp.max(tmax, axis=-1, keepdims=True))
            a = jnp.exp2(m_prev - m_new)
            p = jnp.concatenate([jnp.exp2(tl - m_new).astype(BF16) for tl in tiles], axis=1)
            pv = jnp.dot(p, v_ref[rows, :], preferred_element_type=F32)
            acc_s[qr, :] = jnp.concatenate([a, a], axis=1) * acc_s[qr, :] + pv
            m_s[qr, :] = m_new
        return carry

    lax.fori_loop(0, nk, body, 0)
    acc = acc_s[...]
    o_ref[...] = (acc[:, :MLA_V] / acc[:, MLA_V:MLA_V + 1]).astype(o_ref.dtype)


def _flash(q, k, v, batch, seq, bq, bk):
    t = batch * seq
    nq = seq // bq
    return pl.pallas_call(
        functools.partial(_flash_kernel, bk=bk, nk=seq // bk),
        out_shape=jax.ShapeDtypeStruct((t, MLA_HEADS * MLA_V), BF16),
        grid=(batch, MLA_HEADS, nq),
        in_specs=[pl.BlockSpec((bq, QK_PAD), lambda b, h, i: (b * nq + i, h)),
                  pl.BlockSpec((seq, QK_PAD), lambda b, h, i: (b, h)),
                  pl.BlockSpec((seq, V_PAD), lambda b, h, i: (b, h))],
        out_specs=pl.BlockSpec((bq, MLA_V), lambda b, h, i: (b * nq + i, h)),
        scratch_shapes=[pltpu.VMEM((bq, LANES), F32), pltpu.VMEM((bq, V_PAD), F32)],
        compiler_params=_params(("arbitrary", "arbitrary", "arbitrary")),
        name="mla_flash",
    )(q, k, v)


def _layernorm(y, w, b):
    mu = jnp.mean(y, axis=-1, keepdims=True)
    yc = y - mu
    var = jnp.mean(yc * yc, axis=-1, keepdims=True)
    return yc * lax.rsqrt(var + LN_EPS) * w + b


def _merge_kernel(x_ref, ga_ref, gb_ref, gm_ref, mq_ref, oa_ref, ob_ref, kvm_ref, wout_ref,
                  l1w_ref, l1b_ref, rw_ref, rb_ref,
                  x1_ref, ri_ref, rg_ref, cnt_ref, carry_s, *, tm):
    @pl.when(pl.program_id(0) == 0)
    def _():
        carry_s[...] = jnp.zeros_like(carry_s)

    parts = []
    for h in range(MEM_HEADS):
        cols = slice(h * MEM_HEAD_DIM, (h + 1) * MEM_HEAD_DIM)
        kh = kvm_ref[:, cols]
        vh = kvm_ref[:, D_MODEL + h * MEM_HEAD_DIM:D_MODEL + (h + 1) * MEM_HEAD_DIM]
        s = lax.dot_general(mq_ref[:, cols], kh, NT_DIMS, preferred_element_type=F32) * (MEM_HEAD_DIM ** -0.5)
        s = s - jnp.max(s, axis=-1, keepdims=True)
        p = jnp.exp(s)
        p = p / jnp.sum(p, axis=-1, keepdims=True)
        parts.append(jnp.dot(p.astype(BF16), vh, preferred_element_type=F32))
    om = jnp.concatenate(parts, axis=1)

    merged = (jax.nn.sigmoid(ga_ref[...].astype(F32)) * oa_ref[...].astype(F32)
              + jax.nn.sigmoid(gb_ref[...].astype(F32)) * ob_ref[...].astype(F32)
              + jax.nn.sigmoid(gm_ref[...].astype(F32)) * om)
    y = DN_ALPHA * x_ref[...] + jnp.dot(merged.astype(BF16), wout_ref[...], preferred_element_type=F32)
    x1 = _layernorm(y, l1w_ref[...], l1b_ref[...])
    x1_ref[...] = x1

    logits = jnp.dot(x1, rw_ref[...], preferred_element_type=F32, precision=lax.Precision.HIGHEST) + rb_ref[...]
    lane_i = lax.broadcasted_iota(jnp.int32, (tm, LANES), 1)
    lane = lane_i.astype(F32)
    work = logits
    idx, val = [], []
    for _ in range(TOP_K):
        mx = jnp.max(work, axis=-1, keepdims=True)
        ix = jnp.min(jnp.where(work == mx, lane, float(LANES)), axis=-1, keepdims=True)
        idx.append(ix)
        val.append(mx)
        work = jnp.where(lane == ix, -jnp.inf, work)
    ex = [jnp.exp(v - val[0]) for v in val]
    tot = ex[0] + ex[1] + ex[2] + ex[3]
    hot = [jnp.where(lane == ix, 1.0, 0.0) for ix in idx]
    multi = hot[0] + hot[1] + hot[2] + hot[3]
    r = lax.broadcasted_iota(jnp.int32, (tm, tm), 0)
    c = lax.broadcasted_iota(jnp.int32, (tm, tm), 1)
    lower = jnp.where(r > c, 1.0, 0.0).astype(BF16)
    before = jnp.dot(lower, multi.astype(BF16), preferred_element_type=F32) + carry_s[0:1, :]
    ri = jnp.zeros((tm, LANES), F32)
    rg = jnp.zeros((tm, LANES), F32)
    for j in range(TOP_K):
        rank = jnp.sum(before * hot[j], axis=-1, keepdims=True)
        ri = ri + jnp.where(lane == float(j), idx[j], 0.0) + jnp.where(lane == float(TOP_K + j), rank, 0.0)
        rg = rg + jnp.where(lane == float(j), ex[j] / tot, 0.0)
    ri_ref[...] = ri.astype(jnp.int32)
    rg_ref[...] = rg
    carry_s[...] = carry_s[...] + jnp.sum(multi, axis=0, keepdims=True)
    cnt_ref[...] = carry_s[...]


def _merge(x2d, proj, o_a, o_b, kvm, wout, l1w, l1b, rw, rb, batch, seq, tm):
    t = batch * seq
    per_b = seq // tm
    nmem = kvm.shape[0] // batch
    tile = lambda colblk: pl.BlockSpec((tm, D_MODEL), lambda i: (i, colblk))
    full = lambda shape: pl.BlockSpec(shape, lambda i: (0, 0))
    return pl.pallas_call(
        functools.partial(_merge_kernel, tm=tm),
        out_shape=(jax.ShapeDtypeStruct((t, D_MODEL), F32),
                   jax.ShapeDtypeStruct((t, LANES), jnp.int32),
                   jax.ShapeDtypeStruct((t, LANES), F32),
                   jax.ShapeDtypeStruct((8, LANES), F32)),
        grid=(t // tm,),
        in_specs=[tile(0), tile(COL_GA), tile(COL_GB), tile(COL_GM), tile(COL_MQ), tile(0), tile(0),
                  pl.BlockSpec((nmem, 2 * D_MODEL), lambda i: (i // per_b, 0)),
                  full((D_MODEL, D_MODEL)), full((1, D_MODEL)), full((1, D_MODEL)),
                  full((D_MODEL, LANES)), full((1, LANES))],
        out_specs=(tile(0), pl.BlockSpec((tm, LANES), lambda i: (i, 0)),
                   pl.BlockSpec((tm, LANES), lambda i: (i, 0)), full((8, LANES))),
        scratch_shapes=[pltpu.VMEM((8, LANES), F32)],
        compiler_params=_params(("arbitrary",)),
        name="merge_router",
    )(x2d, proj, proj, proj, proj, o_a, o_b, kvm, wout, l1w, l1b, rw, rb)


def _dispatch_kernel(dest_hbm, x_ref, xs_in, xs_hbm, idx_s, sem_i, sem_r, *, tm):
    del xs_in
    i = pl.program_id(0)
    n = TOP_K * tm
    cp = pltpu.make_async_copy(dest_hbm.at[pl.ds(pl.multiple_of(i * n, n), n)], idx_s, sem_i)
    cp.start()
    cp.wait()

    def body(r, carry):
        src = x_ref.at[pl.ds(r, 1)]
        for j in range(TOP_K):
            d = idx_s[r * TOP_K + j]
            pltpu.make_async_copy(src, xs_hbm.at[pl.ds(d, 1)], sem_r).start()
        return carry

    lax.fori_loop(0, tm, body, 0)
    for _ in range(TOP_K):
        pltpu.make_async_copy(x_ref, xs_hbm.at[pl.ds(0, tm)], sem_r).wait()


def _dispatch(dest, x1, slots, tm):
    t = x1.shape[0]
    zeros = jnp.zeros((slots, D_MODEL), F32)
    return pl.pallas_call(
        functools.partial(_dispatch_kernel, tm=tm),
        out_shape=jax.ShapeDtypeStruct((slots, D_MODEL), F32),
        grid=(t // tm,),
        in_specs=[pl.BlockSpec(memory_space=pl.ANY),
                  pl.BlockSpec((tm, D_MODEL), lambda i: (i, 0)),
                  pl.BlockSpec(memory_space=pl.ANY)],
        out_specs=pl.BlockSpec(memory_space=pl.ANY),
        scratch_shapes=[pltpu.SMEM((TOP_K * tm,), jnp.int32),
                        pltpu.SemaphoreType.DMA, pltpu.SemaphoreType.DMA],
        input_output_aliases={2: 0},
        compiler_params=_params(("arbitrary",)),
        name="moe_dispatch",
    )(dest, x1, zeros)


def _expert_kernel(be_ref, nu_ref, xs_ref, wgu_ref, bgu_ref, wd_ref, bd_ref, ys_ref):
    del be_ref
    i = pl.program_id(0)

    @pl.when(i < nu_ref[0])
    def _():
        h = jnp.dot(xs_ref[...].astype(BF16), wgu_ref[0], preferred_element_type=F32) + bgu_ref[0]
        g = jnp.minimum(h[:, :D_FF], SWIGLU_LIMIT)
        u = jnp.clip(h[:, D_FF:], -SWIGLU_LIMIT, SWIGLU_LIMIT)
        act = (u + 1.0) * g * jax.nn.sigmoid(SWIGLU_ALPHA * g)
        ys_ref[...] = jnp.dot(act.astype(BF16), wd_ref[0], preferred_element_type=F32) + bd_ref[0]

    @pl.when(i >= nu_ref[0])
    def _():
        ys_ref[...] = jnp.zeros_like(ys_ref)


def _experts(block_e, n_used, xs, wgu, bgu, wd, bd):
    slots = xs.shape[0]
    nb = slots // EXPERT_BLOCK
    grid_spec = pltpu.PrefetchScalarGridSpec(
        num_scalar_prefetch=2,
        grid=(nb,),
        in_specs=[pl.BlockSpec((EXPERT_BLOCK, D_MODEL), lambda i, be, nu: (i, 0)),
                  pl.BlockSpec((1, D_MODEL, 2 * D_FF), lambda i, be, nu: (be[i], 0, 0)),
                  pl.BlockSpec((1, 1, 2 * D_FF), lambda i, be, nu: (be[i], 0, 0)),
                  pl.BlockSpec((1, D_FF, D_MODEL), lambda i, be, nu: (be[i], 0, 0)),
                  pl.BlockSpec((1, 1, D_MODEL), lambda i, be, nu: (be[i], 0, 0))],
        out_specs=pl.BlockSpec((EXPERT_BLOCK, D_MODEL), lambda i, be, nu: (i, 0)),
    )
    return pl.pallas_call(
        _expert_kernel,
        out_shape=jax.ShapeDtypeStruct((slots, D_MODEL), F32),
        grid_spec=grid_spec,
        compiler_params=_params(("arbitrary",)),
        name="moe_experts",
    )(block_e, n_used, xs, wgu, bgu, wd, bd)


def _combine_kernel(dest_hbm, x1_ref, rg_ref, ys_hbm, l2w_ref, l2b_ref, o_ref, idx_s, buf, sem_i, sem_r, *, tm):
    i = pl.program_id(0)
    nsteps = pl.num_programs(0)
    n = TOP_K * tm

    def issue(step, slot):
        cp = pltpu.make_async_copy(dest_hbm.at[pl.ds(pl.multiple_of(step * n, n), n)], idx_s, sem_i)
        cp.start()
        cp.wait()

        def body(r, carry):
            for j in range(TOP_K):
                d = idx_s[r * TOP_K + j]
                pltpu.make_async_copy(ys_hbm.at[pl.ds(d, 1)], buf.at[slot, pl.ds(j * tm + r, 1)],
                                      sem_r.at[slot]).start()
            return carry

        lax.fori_loop(0, tm, body, 0)

    @pl.when(i == 0)
    def _():
        issue(0, 0)

    @pl.when(i + 1 < nsteps)
    def _():
        issue(i + 1, (i + 1) % 2)

    slot = i % 2
    pltpu.make_async_copy(ys_hbm.at[pl.ds(0, n)], buf.at[slot], sem_r.at[slot]).wait()
    rg = rg_ref[...]
    moe = rg[:, 0:1] * buf[slot, pl.ds(0, tm), :]
    for j in range(1, TOP_K):
        moe = moe + rg[:, j:j + 1] * buf[slot, pl.ds(j * tm, tm), :]
    o_ref[...] = _layernorm(DN_ALPHA * x1_ref[...] + moe, l2w_ref[...], l2b_ref[...])


def _combine(dest, x1, rg, ys, l2w, l2b, tm):
    t = x1.shape[0]
    full = lambda shape: pl.BlockSpec(shape, lambda i: (0, 0))
    return pl.pallas_call(
        functools.partial(_combine_kernel, tm=tm),
        out_shape=jax.ShapeDtypeStruct((t, D_MODEL), F32),
        grid=(t // tm,),
        in_specs=[pl.BlockSpec(memory_space=pl.ANY),
                  pl.BlockSpec((tm, D_MODEL), lambda i: (i, 0)),
                  pl.BlockSpec((tm, LANES), lambda i: (i, 0)),
                  pl.BlockSpec(memory_space=pl.ANY),
                  full((1, D_MODEL)), full((1, D_MODEL))],
        out_specs=pl.BlockSpec((tm, D_MODEL), lambda i: (i, 0)),
        scratch_shapes=[pltpu.SMEM((TOP_K * tm,), jnp.int32),
                        pltpu.VMEM((2, TOP_K * tm, D_MODEL), F32),
                        pltpu.SemaphoreType.DMA, pltpu.SemaphoreType.DMA((2,))],
        compiler_params=_params(("arbitrary",)),
        name="moe_combine",
    )(dest, x1, rg, ys, l2w, l2b)


def _prep_w_in(w):
    zeros = lambda n: jnp.zeros((D_MODEL, n), w.dtype)
    kr = w[:, 5760:5824]
    kr_sw = jnp.concatenate([kr[:, MLA_ROPE // 2:], kr[:, :MLA_ROPE // 2]], axis=1)
    parts = [w[:, 6848:9920], w[:, 0:5120], w[:, 5824:6848],
             w[:, 5120:5504], zeros(128),
             w[:, 5504:5760], kr, zeros(64), kr_sw, zeros(64)]
    return jnp.concatenate(parts, axis=1).astype(BF16)


def _prep_w_uq(w):
    w3 = w.reshape(MLA_Q_LORA, MLA_HEADS, MLA_NOPE + MLA_ROPE)
    rope = w3[:, :, MLA_NOPE:]
    rope_sw = jnp.concatenate([rope[:, :, MLA_ROPE // 2:], rope[:, :, :MLA_ROPE // 2]], axis=-1)
    pad = jnp.zeros((MLA_Q_LORA, MLA_HEADS, QK_PAD - MLA_NOPE - MLA_ROPE), w.dtype)
    main = jnp.concatenate([w3, pad], axis=-1).reshape(MLA_Q_LORA, MLA_HEADS * QK_PAD)
    swp = jnp.concatenate([rope_sw, pad], axis=-1).reshape(MLA_Q_LORA, MLA_HEADS * LANES)
    both = jnp.concatenate([main, swp], axis=1)
    return jnp.pad(both, ((0, 512 - MLA_Q_LORA), (0, 0))).astype(BF16)


def _rope_tables(seq):
    inv_freq = ROPE_THETA ** (-jnp.arange(0, MLA_ROPE, 2, dtype=F32) / MLA_ROPE)
    ang = jnp.arange(seq, dtype=F32)[:, None] * inv_freq[None, :]
    cos, sin = jnp.cos(ang), jnp.sin(ang)
    pad = jnp.zeros((seq, LANES - MLA_ROPE), F32)
    return (jnp.concatenate([cos, cos, pad], axis=1), jnp.concatenate([-sin, sin, pad], axis=1))


def _tile(n, pref):
    return pref if n % pref == 0 else n


def _layer(x, mem, wts):
    batch, seq, _ = x.shape
    t = batch * seq
    x2d = x.reshape(t, D_MODEL)

    proj = _matmul(x2d, wts["w_in"], BF16, _tile(t, 1024), 1024, "in_proj")
    o_a = _hgrn(proj, wts["lb"], wts["hgrn_norm_w"], batch, seq, _tile(seq, 512))

    cos, sin = _rope_tables(seq)
    q, k, v = _mla_prep(proj, wts["qnw"], wts["kvnw"], wts["w_uq"], wts["w_uk"], wts["w_uv"],
                        cos, sin, batch, seq, _tile(seq, 512))
    o_b = _flash(q, k, v, batch, seq, _tile(seq, 512), _tile(seq, 1024))

    nmem = mem.shape[1]
    kvm = _matmul(mem.reshape(batch * nmem, D_MODEL), wts["mem_w_kv"], BF16,
                  _tile(batch * nmem, 512), 1024, "mem_kv")

    tm = _tile(seq, 512)
    x1, ri, rg, cnt = _merge(x2d, proj, o_a, o_b, kvm, wts["w_out"], wts["ln1_w"], wts["ln1_b"],
                             wts["router_w"], wts["router_b"], batch, seq, tm)

    idx = ri[:, :TOP_K]
    rank = ri[:, TOP_K:2 * TOP_K]
    counts = cnt[0, :N_EXPERTS].astype(jnp.int32)
    padded = (counts + EXPERT_BLOCK - 1) // EXPERT_BLOCK * EXPERT_BLOCK
    pad_end = jnp.cumsum(padded)
    pad_start = pad_end - padded
    dest = (pad_start[idx] + rank).reshape(t * TOP_K).astype(jnp.int32)
    nb = t * TOP_K // EXPERT_BLOCK + N_EXPERTS
    block_e = jnp.minimum(
        jnp.searchsorted(pad_end, jnp.arange(nb, dtype=jnp.int32) * EXPERT_BLOCK, side="right"),
        N_EXPERTS - 1).astype(jnp.int32)
    n_used = (pad_end[-1:] // EXPERT_BLOCK).astype(jnp.int32)

    tmd = _tile(t, 512)
    xs = _dispatch(dest, x1, nb * EXPERT_BLOCK, tmd)
    ys = _experts(block_e, n_used, xs, wts["exp_w_gu"], wts["exp_b_gu"], wts["exp_w_down"], wts["exp_b_down"])
    y = _combine(dest, x1, rg, ys, wts["ln2_w"], wts["ln2_b"], _tile(t, 256))
    return y.reshape(batch, seq, D_MODEL)


def kernel(x_prompt, x_sample, mem_prompt, mem_sample, w_in, hgrn_lb_logits, hgrn_norm_w,
           mla_q_norm_w, mla_w_uq, mla_kv_norm_w, mla_w_uk, mla_w_uv, mem_w_kv, w_out,
           ln1_w, ln1_b, router_w, router_b, exp_w_gu, exp_b_gu, exp_w_down, exp_b_down,
           ln2_w, ln2_b):
    depth = w_in.shape[0]
    gamma = jax.nn.softmax(hgrn_lb_logits.astype(F32), axis=1)
    cum = jnp.cumsum(gamma, axis=1)
    lb_all = cum[:, 1:] - cum[:, :1]
    y_prompt, y_sample = x_prompt, x_sample
    for l in range(depth):
        row = lambda a: a[l].reshape(1, -1).astype(F32)
        wts = {
            "w_in": _prep_w_in(w_in[l]),
            "lb": lb_all[:, l],
            "hgrn_norm_w": hgrn_norm_w[l].astype(F32),
            "qnw": jnp.pad(row(mla_q_norm_w), ((0, 0), (0, 512 - MLA_Q_LORA))),
            "kvnw": row(mla_kv_norm_w),
            "w_uq": _prep_w_uq(mla_w_uq[l]),
            "w_uk": mla_w_uk[l].astype(BF16),
            "w_uv": mla_w_uv[l].astype(BF16),
            "mem_w_kv": mem_w_kv[l].astype(BF16),
            "w_out": w_out[l].astype(BF16),
            "ln1_w": row(ln1_w), "ln1_b": row(ln1_b),
            "router_w": jnp.pad(router_w[l].astype(F32), ((0, 0), (0, LANES - N_EXPERTS))),
            "router_b": jnp.pad(row(router_b), ((0, 0), (0, LANES - N_EXPERTS)), constant_values=-jnp.inf),
            "exp_w_gu": exp_w_gu[l].astype(BF16),
            "exp_b_gu": exp_b_gu[l].reshape(N_EXPERTS, 1, 2 * D_FF).astype(F32),
            "exp_w_down": exp_w_down[l].astype(BF16),
            "exp_b_down": exp_b_down[l].reshape(N_EXPERTS, 1, D_MODEL).astype(F32),
            "ln2_w": row(ln2_w), "ln2_b": row(ln2_b),
        }
        y_prompt = _layer(y_prompt, mem_prompt, wts)
        y_sample = _layer(y_sample, mem_sample, wts)
    return (y_prompt, y_sample)
```

```python
import functools

import numpy as np
import jax
import jax.numpy as jnp
from jax import lax
from jax.experimental import pallas as pl
from jax.experimental.pallas import tpu as pltpu

F32 = jnp.float32
BF16 = jnp.bfloat16

D_MODEL = 1024
HGRN_HEADS = 8
HGRN_DK = 128
MLA_HEADS = 8
MLA_Q_LORA = 384
MLA_KV_LORA = 256
MLA_NOPE = 128
MLA_ROPE = 64
MLA_V = 128
ROPE_THETA = 10000.0
MEM_HEADS = 4
MEM_HEAD_DIM = D_MODEL // MEM_HEADS
N_EXPERTS = 32
TOP_K = 4
D_FF = D_MODEL
SWIGLU_LIMIT = 7.0
SWIGLU_ALPHA = 1.702
DN_ALPHA = 2.0 ** 0.25
LN_EPS = 1e-5
RMS_EPS = 1e-6

LANES = 128
SUBLANES = 8
QK_PAD = 256
V_PAD = 256
LOG2E = 1.4426950408889634
HGRN_CHUNK = 64
HGRN_SAFE_RANGE = 160.0
EXPERT_BLOCK = 512
VMEM_LIMIT = 56 * 1024 * 1024

COL_GA, COL_GB, COL_GM, COL_Q, COL_ZF, COL_ZB, COL_I, COL_G, COL_MQ = range(9)
COL_DQ, COL_DKV = 18, 19
IN_COLS_PAD = 10240

NT_DIMS = (((1,), (1,)), ((), ()))
TN_DIMS = (((0,), (0,)), ((), ()))


def _params(sem, vmem=VMEM_LIMIT):
    return pltpu.CompilerParams(dimension_semantics=sem, vmem_limit_bytes=vmem)


def _mm_kernel(x_ref, w_ref, o_ref, xb_ref):
    @pl.when(pl.program_id(1) == 0)
    def _():
        xb_ref[...] = x_ref[...].astype(BF16)

    o_ref[...] = jnp.dot(xb_ref[...], w_ref[...], preferred_element_type=F32).astype(o_ref.dtype)


def _matmul(x, w, out_dtype, tm, tn, name):
    m, k = x.shape
    n = w.shape[1]
    return pl.pallas_call(
        _mm_kernel,
        out_shape=jax.ShapeDtypeStruct((m, n), out_dtype),
        grid=(m // tm, n // tn),
        in_specs=[pl.BlockSpec((tm, k), lambda i, j: (i, 0)),
                  pl.BlockSpec((k, tn), lambda i, j: (0, j))],
        out_specs=pl.BlockSpec((tm, tn), lambda i, j: (i, j)),
        scratch_shapes=[pltpu.VMEM((tm, k), BF16)],
        compiler_params=_params(("arbitrary", "arbitrary")),
        name=name,
    )(x, w)


def _hgrn_kernel(*refs, reverse, epilogue, sc):
    if epilogue:
        lb_ref, q_ref, z_ref, v_ref, of_ref, g_ref, nw_ref, o_ref, st_ref, kk_s, b_s, vf_s, os_s = refs
    else:
        lb_ref, q_ref, z_ref, v_ref, o_ref, st_ref, kk_s, b_s, vf_s = refs
    C = HGRN_CHUNK
    nch = sc // C

    @pl.when(pl.program_id(1) == 0)
    def _():
        st_ref[...] = jnp.zeros_like(st_ref)

    lb = lb_ref[...]
    row = lax.broadcasted_iota(jnp.int32, (C, C), 0)
    col = lax.broadcasted_iota(jnp.int32, (C, C), 1)
    tri = (row <= col) if reverse else (row >= col)
    trib = jnp.where(tri, 1.0, 0.0).astype(BF16)

    minb = None
    for c in range(nch):
        rows = slice(c * C, (c + 1) * C)
        z = z_ref[rows, :].astype(F32)
        f = lb + (1.0 - lb) * jax.nn.sigmoid(z)
        lf = jnp.log(f)
        kk_s[rows, :] = (1.0 - lb) * jax.nn.sigmoid(-z)
        hi = lf.astype(BF16)
        r1 = lf - hi.astype(F32)
        mid = r1.astype(BF16)
        lo = (r1 - mid.astype(F32)).astype(BF16)
        b = (jnp.dot(trib, hi, preferred_element_type=F32)
             + jnp.dot(trib, mid, preferred_element_type=F32)
             + jnp.dot(trib, lo, preferred_element_type=F32))
        b_s[rows, :] = b
        mb = jnp.min(b)
        minb = mb if minb is None else jnp.minimum(minb, mb)

    rid = lax.broadcasted_iota(jnp.int32, (C, 1), 0)

    def chunk(i, carry, fast):
        c = (nch - 1 - i) if reverse else i
        r0 = pl.multiple_of(c * C, C)
        rows = pl.ds(r0, C)
        for h in range(HGRN_HEADS):
            cols = slice(h * HGRN_DK, (h + 1) * HGRN_DK)
            q = q_ref[rows, cols].astype(F32)
            kk = kk_s[rows, cols]
            b = b_s[rows, cols]
            v = v_ref[rows, cols]
            bl = b[0:1, :] if reverse else b[C - 1:C, :]
            if fast:
                bm = 0.5 * bl
                qd = (q * jnp.exp(b - bm)).astype(BF16)
                kd = (kk * jnp.exp(bm - b)).astype(BF16)
                s = lax.dot_general(qd, kd, NT_DIMS, preferred_element_type=F32)
                s = jnp.where(tri, s, 0.0).astype(BF16)
                o = jnp.dot(s, v, preferred_element_type=F32)
            else:
                def sbody(g_i, o_acc):
                    grp = pl.ds(pl.multiple_of(r0 + g_i * SUBLANES, SUBLANES), SUBLANES)
                    b8 = b_s[grp, cols]
                    k8 = kk_s[grp, cols]
                    v8 = vf_s[grp, cols]
                    for jj in range(SUBLANES):
                        s_i = g_i * SUBLANES + jj
                        w = q * k8[jj:jj + 1, :] * jnp.exp(jnp.minimum(b - b8[jj:jj + 1, :], 0.0))
                        scol = jnp.sum(w, axis=-1, keepdims=True)
                        keep = (rid <= s_i) if reverse else (rid >= s_i)
                        o_acc = o_acc + jnp.where(keep, scol, 0.0) * v8[jj:jj + 1, :]
                    return o_acc
                o = lax.fori_loop(0, C // SUBLANES, sbody, jnp.zeros((C, HGRN_DK), F32))
            st = st_ref[h]
            qi = (q * jnp.exp(b)).astype(BF16)
            o = o + lax.dot_general(qi, st.astype(BF16), NT_DIMS, preferred_element_type=F32)
            ke = (kk * jnp.exp(bl - b)).astype(BF16)
            upd = lax.dot_general(v, ke, TN_DIMS, preferred_element_type=F32)
            st_ref[h] = st * jnp.exp(bl) + upd
            if epilogue:
                os_s[rows, cols] = o + of_ref[rows, cols].astype(F32)
            else:
                o_ref[rows, cols] = o.astype(o_ref.dtype)
        return carry

    safe = minb >= -HGRN_SAFE_RANGE

    @pl.when(safe)
    def _():
        lax.fori_loop(0, nch, functools.partial(chunk, fast=True), 0, unroll=2)

    @pl.when(jnp.logical_not(safe))
    def _():
        vf_s[...] = v_ref[...].astype(F32)
        lax.fori_loop(0, nch, functools.partial(chunk, fast=False), 0)

    if epilogue:
        nw = nw_ref[...]
        for h in range(HGRN_HEADS):
            cols = slice(h * HGRN_DK, (h + 1) * HGRN_DK)
            os = os_s[:, cols]
            ms = jnp.mean(os * os, axis=-1, keepdims=True)
            y = os * lax.rsqrt(ms + RMS_EPS) * nw
            g = g_ref[:, cols].astype(F32)
            o_ref[:, cols] = (y * (g * jax.nn.sigmoid(g))).astype(o_ref.dtype)


def _hgrn(proj, lb, norm_w, batch, seq, sc):
    t = batch * seq
    ns = seq // sc
    blk = (sc, D_MODEL)

    def spec(colblk, reverse):
        if reverse:
            return pl.BlockSpec(blk, lambda b, n: (b * ns + ns - 1 - n, colblk))
        return pl.BlockSpec(blk, lambda b, n: (b * ns + n, colblk))

    def row_spec(reverse):
        if reverse:
            return pl.BlockSpec(blk, lambda b, n: (b * ns + ns - 1 - n, 0))
        return pl.BlockSpec(blk, lambda b, n: (b * ns + n, 0))

    lb_spec = pl.BlockSpec((1, D_MODEL), lambda b, n: (0, 0))
    common_scratch = [pltpu.VMEM((HGRN_HEADS, HGRN_DK, HGRN_DK), F32),
                      pltpu.VMEM(blk, F32), pltpu.VMEM(blk, F32), pltpu.VMEM(blk, F32)]
    o_f = pl.pallas_call(
        functools.partial(_hgrn_kernel, reverse=False, epilogue=False, sc=sc),
        out_shape=jax.ShapeDtypeStruct((t, D_MODEL), BF16),
        grid=(batch, ns),
        in_specs=[lb_spec, spec(COL_Q, False), spec(COL_ZF, False), spec(COL_I, False)],
        out_specs=row_spec(False),
        scratch_shapes=common_scratch,
        compiler_params=_params(("arbitrary", "arbitrary")),
        name="hgrn_fwd",
    )(lb[0:1], proj, proj, proj)
    o_a = pl.pallas_call(
        functools.partial(_hgrn_kernel, reverse=True, epilogue=True, sc=sc),
        out_shape=jax.ShapeDtypeStruct((t, D_MODEL), BF16),
        grid=(batch, ns),
        in_specs=[lb_spec, spec(COL_Q, True), spec(COL_ZB, True), spec(COL_I, True),
                  row_spec(True), spec(COL_G, True),
                  pl.BlockSpec((1, HGRN_DK), lambda b, n: (0, 0))],
        out_specs=row_spec(True),
        scratch_shapes=common_scratch + [pltpu.VMEM(blk, F32)],
        compiler_params=_params(("arbitrary", "arbitrary")),
        name="hgrn_bwd",
    )(lb[1:2], proj, proj, proj, o_f, proj, norm_w.reshape(1, HGRN_DK))
    return o_a


def _mla_prep_kernel(dq_ref, dkv_ref, qnw_ref, kvnw_ref, wq_ref, wk_ref, wv_ref, cos_ref, sin_ref,
                     q_ref, k_ref, v_ref):
    scale = (MLA_NOPE + MLA_ROPE) ** -0.5 * LOG2E
    cos = cos_ref[...]
    sin = sin_ref[...]
    dq = dq_ref[...].astype(F32)
    ms = jnp.sum(dq * dq, axis=-1, keepdims=True) * (1.0 / MLA_Q_LORA)
    cq = (dq * lax.rsqrt(ms + RMS_EPS) * qnw_ref[...]).astype(BF16)
    qa = jnp.dot(cq, wq_ref[...], preferred_element_type=F32)
    for h in range(MLA_HEADS):
        base = h * QK_PAD
        q_ref[:, base:base + MLA_NOPE] = (qa[:, base:base + MLA_NOPE] * scale).astype(BF16)
        rp = qa[:, base + MLA_NOPE:base + QK_PAD]
        sw = qa[:, MLA_HEADS * QK_PAD + h * LANES:MLA_HEADS * QK_PAD + (h + 1) * LANES]
        q_ref[:, base + MLA_NOPE:base + QK_PAD] = ((rp * cos + sw * sin) * scale).astype(BF16)
    dkv = dkv_ref[...].astype(F32)
    ckv = dkv[:, :MLA_KV_LORA]
    msk = jnp.mean(ckv * ckv, axis=-1, keepdims=True)
    cn = (ckv * lax.rsqrt(msk + RMS_EPS) * kvnw_ref[...]).astype(BF16)
    kn = jnp.dot(cn, wk_ref[...], preferred_element_type=F32)
    vv = jnp.dot(cn, wv_ref[...], preferred_element_type=F32).astype(BF16)
    ones_col = jnp.where(lax.broadcasted_iota(jnp.int32, (vv.shape[0], LANES), 1) == 0, 1.0, 0.0).astype(BF16)
    for h in range(MLA_HEADS):
        v_ref[:, h * V_PAD:h * V_PAD + MLA_V] = vv[:, h * MLA_V:(h + 1) * MLA_V]
        v_ref[:, h * V_PAD + MLA_V:(h + 1) * V_PAD] = ones_col
    kr = (dkv[:, MLA_KV_LORA:MLA_KV_LORA + LANES] * cos
          + dkv[:, MLA_KV_LORA + LANES:MLA_KV_LORA + 2 * LANES] * sin).astype(BF16)
    for h in range(MLA_HEADS):
        base = h * QK_PAD
        k_ref[:, base:base + MLA_NOPE] = kn[:, h * MLA_NOPE:(h + 1) * MLA_NOPE].astype(BF16)
        k_ref[:, base + MLA_NOPE:base + QK_PAD] = kr


def _mla_prep(proj, qnw, kvnw, wq, wk, wv, cos, sin, batch, seq, tm):
    t = batch * seq
    npos = seq // tm
    full = lambda shape: pl.BlockSpec(shape, lambda i: (0, 0))
    return pl.pallas_call(
        _mla_prep_kernel,
        out_shape=(jax.ShapeDtypeStruct((t, MLA_HEADS * QK_PAD), BF16),
                   jax.ShapeDtypeStruct((t, MLA_HEADS * QK_PAD), BF16),
                   jax.ShapeDtypeStruct((t, MLA_HEADS * V_PAD), BF16)),
        grid=(t // tm,),
        in_specs=[pl.BlockSpec((tm, 512), lambda i: (i, COL_DQ)),
                  pl.BlockSpec((tm, 512), lambda i: (i, COL_DKV)),
                  full((1, 512)), full((1, MLA_KV_LORA)),
                  full(wq.shape), full(wk.shape), full(wv.shape),
                  pl.BlockSpec((tm, LANES), lambda i: (i % npos, 0)),
                  pl.BlockSpec((tm, LANES), lambda i: (i % npos, 0))],
        out_specs=(pl.BlockSpec((tm, MLA_HEADS * QK_PAD), lambda i: (i, 0)),
                   pl.BlockSpec((tm, MLA_HEADS * QK_PAD), lambda i: (i, 0)),
                   pl.BlockSpec((tm, MLA_HEADS * V_PAD), lambda i: (i, 0))),
        compiler_params=_params(("arbitrary",)),
        name="mla_prep",
    )(proj, proj, qnw, kvnw, wq, wk, wv, cos, sin)


def _flash_kernel(q_ref, k_ref, v_ref, o_ref, m_s, acc_s, s_buf, *, bk, nk):
    m_s[...] = jnp.full(m_s.shape, -jnp.inf, F32)
    acc_s[...] = jnp.zeros(acc_s.shape, F32)
    nt = bk // LANES
    bq = q_ref.shape[0]
    nsplit = 2 if bq % 256 == 0 else 1
    hq = bq // nsplit
    qrs = [slice(hf * hq, (hf + 1) * hq) for hf in range(nsplit)]

    def scores(j, slot):
        rows = pl.ds(pl.multiple_of(j * bk, bk), bk)
        for qr in qrs:
            s_buf[slot, qr, :] = lax.dot_general(q_ref[qr, :], k_ref[rows, :], NT_DIMS,
                                                 preferred_element_type=F32)

    def consume(j, slot):
        rows = pl.ds(pl.multiple_of(j * bk, bk), bk)
        ps, alphas = [], []
        for qr in qrs:
            tiles = [s_buf[slot, qr, t * LANES:(t + 1) * LANES] for t in range(nt)]
            tmax = tiles[0]
            for t in range(1, nt):
                tmax = jnp.maximum(tmax, tiles[t])
            m_prev = m_s[qr, :]
            m_new = jnp.maximum(m_prev, jnp.max(tmax, axis=-1, keepdims=True))
            alphas.append(jnp.exp2(m_prev - m_new))
            ps.append(jnp.concatenate([jnp.exp2(tl - m_new).astype(BF16) for tl in tiles], axis=1))
            m_s[qr, :] = m_new
        for qr, p, a in zip(qrs, ps, alphas):
            pv = jnp.dot(p, v_ref[rows, :], preferred_element_type=F32)
            acc_s[qr, :] = jnp.concatenate([a, a], axis=1) * acc_s[qr, :] + pv

    scores(0, 0)
    npairs = (nk - 1) // 2

    def body(i, carry):
        j = 2 * i
        scores(j + 1, 1)
        consume(j, 0)
        scores(j + 2, 0)
        consume(j + 1, 1)
        return carry

    lax.fori_loop(0, npairs, body, 0)
    if nk - 2 * npairs == 2:
        scores(nk - 1, 1)
        consume(nk - 2, 0)
        consume(nk - 1, 1)
    else:
        consume(nk - 1, 0)
    acc = acc_s[...]
    o_ref[...] = (acc[:, :MLA_V] / acc[:, MLA_V:MLA_V + 1]).astype(o_ref.dtype)


def _flash(q, k, v, batch, seq, bq, bk):
    t = batch * seq
    nq = seq // bq
    return pl.pallas_call(
        functools.partial(_flash_kernel, bk=bk, nk=seq // bk),
        out_shape=jax.ShapeDtypeStruct((t, MLA_HEADS * MLA_V), BF16),
        grid=(batch, MLA_HEADS, nq),
        in_specs=[pl.BlockSpec((bq, QK_PAD), lambda b, h, i: (b * nq + i, h)),
                  pl.BlockSpec((seq, QK_PAD), lambda b, h, i: (b, h)),
                  pl.BlockSpec((seq, V_PAD), lambda b, h, i: (b, h))],
        out_specs=pl.BlockSpec((bq, MLA_V), lambda b, h, i: (b * nq + i, h)),
        scratch_shapes=[pltpu.VMEM((bq, LANES), F32), pltpu.VMEM((bq, V_PAD), F32),
                        pltpu.VMEM((2, bq, bk), F32)],
        compiler_params=_params(("arbitrary", "arbitrary", "arbitrary")),
        name="mla_flash",
    )(q, k, v)


def _layernorm(y, w, b):
    mu = jnp.mean(y, axis=-1, keepdims=True)
    yc = y - mu
    var = jnp.mean(yc * yc, axis=-1, keepdims=True)
    return yc * lax.rsqrt(var + LN_EPS) * w + b


def _merge_kernel(x_ref, ga_ref, gb_ref, gm_ref, mq_ref, oa_ref, ob_ref, kvm_ref, wout_ref,
                  l1w_ref, l1b_ref, rw_ref, rb_ref,
                  x1_ref, ri_ref, rg_ref, cnt_ref, carry_s, *, tm):
    @pl.when(pl.program_id(0) == 0)
    def _():
        carry_s[...] = jnp.zeros_like(carry_s)

    parts = []
    for h in range(MEM_HEADS):
        cols = slice(h * MEM_HEAD_DIM, (h + 1) * MEM_HEAD_DIM)
        kh = kvm_ref[:, cols]
        vh = kvm_ref[:, D_MODEL + h * MEM_HEAD_DIM:D_MODEL + (h + 1) * MEM_HEAD_DIM]
        s = lax.dot_general(mq_ref[:, cols], kh, NT_DIMS, preferred_element_type=F32) * (MEM_HEAD_DIM ** -0.5)
        s = s - jnp.max(s, axis=-1, keepdims=True)
        p = jnp.exp(s)
        p = p / jnp.sum(p, axis=-1, keepdims=True)
        parts.append(jnp.dot(p.astype(BF16), vh, preferred_element_type=F32))
    om = jnp.concatenate(parts, axis=1)

    merged = (jax.nn.sigmoid(ga_ref[...].astype(F32)) * oa_ref[...].astype(F32)
              + jax.nn.sigmoid(gb_ref[...].astype(F32)) * ob_ref[...].astype(F32)
              + jax.nn.sigmoid(gm_ref[...].astype(F32)) * om)
    y = DN_ALPHA * x_ref[...] + jnp.dot(merged.astype(BF16), wout_ref[...], preferred_element_type=F32)
    x1 = _layernorm(y, l1w_ref[...], l1b_ref[...])
    x1_ref[...] = x1

    logits = jnp.dot(x1, rw_ref[...], preferred_element_type=F32, precision=lax.Precision.HIGHEST) + rb_ref[...]
    lane_i = lax.broadcasted_iota(jnp.int32, (tm, LANES), 1)
    lane = lane_i.astype(F32)
    work = logits
    idx, val = [], []
    for _ in range(TOP_K):
        mx = jnp.max(work, axis=-1, keepdims=True)
        ix = jnp.min(jnp.where(work == mx, lane, float(LANES)), axis=-1, keepdims=True)
        idx.append(ix)
        val.append(mx)
        work = jnp.where(lane == ix, -jnp.inf, work)
    ex = [jnp.exp(v - val[0]) for v in val]
    tot = ex[0] + ex[1] + ex[2] + ex[3]
    hot = [jnp.where(lane == ix, 1.0, 0.0) for ix in idx]
    multi = hot[0] + hot[1] + hot[2] + hot[3]
    r = lax.broadcasted_iota(jnp.int32, (tm, tm), 0)
    c = lax.broadcasted_iota(jnp.int32, (tm, tm), 1)
    lower = jnp.where(r > c, 1.0, 0.0).astype(BF16)
    before = jnp.dot(lower, multi.astype(BF16), preferred_element_type=F32) + carry_s[0:1, :]
    ri = jnp.zeros((tm, LANES), F32)
    rg = jnp.zeros((tm, LANES), F32)
    for j in range(TOP_K):
        rank = jnp.sum(before * hot[j], axis=-1, keepdims=True)
        ri = ri + jnp.where(lane == float(j), idx[j], 0.0) + jnp.where(lane == float(TOP_K + j), rank, 0.0)
        rg = rg + jnp.where(lane == float(j), ex[j] / tot, 0.0)
    ri_ref[...] = ri.astype(jnp.int32)
    rg_ref[...] = rg
    carry_s[...] = carry_s[...] + jnp.sum(multi, axis=0, keepdims=True)
    cnt_ref[...] = carry_s[...]


def _merge(x2d, proj, o_a, o_b, kvm, wout, l1w, l1b, rw, rb, batch, seq, tm):
    t = batch * seq
    per_b = seq // tm
    nmem = kvm.shape[0] // batch
    tile = lambda colblk: pl.BlockSpec((tm, D_MODEL), lambda i: (i, colblk))
    full = lambda shape: pl.BlockSpec(shape, lambda i: (0, 0))
    return pl.pallas_call(
        functools.partial(_merge_kernel, tm=tm),
        out_shape=(jax.ShapeDtypeStruct((t, D_MODEL), F32),
                   jax.ShapeDtypeStruct((t, LANES), jnp.int32),
                   jax.ShapeDtypeStruct((t, LANES), F32),
                   jax.ShapeDtypeStruct((8, LANES), F32)),
        grid=(t // tm,),
        in_specs=[tile(0), tile(COL_GA), tile(COL_GB), tile(COL_GM), tile(COL_MQ), tile(0), tile(0),
                  pl.BlockSpec((nmem, 2 * D_MODEL), lambda i: (i // per_b, 0)),
                  full((D_MODEL, D_MODEL)), full((1, D_MODEL)), full((1, D_MODEL)),
                  full((D_MODEL, LANES)), full((1, LANES))],
        out_specs=(tile(0), pl.BlockSpec((tm, LANES), lambda i: (i, 0)),
                   pl.BlockSpec((tm, LANES), lambda i: (i, 0)), full((8, LANES))),
        scratch_shapes=[pltpu.VMEM((8, LANES), F32)],
        compiler_params=_params(("arbitrary",)),
        name="merge_router",
    )(x2d, proj, proj, proj, proj, o_a, o_b, kvm, wout, l1w, l1b, rw, rb)


def _dispatch_kernel(dest_hbm, x_ref, xs_in, xs_hbm, idx_s, sem_i, sem_r, *, tm):
    del xs_in
    i = pl.program_id(0)
    n = TOP_K * tm
    cp = pltpu.make_async_copy(dest_hbm.at[pl.ds(pl.multiple_of(i * n, n), n)], idx_s, sem_i)
    cp.start()
    cp.wait()

    def body(r, carry):
        src = x_ref.at[pl.ds(r, 1)]
        for j in range(TOP_K):
            d = idx_s[r * TOP_K + j]
            pltpu.make_async_copy(src, xs_hbm.at[pl.ds(d, 1)], sem_r).start()
        return carry

    lax.fori_loop(0, tm, body, 0)
    for _ in range(TOP_K):
        pltpu.make_async_copy(x_ref, xs_hbm.at[pl.ds(0, tm)], sem_r).wait()


def _dispatch(dest, x1, slots, tm):
    t = x1.shape[0]
    zeros = jnp.zeros((slots, D_MODEL), F32)
    return pl.pallas_call(
        functools.partial(_dispatch_kernel, tm=tm),
        out_shape=jax.ShapeDtypeStruct((slots, D_MODEL), F32),
        grid=(t // tm,),
        in_specs=[pl.BlockSpec(memory_space=pl.ANY),
                  pl.BlockSpec((tm, D_MODEL), lambda i: (i, 0)),
                  pl.BlockSpec(memory_space=pl.ANY)],
        out_specs=pl.BlockSpec(memory_space=pl.ANY),
        scratch_shapes=[pltpu.SMEM((TOP_K * tm,), jnp.int32),
                        pltpu.SemaphoreType.DMA, pltpu.SemaphoreType.DMA],
        input_output_aliases={2: 0},
        compiler_params=_params(("arbitrary",)),
        name="moe_dispatch",
    )(dest, x1, zeros)


def _expert_kernel(be_ref, nu_ref, xs_ref, wgu_ref, bgu_ref, wd_ref, bd_ref, ys_ref):
    del be_ref
    i = pl.program_id(0)

    @pl.when(i < nu_ref[0])
    def _():
        h = jnp.dot(xs_ref[...].astype(BF16), wgu_ref[0], preferred_element_type=F32) + bgu_ref[0]
        g = jnp.minimum(h[:, :D_FF], SWIGLU_LIMIT)
        u = jnp.clip(h[:, D_FF:], -SWIGLU_LIMIT, SWIGLU_LIMIT)
        act = (u + 1.0) * g * jax.nn.sigmoid(SWIGLU_ALPHA * g)
        ys_ref[...] = jnp.dot(act.astype(BF16), wd_ref[0], preferred_element_type=F32) + bd_ref[0]

    @pl.when(i >= nu_ref[0])
    def _():
        ys_ref[...] = jnp.zeros_like(ys_ref)


def _experts(block_e, n_used, xs, wgu, bgu, wd, bd):
    slots = xs.shape[0]
    nb = slots // EXPERT_BLOCK
    grid_spec = pltpu.PrefetchScalarGridSpec(
        num_scalar_prefetch=2,
        grid=(nb,),
        in_specs=[pl.BlockSpec((EXPERT_BLOCK, D_MODEL), lambda i, be, nu: (i, 0)),
                  pl.BlockSpec((1, D_MODEL, 2 * D_FF), lambda i, be, nu: (be[i], 0, 0)),
                  pl.BlockSpec((1, 1, 2 * D_FF), lambda i, be, nu: (be[i], 0, 0)),
                  pl.BlockSpec((1, D_FF, D_MODEL), lambda i, be, nu: (be[i], 0, 0)),
                  pl.BlockSpec((1, 1, D_MODEL), lambda i, be, nu: (be[i], 0, 0))],
        out_specs=pl.BlockSpec((EXPERT_BLOCK, D_MODEL), lambda i, be, nu: (i, 0)),
    )
    return pl.pallas_call(
        _expert_kernel,
        out_shape=jax.ShapeDtypeStruct((slots, D_MODEL), F32),
        grid_spec=grid_spec,
        compiler_params=_params(("arbitrary",)),
        name="moe_experts",
    )(block_e, n_used, xs, wgu, bgu, wd, bd)


def _combine_kernel(dest_hbm, x1_ref, rg_ref, ys_hbm, l2w_ref, l2b_ref, o_ref, idx_s, buf, sem_i, sem_r, *, tm):
    i = pl.program_id(0)
    nsteps = pl.num_programs(0)
    n = TOP_K * tm

    def issue(step, slot):
        cp = pltpu.make_async_copy(dest_hbm.at[pl.ds(pl.multiple_of(step * n, n), n)], idx_s, sem_i)
        cp.start()
        cp.wait()

        def body(r, carry):
            for j in range(TOP_K):
                d = idx_s[r * TOP_K + j]
                pltpu.make_async_copy(ys_hbm.at[pl.ds(d, 1)], buf.at[slot, pl.ds(j * tm + r, 1)],
                                      sem_r.at[slot]).start()
            return carry

        lax.fori_loop(0, tm, body, 0)

    @pl.when(i == 0)
    def _():
        issue(0, 0)

    @pl.when(i + 1 < nsteps)
    def _():
        issue(i + 1, (i + 1) % 2)

    slot = i % 2
    pltpu.make_async_copy(ys_hbm.at[pl.ds(0, n)], buf.at[slot], sem_r.at[slot]).wait()
    rg = rg_ref[...]
    moe = rg[:, 0:1] * buf[slot, pl.ds(0, tm), :]
    for j in range(1, TOP_K):
        moe = moe + rg[:, j:j + 1] * buf[slot, pl.ds(j * tm, tm), :]
    o_ref[...] = _layernorm(DN_ALPHA * x1_ref[...] + moe, l2w_ref[...], l2b_ref[...])


def _combine(dest, x1, rg, ys, l2w, l2b, tm):
    t = x1.shape[0]
    full = lambda shape: pl.BlockSpec(shape, lambda i: (0, 0))
    return pl.pallas_call(
        functools.partial(_combine_kernel, tm=tm),
        out_shape=jax.ShapeDtypeStruct((t, D_MODEL), F32),
        grid=(t // tm,),
        in_specs=[pl.BlockSpec(memory_space=pl.ANY),
                  pl.BlockSpec((tm, D_MODEL), lambda i: (i, 0)),
                  pl.BlockSpec((tm, LANES), lambda i: (i, 0)),
                  pl.BlockSpec(memory_space=pl.ANY),
                  full((1, D_MODEL)), full((1, D_MODEL))],
        out_specs=pl.BlockSpec((tm, D_MODEL), lambda i: (i, 0)),
        scratch_shapes=[pltpu.SMEM((TOP_K * tm,), jnp.int32),
                        pltpu.VMEM((2, TOP_K * tm, D_MODEL), F32),
                        pltpu.SemaphoreType.DMA, pltpu.SemaphoreType.DMA((2,))],
        compiler_params=_params(("arbitrary",)),
        name="moe_combine",
    )(dest, x1, rg, ys, l2w, l2b)


def _prep_w_in(w):
    zeros = lambda n: jnp.zeros((D_MODEL, n), w.dtype)
    kr = w[:, 5760:5824]
    kr_sw = jnp.concatenate([kr[:, MLA_ROPE // 2:], kr[:, :MLA_ROPE // 2]], axis=1)
    parts = [w[:, 6848:9920], w[:, 0:5120], w[:, 5824:6848],
             w[:, 5120:5504], zeros(128),
             w[:, 5504:5760], kr, zeros(64), kr_sw, zeros(64)]
    return jnp.concatenate(parts, axis=1).astype(BF16)


def _prep_w_uq(w):
    w3 = w.reshape(MLA_Q_LORA, MLA_HEADS, MLA_NOPE + MLA_ROPE)
    rope = w3[:, :, MLA_NOPE:]
    rope_sw = jnp.concatenate([rope[:, :, MLA_ROPE // 2:], rope[:, :, :MLA_ROPE // 2]], axis=-1)
    pad = jnp.zeros((MLA_Q_LORA, MLA_HEADS, QK_PAD - MLA_NOPE - MLA_ROPE), w.dtype)
    main = jnp.concatenate([w3, pad], axis=-1).reshape(MLA_Q_LORA, MLA_HEADS * QK_PAD)
    swp = jnp.concatenate([rope_sw, pad], axis=-1).reshape(MLA_Q_LORA, MLA_HEADS * LANES)
    both = jnp.concatenate([main, swp], axis=1)
    return jnp.pad(both, ((0, 512 - MLA_Q_LORA), (0, 0))).astype(BF16)


def _rope_tables(seq):
    inv_freq = ROPE_THETA ** (-jnp.arange(0, MLA_ROPE, 2, dtype=F32) / MLA_ROPE)
    ang = jnp.arange(seq, dtype=F32)[:, None] * inv_freq[None, :]
    cos, sin = jnp.cos(ang), jnp.sin(ang)
    pad = jnp.zeros((seq, LANES - MLA_ROPE), F32)
    return (jnp.concatenate([cos, cos, pad], axis=1), jnp.concatenate([-sin, sin, pad], axis=1))


def _tile(n, pref):
    return pref if n % pref == 0 else n


def _layer(x, mem, wts):
    batch, seq, _ = x.shape
    t = batch * seq
    x2d = x.reshape(t, D_MODEL)

    proj = _matmul(x2d, wts["w_in"], BF16, _tile(t, 1024), 1024, "in_proj")
    o_a = _hgrn(proj, wts["lb"], wts["hgrn_norm_w"], batch, seq, _tile(seq, 512))

    cos, sin = _rope_tables(seq)
    q, k, v = _mla_prep(proj, wts["qnw"], wts["kvnw"], wts["w_uq"], wts["w_uk"], wts["w_uv"],
                        cos, sin, batch, seq, _tile(seq, 512))
    o_b = _flash(q, k, v, batch, seq, _tile(seq, 1024), _tile(seq, 1024))

    nmem = mem.shape[1]
    kvm = _matmul(mem.reshape(batch * nmem, D_MODEL), wts["mem_w_kv"], BF16,
                  _tile(batch * nmem, 512), 1024, "mem_kv")

    tm = _tile(seq, 512)
    x1, ri, rg, cnt = _merge(x2d, proj, o_a, o_b, kvm, wts["w_out"], wts["ln1_w"], wts["ln1_b"],
                             wts["router_w"], wts["router_b"], batch, seq, tm)

    idx = ri[:, :TOP_K]
    rank = ri[:, TOP_K:2 * TOP_K]
    counts = cnt[0, :N_EXPERTS].astype(jnp.int32)
    padded = (counts + EXPERT_BLOCK - 1) // EXPERT_BLOCK * EXPERT_BLOCK
    pad_end = jnp.cumsum(padded)
    pad_start = pad_end - padded
    dest = (pad_start[idx] + rank).reshape(t * TOP_K).astype(jnp.int32)
    nb = t * TOP_K // EXPERT_BLOCK + N_EXPERTS
    block_e = jnp.minimum(
        jnp.searchsorted(pad_end, jnp.arange(nb, dtype=jnp.int32) * EXPERT_BLOCK, side="right"),
        N_EXPERTS - 1).astype(jnp.int32)
    n_used = (pad_end[-1:] // EXPERT_BLOCK).astype(jnp.int32)

    tmd = _tile(t, 512)
    xs = _dispatch(dest, x1, nb * EXPERT_BLOCK, tmd)
    ys = _experts(block_e, n_used, xs, wts["exp_w_gu"], wts["exp_b_gu"], wts["exp_w_down"], wts["exp_b_down"])
    y = _combine(dest, x1, rg, ys, wts["ln2_w"], wts["ln2_b"], _tile(t, 256))
    return y.reshape(batch, seq, D_MODEL)


def kernel(x_prompt, x_sample, mem_prompt, mem_sample, w_in, hgrn_lb_logits, hgrn_norm_w,
           mla_q_norm_w, mla_w_uq, mla_kv_norm_w, mla_w_uk, mla_w_uv, mem_w_kv, w_out,
           ln1_w, ln1_b, router_w, router_b, exp_w_gu, exp_b_gu, exp_w_down, exp_b_down,
           ln2_w, ln2_b):
    depth = w_in.shape[0]
    gamma = jax.nn.softmax(hgrn_lb_logits.astype(F32), axis=1)
    cum = jnp.cumsum(gamma, axis=1)
    lb_all = cum[:, 1:] - cum[:, :1]
    y_prompt, y_sample = x_prompt, x_sample
    for l in range(depth):
        row = lambda a: a[l].reshape(1, -1).astype(F32)
        wts = {
            "w_in": _prep_w_in(w_in[l]),
            "lb": lb_all[:, l],
            "hgrn_norm_w": hgrn_norm_w[l].astype(F32),
            "qnw": jnp.pad(row(mla_q_norm_w), ((0, 0), (0, 512 - MLA_Q_LORA))),
            "kvnw": row(mla_kv_norm_w),
            "w_uq": _prep_w_uq(mla_w_uq[l]),
            "w_uk": mla_w_uk[l].astype(BF16),
            "w_uv": mla_w_uv[l].astype(BF16),
            "mem_w_kv": mem_w_kv[l].astype(BF16),
            "w_out": w_out[l].astype(BF16),
            "ln1_w": row(ln1_w), "ln1_b": row(ln1_b),
            "router_w": jnp.pad(router_w[l].astype(F32), ((0, 0), (0, LANES - N_EXPERTS))),
            "router_b": jnp.pad(row(router_b), ((0, 0), (0, LANES - N_EXPERTS)), constant_values=-jnp.inf),
            "exp_w_gu": exp_w_gu[l].astype(BF16),
            "exp_b_gu": exp_b_gu[l].reshape(N_EXPERTS, 1, 2 * D_FF).astype(F32),
            "exp_w_down": exp_w_down[l].astype(BF16),
            "exp_b_down": exp_b_down[l].reshape(N_EXPERTS, 1, D_MODEL).astype(F32),
            "ln2_w": row(ln2_w), "ln2_b": row(ln2_b),
        }
        y_prompt = _layer(y_prompt, mem_prompt, wts)
        y_sample = _layer(y_sample, mem_sample, wts)
    return (y_prompt, y_sample)
```

```python
import functools

import numpy as np
import jax
import jax.numpy as jnp
from jax import lax
from jax.experimental import pallas as pl
from jax.experimental.pallas import tpu as pltpu

F32 = jnp.float32
BF16 = jnp.bfloat16

D_MODEL = 1024
HGRN_HEADS = 8
HGRN_DK = 128
MLA_HEADS = 8
MLA_Q_LORA = 384
MLA_KV_LORA = 256
MLA_NOPE = 128
MLA_ROPE = 64
MLA_V = 128
ROPE_THETA = 10000.0
MEM_HEADS = 4
MEM_HEAD_DIM = D_MODEL // MEM_HEADS
N_EXPERTS = 32
TOP_K = 4
D_FF = D_MODEL
SWIGLU_LIMIT = 7.0
SWIGLU_ALPHA = 1.702
DN_ALPHA = 2.0 ** 0.25
LN_EPS = 1e-5
RMS_EPS = 1e-6

LANES = 128
SUBLANES = 8
QK_PAD = 256
V_PAD = 256
LOG2E = 1.4426950408889634
HGRN_CHUNK = 64
HGRN_SAFE_RANGE = 160.0
EXPERT_BLOCK = 512
VMEM_LIMIT = 56 * 1024 * 1024

COL_GA, COL_GB, COL_GM, COL_Q, COL_ZF, COL_ZB, COL_I, COL_G, COL_MQ = range(9)
COL_DQ, COL_DKV = 18, 19
IN_COLS_PAD = 10240

NT_DIMS = (((1,), (1,)), ((), ()))
TN_DIMS = (((0,), (0,)), ((), ()))


def _params(sem, vmem=VMEM_LIMIT):
    return pltpu.CompilerParams(dimension_semantics=sem, vmem_limit_bytes=vmem)


def _mm_kernel(x_ref, w_ref, o_ref, xb_ref):
    @pl.when(pl.program_id(1) == 0)
    def _():
        xb_ref[...] = x_ref[...].astype(BF16)

    o_ref[...] = jnp.dot(xb_ref[...], w_ref[...], preferred_element_type=F32).astype(o_ref.dtype)


def _matmul(x, w, out_dtype, tm, tn, name):
    m, k = x.shape
    n = w.shape[1]
    return pl.pallas_call(
        _mm_kernel,
        out_shape=jax.ShapeDtypeStruct((m, n), out_dtype),
        grid=(m // tm, n // tn),
        in_specs=[pl.BlockSpec((tm, k), lambda i, j: (i, 0)),
                  pl.BlockSpec((k, tn), lambda i, j: (0, j))],
        out_specs=pl.BlockSpec((tm, tn), lambda i, j: (i, j)),
        scratch_shapes=[pltpu.VMEM((tm, k), BF16)],
        compiler_params=_params(("arbitrary", "arbitrary")),
        name=name,
    )(x, w)


def _hgrn_kernel(*refs, reverse, epilogue, sc):
    if epilogue:
        lb_ref, q_ref, z_ref, v_ref, of_ref, g_ref, nw_ref, o_ref, st_ref, kk_s, b_s, vf_s, os_s = refs
    else:
        lb_ref, q_ref, z_ref, v_ref, o_ref, st_ref, kk_s, b_s, vf_s = refs
    C = HGRN_CHUNK
    nch = sc // C

    @pl.when(pl.program_id(1) == 0)
    def _():
        st_ref[...] = jnp.zeros_like(st_ref)

    lb = lb_ref[...]
    row = lax.broadcasted_iota(jnp.int32, (C, C), 0)
    col = lax.broadcasted_iota(jnp.int32, (C, C), 1)
    tri = (row <= col) if reverse else (row >= col)
    trib = jnp.where(tri, 1.0, 0.0).astype(BF16)

    minb = None
    for c in range(nch):
        rows = slice(c * C, (c + 1) * C)
        z = z_ref[rows, :].astype(F32)
        gate = (1.0 - lb) * jax.nn.sigmoid(z)
        lf = jnp.log(lb + gate)
        kk_s[rows, :] = (1.0 - lb) - gate
        hi = lf.astype(BF16)
        r1 = lf - hi.astype(F32)
        mid = r1.astype(BF16)
        lo = (r1 - mid.astype(F32)).astype(BF16)
        b = (jnp.dot(trib, hi, preferred_element_type=F32)
             + jnp.dot(trib, mid, preferred_element_type=F32)
             + jnp.dot(trib, lo, preferred_element_type=F32))
        b_s[rows, :] = b
        mb = jnp.min(b)
        minb = mb if minb is None else jnp.minimum(minb, mb)

    rid = lax.broadcasted_iota(jnp.int32, (C, 1), 0)

    def chunk(i, carry, fast):
        c = (nch - 1 - i) if reverse else i
        r0 = pl.multiple_of(c * C, C)
        rows = pl.ds(r0, C)
        for h in range(HGRN_HEADS):
            cols = slice(h * HGRN_DK, (h + 1) * HGRN_DK)
            q = q_ref[rows, cols].astype(F32)
            kk = kk_s[rows, cols]
            b = b_s[rows, cols]
            v = v_ref[rows, cols]
            bl = b[0:1, :] if reverse else b[C - 1:C, :]
            if fast:
                bm = 0.5 * bl
                qd = (q * jnp.exp(b - bm)).astype(BF16)
                kd = (kk * jnp.exp(bm - b)).astype(BF16)
                s = lax.dot_general(qd, kd, NT_DIMS, preferred_element_type=F32)
                s = jnp.where(tri, s, 0.0).astype(BF16)
                o = jnp.dot(s, v, preferred_element_type=F32)
            else:
                def sbody(g_i, o_acc):
                    grp = pl.ds(pl.multiple_of(r0 + g_i * SUBLANES, SUBLANES), SUBLANES)
                    b8 = b_s[grp, cols]
                    k8 = kk_s[grp, cols]
                    v8 = vf_s[grp, cols]
                    for jj in range(SUBLANES):
                        s_i = g_i * SUBLANES + jj
                        w = q * k8[jj:jj + 1, :] * jnp.exp(jnp.minimum(b - b8[jj:jj + 1, :], 0.0))
                        scol = jnp.sum(w, axis=-1, keepdims=True)
                        keep = (rid <= s_i) if reverse else (rid >= s_i)
                        o_acc = o_acc + jnp.where(keep, scol, 0.0) * v8[jj:jj + 1, :]
                    return o_acc
                o = lax.fori_loop(0, C // SUBLANES, sbody, jnp.zeros((C, HGRN_DK), F32))
            st = st_ref[h]
            qi = (q * jnp.exp(b)).astype(BF16)
            o = o + lax.dot_general(qi, st.astype(BF16), NT_DIMS, preferred_element_type=F32)
            ke = (kk * jnp.exp(bl - b)).astype(BF16)
            upd = lax.dot_general(v, ke, TN_DIMS, preferred_element_type=F32)
            st_ref[h] = st * jnp.exp(bl) + upd
            if epilogue:
                os_s[rows, cols] = o + of_ref[rows, cols].astype(F32)
            else:
                o_ref[rows, cols] = o.astype(o_ref.dtype)
        return carry

    safe = minb >= -HGRN_SAFE_RANGE

    @pl.when(safe)
    def _():
        lax.fori_loop(0, nch, functools.partial(chunk, fast=True), 0, unroll=2)

    @pl.when(jnp.logical_not(safe))
    def _():
        vf_s[...] = v_ref[...].astype(F32)
        lax.fori_loop(0, nch, functools.partial(chunk, fast=False), 0)

    if epilogue:
        nw = nw_ref[...]
        for h in range(HGRN_HEADS):
            cols = slice(h * HGRN_DK, (h + 1) * HGRN_DK)
            os = os_s[:, cols]
            ms = jnp.mean(os * os, axis=-1, keepdims=True)
            y = os * lax.rsqrt(ms + RMS_EPS) * nw
            g = g_ref[:, cols].astype(F32)
            o_ref[:, cols] = (y * (g * jax.nn.sigmoid(g))).astype(o_ref.dtype)


def _hgrn(proj, lb, norm_w, batch, seq, sc):
    t = batch * seq
    ns = seq // sc
    blk = (sc, D_MODEL)

    def spec(colblk, reverse):
        if reverse:
            return pl.BlockSpec(blk, lambda b, n: (b * ns + ns - 1 - n, colblk))
        return pl.BlockSpec(blk, lambda b, n: (b * ns + n, colblk))

    def row_spec(reverse):
        if reverse:
            return pl.BlockSpec(blk, lambda b, n: (b * ns + ns - 1 - n, 0))
        return pl.BlockSpec(blk, lambda b, n: (b * ns + n, 0))

    lb_spec = pl.BlockSpec((1, D_MODEL), lambda b, n: (0, 0))
    common_scratch = [pltpu.VMEM((HGRN_HEADS, HGRN_DK, HGRN_DK), F32),
                      pltpu.VMEM(blk, F32), pltpu.VMEM(blk, F32), pltpu.VMEM(blk, F32)]
    o_f = pl.pallas_call(
        functools.partial(_hgrn_kernel, reverse=False, epilogue=False, sc=sc),
        out_shape=jax.ShapeDtypeStruct((t, D_MODEL), BF16),
        grid=(batch, ns),
        in_specs=[lb_spec, spec(COL_Q, False), spec(COL_ZF, False), spec(COL_I, False)],
        out_specs=row_spec(False),
        scratch_shapes=common_scratch,
        compiler_params=_params(("arbitrary", "arbitrary")),
        name="hgrn_fwd",
    )(lb[0:1], proj, proj, proj)
    o_a = pl.pallas_call(
        functools.partial(_hgrn_kernel, reverse=True, epilogue=True, sc=sc),
        out_shape=jax.ShapeDtypeStruct((t, D_MODEL), BF16),
        grid=(batch, ns),
        in_specs=[lb_spec, spec(COL_Q, True), spec(COL_ZB, True), spec(COL_I, True),
                  row_spec(True), spec(COL_G, True),
                  pl.BlockSpec((1, HGRN_DK), lambda b, n: (0, 0))],
        out_specs=row_spec(True),
        scratch_shapes=common_scratch + [pltpu.VMEM(blk, F32)],
        compiler_params=_params(("arbitrary", "arbitrary")),
        name="hgrn_bwd",
    )(lb[1:2], proj, proj, proj, o_f, proj, norm_w.reshape(1, HGRN_DK))
    return o_a


def _mla_prep_kernel(dq_ref, dkv_ref, qnw_ref, kvnw_ref, wq_ref, wk_ref, wv_ref, cos_ref, sin_ref,
                     q_ref, k_ref, v_ref):
    scale = (MLA_NOPE + MLA_ROPE) ** -0.5 * LOG2E
    cos = cos_ref[...]
    sin = sin_ref[...]
    dq = dq_ref[...].astype(F32)
    ms = jnp.sum(dq * dq, axis=-1, keepdims=True) * (1.0 / MLA_Q_LORA)
    cq = (dq * lax.rsqrt(ms + RMS_EPS) * qnw_ref[...]).astype(BF16)
    qa = jnp.dot(cq, wq_ref[...], preferred_element_type=F32)
    for h in range(MLA_HEADS):
        base = h * QK_PAD
        q_ref[:, base:base + MLA_NOPE] = (qa[:, base:base + MLA_NOPE] * scale).astype(BF16)
        rp = qa[:, base + MLA_NOPE:base + QK_PAD]
        sw = qa[:, MLA_HEADS * QK_PAD + h * LANES:MLA_HEADS * QK_PAD + (h + 1) * LANES]
        q_ref[:, base + MLA_NOPE:base + QK_PAD] = ((rp * cos + sw * sin) * scale).astype(BF16)
    dkv = dkv_ref[...].astype(F32)
    ckv = dkv[:, :MLA_KV_LORA]
    msk = jnp.mean(ckv * ckv, axis=-1, keepdims=True)
    cn = (ckv * lax.rsqrt(msk + RMS_EPS) * kvnw_ref[...]).astype(BF16)
    kn = jnp.dot(cn, wk_ref[...], preferred_element_type=F32)
    vv = jnp.dot(cn, wv_ref[...], preferred_element_type=F32).astype(BF16)
    ones_col = jnp.where(lax.broadcasted_iota(jnp.int32, (vv.shape[0], LANES), 1) == 0, 1.0, 0.0).astype(BF16)
    for h in range(MLA_HEADS):
        v_ref[:, h * V_PAD:h * V_PAD + MLA_V] = vv[:, h * MLA_V:(h + 1) * MLA_V]
        v_ref[:, h * V_PAD + MLA_V:(h + 1) * V_PAD] = ones_col
    kr = (dkv[:, MLA_KV_LORA:MLA_KV_LORA + LANES] * cos
          + dkv[:, MLA_KV_LORA + LANES:MLA_KV_LORA + 2 * LANES] * sin).astype(BF16)
    for h in range(MLA_HEADS):
        base = h * QK_PAD
        k_ref[:, base:base + MLA_NOPE] = kn[:, h * MLA_NOPE:(h + 1) * MLA_NOPE].astype(BF16)
        k_ref[:, base + MLA_NOPE:base + QK_PAD] = kr


def _mla_prep(proj, qnw, kvnw, wq, wk, wv, cos, sin, batch, seq, tm):
    t = batch * seq
    npos = seq // tm
    full = lambda shape: pl.BlockSpec(shape, lambda i: (0, 0))
    return pl.pallas_call(
        _mla_prep_kernel,
        out_shape=(jax.ShapeDtypeStruct((t, MLA_HEADS * QK_PAD), BF16),
                   jax.ShapeDtypeStruct((t, MLA_HEADS * QK_PAD), BF16),
                   jax.ShapeDtypeStruct((t, MLA_HEADS * V_PAD), BF16)),
        grid=(t // tm,),
        in_specs=[pl.BlockSpec((tm, 512), lambda i: (i, COL_DQ)),
                  pl.BlockSpec((tm, 512), lambda i: (i, COL_DKV)),
                  full((1, 512)), full((1, MLA_KV_LORA)),
                  full(wq.shape), full(wk.shape), full(wv.shape),
                  pl.BlockSpec((tm, LANES), lambda i: (i % npos, 0)),
                  pl.BlockSpec((tm, LANES), lambda i: (i % npos, 0))],
        out_specs=(pl.BlockSpec((tm, MLA_HEADS * QK_PAD), lambda i: (i, 0)),
                   pl.BlockSpec((tm, MLA_HEADS * QK_PAD), lambda i: (i, 0)),
                   pl.BlockSpec((tm, MLA_HEADS * V_PAD), lambda i: (i, 0))),
        compiler_params=_params(("arbitrary",)),
        name="mla_prep",
    )(proj, proj, qnw, kvnw, wq, wk, wv, cos, sin)


def _flash_kernel(q_ref, k_ref, v_ref, o_ref, m_s, acc_s, s_buf, *, bk, nk):
    m_s[...] = jnp.full(m_s.shape, -jnp.inf, F32)
    acc_s[...] = jnp.zeros(acc_s.shape, F32)
    nt = bk // LANES
    bq = q_ref.shape[0]
    nsplit = 2 if bq % 256 == 0 else 1
    hq = bq // nsplit
    qrs = [slice(hf * hq, (hf + 1) * hq) for hf in range(nsplit)]

    def scores(j, slot):
        rows = pl.ds(pl.multiple_of(j * bk, bk), bk)
        for qr in qrs:
            s_buf[slot, qr, :] = lax.dot_general(q_ref[qr, :], k_ref[rows, :], NT_DIMS,
                                                 preferred_element_type=F32)

    def consume(j, slot):
        rows = pl.ds(pl.multiple_of(j * bk, bk), bk)
        ps, alphas = [], []
        for qr in qrs:
            tiles = [s_buf[slot, qr, t * LANES:(t + 1) * LANES] for t in range(nt)]
            tmax = tiles[0]
            for t in range(1, nt):
                tmax = jnp.maximum(tmax, tiles[t])
            m_prev = m_s[qr, :]
            m_new = jnp.maximum(m_prev, jnp.max(tmax, axis=-1, keepdims=True))
            alphas.append(jnp.exp2(m_prev - m_new))
            ps.append(jnp.concatenate([jnp.exp2(tl - m_new).astype(BF16) for tl in tiles], axis=1))
            m_s[qr, :] = m_new
        for qr, p, a in zip(qrs, ps, alphas):
            pv = jnp.dot(p, v_ref[rows, :], preferred_element_type=F32)
            acc_s[qr, :] = jnp.concatenate([a, a], axis=1) * acc_s[qr, :] + pv

    scores(0, 0)
    npairs = (nk - 1) // 2

    def body(i, carry):
        j = 2 * i
        scores(j + 1, 1)
        consume(j, 0)
        scores(j + 2, 0)
        consume(j + 1, 1)
        return carry

    lax.fori_loop(0, npairs, body, 0)
    if nk - 2 * npairs == 2:
        scores(nk - 1, 1)
        consume(nk - 2, 0)
        consume(nk - 1, 1)
    else:
        consume(nk - 1, 0)
    acc = acc_s[...]
    o_ref[...] = (acc[:, :MLA_V] / acc[:, MLA_V:MLA_V + 1]).astype(o_ref.dtype)


def _flash(q, k, v, batch, seq, bq, bk):
    t = batch * seq
    nq = seq // bq
    return pl.pallas_call(
        functools.partial(_flash_kernel, bk=bk, nk=seq // bk),
        out_shape=jax.ShapeDtypeStruct((t, MLA_HEADS * MLA_V), BF16),
        grid=(batch, MLA_HEADS, nq),
        in_specs=[pl.BlockSpec((bq, QK_PAD), lambda b, h, i: (b * nq + i, h)),
                  pl.BlockSpec((seq, QK_PAD), lambda b, h, i: (b, h)),
                  pl.BlockSpec((seq, V_PAD), lambda b, h, i: (b, h))],
        out_specs=pl.BlockSpec((bq, MLA_V), lambda b, h, i: (b * nq + i, h)),
        scratch_shapes=[pltpu.VMEM((bq, LANES), F32), pltpu.VMEM((bq, V_PAD), F32),
                        pltpu.VMEM((2, bq, bk), F32)],
        compiler_params=_params(("arbitrary", "arbitrary", "arbitrary")),
        name="mla_flash",
    )(q, k, v)


def _layernorm(y, w, b):
    mu = jnp.mean(y, axis=-1, keepdims=True)
    yc = y - mu
    var = jnp.mean(yc * yc, axis=-1, keepdims=True)
    return yc * lax.rsqrt(var + LN_EPS) * w + b


def _merge_kernel(x_ref, ga_ref, gb_ref, gm_ref, mq_ref, oa_ref, ob_ref, kvm_ref, wout_ref,
                  l1w_ref, l1b_ref, rw_ref, rb_ref,
                  x1_ref, ri_ref, rg_ref, cnt_ref, carry_s, *, tm):
    @pl.when(pl.program_id(0) == 0)
    def _():
        carry_s[...] = jnp.zeros_like(carry_s)

    parts = []
    for h in range(MEM_HEADS):
        cols = slice(h * MEM_HEAD_DIM, (h + 1) * MEM_HEAD_DIM)
        kh = kvm_ref[:, cols]
        vh = kvm_ref[:, D_MODEL + h * MEM_HEAD_DIM:D_MODEL + (h + 1) * MEM_HEAD_DIM]
        s = lax.dot_general(mq_ref[:, cols], kh, NT_DIMS, preferred_element_type=F32) * (MEM_HEAD_DIM ** -0.5)
        s = s - jnp.max(s, axis=-1, keepdims=True)
        p = jnp.exp(s)
        p = p / jnp.sum(p, axis=-1, keepdims=True)
        parts.append(jnp.dot(p.astype(BF16), vh, preferred_element_type=F32))
    om = jnp.concatenate(parts, axis=1)

    merged = (jax.nn.sigmoid(ga_ref[...].astype(F32)) * oa_ref[...].astype(F32)
              + jax.nn.sigmoid(gb_ref[...].astype(F32)) * ob_ref[...].astype(F32)
              + jax.nn.sigmoid(gm_ref[...].astype(F32)) * om)
    y = DN_ALPHA * x_ref[...] + jnp.dot(merged.astype(BF16), wout_ref[...], preferred_element_type=F32)
    x1 = _layernorm(y, l1w_ref[...], l1b_ref[...])
    x1_ref[...] = x1

    logits = jnp.dot(x1, rw_ref[...], preferred_element_type=F32, precision=lax.Precision.HIGHEST) + rb_ref[...]
    lane_i = lax.broadcasted_iota(jnp.int32, (tm, LANES), 1)
    lane = lane_i.astype(F32)
    work = logits
    idx, val = [], []
    for _ in range(TOP_K):
        mx = jnp.max(work, axis=-1, keepdims=True)
        ix = jnp.min(jnp.where(work == mx, lane, float(LANES)), axis=-1, keepdims=True)
        idx.append(ix)
        val.append(mx)
        work = jnp.where(lane == ix, -jnp.inf, work)
    ex = [jnp.exp(v - val[0]) for v in val]
    tot = ex[0] + ex[1] + ex[2] + ex[3]
    hot = [jnp.where(lane == ix, 1.0, 0.0) for ix in idx]
    multi = hot[0] + hot[1] + hot[2] + hot[3]
    r = lax.broadcasted_iota(jnp.int32, (tm, tm), 0)
    c = lax.broadcasted_iota(jnp.int32, (tm, tm), 1)
    lower = jnp.where(r > c, 1.0, 0.0).astype(BF16)
    before = jnp.dot(lower, multi.astype(BF16), preferred_element_type=F32) + carry_s[0:1, :]
    ri = jnp.zeros((tm, LANES), F32)
    rg = jnp.zeros((tm, LANES), F32)
    for j in range(TOP_K):
        rank = jnp.sum(before * hot[j], axis=-1, keepdims=True)
        ri = ri + jnp.where(lane == float(j), idx[j], 0.0) + jnp.where(lane == float(TOP_K + j), rank, 0.0)
        rg = rg + jnp.where(lane == float(j), ex[j] / tot, 0.0)
    ri_ref[...] = ri.astype(jnp.int32)
    rg_ref[...] = rg
    carry_s[...] = carry_s[...] + jnp.sum(multi, axis=0, keepdims=True)
    cnt_ref[...] = carry_s[...]


def _merge(x2d, proj, o_a, o_b, kvm, wout, l1w, l1b, rw, rb, batch, seq, tm):
    t = batch * seq
    per_b = seq // tm
    nmem = kvm.shape[0] // batch
    tile = lambda colblk: pl.BlockSpec((tm, D_MODEL), lambda i: (i, colblk))
    full = lambda shape: pl.BlockSpec(shape, lambda i: (0, 0))
    return pl.pallas_call(
        functools.partial(_merge_kernel, tm=tm),
        out_shape=(jax.ShapeDtypeStruct((t, D_MODEL), F32),
                   jax.ShapeDtypeStruct((t, LANES), jnp.int32),
                   jax.ShapeDtypeStruct((t, LANES), F32),
                   jax.ShapeDtypeStruct((8, LANES), F32)),
        grid=(t // tm,),
        in_specs=[tile(0), tile(COL_GA), tile(COL_GB), tile(COL_GM), tile(COL_MQ), tile(0), tile(0),
                  pl.BlockSpec((nmem, 2 * D_MODEL), lambda i: (i // per_b, 0)),
                  full((D_MODEL, D_MODEL)), full((1, D_MODEL)), full((1, D_MODEL)),
                  full((D_MODEL, LANES)), full((1, LANES))],
        out_specs=(tile(0), pl.BlockSpec((tm, LANES), lambda i: (i, 0)),
                   pl.BlockSpec((tm, LANES), lambda i: (i, 0)), full((8, LANES))),
        scratch_shapes=[pltpu.VMEM((8, LANES), F32)],
        compiler_params=_params(("arbitrary",)),
        name="merge_router",
    )(x2d, proj, proj, proj, proj, o_a, o_b, kvm, wout, l1w, l1b, rw, rb)


def _dispatch_kernel(zblk_ref, zflag_ref, dest_hbm, x_ref, xs_hbm, idx_s, zbuf, sem_i, sem_r, sem_z, *, tm):
    i = pl.program_id(0)
    n = TOP_K * tm

    @pl.when(i == 0)
    def _():
        zbuf[...] = jnp.zeros_like(zbuf)

        def zero_copy(e):
            start = pl.multiple_of(zblk_ref[e] * EXPERT_BLOCK, EXPERT_BLOCK)
            return pltpu.make_async_copy(zbuf, xs_hbm.at[pl.ds(start, EXPERT_BLOCK)], sem_z)

        for e in range(2 * N_EXPERTS):
            @pl.when(zflag_ref[e] == 1)
            def _():
                zero_copy(e).start()
        for e in range(2 * N_EXPERTS):
            @pl.when(zflag_ref[e] == 1)
            def _():
                zero_copy(e).wait()

    cp = pltpu.make_async_copy(dest_hbm.at[pl.ds(pl.multiple_of(i * n, n), n)], idx_s, sem_i)
    cp.start()
    cp.wait()

    def body(g, carry):
        for jj in range(SUBLANES):
            src = x_ref.at[g, pl.ds(jj, 1)]
            for j in range(TOP_K):
                d = idx_s[g * (SUBLANES * TOP_K) + jj * TOP_K + j]
                pltpu.make_async_copy(src, xs_hbm.at[pl.ds(d, 1)], sem_r).start(priority=j % 2)
        return carry

    lax.fori_loop(0, tm // SUBLANES, body, 0)
    for _ in range(TOP_K):
        pltpu.make_async_copy(xs_hbm.at[pl.ds(0, tm)], xs_hbm.at[pl.ds(0, tm)], sem_r).wait()


def _dispatch(last_blk, has_blk, dest, x1, slots, tm):
    t = x1.shape[0]
    grid_spec = pltpu.PrefetchScalarGridSpec(
        num_scalar_prefetch=2,
        grid=(t // tm,),
        in_specs=[pl.BlockSpec(memory_space=pl.ANY),
                  pl.BlockSpec((tm // SUBLANES, SUBLANES, D_MODEL), lambda i, lb, hb: (i, 0, 0))],
        out_specs=pl.BlockSpec(memory_space=pl.ANY),
        scratch_shapes=[pltpu.SMEM((TOP_K * tm,), jnp.int32),
                        pltpu.VMEM((EXPERT_BLOCK, D_MODEL), F32),
                        pltpu.SemaphoreType.DMA, pltpu.SemaphoreType.DMA, pltpu.SemaphoreType.DMA],
    )
    return pl.pallas_call(
        functools.partial(_dispatch_kernel, tm=tm),
        out_shape=jax.ShapeDtypeStruct((slots, D_MODEL), F32),
        grid_spec=grid_spec,
        compiler_params=_params(("arbitrary",)),
        name="moe_dispatch",
    )(last_blk, has_blk, dest, x1.reshape(t // SUBLANES, SUBLANES, D_MODEL))


def _expert_kernel(be_ref, nu_ref, xs_ref, wgu_ref, bgu_ref, wd_ref, bd_ref, ys_ref):
    del be_ref
    i = pl.program_id(0)

    @pl.when(i < nu_ref[0])
    def _():
        h = jnp.dot(xs_ref[...].astype(BF16), wgu_ref[0], preferred_element_type=F32) + bgu_ref[0]
        g = jnp.minimum(h[:, :D_FF], SWIGLU_LIMIT)
        u = jnp.clip(h[:, D_FF:], -SWIGLU_LIMIT, SWIGLU_LIMIT)
        act = (u + 1.0) * g * jax.nn.sigmoid(SWIGLU_ALPHA * g)
        ys_ref[...] = jnp.dot(act.astype(BF16), wd_ref[0], preferred_element_type=F32) + bd_ref[0]

    @pl.when(i >= nu_ref[0])
    def _():
        ys_ref[...] = jnp.zeros_like(ys_ref)


def _experts(block_e, n_used, xs, wgu, bgu, wd, bd):
    slots = xs.shape[0]
    nb = slots // EXPERT_BLOCK
    grid_spec = pltpu.PrefetchScalarGridSpec(
        num_scalar_prefetch=2,
        grid=(nb,),
        in_specs=[pl.BlockSpec((EXPERT_BLOCK, D_MODEL), lambda i, be, nu: (jnp.minimum(i, nu[0] - 1), 0)),
                  pl.BlockSpec((1, D_MODEL, 2 * D_FF), lambda i, be, nu: (be[i], 0, 0)),
                  pl.BlockSpec((1, 1, 2 * D_FF), lambda i, be, nu: (be[i], 0, 0)),
                  pl.BlockSpec((1, D_FF, D_MODEL), lambda i, be, nu: (be[i], 0, 0)),
                  pl.BlockSpec((1, 1, D_MODEL), lambda i, be, nu: (be[i], 0, 0))],
        out_specs=pl.BlockSpec((EXPERT_BLOCK, D_MODEL), lambda i, be, nu: (i, 0)),
    )
    return pl.pallas_call(
        _expert_kernel,
        out_shape=jax.ShapeDtypeStruct((slots, D_MODEL), F32),
        grid_spec=grid_spec,
        compiler_params=_params(("arbitrary",)),
        name="moe_experts",
    )(block_e, n_used, xs, wgu, bgu, wd, bd)


def _combine_kernel(dest_hbm, x1_ref, rg_ref, ys_hbm, l2w_ref, l2b_ref, o_ref, idx_s, buf, sem_i, sem_r, *, tm):
    i = pl.program_id(0)
    nsteps = pl.num_programs(0)
    n = TOP_K * tm

    def issue(step, slot):
        cp = pltpu.make_async_copy(dest_hbm.at[pl.ds(pl.multiple_of(step * n, n), n)], idx_s, sem_i)
        cp.start()
        cp.wait()

        def body(g, carry):
            for jj in range(SUBLANES):
                for j in range(TOP_K):
                    d = idx_s[g * (SUBLANES * TOP_K) + jj * TOP_K + j]
                    pltpu.make_async_copy(ys_hbm.at[pl.ds(d, 1)], buf.at[slot, j, g, pl.ds(jj, 1)],
                                          sem_r.at[slot]).start(priority=j % 2)
            return carry

        lax.fori_loop(0, tm // SUBLANES, body, 0)

    @pl.when(i == 0)
    def _():
        issue(0, 0)

    @pl.when(i + 1 < nsteps)
    def _():
        issue(i + 1, (i + 1) % 2)

    slot = i % 2
    for _ in range(TOP_K):
        pltpu.make_async_copy(ys_hbm.at[pl.ds(0, tm)], o_ref, sem_r.at[slot]).wait()
    rg = rg_ref[...]
    moe = rg[:, 0:1] * buf[slot, 0].reshape(tm, D_MODEL)
    for j in range(1, TOP_K):
        moe = moe + rg[:, j:j + 1] * buf[slot, j].reshape(tm, D_MODEL)
    o_ref[...] = _layernorm(DN_ALPHA * x1_ref[...] + moe, l2w_ref[...], l2b_ref[...])


def _combine(dest, x1, rg, ys, l2w, l2b, tm):
    t = x1.shape[0]
    full = lambda shape: pl.BlockSpec(shape, lambda i: (0, 0))
    return pl.pallas_call(
        functools.partial(_combine_kernel, tm=tm),
        out_shape=jax.ShapeDtypeStruct((t, D_MODEL), F32),
        grid=(t // tm,),
        in_specs=[pl.BlockSpec(memory_space=pl.ANY),
                  pl.BlockSpec((tm, D_MODEL), lambda i: (i, 0)),
                  pl.BlockSpec((tm, LANES), lambda i: (i, 0)),
                  pl.BlockSpec(memory_space=pl.ANY),
                  full((1, D_MODEL)), full((1, D_MODEL))],
        out_specs=pl.BlockSpec((tm, D_MODEL), lambda i: (i, 0)),
        scratch_shapes=[pltpu.SMEM((TOP_K * tm,), jnp.int32),
                        pltpu.VMEM((2, TOP_K, tm // SUBLANES, SUBLANES, D_MODEL), F32),
                        pltpu.SemaphoreType.DMA, pltpu.SemaphoreType.DMA((2,))],
        compiler_params=_params(("arbitrary",)),
        name="moe_combine",
    )(dest, x1, rg, ys, l2w, l2b)


def _prep_w_in(w):
    zeros = lambda n: jnp.zeros((D_MODEL, n), w.dtype)
    kr = w[:, 5760:5824]
    kr_sw = jnp.concatenate([kr[:, MLA_ROPE // 2:], kr[:, :MLA_ROPE // 2]], axis=1)
    parts = [w[:, 6848:9920], w[:, 0:5120], w[:, 5824:6848],
             w[:, 5120:5504], zeros(128),
             w[:, 5504:5760], kr, zeros(64), kr_sw, zeros(64)]
    return jnp.concatenate(parts, axis=1).astype(BF16)


def _prep_w_uq(w):
    w3 = w.reshape(MLA_Q_LORA, MLA_HEADS, MLA_NOPE + MLA_ROPE)
    rope = w3[:, :, MLA_NOPE:]
    rope_sw = jnp.concatenate([rope[:, :, MLA_ROPE // 2:], rope[:, :, :MLA_ROPE // 2]], axis=-1)
    pad = jnp.zeros((MLA_Q_LORA, MLA_HEADS, QK_PAD - MLA_NOPE - MLA_ROPE), w.dtype)
    main = jnp.concatenate([w3, pad], axis=-1).reshape(MLA_Q_LORA, MLA_HEADS * QK_PAD)
    swp = jnp.concatenate([rope_sw, pad], axis=-1).reshape(MLA_Q_LORA, MLA_HEADS * LANES)
    both = jnp.concatenate([main, swp], axis=1)
    return jnp.pad(both, ((0, 512 - MLA_Q_LORA), (0, 0))).astype(BF16)


def _rope_tables(seq):
    inv_freq = ROPE_THETA ** (-jnp.arange(0, MLA_ROPE, 2, dtype=F32) / MLA_ROPE)
    ang = jnp.arange(seq, dtype=F32)[:, None] * inv_freq[None, :]
    cos, sin = jnp.cos(ang), jnp.sin(ang)
    pad = jnp.zeros((seq, LANES - MLA_ROPE), F32)
    return (jnp.concatenate([cos, cos, pad], axis=1), jnp.concatenate([-sin, sin, pad], axis=1))


def _tile(n, pref):
    return pref if n % pref == 0 else n


def _layer(x, mem, wts):
    batch, seq, _ = x.shape
    t = batch * seq
    x2d = x.reshape(t, D_MODEL)

    proj = _matmul(x2d, wts["w_in"], BF16, _tile(t, 1024), 1024, "in_proj")
    o_a = _hgrn(proj, wts["lb"], wts["hgrn_norm_w"], batch, seq, _tile(seq, 512))

    cos, sin = _rope_tables(seq)
    q, k, v = _mla_prep(proj, wts["qnw"], wts["kvnw"], wts["w_uq"], wts["w_uk"], wts["w_uv"],
                        cos, sin, batch, seq, _tile(seq, 512))
    o_b = _flash(q, k, v, batch, seq, _tile(seq, 1024), _tile(seq, 1024))

    nmem = mem.shape[1]
    kvm = _matmul(mem.reshape(batch * nmem, D_MODEL), wts["mem_w_kv"], BF16,
                  _tile(batch * nmem, 512), 1024, "mem_kv")

    tm = _tile(seq, 512)
    x1, ri, rg, cnt = _merge(x2d, proj, o_a, o_b, kvm, wts["w_out"], wts["ln1_w"], wts["ln1_b"],
                             wts["router_w"], wts["router_b"], batch, seq, tm)

    idx = ri[:, :TOP_K]
    rank = ri[:, TOP_K:2 * TOP_K]
    counts = cnt[0, :N_EXPERTS].astype(jnp.int32)
    padded = (counts + EXPERT_BLOCK - 1) // EXPERT_BLOCK * EXPERT_BLOCK
    pad_end = jnp.cumsum(padded)
    pad_start = pad_end - padded
    dest = (pad_start[idx] + rank).reshape(t * TOP_K).astype(jnp.int32)
    nb = t * TOP_K // EXPERT_BLOCK + N_EXPERTS
    blk_start = jnp.arange(nb, dtype=jnp.int32) * EXPERT_BLOCK
    block_e = jnp.minimum(jnp.sum((pad_end[None, :] <= blk_start[:, None]).astype(jnp.int32), axis=1),
                          N_EXPERTS - 1).astype(jnp.int32)
    n_used = (pad_end[-1:] // EXPERT_BLOCK).astype(jnp.int32)
    tail = n_used[0] + jnp.arange(N_EXPERTS, dtype=jnp.int32)
    zero_blk = jnp.concatenate([jnp.maximum(pad_end // EXPERT_BLOCK - 1, 0), jnp.minimum(tail, nb - 1)])
    zero_flag = jnp.concatenate([padded > 0, tail < nb]).astype(jnp.int32)

    tmd = _tile(t, 512)
    xs = _dispatch(zero_blk.astype(jnp.int32), zero_flag, dest, x1, nb * EXPERT_BLOCK, tmd)
    ys = _experts(block_e, n_used, xs, wts["exp_w_gu"], wts["exp_b_gu"], wts["exp_w_down"], wts["exp_b_down"])
    y = _combine(dest, x1, rg, ys, wts["ln2_w"], wts["ln2_b"], _tile(t, 256))
    return y.reshape(batch, seq, D_MODEL)


def kernel(x_prompt, x_sample, mem_prompt, mem_sample, w_in, hgrn_lb_logits, hgrn_norm_w,
           mla_q_norm_w, mla_w_uq, mla_kv_norm_w, mla_w_uk, mla_w_uv, mem_w_kv, w_out,
           ln1_w, ln1_b, router_w, router_b, exp_w_gu, exp_b_gu, exp_w_down, exp_b_down,
           ln2_w, ln2_b):
    depth = w_in.shape[0]
    gamma = jax.nn.softmax(hgrn_lb_logits.astype(F32), axis=1)
    cum = jnp.cumsum(gamma, axis=1)
    lb_all = cum[:, 1:] - cum[:, :1]
    y_prompt, y_sample = x_prompt, x_sample
    for l in range(depth):
        row = lambda a: a[l].reshape(1, -1).astype(F32)
        wts = {
            "w_in": _prep_w_in(w_in[l]),
            "lb": lb_all[:, l],
            "hgrn_norm_w": hgrn_norm_w[l].astype(F32),
            "qnw": jnp.pad(row(mla_q_norm_w), ((0, 0), (0, 512 - MLA_Q_LORA))),
            "kvnw": row(mla_kv_norm_w),
            "w_uq": _prep_w_uq(mla_w_uq[l]),
            "w_uk": mla_w_uk[l].astype(BF16),
            "w_uv": mla_w_uv[l].astype(BF16),
            "mem_w_kv": mem_w_kv[l].astype(BF16),
            "w_out": w_out[l].astype(BF16),
            "ln1_w": row(ln1_w), "ln1_b": row(ln1_b),
            "router_w": jnp.pad(router_w[l].astype(F32), ((0, 0), (0, LANES - N_EXPERTS))),
            "router_b": jnp.pad(row(router_b), ((0, 0), (0, LANES - N_EXPERTS)), constant_values=-jnp.inf),
            "exp_w_gu": exp_w_gu[l].astype(BF16),
            "exp_b_gu": exp_b_gu[l].reshape(N_EXPERTS, 1, 2 * D_FF).astype(F32),
            "exp_w_down": exp_w_down[l].astype(BF16),
            "exp_b_down": exp_b_down[l].reshape(N_EXPERTS, 1, D_MODEL).astype(F32),
            "ln2_w": row(ln2_w), "ln2_b": row(ln2_b),
        }
        y_prompt = _layer(y_prompt, mem_prompt, wts)
        y_sample = _layer(y_sample, mem_sample, wts)
    return (y_prompt, y_sample)
```

```python
import functools

import numpy as np
import jax
import jax.numpy as jnp
from jax import lax
from jax.experimental import pallas as pl
from jax.experimental.pallas import tpu as pltpu
from jax.experimental.pallas import tpu_sc as plsc

F32 = jnp.float32
BF16 = jnp.bfloat16

D_MODEL = 1024
HGRN_HEADS = 8
HGRN_DK = 128
MLA_HEADS = 8
MLA_Q_LORA = 384
MLA_KV_LORA = 256
MLA_NOPE = 128
MLA_ROPE = 64
MLA_V = 128
ROPE_THETA = 10000.0
MEM_HEADS = 4
MEM_HEAD_DIM = D_MODEL // MEM_HEADS
N_EXPERTS = 32
TOP_K = 4
D_FF = D_MODEL
SWIGLU_LIMIT = 7.0
SWIGLU_ALPHA = 1.702
DN_ALPHA = 2.0 ** 0.25
LN_EPS = 1e-5
RMS_EPS = 1e-6

LANES = 128
SUBLANES = 8
QK_PAD = 256
V_PAD = 256
LOG2E = 1.4426950408889634
HGRN_CHUNK = 64
HGRN_SAFE_RANGE = 160.0
EXPERT_BLOCK = 512
SC_CORES = 2
SC_SUBCORES = 16
SC_WORKERS = SC_CORES * SC_SUBCORES
SC_GATHER_ROWS = 64
VMEM_LIMIT = 56 * 1024 * 1024

COL_GA, COL_GB, COL_GM, COL_Q, COL_ZF, COL_ZB, COL_I, COL_G, COL_MQ = range(9)
COL_DQ, COL_DKV = 18, 19
IN_COLS_PAD = 10240

NT_DIMS = (((1,), (1,)), ((), ()))
TN_DIMS = (((0,), (0,)), ((), ()))


def _params(sem, vmem=VMEM_LIMIT):
    return pltpu.CompilerParams(dimension_semantics=sem, vmem_limit_bytes=vmem)


def _mm_kernel(x_ref, w_ref, o_ref, xb_ref):
    @pl.when(pl.program_id(1) == 0)
    def _():
        xb_ref[...] = x_ref[...].astype(BF16)

    o_ref[...] = jnp.dot(xb_ref[...], w_ref[...], preferred_element_type=F32).astype(o_ref.dtype)


def _matmul(x, w, out_dtype, tm, tn, name):
    m, k = x.shape
    n = w.shape[1]
    return pl.pallas_call(
        _mm_kernel,
        out_shape=jax.ShapeDtypeStruct((m, n), out_dtype),
        grid=(m // tm, n // tn),
        in_specs=[pl.BlockSpec((tm, k), lambda i, j: (i, 0)),
                  pl.BlockSpec((k, tn), lambda i, j: (0, j))],
        out_specs=pl.BlockSpec((tm, tn), lambda i, j: (i, j)),
        scratch_shapes=[pltpu.VMEM((tm, k), BF16)],
        compiler_params=_params(("arbitrary", "arbitrary")),
        name=name,
    )(x, w)


def _hgrn_kernel(*refs, reverse, epilogue, sc):
    if epilogue:
        lb_ref, q_ref, z_ref, v_ref, of_ref, g_ref, nw_ref, o_ref, st_ref, kk_s, b_s, vf_s, os_s = refs
    else:
        lb_ref, q_ref, z_ref, v_ref, o_ref, st_ref, kk_s, b_s, vf_s = refs
    C = HGRN_CHUNK
    nch = sc // C

    @pl.when(pl.program_id(1) == 0)
    def _():
        st_ref[...] = jnp.zeros_like(st_ref)

    lb = lb_ref[...]
    row = lax.broadcasted_iota(jnp.int32, (C, C), 0)
    col = lax.broadcasted_iota(jnp.int32, (C, C), 1)
    tri = (row <= col) if reverse else (row >= col)
    trib = jnp.where(tri, 1.0, 0.0).astype(BF16)

    minb = None
    for c in range(nch):
        rows = slice(c * C, (c + 1) * C)
        z = z_ref[rows, :].astype(F32)
        gate = (1.0 - lb) * jax.nn.sigmoid(z)
        lf = jnp.log(lb + gate)
        kk_s[rows, :] = (1.0 - lb) - gate
        hi = lf.astype(BF16)
        r1 = lf - hi.astype(F32)
        mid = r1.astype(BF16)
        lo = (r1 - mid.astype(F32)).astype(BF16)
        b = (jnp.dot(trib, hi, preferred_element_type=F32)
             + jnp.dot(trib, mid, preferred_element_type=F32)
             + jnp.dot(trib, lo, preferred_element_type=F32))
        b_s[rows, :] = b
        mb = jnp.min(b)
        minb = mb if minb is None else jnp.minimum(minb, mb)

    rid = lax.broadcasted_iota(jnp.int32, (C, 1), 0)

    def chunk(i, carry, fast):
        c = (nch - 1 - i) if reverse else i
        r0 = pl.multiple_of(c * C, C)
        rows = pl.ds(r0, C)
        for h in range(HGRN_HEADS):
            cols = slice(h * HGRN_DK, (h + 1) * HGRN_DK)
            q = q_ref[rows, cols].astype(F32)
            kk = kk_s[rows, cols]
            b = b_s[rows, cols]
            v = v_ref[rows, cols]
            bl = b[0:1, :] if reverse else b[C - 1:C, :]
            if fast:
                bm = 0.5 * bl
                qd = (q * jnp.exp(b - bm)).astype(BF16)
                kd = (kk * jnp.exp(bm - b)).astype(BF16)
                s = lax.dot_general(qd, kd, NT_DIMS, preferred_element_type=F32)
                s = jnp.where(tri, s, 0.0).astype(BF16)
                o = jnp.dot(s, v, preferred_element_type=F32)
            else:
                def sbody(g_i, o_acc):
                    grp = pl.ds(pl.multiple_of(r0 + g_i * SUBLANES, SUBLANES), SUBLANES)
                    b8 = b_s[grp, cols]
                    k8 = kk_s[grp, cols]
                    v8 = vf_s[grp, cols]
                    for jj in range(SUBLANES):
                        s_i = g_i * SUBLANES + jj
                        w = q * k8[jj:jj + 1, :] * jnp.exp(jnp.minimum(b - b8[jj:jj + 1, :], 0.0))
                        scol = jnp.sum(w, axis=-1, keepdims=True)
                        keep = (rid <= s_i) if reverse else (rid >= s_i)
                        o_acc = o_acc + jnp.where(keep, scol, 0.0) * v8[jj:jj + 1, :]
                    return o_acc
                o = lax.fori_loop(0, C // SUBLANES, sbody, jnp.zeros((C, HGRN_DK), F32))
            st = st_ref[h]
            qi = (q * jnp.exp(b)).astype(BF16)
            o = o + lax.dot_general(qi, st.astype(BF16), NT_DIMS, preferred_element_type=F32)
            ke = (kk * jnp.exp(bl - b)).astype(BF16)
            upd = lax.dot_general(v, ke, TN_DIMS, preferred_element_type=F32)
            st_ref[h] = st * jnp.exp(bl) + upd
            if epilogue:
                os_s[rows, cols] = o + of_ref[rows, cols].astype(F32)
            else:
                o_ref[rows, cols] = o.astype(o_ref.dtype)
        return carry

    safe = minb >= -HGRN_SAFE_RANGE

    @pl.when(safe)
    def _():
        lax.fori_loop(0, nch, functools.partial(chunk, fast=True), 0, unroll=2)

    @pl.when(jnp.logical_not(safe))
    def _():
        vf_s[...] = v_ref[...].astype(F32)
        lax.fori_loop(0, nch, functools.partial(chunk, fast=False), 0)

    if epilogue:
        nw = nw_ref[...]
        for h in range(HGRN_HEADS):
            cols = slice(h * HGRN_DK, (h + 1) * HGRN_DK)
            os = os_s[:, cols]
            ms = jnp.mean(os * os, axis=-1, keepdims=True)
            y = os * lax.rsqrt(ms + RMS_EPS) * nw
            g = g_ref[:, cols].astype(F32)
            o_ref[:, cols] = (y * (g * jax.nn.sigmoid(g))).astype(o_ref.dtype)


def _hgrn(proj, lb, norm_w, batch, seq, sc):
    t = batch * seq
    ns = seq // sc
    blk = (sc, D_MODEL)

    def spec(colblk, reverse):
        if reverse:
            return pl.BlockSpec(blk, lambda b, n: (b * ns + ns - 1 - n, colblk))
        return pl.BlockSpec(blk, lambda b, n: (b * ns + n, colblk))

    def row_spec(reverse):
        if reverse:
            return pl.BlockSpec(blk, lambda b, n: (b * ns + ns - 1 - n, 0))
        return pl.BlockSpec(blk, lambda b, n: (b * ns + n, 0))

    lb_spec = pl.BlockSpec((1, D_MODEL), lambda b, n: (0, 0))
    common_scratch = [pltpu.VMEM((HGRN_HEADS, HGRN_DK, HGRN_DK), F32),
                      pltpu.VMEM(blk, F32), pltpu.VMEM(blk, F32), pltpu.VMEM(blk, F32)]
    o_f = pl.pallas_call(
        functools.partial(_hgrn_kernel, reverse=False, epilogue=False, sc=sc),
        out_shape=jax.ShapeDtypeStruct((t, D_MODEL), BF16),
        grid=(batch, ns),
        in_specs=[lb_spec, spec(COL_Q, False), spec(COL_ZF, False), spec(COL_I, False)],
        out_specs=row_spec(False),
        scratch_shapes=common_scratch,
        compiler_params=_params(("arbitrary", "arbitrary")),
        name="hgrn_fwd",
    )(lb[0:1], proj, proj, proj)
    o_a = pl.pallas_call(
        functools.partial(_hgrn_kernel, reverse=True, epilogue=True, sc=sc),
        out_shape=jax.ShapeDtypeStruct((t, D_MODEL), BF16),
        grid=(batch, ns),
        in_specs=[lb_spec, spec(COL_Q, True), spec(COL_ZB, True), spec(COL_I, True),
                  row_spec(True), spec(COL_G, True),
                  pl.BlockSpec((1, HGRN_DK), lambda b, n: (0, 0))],
        out_specs=row_spec(True),
        scratch_shapes=common_scratch + [pltpu.VMEM(blk, F32)],
        compiler_params=_params(("arbitrary", "arbitrary")),
        name="hgrn_bwd",
    )(lb[1:2], proj, proj, proj, o_f, proj, norm_w.reshape(1, HGRN_DK))
    return o_a


def _mla_prep_kernel(dq_ref, dkv_ref, qnw_ref, kvnw_ref, wq_ref, wk_ref, wv_ref, cos_ref, sin_ref,
                     q_ref, k_ref, v_ref):
    scale = (MLA_NOPE + MLA_ROPE) ** -0.5 * LOG2E
    cos = cos_ref[...]
    sin = sin_ref[...]
    dq = dq_ref[...].astype(F32)
    ms = jnp.sum(dq * dq, axis=-1, keepdims=True) * (1.0 / MLA_Q_LORA)
    cq = (dq * lax.rsqrt(ms + RMS_EPS) * qnw_ref[...]).astype(BF16)
    qa = jnp.dot(cq, wq_ref[...], preferred_element_type=F32)
    for h in range(MLA_HEADS):
        base = h * QK_PAD
        q_ref[:, base:base + MLA_NOPE] = (qa[:, base:base + MLA_NOPE] * scale).astype(BF16)
        rp = qa[:, base + MLA_NOPE:base + QK_PAD]
        sw = qa[:, MLA_HEADS * QK_PAD + h * LANES:MLA_HEADS * QK_PAD + (h + 1) * LANES]
        q_ref[:, base + MLA_NOPE:base + QK_PAD] = ((rp * cos + sw * sin) * scale).astype(BF16)
    dkv = dkv_ref[...].astype(F32)
    ckv = dkv[:, :MLA_KV_LORA]
    msk = jnp.mean(ckv * ckv, axis=-1, keepdims=True)
    cn = (ckv * lax.rsqrt(msk + RMS_EPS) * kvnw_ref[...]).astype(BF16)
    kn = jnp.dot(cn, wk_ref[...], preferred_element_type=F32)
    vv = jnp.dot(cn, wv_ref[...], preferred_element_type=F32).astype(BF16)
    ones_col = jnp.where(lax.broadcasted_iota(jnp.int32, (vv.shape[0], LANES), 1) == 0, 1.0, 0.0).astype(BF16)
    for h in range(MLA_HEADS):
        v_ref[:, h * V_PAD:h * V_PAD + MLA_V] = vv[:, h * MLA_V:(h + 1) * MLA_V]
        v_ref[:, h * V_PAD + MLA_V:(h + 1) * V_PAD] = ones_col
    kr = (dkv[:, MLA_KV_LORA:MLA_KV_LORA + LANES] * cos
          + dkv[:, MLA_KV_LORA + LANES:MLA_KV_LORA + 2 * LANES] * sin).astype(BF16)
    for h in range(MLA_HEADS):
        base = h * QK_PAD
        k_ref[:, base:base + MLA_NOPE] = kn[:, h * MLA_NOPE:(h + 1) * MLA_NOPE].astype(BF16)
        k_ref[:, base + MLA_NOPE:base + QK_PAD] = kr


def _mla_prep(proj, qnw, kvnw, wq, wk, wv, cos, sin, batch, seq, tm):
    t = batch * seq
    npos = seq // tm
    full = lambda shape: pl.BlockSpec(shape, lambda i: (0, 0))
    return pl.pallas_call(
        _mla_prep_kernel,
        out_shape=(jax.ShapeDtypeStruct((t, MLA_HEADS * QK_PAD), BF16),
                   jax.ShapeDtypeStruct((t, MLA_HEADS * QK_PAD), BF16),
                   jax.ShapeDtypeStruct((t, MLA_HEADS * V_PAD), BF16)),
        grid=(t // tm,),
        in_specs=[pl.BlockSpec((tm, 512), lambda i: (i, COL_DQ)),
                  pl.BlockSpec((tm, 512), lambda i: (i, COL_DKV)),
                  full((1, 512)), full((1, MLA_KV_LORA)),
                  full(wq.shape), full(wk.shape), full(wv.shape),
                  pl.BlockSpec((tm, LANES), lambda i: (i % npos, 0)),
                  pl.BlockSpec((tm, LANES), lambda i: (i % npos, 0))],
        out_specs=(pl.BlockSpec((tm, MLA_HEADS * QK_PAD), lambda i: (i, 0)),
                   pl.BlockSpec((tm, MLA_HEADS * QK_PAD), lambda i: (i, 0)),
                   pl.BlockSpec((tm, MLA_HEADS * V_PAD), lambda i: (i, 0))),
        compiler_params=_params(("arbitrary",)),
        name="mla_prep",
    )(proj, proj, qnw, kvnw, wq, wk, wv, cos, sin)


def _flash_kernel(q_ref, k_ref, v_ref, o_ref, m_s, acc_s, s_buf, *, bk, nk):
    m_s[...] = jnp.full(m_s.shape, -jnp.inf, F32)
    acc_s[...] = jnp.zeros(acc_s.shape, F32)
    nt = bk // LANES
    bq = q_ref.shape[0]
    nsplit = 2 if bq % 256 == 0 else 1
    hq = bq // nsplit
    qrs = [slice(hf * hq, (hf + 1) * hq) for hf in range(nsplit)]

    def scores(j, slot):
        rows = pl.ds(pl.multiple_of(j * bk, bk), bk)
        for qr in qrs:
            s_buf[slot, qr, :] = lax.dot_general(q_ref[qr, :], k_ref[rows, :], NT_DIMS,
                                                 preferred_element_type=F32)

    def consume(j, slot):
        rows = pl.ds(pl.multiple_of(j * bk, bk), bk)
        ps, alphas = [], []
        for qr in qrs:
            tiles = [s_buf[slot, qr, t * LANES:(t + 1) * LANES] for t in range(nt)]
            tmax = tiles[0]
            for t in range(1, nt):
                tmax = jnp.maximum(tmax, tiles[t])
            m_prev = m_s[qr, :]
            m_new = jnp.maximum(m_prev, jnp.max(tmax, axis=-1, keepdims=True))
            alphas.append(jnp.exp2(m_prev - m_new))
            ps.append(jnp.concatenate([jnp.exp2(tl - m_new).astype(BF16) for tl in tiles], axis=1))
            m_s[qr, :] = m_new
        for qr, p, a in zip(qrs, ps, alphas):
            pv = jnp.dot(p, v_ref[rows, :], preferred_element_type=F32)
            acc_s[qr, :] = jnp.concatenate([a, a], axis=1) * acc_s[qr, :] + pv

    scores(0, 0)
    npairs = (nk - 1) // 2

    def body(i, carry):
        j = 2 * i
        scores(j + 1, 1)
        consume(j, 0)
        scores(j + 2, 0)
        consume(j + 1, 1)
        return carry

    lax.fori_loop(0, npairs, body, 0)
    if nk - 2 * npairs == 2:
        scores(nk - 1, 1)
        consume(nk - 2, 0)
        consume(nk - 1, 1)
    else:
        consume(nk - 1, 0)
    acc = acc_s[...]
    o_ref[...] = (acc[:, :MLA_V] / acc[:, MLA_V:MLA_V + 1]).astype(o_ref.dtype)


def _flash(q, k, v, batch, seq, bq, bk):
    t = batch * seq
    nq = seq // bq
    return pl.pallas_call(
        functools.partial(_flash_kernel, bk=bk, nk=seq // bk),
        out_shape=jax.ShapeDtypeStruct((t, MLA_HEADS * MLA_V), BF16),
        grid=(batch, MLA_HEADS, nq),
        in_specs=[pl.BlockSpec((bq, QK_PAD), lambda b, h, i: (b * nq + i, h)),
                  pl.BlockSpec((seq, QK_PAD), lambda b, h, i: (b, h)),
                  pl.BlockSpec((seq, V_PAD), lambda b, h, i: (b, h))],
        out_specs=pl.BlockSpec((bq, MLA_V), lambda b, h, i: (b * nq + i, h)),
        scratch_shapes=[pltpu.VMEM((bq, LANES), F32), pltpu.VMEM((bq, V_PAD), F32),
                        pltpu.VMEM((2, bq, bk), F32)],
        compiler_params=_params(("arbitrary", "arbitrary", "arbitrary")),
        name="mla_flash",
    )(q, k, v)


def _layernorm(y, w, b):
    mu = jnp.mean(y, axis=-1, keepdims=True)
    yc = y - mu
    var = jnp.mean(yc * yc, axis=-1, keepdims=True)
    return yc * lax.rsqrt(var + LN_EPS) * w + b


def _merge_kernel(x_ref, ga_ref, gb_ref, gm_ref, mq_ref, oa_ref, ob_ref, kvm_ref, wout_ref,
                  l1w_ref, l1b_ref, rw_ref, rb_ref,
                  x1_ref, ri_ref, rg_ref, cnt_ref, carry_s, *, tm):
    @pl.when(pl.program_id(0) == 0)
    def _():
        carry_s[...] = jnp.zeros_like(carry_s)

    parts = []
    for h in range(MEM_HEADS):
        cols = slice(h * MEM_HEAD_DIM, (h + 1) * MEM_HEAD_DIM)
        kh = kvm_ref[:, cols]
        vh = kvm_ref[:, D_MODEL + h * MEM_HEAD_DIM:D_MODEL + (h + 1) * MEM_HEAD_DIM]
        s = lax.dot_general(mq_ref[:, cols], kh, NT_DIMS, preferred_element_type=F32) * (MEM_HEAD_DIM ** -0.5)
        s = s - jnp.max(s, axis=-1, keepdims=True)
        p = jnp.exp(s)
        p = p / jnp.sum(p, axis=-1, keepdims=True)
        parts.append(jnp.dot(p.astype(BF16), vh, preferred_element_type=F32))
    om = jnp.concatenate(parts, axis=1)

    merged = (jax.nn.sigmoid(ga_ref[...].astype(F32)) * oa_ref[...].astype(F32)
              + jax.nn.sigmoid(gb_ref[...].astype(F32)) * ob_ref[...].astype(F32)
              + jax.nn.sigmoid(gm_ref[...].astype(F32)) * om)
    y = DN_ALPHA * x_ref[...] + jnp.dot(merged.astype(BF16), wout_ref[...], preferred_element_type=F32)
    x1 = _layernorm(y, l1w_ref[...], l1b_ref[...])
    x1_ref[...] = x1

    logits = jnp.dot(x1, rw_ref[...], preferred_element_type=F32, precision=lax.Precision.HIGHEST) + rb_ref[...]
    lane_i = lax.broadcasted_iota(jnp.int32, (tm, LANES), 1)
    lane = lane_i.astype(F32)
    work = logits
    idx, val = [], []
    for _ in range(TOP_K):
        mx = jnp.max(work, axis=-1, keepdims=True)
        ix = jnp.min(jnp.where(work == mx, lane, float(LANES)), axis=-1, keepdims=True)
        idx.append(ix)
        val.append(mx)
        work = jnp.where(lane == ix, -jnp.inf, work)
    ex = [jnp.exp(v - val[0]) for v in val]
    tot = ex[0] + ex[1] + ex[2] + ex[3]
    hot = [jnp.where(lane == ix, 1.0, 0.0) for ix in idx]
    multi = hot[0] + hot[1] + hot[2] + hot[3]
    r = lax.broadcasted_iota(jnp.int32, (tm, tm), 0)
    c = lax.broadcasted_iota(jnp.int32, (tm, tm), 1)
    lower = jnp.where(r > c, 1.0, 0.0).astype(BF16)
    before = jnp.dot(lower, multi.astype(BF16), preferred_element_type=F32) + carry_s[0:1, :]
    ri = jnp.zeros((tm, LANES), F32)
    rg = jnp.zeros((tm, LANES), F32)
    for j in range(TOP_K):
        rank = jnp.sum(before * hot[j], axis=-1, keepdims=True)
        ri = ri + jnp.where(lane == float(j), idx[j], 0.0) + jnp.where(lane == float(TOP_K + j), rank, 0.0)
        rg = rg + jnp.where(lane == float(j), ex[j] / tot, 0.0)
    ri_ref[...] = ri.astype(jnp.int32)
    rg_ref[...] = rg
    carry_s[...] = carry_s[...] + jnp.sum(multi, axis=0, keepdims=True)
    cnt_ref[...] = carry_s[...]


def _merge(x2d, proj, o_a, o_b, kvm, wout, l1w, l1b, rw, rb, batch, seq, tm):
    t = batch * seq
    per_b = seq // tm
    nmem = kvm.shape[0] // batch
    tile = lambda colblk: pl.BlockSpec((tm, D_MODEL), lambda i: (i, colblk))
    full = lambda shape: pl.BlockSpec(shape, lambda i: (0, 0))
    return pl.pallas_call(
        functools.partial(_merge_kernel, tm=tm),
        out_shape=(jax.ShapeDtypeStruct((t, D_MODEL), F32),
                   jax.ShapeDtypeStruct((t, LANES), jnp.int32),
                   jax.ShapeDtypeStruct((t, LANES), F32),
                   jax.ShapeDtypeStruct((8, LANES), F32)),
        grid=(t // tm,),
        in_specs=[tile(0), tile(COL_GA), tile(COL_GB), tile(COL_GM), tile(COL_MQ), tile(0), tile(0),
                  pl.BlockSpec((nmem, 2 * D_MODEL), lambda i: (i // per_b, 0)),
                  full((D_MODEL, D_MODEL)), full((1, D_MODEL)), full((1, D_MODEL)),
                  full((D_MODEL, LANES)), full((1, LANES))],
        out_specs=(tile(0), pl.BlockSpec((tm, LANES), lambda i: (i, 0)),
                   pl.BlockSpec((tm, LANES), lambda i: (i, 0)), full((8, LANES))),
        scratch_shapes=[pltpu.VMEM((8, LANES), F32)],
        compiler_params=_params(("arbitrary",)),
        name="merge_router",
    )(x2d, proj, proj, proj, proj, o_a, o_b, kvm, wout, l1w, l1b, rw, rb)


def _dispatch_kernel(zblk_ref, zflag_ref, dest_hbm, x_ref, xs_hbm, idx_s, zbuf, sem_i, sem_r, sem_z, *, tm):
    i = pl.program_id(0)
    n = TOP_K * tm

    @pl.when(i == 0)
    def _():
        zbuf[...] = jnp.zeros_like(zbuf)

        def zero_copy(e):
            start = pl.multiple_of(zblk_ref[e] * EXPERT_BLOCK, EXPERT_BLOCK)
            return pltpu.make_async_copy(zbuf, xs_hbm.at[pl.ds(start, EXPERT_BLOCK)], sem_z)

        for e in range(2 * N_EXPERTS):
            @pl.when(zflag_ref[e] == 1)
            def _():
                zero_copy(e).start()
        for e in range(2 * N_EXPERTS):
            @pl.when(zflag_ref[e] == 1)
            def _():
                zero_copy(e).wait()

    cp = pltpu.make_async_copy(dest_hbm.at[pl.ds(pl.multiple_of(i * n, n), n)], idx_s, sem_i)
    cp.start()
    cp.wait()

    def body(g, carry):
        for jj in range(SUBLANES):
            src = x_ref.at[g, pl.ds(jj, 1)]
            for j in range(TOP_K):
                d = idx_s[g * (SUBLANES * TOP_K) + jj * TOP_K + j]
                pltpu.make_async_copy(src, xs_hbm.at[pl.ds(d, 1)], sem_r).start(priority=j % 2)
        return carry

    lax.fori_loop(0, tm // SUBLANES, body, 0)
    for _ in range(TOP_K):
        pltpu.make_async_copy(xs_hbm.at[pl.ds(0, tm)], xs_hbm.at[pl.ds(0, tm)], sem_r).wait()


def _dispatch(last_blk, has_blk, dest, x1, slots, tm):
    t = x1.shape[0]
    grid_spec = pltpu.PrefetchScalarGridSpec(
        num_scalar_prefetch=2,
        grid=(t // tm,),
        in_specs=[pl.BlockSpec(memory_space=pl.ANY),
                  pl.BlockSpec((tm // SUBLANES, SUBLANES, D_MODEL), lambda i, lb, hb: (i, 0, 0))],
        out_specs=pl.BlockSpec(memory_space=pl.ANY),
        scratch_shapes=[pltpu.SMEM((TOP_K * tm,), jnp.int32),
                        pltpu.VMEM((EXPERT_BLOCK, D_MODEL), F32),
                        pltpu.SemaphoreType.DMA, pltpu.SemaphoreType.DMA, pltpu.SemaphoreType.DMA],
    )
    return pl.pallas_call(
        functools.partial(_dispatch_kernel, tm=tm),
        out_shape=jax.ShapeDtypeStruct((slots, D_MODEL), F32),
        grid_spec=grid_spec,
        compiler_params=_params(("arbitrary",)),
        name="moe_dispatch",
    )(last_blk, has_blk, dest, x1.reshape(t // SUBLANES, SUBLANES, D_MODEL))


MLP_TILE = 256
MLP_PIECES = 1 + D_FF // MLP_TILE + D_MODEL // MLP_TILE


def _expert_mlp(x_ref, wgu_ref, bgu_ref, wd_ref, bd_ref, out_ref, rows, xb_s, act_s, between):
    xb_s[...] = x_ref[...].astype(BF16)
    between(0)
    for n in range(D_FF // MLP_TILE):
        cg = slice(n * MLP_TILE, (n + 1) * MLP_TILE)
        cu = slice(D_FF + n * MLP_TILE, D_FF + (n + 1) * MLP_TILE)
        hg = jnp.dot(xb_s[...], wgu_ref[0, :, cg], preferred_element_type=F32) + bgu_ref[0, :, cg]
        hu = jnp.dot(xb_s[...], wgu_ref[0, :, cu], preferred_element_type=F32) + bgu_ref[0, :, cu]
        g = jnp.minimum(hg, SWIGLU_LIMIT)
        u = jnp.clip(hu, -SWIGLU_LIMIT, SWIGLU_LIMIT)
        act_s[:, cg] = ((u + 1.0) * g * jax.nn.sigmoid(SWIGLU_ALPHA * g)).astype(BF16)
        between(1 + n)
    for m in range(D_MODEL // MLP_TILE):
        cm = slice(m * MLP_TILE, (m + 1) * MLP_TILE)
        out_ref[rows, cm] = jnp.dot(act_s[...], wd_ref[0, :, cm], preferred_element_type=F32) + bd_ref[0, :, cm]
        between(1 + D_FF // MLP_TILE + m)


def _expert_kernel(be_ref, nu_ref, tok_hbm, x_hbm,
                   wgu_a, bgu_a, wd_a, bd_a, wgu_b, bgu_b, wd_b, bd_b,
                   ys_ref, xa, xb, xb16, act_s, idx_a, idx_b, sem_i, sem_a, sem_b):
    del be_ref
    i = pl.program_id(0)
    blk_a = 2 * i
    nu = nu_ref[0]
    active = blk_a < nu
    next_active = blk_a + 2 < nu
    half = EXPERT_BLOCK
    per_piece = -(-EXPERT_BLOCK // MLP_PIECES)

    def load_idx(blk, idx_s):
        cp = pltpu.make_async_copy(tok_hbm.at[blk], idx_s, sem_i)
        cp.start()
        cp.wait()

    def issue(idx_s, xbuf, sem, lo, hi):
        for r in range(lo, hi):
            pltpu.make_async_copy(x_hbm.at[pl.ds(idx_s[r], 1)], xbuf.at[pl.ds(r, 1)], sem).start(priority=r % 2)

    def issue_group(idx_s, xbuf, sem):
        return lambda k: issue(idx_s, xbuf, sem, k * per_piece, min((k + 1) * per_piece, EXPERT_BLOCK))

    def no_issue(k):
        del k

    def wait_rows(xbuf, sem):
        pltpu.make_async_copy(x_hbm.at[pl.ds(0, EXPERT_BLOCK)], xbuf, sem).wait()

    rows_a = slice(0, half)
    rows_b = slice(half, 2 * half)

    @pl.when(i == 0)
    def _():
        load_idx(0, idx_a)
        issue(idx_a, xa, sem_a, 0, EXPERT_BLOCK)

    @pl.when(active)
    def _():
        load_idx(blk_a + 1, idx_b)
        wait_rows(xa, sem_a)
        _expert_mlp(xa, wgu_a, bgu_a, wd_a, bd_a, ys_ref, rows_a, xb16, act_s, issue_group(idx_b, xb, sem_b))
        wait_rows(xb, sem_b)

    @pl.when(jnp.logical_and(active, next_active))
    def _():
        load_idx(blk_a + 2, idx_a)
        _expert_mlp(xb, wgu_b, bgu_b, wd_b, bd_b, ys_ref, rows_b, xb16, act_s, issue_group(idx_a, xa, sem_a))

    @pl.when(jnp.logical_and(active, jnp.logical_not(next_active)))
    def _():
        _expert_mlp(xb, wgu_b, bgu_b, wd_b, bd_b, ys_ref, rows_b, xb16, act_s, no_issue)

    @pl.when(jnp.logical_not(active))
    def _():
        ys_ref[...] = jnp.zeros_like(ys_ref)


def _experts(block_e, n_used, slot_tok, x1, wgu, bgu, wd, bd):
    nb = slot_tok.shape[0]
    assert nb % 2 == 0
    wspecs = []
    for off in (0, 1):
        wspecs += [pl.BlockSpec((1, D_MODEL, 2 * D_FF), lambda i, be, nu, off=off: (be[2 * i + off], 0, 0)),
                   pl.BlockSpec((1, 1, 2 * D_FF), lambda i, be, nu, off=off: (be[2 * i + off], 0, 0)),
                   pl.BlockSpec((1, D_FF, D_MODEL), lambda i, be, nu, off=off: (be[2 * i + off], 0, 0)),
                   pl.BlockSpec((1, 1, D_MODEL), lambda i, be, nu, off=off: (be[2 * i + off], 0, 0))]
    grid_spec = pltpu.PrefetchScalarGridSpec(
        num_scalar_prefetch=2,
        grid=(nb // 2,),
        in_specs=[pl.BlockSpec(memory_space=pl.ANY), pl.BlockSpec(memory_space=pl.ANY)] + wspecs,
        out_specs=pl.BlockSpec((2 * EXPERT_BLOCK, D_MODEL), lambda i, be, nu: (i, 0)),
        scratch_shapes=[pltpu.VMEM((EXPERT_BLOCK, D_MODEL), F32), pltpu.VMEM((EXPERT_BLOCK, D_MODEL), F32),
                        pltpu.VMEM((EXPERT_BLOCK, D_MODEL), BF16), pltpu.VMEM((EXPERT_BLOCK, D_FF), BF16),
                        pltpu.SMEM((EXPERT_BLOCK,), jnp.int32), pltpu.SMEM((EXPERT_BLOCK,), jnp.int32),
                        pltpu.SemaphoreType.DMA, pltpu.SemaphoreType.DMA, pltpu.SemaphoreType.DMA],
    )
    return pl.pallas_call(
        _expert_kernel,
        out_shape=jax.ShapeDtypeStruct((nb * EXPERT_BLOCK, D_MODEL), F32),
        grid_spec=grid_spec,
        compiler_params=_params(("arbitrary",)),
        name="moe_experts",
    )(block_e, n_used, slot_tok, x1, wgu, bgu, wd, bd, wgu, bgu, wd, bd)


def _sc_gather(table, idx):
    p, d = idx.shape[0], table.shape[1]
    per_w = p // SC_WORKERS
    assert per_w * SC_WORKERS == p and per_w % SC_GATHER_ROWS == 0
    mesh = plsc.VectorSubcoreMesh(core_axis_name="c", subcore_axis_name="s")

    @functools.partial(
        pl.kernel, mesh=mesh,
        out_type=jax.ShapeDtypeStruct((p, d), table.dtype),
        scratch_types=[pltpu.VMEM((SC_GATHER_ROWS,), jnp.int32),
                       pltpu.VMEM((SC_GATHER_ROWS, d), table.dtype),
                       pltpu.SemaphoreType.DMA],
        name="moe_dispatch_sc",
    )
    def gather_kernel(table_hbm, idx_hbm, out_hbm, idx_v, rows_v, sem):
        wid = lax.axis_index("s") * SC_CORES + lax.axis_index("c")
        base = wid * per_w

        @pl.loop(0, per_w // SC_GATHER_ROWS)
        def _(c):
            off = base + c * SC_GATHER_ROWS
            pltpu.sync_copy(idx_hbm.at[pl.ds(off, SC_GATHER_ROWS)], idx_v)
            pltpu.async_copy(table_hbm.at[idx_v], rows_v, sem).wait()
            pltpu.sync_copy(rows_v, out_hbm.at[pl.ds(off, SC_GATHER_ROWS)])

    return gather_kernel(table, idx)


def _expert_plain_kernel(be_ref, nu_ref, xs_ref, wgu_ref, bgu_ref, wd_ref, bd_ref, ys_ref):
    del be_ref
    i = pl.program_id(0)

    @pl.when(i < nu_ref[0])
    def _():
        h = jnp.dot(xs_ref[...].astype(BF16), wgu_ref[0], preferred_element_type=F32) + bgu_ref[0]
        g = jnp.minimum(h[:, :D_FF], SWIGLU_LIMIT)
        u = jnp.clip(h[:, D_FF:], -SWIGLU_LIMIT, SWIGLU_LIMIT)
        act = (u + 1.0) * g * jax.nn.sigmoid(SWIGLU_ALPHA * g)
        ys_ref[...] = jnp.dot(act.astype(BF16), wd_ref[0], preferred_element_type=F32) + bd_ref[0]

    @pl.when(i >= nu_ref[0])
    def _():
        ys_ref[...] = jnp.zeros_like(ys_ref)


def _experts_plain(block_e, n_used, xs, wgu, bgu, wd, bd):
    slots = xs.shape[0]
    nb = slots // EXPERT_BLOCK
    grid_spec = pltpu.PrefetchScalarGridSpec(
        num_scalar_prefetch=2,
        grid=(nb,),
        in_specs=[pl.BlockSpec((EXPERT_BLOCK, D_MODEL), lambda i, be, nu: (i, 0)),
                  pl.BlockSpec((1, D_MODEL, 2 * D_FF), lambda i, be, nu: (be[i], 0, 0)),
                  pl.BlockSpec((1, 1, 2 * D_FF), lambda i, be, nu: (be[i], 0, 0)),
                  pl.BlockSpec((1, D_FF, D_MODEL), lambda i, be, nu: (be[i], 0, 0)),
                  pl.BlockSpec((1, 1, D_MODEL), lambda i, be, nu: (be[i], 0, 0))],
        out_specs=pl.BlockSpec((EXPERT_BLOCK, D_MODEL), lambda i, be, nu: (i, 0)),
    )
    return pl.pallas_call(
        _expert_plain_kernel,
        out_shape=jax.ShapeDtypeStruct((slots, D_MODEL), F32),
        grid_spec=grid_spec,
        compiler_params=_params(("arbitrary",)),
        name="moe_experts",
    )(block_e, n_used, xs, wgu, bgu, wd, bd)


def _combine_kernel(dest_hbm, x1_ref, rg_ref, ys_hbm, l2w_ref, l2b_ref, o_ref, idx_s, buf, sem_i, sem_r, *, tm):
    i = pl.program_id(0)
    nsteps = pl.num_programs(0)
    n = TOP_K * tm

    def issue(step, slot):
        cp = pltpu.make_async_copy(dest_hbm.at[pl.ds(pl.multiple_of(step * n, n), n)], idx_s, sem_i)
        cp.start()
        cp.wait()

        def body(g, carry):
            for jj in range(SUBLANES):
                for j in range(TOP_K):
                    d = idx_s[g * (SUBLANES * TOP_K) + jj * TOP_K + j]
                    pltpu.make_async_copy(ys_hbm.at[pl.ds(d, 1)], buf.at[slot, j, g, pl.ds(jj, 1)],
                                          sem_r.at[slot]).start(priority=j % 2)
            return carry

        lax.fori_loop(0, tm // SUBLANES, body, 0)

    @pl.when(i == 0)
    def _():
        issue(0, 0)

    @pl.when(i + 1 < nsteps)
    def _():
        issue(i + 1, (i + 1) % 2)

    slot = i % 2
    for _ in range(TOP_K):
        pltpu.make_async_copy(ys_hbm.at[pl.ds(0, tm)], o_ref, sem_r.at[slot]).wait()
    rg = rg_ref[...]
    moe = rg[:, 0:1] * buf[slot, 0].reshape(tm, D_MODEL)
    for j in range(1, TOP_K):
        moe = moe + rg[:, j:j + 1] * buf[slot, j].reshape(tm, D_MODEL)
    o_ref[...] = _layernorm(DN_ALPHA * x1_ref[...] + moe, l2w_ref[...], l2b_ref[...])


def _combine(dest, x1, rg, ys, l2w, l2b, tm):
    t = x1.shape[0]
    full = lambda shape: pl.BlockSpec(shape, lambda i: (0, 0))
    return pl.pallas_call(
        functools.partial(_combine_kernel, tm=tm),
        out_shape=jax.ShapeDtypeStruct((t, D_MODEL), F32),
        grid=(t // tm,),
        in_specs=[pl.BlockSpec(memory_space=pl.ANY),
                  pl.BlockSpec((tm, D_MODEL), lambda i: (i, 0)),
                  pl.BlockSpec((tm, LANES), lambda i: (i, 0)),
                  pl.BlockSpec(memory_space=pl.ANY),
                  full((1, D_MODEL)), full((1, D_MODEL))],
        out_specs=pl.BlockSpec((tm, D_MODEL), lambda i: (i, 0)),
        scratch_shapes=[pltpu.SMEM((TOP_K * tm,), jnp.int32),
                        pltpu.VMEM((2, TOP_K, tm // SUBLANES, SUBLANES, D_MODEL), F32),
                        pltpu.SemaphoreType.DMA, pltpu.SemaphoreType.DMA((2,))],
        compiler_params=_params(("arbitrary",)),
        name="moe_combine",
    )(dest, x1, rg, ys, l2w, l2b)


def _prep_w_in(w):
    zeros = lambda n: jnp.zeros((D_MODEL, n), w.dtype)
    kr = w[:, 5760:5824]
    kr_sw = jnp.concatenate([kr[:, MLA_ROPE // 2:], kr[:, :MLA_ROPE // 2]], axis=1)
    parts = [w[:, 6848:9920], w[:, 0:5120], w[:, 5824:6848],
             w[:, 5120:5504], zeros(128),
             w[:, 5504:5760], kr, zeros(64), kr_sw, zeros(64)]
    return jnp.concatenate(parts, axis=1).astype(BF16)


def _prep_w_uq(w):
    w3 = w.reshape(MLA_Q_LORA, MLA_HEADS, MLA_NOPE + MLA_ROPE)
    rope = w3[:, :, MLA_NOPE:]
    rope_sw = jnp.concatenate([rope[:, :, MLA_ROPE // 2:], rope[:, :, :MLA_ROPE // 2]], axis=-1)
    pad = jnp.zeros((MLA_Q_LORA, MLA_HEADS, QK_PAD - MLA_NOPE - MLA_ROPE), w.dtype)
    main = jnp.concatenate([w3, pad], axis=-1).reshape(MLA_Q_LORA, MLA_HEADS * QK_PAD)
    swp = jnp.concatenate([rope_sw, pad], axis=-1).reshape(MLA_Q_LORA, MLA_HEADS * LANES)
    both = jnp.concatenate([main, swp], axis=1)
    return jnp.pad(both, ((0, 512 - MLA_Q_LORA), (0, 0))).astype(BF16)


def _rope_tables(seq):
    inv_freq = ROPE_THETA ** (-jnp.arange(0, MLA_ROPE, 2, dtype=F32) / MLA_ROPE)
    ang = jnp.arange(seq, dtype=F32)[:, None] * inv_freq[None, :]
    cos, sin = jnp.cos(ang), jnp.sin(ang)
    pad = jnp.zeros((seq, LANES - MLA_ROPE), F32)
    return (jnp.concatenate([cos, cos, pad], axis=1), jnp.concatenate([-sin, sin, pad], axis=1))


def _tile(n, pref):
    return pref if n % pref == 0 else n


def _layer(x, mem, wts):
    batch, seq, _ = x.shape
    t = batch * seq
    x2d = x.reshape(t, D_MODEL)

    proj = _matmul(x2d, wts["w_in"], BF16, _tile(t, 1024), 1024, "in_proj")
    o_a = _hgrn(proj, wts["lb"], wts["hgrn_norm_w"], batch, seq, _tile(seq, 512))

    cos, sin = _rope_tables(seq)
    q, k, v = _mla_prep(proj, wts["qnw"], wts["kvnw"], wts["w_uq"], wts["w_uk"], wts["w_uv"],
                        cos, sin, batch, seq, _tile(seq, 512))
    o_b = _flash(q, k, v, batch, seq, _tile(seq, 1024), _tile(seq, 1024))

    nmem = mem.shape[1]
    kvm = _matmul(mem.reshape(batch * nmem, D_MODEL), wts["mem_w_kv"], BF16,
                  _tile(batch * nmem, 512), 1024, "mem_kv")

    tm = _tile(seq, 512)
    x1, ri, rg, cnt = _merge(x2d, proj, o_a, o_b, kvm, wts["w_out"], wts["ln1_w"], wts["ln1_b"],
                             wts["router_w"], wts["router_b"], batch, seq, tm)

    idx = ri[:, :TOP_K]
    rank = ri[:, TOP_K:2 * TOP_K]
    counts = cnt[0, :N_EXPERTS].astype(jnp.int32)
    padded = (counts + EXPERT_BLOCK - 1) // EXPERT_BLOCK * EXPERT_BLOCK
    pad_end = jnp.cumsum(padded)
    pad_start = pad_end - padded
    dest = (pad_start[idx] + rank).reshape(t * TOP_K).astype(jnp.int32)
    nb = t * TOP_K // EXPERT_BLOCK + N_EXPERTS
    blk_start = jnp.arange(nb, dtype=jnp.int32) * EXPERT_BLOCK
    block_e = jnp.minimum(jnp.sum((pad_end[None, :] <= blk_start[:, None]).astype(jnp.int32), axis=1),
                          N_EXPERTS - 1).astype(jnp.int32)
    n_used = (pad_end[-1:] // EXPERT_BLOCK).astype(jnp.int32)
    tok_ids = jnp.arange(t * TOP_K, dtype=jnp.int32) // TOP_K
    slot_tok = jnp.zeros((nb * EXPERT_BLOCK,), jnp.int32).at[dest].set(tok_ids)
    xs = _sc_gather(x1, slot_tok)
    ys = _experts_plain(block_e, n_used, xs, wts["exp_w_gu"], wts["exp_b_gu"], wts["exp_w_down"],
                        wts["exp_b_down"])
    y = _combine(dest, x1, rg, ys, wts["ln2_w"], wts["ln2_b"], _tile(t, 256))
    return y.reshape(batch, seq, D_MODEL)


def kernel(x_prompt, x_sample, mem_prompt, mem_sample, w_in, hgrn_lb_logits, hgrn_norm_w,
           mla_q_norm_w, mla_w_uq, mla_kv_norm_w, mla_w_uk, mla_w_uv, mem_w_kv, w_out,
           ln1_w, ln1_b, router_w, router_b, exp_w_gu, exp_b_gu, exp_w_down, exp_b_down,
           ln2_w, ln2_b):
    depth = w_in.shape[0]
    gamma = jax.nn.softmax(hgrn_lb_logits.astype(F32), axis=1)
    cum = jnp.cumsum(gamma, axis=1)
    lb_all = cum[:, 1:] - cum[:, :1]
    y_prompt, y_sample = x_prompt, x_sample
    for l in range(depth):
        row = lambda a: a[l].reshape(1, -1).astype(F32)
        wts = {
            "w_in": _prep_w_in(w_in[l]),
            "lb": lb_all[:, l],
            "hgrn_norm_w": hgrn_norm_w[l].astype(F32),
            "qnw": jnp.pad(row(mla_q_norm_w), ((0, 0), (0, 512 - MLA_Q_LORA))),
            "kvnw": row(mla_kv_norm_w),
            "w_uq": _prep_w_uq(mla_w_uq[l]),
            "w_uk": mla_w_uk[l].astype(BF16),
            "w_uv": mla_w_uv[l].astype(BF16),
            "mem_w_kv": mem_w_kv[l].astype(BF16),
            "w_out": w_out[l].astype(BF16),
            "ln1_w": row(ln1_w), "ln1_b": row(ln1_b),
            "router_w": jnp.pad(router_w[l].astype(F32), ((0, 0), (0, LANES - N_EXPERTS))),
            "router_b": jnp.pad(row(router_b), ((0, 0), (0, LANES - N_EXPERTS)), constant_values=-jnp.inf),
            "exp_w_gu": exp_w_gu[l].astype(BF16),
            "exp_b_gu": exp_b_gu[l].reshape(N_EXPERTS, 1, 2 * D_FF).astype(F32),
            "exp_w_down": exp_w_down[l].astype(BF16),
            "exp_b_down": exp_b_down[l].reshape(N_EXPERTS, 1, D_MODEL).astype(F32),
            "ln2_w": row(ln2_w), "ln2_b": row(ln2_b),
        }
        y_prompt = _layer(y_prompt, mem_prompt, wts)
        y_sample = _layer(y_sample, mem_sample, wts)
    return (y_prompt, y_sample)
```

```python
import functools

import numpy as np
import jax
import jax.numpy as jnp
from jax import lax
from jax.experimental import pallas as pl
from jax.experimental.pallas import tpu as pltpu
from jax.experimental.pallas import tpu_sc as plsc

F32 = jnp.float32
BF16 = jnp.bfloat16

D_MODEL = 1024
HGRN_HEADS = 8
HGRN_DK = 128
MLA_HEADS = 8
MLA_Q_LORA = 384
MLA_KV_LORA = 256
MLA_NOPE = 128
MLA_ROPE = 64
MLA_V = 128
ROPE_THETA = 10000.0
MEM_HEADS = 4
MEM_HEAD_DIM = D_MODEL // MEM_HEADS
N_EXPERTS = 32
TOP_K = 4
D_FF = D_MODEL
SWIGLU_LIMIT = 7.0
SWIGLU_ALPHA = 1.702
DN_ALPHA = 2.0 ** 0.25
LN_EPS = 1e-5
RMS_EPS = 1e-6

LANES = 128
SUBLANES = 8
QK_PAD = 256
V_PAD = 256
LOG2E = 1.4426950408889634
HGRN_CHUNK = 64
HGRN_SAFE_RANGE = 160.0
EXPERT_BLOCK = 512
SC_CORES = 2
SC_SUBCORES = 16
SC_WORKERS = SC_CORES * SC_SUBCORES
SC_GATHER_ROWS = 64
VMEM_LIMIT = 56 * 1024 * 1024

COL_GA, COL_GB, COL_GM, COL_Q, COL_ZF, COL_ZB, COL_I, COL_G, COL_MQ = range(9)
COL_DQ, COL_DKV = 18, 19
IN_COLS_PAD = 10240

NT_DIMS = (((1,), (1,)), ((), ()))
TN_DIMS = (((0,), (0,)), ((), ()))


def _params(sem, vmem=VMEM_LIMIT):
    return pltpu.CompilerParams(dimension_semantics=sem, vmem_limit_bytes=vmem)


def _mm_kernel(x_ref, w_ref, o_ref, xb_ref):
    @pl.when(pl.program_id(1) == 0)
    def _():
        xb_ref[...] = x_ref[...].astype(BF16)

    o_ref[...] = jnp.dot(xb_ref[...], w_ref[...], preferred_element_type=F32).astype(o_ref.dtype)


def _matmul(x, w, out_dtype, tm, tn, name):
    m, k = x.shape
    n = w.shape[1]
    return pl.pallas_call(
        _mm_kernel,
        out_shape=jax.ShapeDtypeStruct((m, n), out_dtype),
        grid=(m // tm, n // tn),
        in_specs=[pl.BlockSpec((tm, k), lambda i, j: (i, 0)),
                  pl.BlockSpec((k, tn), lambda i, j: (0, j))],
        out_specs=pl.BlockSpec((tm, tn), lambda i, j: (i, j)),
        scratch_shapes=[pltpu.VMEM((tm, k), BF16)],
        compiler_params=_params(("arbitrary", "arbitrary")),
        name=name,
    )(x, w)


def _hgrn_kernel(*refs, reverse, epilogue, sc):
    if epilogue:
        lb_ref, q_ref, z_ref, v_ref, of_ref, g_ref, nw_ref, o_ref, st_ref, kk_s, b_s, vf_s, os_s = refs
    else:
        lb_ref, q_ref, z_ref, v_ref, o_ref, st_ref, kk_s, b_s, vf_s = refs
    C = HGRN_CHUNK
    nch = sc // C

    @pl.when(pl.program_id(1) == 0)
    def _():
        st_ref[...] = jnp.zeros_like(st_ref)

    lb = lb_ref[...]
    row = lax.broadcasted_iota(jnp.int32, (C, C), 0)
    col = lax.broadcasted_iota(jnp.int32, (C, C), 1)
    tri = (row <= col) if reverse else (row >= col)
    trib = jnp.where(tri, 1.0, 0.0).astype(BF16)

    minb = None
    for c in range(nch):
        rows = slice(c * C, (c + 1) * C)
        z = z_ref[rows, :].astype(F32)
        gate = (1.0 - lb) * jax.nn.sigmoid(z)
        lf = jnp.log(lb + gate)
        kk_s[rows, :] = (1.0 - lb) - gate
        hi = lf.astype(BF16)
        r1 = lf - hi.astype(F32)
        mid = r1.astype(BF16)
        lo = (r1 - mid.astype(F32)).astype(BF16)
        b = (jnp.dot(trib, hi, preferred_element_type=F32)
             + jnp.dot(trib, mid, preferred_element_type=F32)
             + jnp.dot(trib, lo, preferred_element_type=F32))
        b_s[rows, :] = b
        mb = jnp.min(b)
        minb = mb if minb is None else jnp.minimum(minb, mb)

    rid = lax.broadcasted_iota(jnp.int32, (C, 1), 0)

    def chunk(i, carry, fast):
        c = (nch - 1 - i) if reverse else i
        r0 = pl.multiple_of(c * C, C)
        rows = pl.ds(r0, C)
        for h in range(HGRN_HEADS):
            cols = slice(h * HGRN_DK, (h + 1) * HGRN_DK)
            q = q_ref[rows, cols].astype(F32)
            kk = kk_s[rows, cols]
            b = b_s[rows, cols]
            v = v_ref[rows, cols]
            bl = b[0:1, :] if reverse else b[C - 1:C, :]
            if fast:
                bm = 0.5 * bl
                qd = (q * jnp.exp(b - bm)).astype(BF16)
                kd = (kk * jnp.exp(bm - b)).astype(BF16)
                s = lax.dot_general(qd, kd, NT_DIMS, preferred_element_type=F32)
                s = jnp.where(tri, s, 0.0).astype(BF16)
                o = jnp.dot(s, v, preferred_element_type=F32)
            else:
                def sbody(g_i, o_acc):
                    grp = pl.ds(pl.multiple_of(r0 + g_i * SUBLANES, SUBLANES), SUBLANES)
                    b8 = b_s[grp, cols]
                    k8 = kk_s[grp, cols]
                    v8 = vf_s[grp, cols]
                    for jj in range(SUBLANES):
                        s_i = g_i * SUBLANES + jj
                        w = q * k8[jj:jj + 1, :] * jnp.exp(jnp.minimum(b - b8[jj:jj + 1, :], 0.0))
                        scol = jnp.sum(w, axis=-1, keepdims=True)
                        keep = (rid <= s_i) if reverse else (rid >= s_i)
                        o_acc = o_acc + jnp.where(keep, scol, 0.0) * v8[jj:jj + 1, :]
                    return o_acc
                o = lax.fori_loop(0, C // SUBLANES, sbody, jnp.zeros((C, HGRN_DK), F32))
            st = st_ref[h]
            qi = (q * jnp.exp(b)).astype(BF16)
            o = o + lax.dot_general(qi, st.astype(BF16), NT_DIMS, preferred_element_type=F32)
            ke = (kk * jnp.exp(bl - b)).astype(BF16)
            upd = lax.dot_general(v, ke, TN_DIMS, preferred_element_type=F32)
            st_ref[h] = st * jnp.exp(bl) + upd
            if epilogue:
                os_s[rows, cols] = o + of_ref[rows, cols].astype(F32)
            else:
                o_ref[rows, cols] = o.astype(o_ref.dtype)
        return carry

    safe = minb >= -HGRN_SAFE_RANGE

    @pl.when(safe)
    def _():
        lax.fori_loop(0, nch, functools.partial(chunk, fast=True), 0, unroll=2)

    @pl.when(jnp.logical_not(safe))
    def _():
        vf_s[...] = v_ref[...].astype(F32)
        lax.fori_loop(0, nch, functools.partial(chunk, fast=False), 0)

    if epilogue:
        nw = nw_ref[...]
        for h in range(HGRN_HEADS):
            cols = slice(h * HGRN_DK, (h + 1) * HGRN_DK)
            os = os_s[:, cols]
            ms = jnp.mean(os * os, axis=-1, keepdims=True)
            y = os * lax.rsqrt(ms + RMS_EPS) * nw
            g = g_ref[:, cols].astype(F32)
            o_ref[:, cols] = (y * (g * jax.nn.sigmoid(g))).astype(o_ref.dtype)


def _hgrn(proj, lb, norm_w, batch, seq, sc):
    t = batch * seq
    ns = seq // sc
    blk = (sc, D_MODEL)

    def spec(colblk, reverse):
        if reverse:
            return pl.BlockSpec(blk, lambda b, n: (b * ns + ns - 1 - n, colblk))
        return pl.BlockSpec(blk, lambda b, n: (b * ns + n, colblk))

    def row_spec(reverse):
        if reverse:
            return pl.BlockSpec(blk, lambda b, n: (b * ns + ns - 1 - n, 0))
        return pl.BlockSpec(blk, lambda b, n: (b * ns + n, 0))

    lb_spec = pl.BlockSpec((1, D_MODEL), lambda b, n: (0, 0))
    common_scratch = [pltpu.VMEM((HGRN_HEADS, HGRN_DK, HGRN_DK), F32),
                      pltpu.VMEM(blk, F32), pltpu.VMEM(blk, F32), pltpu.VMEM(blk, F32)]
    o_f = pl.pallas_call(
        functools.partial(_hgrn_kernel, reverse=False, epilogue=False, sc=sc),
        out_shape=jax.ShapeDtypeStruct((t, D_MODEL), BF16),
        grid=(batch, ns),
        in_specs=[lb_spec, spec(COL_Q, False), spec(COL_ZF, False), spec(COL_I, False)],
        out_specs=row_spec(False),
        scratch_shapes=common_scratch,
        compiler_params=_params(("arbitrary", "arbitrary")),
        name="hgrn_fwd",
    )(lb[0:1], proj, proj, proj)
    o_a = pl.pallas_call(
        functools.partial(_hgrn_kernel, reverse=True, epilogue=True, sc=sc),
        out_shape=jax.ShapeDtypeStruct((t, D_MODEL), BF16),
        grid=(batch, ns),
        in_specs=[lb_spec, spec(COL_Q, True), spec(COL_ZB, True), spec(COL_I, True),
                  row_spec(True), spec(COL_G, True),
                  pl.BlockSpec((1, HGRN_DK), lambda b, n: (0, 0))],
        out_specs=row_spec(True),
        scratch_shapes=common_scratch + [pltpu.VMEM(blk, F32)],
        compiler_params=_params(("arbitrary", "arbitrary")),
        name="hgrn_bwd",
    )(lb[1:2], proj, proj, proj, o_f, proj, norm_w.reshape(1, HGRN_DK))
    return o_a


def _mla_prep_kernel(dq_ref, dkv_ref, qnw_ref, kvnw_ref, wq_ref, wk_ref, wv_ref, cos_ref, sin_ref,
                     q_ref, k_ref, v_ref):
    scale = (MLA_NOPE + MLA_ROPE) ** -0.5 * LOG2E
    cos = cos_ref[...]
    sin = sin_ref[...]
    dq = dq_ref[...].astype(F32)
    ms = jnp.sum(dq * dq, axis=-1, keepdims=True) * (1.0 / MLA_Q_LORA)
    cq = (dq * lax.rsqrt(ms + RMS_EPS) * qnw_ref[...]).astype(BF16)
    qa = jnp.dot(cq, wq_ref[...], preferred_element_type=F32)
    for h in range(MLA_HEADS):
        base = h * QK_PAD
        q_ref[:, base:base + MLA_NOPE] = (qa[:, base:base + MLA_NOPE] * scale).astype(BF16)
        rp = qa[:, base + MLA_NOPE:base + QK_PAD]
        sw = qa[:, MLA_HEADS * QK_PAD + h * LANES:MLA_HEADS * QK_PAD + (h + 1) * LANES]
        q_ref[:, base + MLA_NOPE:base + QK_PAD] = ((rp * cos + sw * sin) * scale).astype(BF16)
    dkv = dkv_ref[...].astype(F32)
    ckv = dkv[:, :MLA_KV_LORA]
    msk = jnp.mean(ckv * ckv, axis=-1, keepdims=True)
    cn = (ckv * lax.rsqrt(msk + RMS_EPS) * kvnw_ref[...]).astype(BF16)
    kn = jnp.dot(cn, wk_ref[...], preferred_element_type=F32)
    vv = jnp.dot(cn, wv_ref[...], preferred_element_type=F32).astype(BF16)
    ones_col = jnp.where(lax.broadcasted_iota(jnp.int32, (vv.shape[0], LANES), 1) == 0, 1.0, 0.0).astype(BF16)
    for h in range(MLA_HEADS):
        v_ref[:, h * V_PAD:h * V_PAD + MLA_V] = vv[:, h * MLA_V:(h + 1) * MLA_V]
        v_ref[:, h * V_PAD + MLA_V:(h + 1) * V_PAD] = ones_col
    kr = (dkv[:, MLA_KV_LORA:MLA_KV_LORA + LANES] * cos
          + dkv[:, MLA_KV_LORA + LANES:MLA_KV_LORA + 2 * LANES] * sin).astype(BF16)
    for h in range(MLA_HEADS):
        base = h * QK_PAD
        k_ref[:, base:base + MLA_NOPE] = kn[:, h * MLA_NOPE:(h + 1) * MLA_NOPE].astype(BF16)
        k_ref[:, base + MLA_NOPE:base + QK_PAD] = kr


def _mla_prep(proj, qnw, kvnw, wq, wk, wv, cos, sin, batch, seq, tm):
    t = batch * seq
    npos = seq // tm
    full = lambda shape: pl.BlockSpec(shape, lambda i: (0, 0))
    return pl.pallas_call(
        _mla_prep_kernel,
        out_shape=(jax.ShapeDtypeStruct((t, MLA_HEADS * QK_PAD), BF16),
                   jax.ShapeDtypeStruct((t, MLA_HEADS * QK_PAD), BF16),
                   jax.ShapeDtypeStruct((t, MLA_HEADS * V_PAD), BF16)),
        grid=(t // tm,),
        in_specs=[pl.BlockSpec((tm, 512), lambda i: (i, COL_DQ)),
                  pl.BlockSpec((tm, 512), lambda i: (i, COL_DKV)),
                  full((1, 512)), full((1, MLA_KV_LORA)),
                  full(wq.shape), full(wk.shape), full(wv.shape),
                  pl.BlockSpec((tm, LANES), lambda i: (i % npos, 0)),
                  pl.BlockSpec((tm, LANES), lambda i: (i % npos, 0))],
        out_specs=(pl.BlockSpec((tm, MLA_HEADS * QK_PAD), lambda i: (i, 0)),
                   pl.BlockSpec((tm, MLA_HEADS * QK_PAD), lambda i: (i, 0)),
                   pl.BlockSpec((tm, MLA_HEADS * V_PAD), lambda i: (i, 0))),
        compiler_params=_params(("arbitrary",)),
        name="mla_prep",
    )(proj, proj, qnw, kvnw, wq, wk, wv, cos, sin)


def _flash_kernel(q_ref, k_ref, v_ref, o_ref, m_s, acc_s, s_buf, *, bk, nk):
    m_s[...] = jnp.full(m_s.shape, -jnp.inf, F32)
    acc_s[...] = jnp.zeros(acc_s.shape, F32)
    nt = bk // LANES
    bq = q_ref.shape[0]
    nsplit = 2 if bq % 256 == 0 else 1
    hq = bq // nsplit
    qrs = [slice(hf * hq, (hf + 1) * hq) for hf in range(nsplit)]

    def scores(j, slot):
        rows = pl.ds(pl.multiple_of(j * bk, bk), bk)
        for qr in qrs:
            s_buf[slot, qr, :] = lax.dot_general(q_ref[qr, :], k_ref[rows, :], NT_DIMS,
                                                 preferred_element_type=F32)

    def consume(j, slot):
        rows = pl.ds(pl.multiple_of(j * bk, bk), bk)
        ps, alphas = [], []
        for qr in qrs:
            tiles = [s_buf[slot, qr, t * LANES:(t + 1) * LANES] for t in range(nt)]
            tmax = tiles[0]
            for t in range(1, nt):
                tmax = jnp.maximum(tmax, tiles[t])
            m_prev = m_s[qr, :]
            m_new = jnp.maximum(m_prev, jnp.max(tmax, axis=-1, keepdims=True))
            alphas.append(jnp.exp2(m_prev - m_new))
            ps.append(jnp.concatenate([jnp.exp2(tl - m_new).astype(BF16) for tl in tiles], axis=1))
            m_s[qr, :] = m_new
        for qr, p, a in zip(qrs, ps, alphas):
            pv = jnp.dot(p, v_ref[rows, :], preferred_element_type=F32)
            acc_s[qr, :] = jnp.concatenate([a, a], axis=1) * acc_s[qr, :] + pv

    scores(0, 0)
    npairs = (nk - 1) // 2

    def body(i, carry):
        j = 2 * i
        scores(j + 1, 1)
        consume(j, 0)
        scores(j + 2, 0)
        consume(j + 1, 1)
        return carry

    lax.fori_loop(0, npairs, body, 0)
    if nk - 2 * npairs == 2:
        scores(nk - 1, 1)
        consume(nk - 2, 0)
        consume(nk - 1, 1)
    else:
        consume(nk - 1, 0)
    acc = acc_s[...]
    o_ref[...] = (acc[:, :MLA_V] / acc[:, MLA_V:MLA_V + 1]).astype(o_ref.dtype)


def _flash(q, k, v, batch, seq, bq, bk):
    t = batch * seq
    nq = seq // bq
    return pl.pallas_call(
        functools.partial(_flash_kernel, bk=bk, nk=seq // bk),
        out_shape=jax.ShapeDtypeStruct((t, MLA_HEADS * MLA_V), BF16),
        grid=(batch, MLA_HEADS, nq),
        in_specs=[pl.BlockSpec((bq, QK_PAD), lambda b, h, i: (b * nq + i, h)),
                  pl.BlockSpec((seq, QK_PAD), lambda b, h, i: (b, h)),
                  pl.BlockSpec((seq, V_PAD), lambda b, h, i: (b, h))],
        out_specs=pl.BlockSpec((bq, MLA_V), lambda b, h, i: (b * nq + i, h)),
        scratch_shapes=[pltpu.VMEM((bq, LANES), F32), pltpu.VMEM((bq, V_PAD), F32),
                        pltpu.VMEM((2, bq, bk), F32)],
        compiler_params=_params(("arbitrary", "arbitrary", "arbitrary")),
        name="mla_flash",
    )(q, k, v)


def _layernorm(y, w, b):
    mu = jnp.mean(y, axis=-1, keepdims=True)
    yc = y - mu
    var = jnp.mean(yc * yc, axis=-1, keepdims=True)
    return yc * lax.rsqrt(var + LN_EPS) * w + b


def _merge_kernel(x_ref, ga_ref, gb_ref, gm_ref, mq_ref, oa_ref, ob_ref, kvm_ref, wout_ref,
                  l1w_ref, l1b_ref, rw_ref, rb_ref,
                  x1_ref, ri_ref, rg_ref, cnt_ref, carry_s, *, tm):
    @pl.when(pl.program_id(0) == 0)
    def _():
        carry_s[...] = jnp.zeros_like(carry_s)

    parts = []
    for h in range(MEM_HEADS):
        cols = slice(h * MEM_HEAD_DIM, (h + 1) * MEM_HEAD_DIM)
        kh = kvm_ref[:, cols]
        vh = kvm_ref[:, D_MODEL + h * MEM_HEAD_DIM:D_MODEL + (h + 1) * MEM_HEAD_DIM]
        s = lax.dot_general(mq_ref[:, cols], kh, NT_DIMS, preferred_element_type=F32) * (MEM_HEAD_DIM ** -0.5)
        s = s - jnp.max(s, axis=-1, keepdims=True)
        p = jnp.exp(s)
        p = p / jnp.sum(p, axis=-1, keepdims=True)
        parts.append(jnp.dot(p.astype(BF16), vh, preferred_element_type=F32))
    om = jnp.concatenate(parts, axis=1)

    merged = (jax.nn.sigmoid(ga_ref[...].astype(F32)) * oa_ref[...].astype(F32)
              + jax.nn.sigmoid(gb_ref[...].astype(F32)) * ob_ref[...].astype(F32)
              + jax.nn.sigmoid(gm_ref[...].astype(F32)) * om)
    y = DN_ALPHA * x_ref[...] + jnp.dot(merged.astype(BF16), wout_ref[...], preferred_element_type=F32)
    x1 = _layernorm(y, l1w_ref[...], l1b_ref[...])
    x1_ref[...] = x1

    logits = jnp.dot(x1, rw_ref[...], preferred_element_type=F32, precision=lax.Precision.HIGHEST) + rb_ref[...]
    lane_i = lax.broadcasted_iota(jnp.int32, (tm, LANES), 1)
    lane = lane_i.astype(F32)
    work = logits
    idx, val = [], []
    for _ in range(TOP_K):
        mx = jnp.max(work, axis=-1, keepdims=True)
        ix = jnp.min(jnp.where(work == mx, lane, float(LANES)), axis=-1, keepdims=True)
        idx.append(ix)
        val.append(mx)
        work = jnp.where(lane == ix, -jnp.inf, work)
    ex = [jnp.exp(v - val[0]) for v in val]
    tot = ex[0] + ex[1] + ex[2] + ex[3]
    hot = [jnp.where(lane == ix, 1.0, 0.0) for ix in idx]
    multi = hot[0] + hot[1] + hot[2] + hot[3]
    r = lax.broadcasted_iota(jnp.int32, (tm, tm), 0)
    c = lax.broadcasted_iota(jnp.int32, (tm, tm), 1)
    lower = jnp.where(r > c, 1.0, 0.0).astype(BF16)
    before = jnp.dot(lower, multi.astype(BF16), preferred_element_type=F32) + carry_s[0:1, :]
    ri = jnp.zeros((tm, LANES), F32)
    rg = jnp.zeros((tm, LANES), F32)
    for j in range(TOP_K):
        rank = jnp.sum(before * hot[j], axis=-1, keepdims=True)
        ri = ri + jnp.where(lane == float(j), idx[j], 0.0) + jnp.where(lane == float(TOP_K + j), rank, 0.0)
        rg = rg + jnp.where(lane == float(j), ex[j] / tot, 0.0)
    ri_ref[...] = ri.astype(jnp.int32)
    rg_ref[...] = rg
    carry_s[...] = carry_s[...] + jnp.sum(multi, axis=0, keepdims=True)
    cnt_ref[...] = carry_s[...]


def _merge(x2d, proj, o_a, o_b, kvm, wout, l1w, l1b, rw, rb, batch, seq, tm):
    t = batch * seq
    per_b = seq // tm
    nmem = kvm.shape[0] // batch
    tile = lambda colblk: pl.BlockSpec((tm, D_MODEL), lambda i: (i, colblk))
    full = lambda shape: pl.BlockSpec(shape, lambda i: (0, 0))
    return pl.pallas_call(
        functools.partial(_merge_kernel, tm=tm),
        out_shape=(jax.ShapeDtypeStruct((t, D_MODEL), F32),
                   jax.ShapeDtypeStruct((t, LANES), jnp.int32),
                   jax.ShapeDtypeStruct((t, LANES), F32),
                   jax.ShapeDtypeStruct((8, LANES), F32)),
        grid=(t // tm,),
        in_specs=[tile(0), tile(COL_GA), tile(COL_GB), tile(COL_GM), tile(COL_MQ), tile(0), tile(0),
                  pl.BlockSpec((nmem, 2 * D_MODEL), lambda i: (i // per_b, 0)),
                  full((D_MODEL, D_MODEL)), full((1, D_MODEL)), full((1, D_MODEL)),
                  full((D_MODEL, LANES)), full((1, LANES))],
        out_specs=(tile(0), pl.BlockSpec((tm, LANES), lambda i: (i, 0)),
                   pl.BlockSpec((tm, LANES), lambda i: (i, 0)), full((8, LANES))),
        scratch_shapes=[pltpu.VMEM((8, LANES), F32)],
        compiler_params=_params(("arbitrary",)),
        name="merge_router",
    )(x2d, proj, proj, proj, proj, o_a, o_b, kvm, wout, l1w, l1b, rw, rb)


def _dispatch_kernel(zblk_ref, zflag_ref, dest_hbm, x_ref, xs_hbm, idx_s, zbuf, sem_i, sem_r, sem_z, *, tm):
    i = pl.program_id(0)
    n = TOP_K * tm

    @pl.when(i == 0)
    def _():
        zbuf[...] = jnp.zeros_like(zbuf)

        def zero_copy(e):
            start = pl.multiple_of(zblk_ref[e] * EXPERT_BLOCK, EXPERT_BLOCK)
            return pltpu.make_async_copy(zbuf, xs_hbm.at[pl.ds(start, EXPERT_BLOCK)], sem_z)

        for e in range(2 * N_EXPERTS):
            @pl.when(zflag_ref[e] == 1)
            def _():
                zero_copy(e).start()
        for e in range(2 * N_EXPERTS):
            @pl.when(zflag_ref[e] == 1)
            def _():
                zero_copy(e).wait()

    cp = pltpu.make_async_copy(dest_hbm.at[pl.ds(pl.multiple_of(i * n, n), n)], idx_s, sem_i)
    cp.start()
    cp.wait()

    def body(g, carry):
        for jj in range(SUBLANES):
            src = x_ref.at[g, pl.ds(jj, 1)]
            for j in range(TOP_K):
                d = idx_s[g * (SUBLANES * TOP_K) + jj * TOP_K + j]
                pltpu.make_async_copy(src, xs_hbm.at[pl.ds(d, 1)], sem_r).start(priority=j % 2)
        return carry

    lax.fori_loop(0, tm // SUBLANES, body, 0)
    for _ in range(TOP_K):
        pltpu.make_async_copy(xs_hbm.at[pl.ds(0, tm)], xs_hbm.at[pl.ds(0, tm)], sem_r).wait()


def _dispatch(last_blk, has_blk, dest, x1, slots, tm):
    t = x1.shape[0]
    grid_spec = pltpu.PrefetchScalarGridSpec(
        num_scalar_prefetch=2,
        grid=(t // tm,),
        in_specs=[pl.BlockSpec(memory_space=pl.ANY),
                  pl.BlockSpec((tm // SUBLANES, SUBLANES, D_MODEL), lambda i, lb, hb: (i, 0, 0))],
        out_specs=pl.BlockSpec(memory_space=pl.ANY),
        scratch_shapes=[pltpu.SMEM((TOP_K * tm,), jnp.int32),
                        pltpu.VMEM((EXPERT_BLOCK, D_MODEL), F32),
                        pltpu.SemaphoreType.DMA, pltpu.SemaphoreType.DMA, pltpu.SemaphoreType.DMA],
    )
    return pl.pallas_call(
        functools.partial(_dispatch_kernel, tm=tm),
        out_shape=jax.ShapeDtypeStruct((slots, D_MODEL), F32),
        grid_spec=grid_spec,
        compiler_params=_params(("arbitrary",)),
        name="moe_dispatch",
    )(last_blk, has_blk, dest, x1.reshape(t // SUBLANES, SUBLANES, D_MODEL))


MLP_TILE = 256
MLP_PIECES = 1 + D_FF // MLP_TILE + D_MODEL // MLP_TILE


def _expert_mlp(x_ref, wgu_ref, bgu_ref, wd_ref, bd_ref, out_ref, rows, xb_s, act_s, between):
    xb_s[...] = x_ref[...].astype(BF16)
    between(0)
    for n in range(D_FF // MLP_TILE):
        cg = slice(n * MLP_TILE, (n + 1) * MLP_TILE)
        cu = slice(D_FF + n * MLP_TILE, D_FF + (n + 1) * MLP_TILE)
        hg = jnp.dot(xb_s[...], wgu_ref[0, :, cg], preferred_element_type=F32) + bgu_ref[0, :, cg]
        hu = jnp.dot(xb_s[...], wgu_ref[0, :, cu], preferred_element_type=F32) + bgu_ref[0, :, cu]
        g = jnp.minimum(hg, SWIGLU_LIMIT)
        u = jnp.clip(hu, -SWIGLU_LIMIT, SWIGLU_LIMIT)
        act_s[:, cg] = ((u + 1.0) * g * jax.nn.sigmoid(SWIGLU_ALPHA * g)).astype(BF16)
        between(1 + n)
    for m in range(D_MODEL // MLP_TILE):
        cm = slice(m * MLP_TILE, (m + 1) * MLP_TILE)
        out_ref[rows, cm] = jnp.dot(act_s[...], wd_ref[0, :, cm], preferred_element_type=F32) + bd_ref[0, :, cm]
        between(1 + D_FF // MLP_TILE + m)


def _expert_kernel(be_ref, nu_ref, tok_hbm, x_hbm,
                   wgu_a, bgu_a, wd_a, bd_a, wgu_b, bgu_b, wd_b, bd_b,
                   ys_ref, xa, xb, xb16, act_s, idx_a, idx_b, sem_i, sem_a, sem_b):
    del be_ref
    i = pl.program_id(0)
    blk_a = 2 * i
    nu = nu_ref[0]
    active = blk_a < nu
    next_active = blk_a + 2 < nu
    half = EXPERT_BLOCK
    per_piece = -(-EXPERT_BLOCK // MLP_PIECES)

    def load_idx(blk, idx_s):
        cp = pltpu.make_async_copy(tok_hbm.at[blk], idx_s, sem_i)
        cp.start()
        cp.wait()

    def issue(idx_s, xbuf, sem, lo, hi):
        for r in range(lo, hi):
            pltpu.make_async_copy(x_hbm.at[pl.ds(idx_s[r], 1)], xbuf.at[pl.ds(r, 1)], sem).start(priority=r % 2)

    def issue_group(idx_s, xbuf, sem):
        return lambda k: issue(idx_s, xbuf, sem, k * per_piece, min((k + 1) * per_piece, EXPERT_BLOCK))

    def no_issue(k):
        del k

    def wait_rows(xbuf, sem):
        pltpu.make_async_copy(x_hbm.at[pl.ds(0, EXPERT_BLOCK)], xbuf, sem).wait()

    rows_a = slice(0, half)
    rows_b = slice(half, 2 * half)

    @pl.when(i == 0)
    def _():
        load_idx(0, idx_a)
        issue(idx_a, xa, sem_a, 0, EXPERT_BLOCK)

    @pl.when(active)
    def _():
        load_idx(blk_a + 1, idx_b)
        wait_rows(xa, sem_a)
        _expert_mlp(xa, wgu_a, bgu_a, wd_a, bd_a, ys_ref, rows_a, xb16, act_s, issue_group(idx_b, xb, sem_b))
        wait_rows(xb, sem_b)

    @pl.when(jnp.logical_and(active, next_active))
    def _():
        load_idx(blk_a + 2, idx_a)
        _expert_mlp(xb, wgu_b, bgu_b, wd_b, bd_b, ys_ref, rows_b, xb16, act_s, issue_group(idx_a, xa, sem_a))

    @pl.when(jnp.logical_and(active, jnp.logical_not(next_active)))
    def _():
        _expert_mlp(xb, wgu_b, bgu_b, wd_b, bd_b, ys_ref, rows_b, xb16, act_s, no_issue)

    @pl.when(jnp.logical_not(active))
    def _():
        ys_ref[...] = jnp.zeros_like(ys_ref)


def _experts(block_e, n_used, slot_tok, x1, wgu, bgu, wd, bd):
    nb = slot_tok.shape[0]
    assert nb % 2 == 0
    wspecs = []
    for off in (0, 1):
        wspecs += [pl.BlockSpec((1, D_MODEL, 2 * D_FF), lambda i, be, nu, off=off: (be[2 * i + off], 0, 0)),
                   pl.BlockSpec((1, 1, 2 * D_FF), lambda i, be, nu, off=off: (be[2 * i + off], 0, 0)),
                   pl.BlockSpec((1, D_FF, D_MODEL), lambda i, be, nu, off=off: (be[2 * i + off], 0, 0)),
                   pl.BlockSpec((1, 1, D_MODEL), lambda i, be, nu, off=off: (be[2 * i + off], 0, 0))]
    grid_spec = pltpu.PrefetchScalarGridSpec(
        num_scalar_prefetch=2,
        grid=(nb // 2,),
        in_specs=[pl.BlockSpec(memory_space=pl.ANY), pl.BlockSpec(memory_space=pl.ANY)] + wspecs,
        out_specs=pl.BlockSpec((2 * EXPERT_BLOCK, D_MODEL), lambda i, be, nu: (i, 0)),
        scratch_shapes=[pltpu.VMEM((EXPERT_BLOCK, D_MODEL), F32), pltpu.VMEM((EXPERT_BLOCK, D_MODEL), F32),
                        pltpu.VMEM((EXPERT_BLOCK, D_MODEL), BF16), pltpu.VMEM((EXPERT_BLOCK, D_FF), BF16),
                        pltpu.SMEM((EXPERT_BLOCK,), jnp.int32), pltpu.SMEM((EXPERT_BLOCK,), jnp.int32),
                        pltpu.SemaphoreType.DMA, pltpu.SemaphoreType.DMA, pltpu.SemaphoreType.DMA],
    )
    return pl.pallas_call(
        _expert_kernel,
        out_shape=jax.ShapeDtypeStruct((nb * EXPERT_BLOCK, D_MODEL), F32),
        grid_spec=grid_spec,
        compiler_params=_params(("arbitrary",)),
        name="moe_experts",
    )(block_e, n_used, slot_tok, x1, wgu, bgu, wd, bd, wgu, bgu, wd, bd)


def _sc_dispatch(x, dest_t, slots):
    t, d = x.shape
    per_w = t // SC_WORKERS
    assert per_w * SC_WORKERS == t and per_w % SC_GATHER_ROWS == 0
    mesh = plsc.VectorSubcoreMesh(core_axis_name="c", subcore_axis_name="s")

    @functools.partial(
        pl.kernel, mesh=mesh,
        out_type=jax.ShapeDtypeStruct((slots, d), x.dtype),
        scratch_types=[pltpu.VMEM((SC_GATHER_ROWS,), jnp.int32),
                       pltpu.VMEM((SC_GATHER_ROWS, d), x.dtype)],
        name="moe_dispatch_sc",
    )
    def scatter_kernel(x_hbm, idx_hbm, out_hbm, idx_v, rows_v):
        wid = lax.axis_index("s") * SC_CORES + lax.axis_index("c")
        base = wid * per_w

        @pl.loop(0, per_w // SC_GATHER_ROWS)
        def _(c):
            off = base + c * SC_GATHER_ROWS
            pltpu.sync_copy(x_hbm.at[pl.ds(off, SC_GATHER_ROWS)], rows_v)
            for j in range(TOP_K):
                pltpu.sync_copy(idx_hbm.at[pl.ds(j * t + off, SC_GATHER_ROWS)], idx_v)
                pltpu.sync_copy(rows_v, out_hbm.at[idx_v])

    return scatter_kernel(x, dest_t)


def _expert_plain_kernel(be_ref, nu_ref, xs_ref, wgu_ref, bgu_ref, wd_ref, bd_ref, ys_ref):
    del be_ref
    i = pl.program_id(0)

    @pl.when(i < nu_ref[0])
    def _():
        h = jnp.dot(xs_ref[...].astype(BF16), wgu_ref[0], preferred_element_type=F32) + bgu_ref[0]
        g = jnp.minimum(h[:, :D_FF], SWIGLU_LIMIT)
        u = jnp.clip(h[:, D_FF:], -SWIGLU_LIMIT, SWIGLU_LIMIT)
        act = (u + 1.0) * g * jax.nn.sigmoid(SWIGLU_ALPHA * g)
        ys_ref[...] = jnp.dot(act.astype(BF16), wd_ref[0], preferred_element_type=F32) + bd_ref[0]

    @pl.when(i >= nu_ref[0])
    def _():
        ys_ref[...] = jnp.zeros_like(ys_ref)


def _experts_plain(block_e, n_used, xs, wgu, bgu, wd, bd):
    slots = xs.shape[0]
    nb = slots // EXPERT_BLOCK
    grid_spec = pltpu.PrefetchScalarGridSpec(
        num_scalar_prefetch=2,
        grid=(nb,),
        in_specs=[pl.BlockSpec((EXPERT_BLOCK, D_MODEL), lambda i, be, nu: (i, 0)),
                  pl.BlockSpec((1, D_MODEL, 2 * D_FF), lambda i, be, nu: (be[i], 0, 0)),
                  pl.BlockSpec((1, 1, 2 * D_FF), lambda i, be, nu: (be[i], 0, 0)),
                  pl.BlockSpec((1, D_FF, D_MODEL), lambda i, be, nu: (be[i], 0, 0)),
                  pl.BlockSpec((1, 1, D_MODEL), lambda i, be, nu: (be[i], 0, 0))],
        out_specs=pl.BlockSpec((EXPERT_BLOCK, D_MODEL), lambda i, be, nu: (i, 0)),
    )
    return pl.pallas_call(
        _expert_plain_kernel,
        out_shape=jax.ShapeDtypeStruct((slots, D_MODEL), F32),
        grid_spec=grid_spec,
        compiler_params=_params(("arbitrary",)),
        name="moe_experts",
    )(block_e, n_used, xs, wgu, bgu, wd, bd)


def _combine_kernel(dest_hbm, x1_ref, rg_ref, ys_hbm, l2w_ref, l2b_ref, o_ref, idx_s, buf, sem_i, sem_r, *, tm):
    i = pl.program_id(0)
    nsteps = pl.num_programs(0)
    n = TOP_K * tm

    def issue(step, slot):
        cp = pltpu.make_async_copy(dest_hbm.at[pl.ds(pl.multiple_of(step * n, n), n)], idx_s, sem_i)
        cp.start()
        cp.wait()

        def body(g, carry):
            for jj in range(SUBLANES):
                for j in range(TOP_K):
                    d = idx_s[g * (SUBLANES * TOP_K) + jj * TOP_K + j]
                    pltpu.make_async_copy(ys_hbm.at[pl.ds(d, 1)], buf.at[slot, j, g, pl.ds(jj, 1)],
                                          sem_r.at[slot]).start(priority=j % 2)
            return carry

        lax.fori_loop(0, tm // SUBLANES, body, 0)

    @pl.when(i == 0)
    def _():
        issue(0, 0)

    @pl.when(i + 1 < nsteps)
    def _():
        issue(i + 1, (i + 1) % 2)

    slot = i % 2
    for _ in range(TOP_K):
        pltpu.make_async_copy(ys_hbm.at[pl.ds(0, tm)], o_ref, sem_r.at[slot]).wait()
    rg = rg_ref[...]
    moe = rg[:, 0:1] * buf[slot, 0].reshape(tm, D_MODEL)
    for j in range(1, TOP_K):
        moe = moe + rg[:, j:j + 1] * buf[slot, j].reshape(tm, D_MODEL)
    o_ref[...] = _layernorm(DN_ALPHA * x1_ref[...] + moe, l2w_ref[...], l2b_ref[...])


def _combine(dest, x1, rg, ys, l2w, l2b, tm):
    t = x1.shape[0]
    full = lambda shape: pl.BlockSpec(shape, lambda i: (0, 0))
    return pl.pallas_call(
        functools.partial(_combine_kernel, tm=tm),
        out_shape=jax.ShapeDtypeStruct((t, D_MODEL), F32),
        grid=(t // tm,),
        in_specs=[pl.BlockSpec(memory_space=pl.ANY),
                  pl.BlockSpec((tm, D_MODEL), lambda i: (i, 0)),
                  pl.BlockSpec((tm, LANES), lambda i: (i, 0)),
                  pl.BlockSpec(memory_space=pl.ANY),
                  full((1, D_MODEL)), full((1, D_MODEL))],
        out_specs=pl.BlockSpec((tm, D_MODEL), lambda i: (i, 0)),
        scratch_shapes=[pltpu.SMEM((TOP_K * tm,), jnp.int32),
                        pltpu.VMEM((2, TOP_K, tm // SUBLANES, SUBLANES, D_MODEL), F32),
                        pltpu.SemaphoreType.DMA, pltpu.SemaphoreType.DMA((2,))],
        compiler_params=_params(("arbitrary",)),
        name="moe_combine",
    )(dest, x1, rg, ys, l2w, l2b)


def _prep_w_in(w):
    zeros = lambda n: jnp.zeros((D_MODEL, n), w.dtype)
    kr = w[:, 5760:5824]
    kr_sw = jnp.concatenate([kr[:, MLA_ROPE // 2:], kr[:, :MLA_ROPE // 2]], axis=1)
    parts = [w[:, 6848:9920], w[:, 0:5120], w[:, 5824:6848],
             w[:, 5120:5504], zeros(128),
             w[:, 5504:5760], kr, zeros(64), kr_sw, zeros(64)]
    return jnp.concatenate(parts, axis=1).astype(BF16)


def _prep_w_uq(w):
    w3 = w.reshape(MLA_Q_LORA, MLA_HEADS, MLA_NOPE + MLA_ROPE)
    rope = w3[:, :, MLA_NOPE:]
    rope_sw = jnp.concatenate([rope[:, :, MLA_ROPE // 2:], rope[:, :, :MLA_ROPE // 2]], axis=-1)
    pad = jnp.zeros((MLA_Q_LORA, MLA_HEADS, QK_PAD - MLA_NOPE - MLA_ROPE), w.dtype)
    main = jnp.concatenate([w3, pad], axis=-1).reshape(MLA_Q_LORA, MLA_HEADS * QK_PAD)
    swp = jnp.concatenate([rope_sw, pad], axis=-1).reshape(MLA_Q_LORA, MLA_HEADS * LANES)
    both = jnp.concatenate([main, swp], axis=1)
    return jnp.pad(both, ((0, 512 - MLA_Q_LORA), (0, 0))).astype(BF16)


def _rope_tables(seq):
    inv_freq = ROPE_THETA ** (-jnp.arange(0, MLA_ROPE, 2, dtype=F32) / MLA_ROPE)
    ang = jnp.arange(seq, dtype=F32)[:, None] * inv_freq[None, :]
    cos, sin = jnp.cos(ang), jnp.sin(ang)
    pad = jnp.zeros((seq, LANES - MLA_ROPE), F32)
    return (jnp.concatenate([cos, cos, pad], axis=1), jnp.concatenate([-sin, sin, pad], axis=1))


def _tile(n, pref):
    return pref if n % pref == 0 else n


def _layer(x, mem, wts):
    batch, seq, _ = x.shape
    t = batch * seq
    x2d = x.reshape(t, D_MODEL)

    proj = _matmul(x2d, wts["w_in"], BF16, _tile(t, 1024), 1024, "in_proj")
    o_a = _hgrn(proj, wts["lb"], wts["hgrn_norm_w"], batch, seq, _tile(seq, 512))

    cos, sin = _rope_tables(seq)
    q, k, v = _mla_prep(proj, wts["qnw"], wts["kvnw"], wts["w_uq"], wts["w_uk"], wts["w_uv"],
                        cos, sin, batch, seq, _tile(seq, 512))
    o_b = _flash(q, k, v, batch, seq, _tile(seq, 1024), _tile(seq, 1024))

    nmem = mem.shape[1]
    kvm = _matmul(mem.reshape(batch * nmem, D_MODEL), wts["mem_w_kv"], BF16,
                  _tile(batch * nmem, 512), 1024, "mem_kv")

    tm = _tile(seq, 512)
    x1, ri, rg, cnt = _merge(x2d, proj, o_a, o_b, kvm, wts["w_out"], wts["ln1_w"], wts["ln1_b"],
                             wts["router_w"], wts["router_b"], batch, seq, tm)

    idx = ri[:, :TOP_K]
    rank = ri[:, TOP_K:2 * TOP_K]
    counts = cnt[0, :N_EXPERTS].astype(jnp.int32)
    padded = (counts + EXPERT_BLOCK - 1) // EXPERT_BLOCK * EXPERT_BLOCK
    pad_end = jnp.cumsum(padded)
    pad_start = pad_end - padded
    dest = (pad_start[idx] + rank).reshape(t * TOP_K).astype(jnp.int32)
    nb = t * TOP_K // EXPERT_BLOCK + N_EXPERTS
    blk_start = jnp.arange(nb, dtype=jnp.int32) * EXPERT_BLOCK
    block_e = jnp.minimum(jnp.sum((pad_end[None, :] <= blk_start[:, None]).astype(jnp.int32), axis=1),
                          N_EXPERTS - 1).astype(jnp.int32)
    n_used = (pad_end[-1:] // EXPERT_BLOCK).astype(jnp.int32)
    dest_t = dest.reshape(t, TOP_K).T.reshape(TOP_K * t)
    xs = _sc_dispatch(x1, dest_t, nb * EXPERT_BLOCK)
    ys = _experts_plain(block_e, n_used, xs, wts["exp_w_gu"], wts["exp_b_gu"], wts["exp_w_down"],
                        wts["exp_b_down"])
    y = _combine(dest, x1, rg, ys, wts["ln2_w"], wts["ln2_b"], _tile(t, 256))
    return y.reshape(batch, seq, D_MODEL)


def kernel(x_prompt, x_sample, mem_prompt, mem_sample, w_in, hgrn_lb_logits, hgrn_norm_w,
           mla_q_norm_w, mla_w_uq, mla_kv_norm_w, mla_w_uk, mla_w_uv, mem_w_kv, w_out,
           ln1_w, ln1_b, router_w, router_b, exp_w_gu, exp_b_gu, exp_w_down, exp_b_down,
           ln2_w, ln2_b):
    depth = w_in.shape[0]
    gamma = jax.nn.softmax(hgrn_lb_logits.astype(F32), axis=1)
    cum = jnp.cumsum(gamma, axis=1)
    lb_all = cum[:, 1:] - cum[:, :1]
    y_prompt, y_sample = x_prompt, x_sample
    for l in range(depth):
        row = lambda a: a[l].reshape(1, -1).astype(F32)
        wts = {
            "w_in": _prep_w_in(w_in[l]),
            "lb": lb_all[:, l],
            "hgrn_norm_w": hgrn_norm_w[l].astype(F32),
            "qnw": jnp.pad(row(mla_q_norm_w), ((0, 0), (0, 512 - MLA_Q_LORA))),
            "kvnw": row(mla_kv_norm_w),
            "w_uq": _prep_w_uq(mla_w_uq[l]),
            "w_uk": mla_w_uk[l].astype(BF16),
            "w_uv": mla_w_uv[l].astype(BF16),
            "mem_w_kv": mem_w_kv[l].astype(BF16),
            "w_out": w_out[l].astype(BF16),
            "ln1_w": row(ln1_w), "ln1_b": row(ln1_b),
            "router_w": jnp.pad(router_w[l].astype(F32), ((0, 0), (0, LANES - N_EXPERTS))),
            "router_b": jnp.pad(row(router_b), ((0, 0), (0, LANES - N_EXPERTS)), constant_values=-jnp.inf),
            "exp_w_gu": exp_w_gu[l].astype(BF16),
            "exp_b_gu": exp_b_gu[l].reshape(N_EXPERTS, 1, 2 * D_FF).astype(F32),
            "exp_w_down": exp_w_down[l].astype(BF16),
            "exp_b_down": exp_b_down[l].reshape(N_EXPERTS, 1, D_MODEL).astype(F32),
            "ln2_w": row(ln2_w), "ln2_b": row(ln2_b),
        }
        y_prompt = _layer(y_prompt, mem_prompt, wts)
        y_sample = _layer(y_sample, mem_sample, wts)
    return (y_prompt, y_sample)
```

```python
import functools

import numpy as np
import jax
import jax.numpy as jnp
from jax import lax
from jax.experimental import pallas as pl
from jax.experimental.pallas import tpu as pltpu
from jax.experimental.pallas import tpu_sc as plsc

F32 = jnp.float32
BF16 = jnp.bfloat16

D_MODEL = 1024
HGRN_HEADS = 8
HGRN_DK = 128
MLA_HEADS = 8
MLA_Q_LORA = 384
MLA_KV_LORA = 256
MLA_NOPE = 128
MLA_ROPE = 64
MLA_V = 128
ROPE_THETA = 10000.0
MEM_HEADS = 4
MEM_HEAD_DIM = D_MODEL // MEM_HEADS
N_EXPERTS = 32
TOP_K = 4
D_FF = D_MODEL
SWIGLU_LIMIT = 7.0
SWIGLU_ALPHA = 1.702
DN_ALPHA = 2.0 ** 0.25
LN_EPS = 1e-5
RMS_EPS = 1e-6

LANES = 128
SUBLANES = 8
QK_PAD = 256
V_PAD = 256
LOG2E = 1.4426950408889634
HGRN_CHUNK = 64
HGRN_SAFE_RANGE = 160.0
EXPERT_BLOCK = 512
SC_CORES = 2
SC_SUBCORES = 16
SC_WORKERS = SC_CORES * SC_SUBCORES
SC_GATHER_ROWS = 64
VMEM_LIMIT = 56 * 1024 * 1024

COL_GA, COL_GB, COL_GM, COL_Q, COL_ZF, COL_ZB, COL_I, COL_G, COL_MQ = range(9)
COL_DQ, COL_DKV = 18, 19
IN_COLS_PAD = 10240

NT_DIMS = (((1,), (1,)), ((), ()))
TN_DIMS = (((0,), (0,)), ((), ()))


def _params(sem, vmem=VMEM_LIMIT):
    return pltpu.CompilerParams(dimension_semantics=sem, vmem_limit_bytes=vmem)


def _mm_kernel(x_ref, w_ref, o_ref, xb_ref):
    @pl.when(pl.program_id(1) == 0)
    def _():
        xb_ref[...] = x_ref[...].astype(BF16)

    o_ref[...] = jnp.dot(xb_ref[...], w_ref[...], preferred_element_type=F32).astype(o_ref.dtype)


def _matmul(x, w, out_dtype, tm, tn, name):
    m, k = x.shape
    n = w.shape[1]
    return pl.pallas_call(
        _mm_kernel,
        out_shape=jax.ShapeDtypeStruct((m, n), out_dtype),
        grid=(m // tm, n // tn),
        in_specs=[pl.BlockSpec((tm, k), lambda i, j: (i, 0)),
                  pl.BlockSpec((k, tn), lambda i, j: (0, j))],
        out_specs=pl.BlockSpec((tm, tn), lambda i, j: (i, j)),
        scratch_shapes=[pltpu.VMEM((tm, k), BF16)],
        compiler_params=_params(("arbitrary", "arbitrary")),
        name=name,
    )(x, w)


def _hgrn_kernel(*refs, reverse, epilogue, sc):
    if epilogue:
        lb_ref, q_ref, z_ref, v_ref, of_ref, g_ref, nw_ref, o_ref, st_ref, kk_s, b_s, vf_s, os_s = refs
    else:
        lb_ref, q_ref, z_ref, v_ref, o_ref, st_ref, kk_s, b_s, vf_s = refs
    C = HGRN_CHUNK
    nch = sc // C

    @pl.when(pl.program_id(1) == 0)
    def _():
        st_ref[...] = jnp.zeros_like(st_ref)

    lb = lb_ref[...]
    row = lax.broadcasted_iota(jnp.int32, (C, C), 0)
    col = lax.broadcasted_iota(jnp.int32, (C, C), 1)
    tri = (row <= col) if reverse else (row >= col)
    trib = jnp.where(tri, 1.0, 0.0).astype(BF16)

    minb = None
    for c in range(nch):
        rows = slice(c * C, (c + 1) * C)
        z = z_ref[rows, :].astype(F32)
        gate = (1.0 - lb) * jax.nn.sigmoid(z)
        lf = jnp.log(lb + gate)
        kk_s[rows, :] = (1.0 - lb) - gate
        hi = lf.astype(BF16)
        r1 = lf - hi.astype(F32)
        mid = r1.astype(BF16)
        lo = (r1 - mid.astype(F32)).astype(BF16)
        b = (jnp.dot(trib, hi, preferred_element_type=F32)
             + jnp.dot(trib, mid, preferred_element_type=F32)
             + jnp.dot(trib, lo, preferred_element_type=F32))
        b_s[rows, :] = b
        mb = jnp.min(b)
        minb = mb if minb is None else jnp.minimum(minb, mb)

    rid = lax.broadcasted_iota(jnp.int32, (C, 1), 0)

    def chunk(i, carry, fast):
        c = (nch - 1 - i) if reverse else i
        r0 = pl.multiple_of(c * C, C)
        rows = pl.ds(r0, C)
        for h in range(HGRN_HEADS):
            cols = slice(h * HGRN_DK, (h + 1) * HGRN_DK)
            q = q_ref[rows, cols].astype(F32)
            kk = kk_s[rows, cols]
            b = b_s[rows, cols]
            v = v_ref[rows, cols]
            bl = b[0:1, :] if reverse else b[C - 1:C, :]
            if fast:
                bm = 0.5 * bl
                qd = (q * jnp.exp(b - bm)).astype(BF16)
                kd = (kk * jnp.exp(bm - b)).astype(BF16)
                s = lax.dot_general(qd, kd, NT_DIMS, preferred_element_type=F32)
                s = jnp.where(tri, s, 0.0).astype(BF16)
                o = jnp.dot(s, v, preferred_element_type=F32)
            else:
                def sbody(g_i, o_acc):
                    grp = pl.ds(pl.multiple_of(r0 + g_i * SUBLANES, SUBLANES), SUBLANES)
                    b8 = b_s[grp, cols]
                    k8 = kk_s[grp, cols]
                    v8 = vf_s[grp, cols]
                    for jj in range(SUBLANES):
                        s_i = g_i * SUBLANES + jj
                        w = q * k8[jj:jj + 1, :] * jnp.exp(jnp.minimum(b - b8[jj:jj + 1, :], 0.0))
                        scol = jnp.sum(w, axis=-1, keepdims=True)
                        keep = (rid <= s_i) if reverse else (rid >= s_i)
                        o_acc = o_acc + jnp.where(keep, scol, 0.0) * v8[jj:jj + 1, :]
                    return o_acc
                o = lax.fori_loop(0, C // SUBLANES, sbody, jnp.zeros((C, HGRN_DK), F32))
            st = st_ref[h]
            qi = (q * jnp.exp(b)).astype(BF16)
            o = o + lax.dot_general(qi, st.astype(BF16), NT_DIMS, preferred_element_type=F32)
            ke = (kk * jnp.exp(bl - b)).astype(BF16)
            upd = lax.dot_general(v, ke, TN_DIMS, preferred_element_type=F32)
            st_ref[h] = st * jnp.exp(bl) + upd
            if epilogue:
                os_s[rows, cols] = o + of_ref[rows, cols].astype(F32)
            else:
                o_ref[rows, cols] = o.astype(o_ref.dtype)
        return carry

    safe = minb >= -HGRN_SAFE_RANGE

    @pl.when(safe)
    def _():
        lax.fori_loop(0, nch, functools.partial(chunk, fast=True), 0, unroll=2)

    @pl.when(jnp.logical_not(safe))
    def _():
        vf_s[...] = v_ref[...].astype(F32)
        lax.fori_loop(0, nch, functools.partial(chunk, fast=False), 0)

    if epilogue:
        nw = nw_ref[...]
        for h in range(HGRN_HEADS):
            cols = slice(h * HGRN_DK, (h + 1) * HGRN_DK)
            os = os_s[:, cols]
            ms = jnp.mean(os * os, axis=-1, keepdims=True)
            y = os * lax.rsqrt(ms + RMS_EPS) * nw
            g = g_ref[:, cols].astype(F32)
            o_ref[:, cols] = (y * (g * jax.nn.sigmoid(g))).astype(o_ref.dtype)


def _hgrn(proj, lb, norm_w, batch, seq, sc):
    t = batch * seq
    ns = seq // sc
    blk = (sc, D_MODEL)

    def spec(colblk, reverse):
        if reverse:
            return pl.BlockSpec(blk, lambda b, n: (b * ns + ns - 1 - n, colblk))
        return pl.BlockSpec(blk, lambda b, n: (b * ns + n, colblk))

    def row_spec(reverse):
        if reverse:
            return pl.BlockSpec(blk, lambda b, n: (b * ns + ns - 1 - n, 0))
        return pl.BlockSpec(blk, lambda b, n: (b * ns + n, 0))

    lb_spec = pl.BlockSpec((1, D_MODEL), lambda b, n: (0, 0))
    common_scratch = [pltpu.VMEM((HGRN_HEADS, HGRN_DK, HGRN_DK), F32),
                      pltpu.VMEM(blk, F32), pltpu.VMEM(blk, F32), pltpu.VMEM(blk, F32)]
    o_f = pl.pallas_call(
        functools.partial(_hgrn_kernel, reverse=False, epilogue=False, sc=sc),
        out_shape=jax.ShapeDtypeStruct((t, D_MODEL), BF16),
        grid=(batch, ns),
        in_specs=[lb_spec, spec(COL_Q, False), spec(COL_ZF, False), spec(COL_I, False)],
        out_specs=row_spec(False),
        scratch_shapes=common_scratch,
        compiler_params=_params(("arbitrary", "arbitrary")),
        name="hgrn_fwd",
    )(lb[0:1], proj, proj, proj)
    o_a = pl.pallas_call(
        functools.partial(_hgrn_kernel, reverse=True, epilogue=True, sc=sc),
        out_shape=jax.ShapeDtypeStruct((t, D_MODEL), BF16),
        grid=(batch, ns),
        in_specs=[lb_spec, spec(COL_Q, True), spec(COL_ZB, True), spec(COL_I, True),
                  row_spec(True), spec(COL_G, True),
                  pl.BlockSpec((1, HGRN_DK), lambda b, n: (0, 0))],
        out_specs=row_spec(True),
        scratch_shapes=common_scratch + [pltpu.VMEM(blk, F32)],
        compiler_params=_params(("arbitrary", "arbitrary")),
        name="hgrn_bwd",
    )(lb[1:2], proj, proj, proj, o_f, proj, norm_w.reshape(1, HGRN_DK))
    return o_a


def _mla_prep_kernel(dq_ref, dkv_ref, qnw_ref, kvnw_ref, wq_ref, wk_ref, wv_ref, cos_ref, sin_ref,
                     q_ref, k_ref, v_ref):
    scale = (MLA_NOPE + MLA_ROPE) ** -0.5 * LOG2E
    cos = cos_ref[...]
    sin = sin_ref[...]
    dq = dq_ref[...].astype(F32)
    ms = jnp.sum(dq * dq, axis=-1, keepdims=True) * (1.0 / MLA_Q_LORA)
    cq = (dq * lax.rsqrt(ms + RMS_EPS) * qnw_ref[...]).astype(BF16)
    qa = jnp.dot(cq, wq_ref[...], preferred_element_type=F32)
    for h in range(MLA_HEADS):
        base = h * QK_PAD
        q_ref[:, base:base + MLA_NOPE] = (qa[:, base:base + MLA_NOPE] * scale).astype(BF16)
        rp = qa[:, base + MLA_NOPE:base + QK_PAD]
        sw = qa[:, MLA_HEADS * QK_PAD + h * LANES:MLA_HEADS * QK_PAD + (h + 1) * LANES]
        q_ref[:, base + MLA_NOPE:base + QK_PAD] = ((rp * cos + sw * sin) * scale).astype(BF16)
    dkv = dkv_ref[...].astype(F32)
    ckv = dkv[:, :MLA_KV_LORA]
    msk = jnp.mean(ckv * ckv, axis=-1, keepdims=True)
    cn = (ckv * lax.rsqrt(msk + RMS_EPS) * kvnw_ref[...]).astype(BF16)
    kn = jnp.dot(cn, wk_ref[...], preferred_element_type=F32)
    vv = jnp.dot(cn, wv_ref[...], preferred_element_type=F32).astype(BF16)
    ones_col = jnp.where(lax.broadcasted_iota(jnp.int32, (vv.shape[0], LANES), 1) == 0, 1.0, 0.0).astype(BF16)
    for h in range(MLA_HEADS):
        v_ref[:, h * V_PAD:h * V_PAD + MLA_V] = vv[:, h * MLA_V:(h + 1) * MLA_V]
        v_ref[:, h * V_PAD + MLA_V:(h + 1) * V_PAD] = ones_col
    kr = (dkv[:, MLA_KV_LORA:MLA_KV_LORA + LANES] * cos
          + dkv[:, MLA_KV_LORA + LANES:MLA_KV_LORA + 2 * LANES] * sin).astype(BF16)
    for h in range(MLA_HEADS):
        base = h * QK_PAD
        k_ref[:, base:base + MLA_NOPE] = kn[:, h * MLA_NOPE:(h + 1) * MLA_NOPE].astype(BF16)
        k_ref[:, base + MLA_NOPE:base + QK_PAD] = kr


def _mla_prep(proj, qnw, kvnw, wq, wk, wv, cos, sin, batch, seq, tm):
    t = batch * seq
    npos = seq // tm
    full = lambda shape: pl.BlockSpec(shape, lambda i: (0, 0))
    return pl.pallas_call(
        _mla_prep_kernel,
        out_shape=(jax.ShapeDtypeStruct((t, MLA_HEADS * QK_PAD), BF16),
                   jax.ShapeDtypeStruct((t, MLA_HEADS * QK_PAD), BF16),
                   jax.ShapeDtypeStruct((t, MLA_HEADS * V_PAD), BF16)),
        grid=(t // tm,),
        in_specs=[pl.BlockSpec((tm, 512), lambda i: (i, COL_DQ)),
                  pl.BlockSpec((tm, 512), lambda i: (i, COL_DKV)),
                  full((1, 512)), full((1, MLA_KV_LORA)),
                  full(wq.shape), full(wk.shape), full(wv.shape),
                  pl.BlockSpec((tm, LANES), lambda i: (i % npos, 0)),
                  pl.BlockSpec((tm, LANES), lambda i: (i % npos, 0))],
        out_specs=(pl.BlockSpec((tm, MLA_HEADS * QK_PAD), lambda i: (i, 0)),
                   pl.BlockSpec((tm, MLA_HEADS * QK_PAD), lambda i: (i, 0)),
                   pl.BlockSpec((tm, MLA_HEADS * V_PAD), lambda i: (i, 0))),
        compiler_params=_params(("arbitrary",)),
        name="mla_prep",
    )(proj, proj, qnw, kvnw, wq, wk, wv, cos, sin)


def _flash_kernel(q_ref, k_ref, v_ref, o_ref, m_s, acc_s, s_buf, *, bk, nk):
    m_s[...] = jnp.full(m_s.shape, -jnp.inf, F32)
    acc_s[...] = jnp.zeros(acc_s.shape, F32)
    nt = bk // LANES
    bq = q_ref.shape[0]
    nsplit = 2 if bq % 256 == 0 else 1
    hq = bq // nsplit
    qrs = [slice(hf * hq, (hf + 1) * hq) for hf in range(nsplit)]

    def scores(j, slot):
        rows = pl.ds(pl.multiple_of(j * bk, bk), bk)
        for qr in qrs:
            s_buf[slot, qr, :] = lax.dot_general(q_ref[qr, :], k_ref[rows, :], NT_DIMS,
                                                 preferred_element_type=F32)

    def consume(j, slot):
        rows = pl.ds(pl.multiple_of(j * bk, bk), bk)
        ps, alphas = [], []
        for qr in qrs:
            tiles = [s_buf[slot, qr, t * LANES:(t + 1) * LANES] for t in range(nt)]
            tmax = tiles[0]
            for t in range(1, nt):
                tmax = jnp.maximum(tmax, tiles[t])
            m_prev = m_s[qr, :]
            m_new = jnp.maximum(m_prev, jnp.max(tmax, axis=-1, keepdims=True))
            alphas.append(jnp.exp2(m_prev - m_new))
            ps.append(jnp.concatenate([jnp.exp2(tl - m_new).astype(BF16) for tl in tiles], axis=1))
            m_s[qr, :] = m_new
        for qr, p, a in zip(qrs, ps, alphas):
            pv = jnp.dot(p, v_ref[rows, :], preferred_element_type=F32)
            acc_s[qr, :] = jnp.concatenate([a, a], axis=1) * acc_s[qr, :] + pv

    scores(0, 0)
    npairs = (nk - 1) // 2

    def body(i, carry):
        j = 2 * i
        scores(j + 1, 1)
        consume(j, 0)
        scores(j + 2, 0)
        consume(j + 1, 1)
        return carry

    lax.fori_loop(0, npairs, body, 0)
    if nk - 2 * npairs == 2:
        scores(nk - 1, 1)
        consume(nk - 2, 0)
        consume(nk - 1, 1)
    else:
        consume(nk - 1, 0)
    acc = acc_s[...]
    o_ref[...] = (acc[:, :MLA_V] / acc[:, MLA_V:MLA_V + 1]).astype(o_ref.dtype)


def _flash(q, k, v, batch, seq, bq, bk):
    t = batch * seq
    nq = seq // bq
    return pl.pallas_call(
        functools.partial(_flash_kernel, bk=bk, nk=seq // bk),
        out_shape=jax.ShapeDtypeStruct((t, MLA_HEADS * MLA_V), BF16),
        grid=(batch, MLA_HEADS, nq),
        in_specs=[pl.BlockSpec((bq, QK_PAD), lambda b, h, i: (b * nq + i, h)),
                  pl.BlockSpec((seq, QK_PAD), lambda b, h, i: (b, h)),
                  pl.BlockSpec((seq, V_PAD), lambda b, h, i: (b, h))],
        out_specs=pl.BlockSpec((bq, MLA_V), lambda b, h, i: (b * nq + i, h)),
        scratch_shapes=[pltpu.VMEM((bq, LANES), F32), pltpu.VMEM((bq, V_PAD), F32),
                        pltpu.VMEM((2, bq, bk), F32)],
        compiler_params=_params(("arbitrary", "arbitrary", "arbitrary")),
        name="mla_flash",
    )(q, k, v)


def _layernorm(y, w, b):
    mu = jnp.mean(y, axis=-1, keepdims=True)
    yc = y - mu
    var = jnp.mean(yc * yc, axis=-1, keepdims=True)
    return yc * lax.rsqrt(var + LN_EPS) * w + b


def _merge_kernel(x_ref, ga_ref, gb_ref, gm_ref, mq_ref, oa_ref, ob_ref, kvm_ref, wout_ref,
                  l1w_ref, l1b_ref, rw_ref, rb_ref,
                  x1_ref, ri_ref, rg_ref, cnt_ref, carry_s, *, tm):
    @pl.when(pl.program_id(0) == 0)
    def _():
        carry_s[...] = jnp.zeros_like(carry_s)

    parts = []
    for h in range(MEM_HEADS):
        cols = slice(h * MEM_HEAD_DIM, (h + 1) * MEM_HEAD_DIM)
        kh = kvm_ref[:, cols]
        vh = kvm_ref[:, D_MODEL + h * MEM_HEAD_DIM:D_MODEL + (h + 1) * MEM_HEAD_DIM]
        s = lax.dot_general(mq_ref[:, cols], kh, NT_DIMS, preferred_element_type=F32) * (MEM_HEAD_DIM ** -0.5)
        s = s - jnp.max(s, axis=-1, keepdims=True)
        p = jnp.exp(s)
        p = p / jnp.sum(p, axis=-1, keepdims=True)
        parts.append(jnp.dot(p.astype(BF16), vh, preferred_element_type=F32))
    om = jnp.concatenate(parts, axis=1)

    merged = (jax.nn.sigmoid(ga_ref[...].astype(F32)) * oa_ref[...].astype(F32)
              + jax.nn.sigmoid(gb_ref[...].astype(F32)) * ob_ref[...].astype(F32)
              + jax.nn.sigmoid(gm_ref[...].astype(F32)) * om)
    y = DN_ALPHA * x_ref[...] + jnp.dot(merged.astype(BF16), wout_ref[...], preferred_element_type=F32)
    x1 = _layernorm(y, l1w_ref[...], l1b_ref[...])
    x1_ref[...] = x1

    logits = jnp.dot(x1, rw_ref[...], preferred_element_type=F32, precision=lax.Precision.HIGHEST) + rb_ref[...]
    lane_i = lax.broadcasted_iota(jnp.int32, (tm, LANES), 1)
    lane = lane_i.astype(F32)
    work = logits
    idx, val = [], []
    for _ in range(TOP_K):
        mx = jnp.max(work, axis=-1, keepdims=True)
        ix = jnp.min(jnp.where(work == mx, lane, float(LANES)), axis=-1, keepdims=True)
        idx.append(ix)
        val.append(mx)
        work = jnp.where(lane == ix, -jnp.inf, work)
    ex = [jnp.exp(v - val[0]) for v in val]
    tot = ex[0] + ex[1] + ex[2] + ex[3]
    hot = [jnp.where(lane == ix, 1.0, 0.0) for ix in idx]
    multi = hot[0] + hot[1] + hot[2] + hot[3]
    r = lax.broadcasted_iota(jnp.int32, (tm, tm), 0)
    c = lax.broadcasted_iota(jnp.int32, (tm, tm), 1)
    lower = jnp.where(r > c, 1.0, 0.0).astype(BF16)
    before = jnp.dot(lower, multi.astype(BF16), preferred_element_type=F32) + carry_s[0:1, :]
    ri = jnp.zeros((tm, LANES), F32)
    rg = jnp.zeros((tm, LANES), F32)
    for j in range(TOP_K):
        rank = jnp.sum(before * hot[j], axis=-1, keepdims=True)
        ri = ri + jnp.where(lane == float(j), idx[j], 0.0) + jnp.where(lane == float(TOP_K + j), rank, 0.0)
        rg = rg + jnp.where(lane == float(j), ex[j] / tot, 0.0)
    ri_ref[...] = ri.astype(jnp.int32)
    rg_ref[...] = rg
    carry_s[...] = carry_s[...] + jnp.sum(multi, axis=0, keepdims=True)
    cnt_ref[...] = carry_s[...]


def _merge(x2d, proj, o_a, o_b, kvm, wout, l1w, l1b, rw, rb, batch, seq, tm):
    t = batch * seq
    per_b = seq // tm
    nmem = kvm.shape[0] // batch
    tile = lambda colblk: pl.BlockSpec((tm, D_MODEL), lambda i: (i, colblk))
    full = lambda shape: pl.BlockSpec(shape, lambda i: (0, 0))
    return pl.pallas_call(
        functools.partial(_merge_kernel, tm=tm),
        out_shape=(jax.ShapeDtypeStruct((t, D_MODEL), F32),
                   jax.ShapeDtypeStruct((t, LANES), jnp.int32),
                   jax.ShapeDtypeStruct((t, LANES), F32),
                   jax.ShapeDtypeStruct((8, LANES), F32)),
        grid=(t // tm,),
        in_specs=[tile(0), tile(COL_GA), tile(COL_GB), tile(COL_GM), tile(COL_MQ), tile(0), tile(0),
                  pl.BlockSpec((nmem, 2 * D_MODEL), lambda i: (i // per_b, 0)),
                  full((D_MODEL, D_MODEL)), full((1, D_MODEL)), full((1, D_MODEL)),
                  full((D_MODEL, LANES)), full((1, LANES))],
        out_specs=(tile(0), pl.BlockSpec((tm, LANES), lambda i: (i, 0)),
                   pl.BlockSpec((tm, LANES), lambda i: (i, 0)), full((8, LANES))),
        scratch_shapes=[pltpu.VMEM((8, LANES), F32)],
        compiler_params=_params(("arbitrary",)),
        name="merge_router",
    )(x2d, proj, proj, proj, proj, o_a, o_b, kvm, wout, l1w, l1b, rw, rb)


def _sc_gather(table, idx):
    p, d = idx.shape[0], table.shape[1]
    per_w = p // SC_WORKERS
    assert per_w * SC_WORKERS == p and per_w % SC_GATHER_ROWS == 0
    mesh = plsc.VectorSubcoreMesh(core_axis_name="c", subcore_axis_name="s")

    @functools.partial(
        pl.kernel, mesh=mesh,
        out_type=jax.ShapeDtypeStruct((p, d), table.dtype),
        scratch_types=[pltpu.VMEM((SC_GATHER_ROWS,), jnp.int32),
                       pltpu.VMEM((SC_GATHER_ROWS, d), table.dtype),
                       pltpu.SemaphoreType.DMA],
        name="moe_gather_sc",
    )
    def gather_kernel(table_hbm, idx_hbm, out_hbm, idx_v, rows_v, sem):
        wid = lax.axis_index("s") * SC_CORES + lax.axis_index("c")
        base = wid * per_w

        @pl.loop(0, per_w // SC_GATHER_ROWS)
        def _(c):
            off = base + c * SC_GATHER_ROWS
            pltpu.sync_copy(idx_hbm.at[pl.ds(off, SC_GATHER_ROWS)], idx_v)
            pltpu.async_copy(table_hbm.at[idx_v], rows_v, sem).wait()
            pltpu.sync_copy(rows_v, out_hbm.at[pl.ds(off, SC_GATHER_ROWS)])

    return gather_kernel(table, idx)


def _sc_dispatch(x, dest_t, slots):
    t, d = x.shape
    per_w = t // SC_WORKERS
    assert per_w * SC_WORKERS == t and per_w % SC_GATHER_ROWS == 0
    mesh = plsc.VectorSubcoreMesh(core_axis_name="c", subcore_axis_name="s")

    @functools.partial(
        pl.kernel, mesh=mesh,
        out_type=jax.ShapeDtypeStruct((slots, d), x.dtype),
        scratch_types=[pltpu.VMEM((SC_GATHER_ROWS,), jnp.int32),
                       pltpu.VMEM((SC_GATHER_ROWS, d), x.dtype)],
        name="moe_dispatch_sc",
    )
    def scatter_kernel(x_hbm, idx_hbm, out_hbm, idx_v, rows_v):
        wid = lax.axis_index("s") * SC_CORES + lax.axis_index("c")
        base = wid * per_w

        @pl.loop(0, per_w // SC_GATHER_ROWS)
        def _(c):
            off = base + c * SC_GATHER_ROWS
            pltpu.sync_copy(x_hbm.at[pl.ds(off, SC_GATHER_ROWS)], rows_v)
            for j in range(TOP_K):
                pltpu.sync_copy(idx_hbm.at[pl.ds(j * t + off, SC_GATHER_ROWS)], idx_v)
                pltpu.sync_copy(rows_v, out_hbm.at[idx_v])

    return scatter_kernel(x, dest_t)


def _expert_kernel(be_ref, nu_ref, xs_ref, wgu_ref, bgu_ref, wd_ref, bd_ref, ys_ref):
    del be_ref
    i = pl.program_id(0)

    @pl.when(i < nu_ref[0])
    def _():
        h = jnp.dot(xs_ref[...].astype(BF16), wgu_ref[0], preferred_element_type=F32) + bgu_ref[0]
        g = jnp.minimum(h[:, :D_FF], SWIGLU_LIMIT)
        u = jnp.clip(h[:, D_FF:], -SWIGLU_LIMIT, SWIGLU_LIMIT)
        act = (u + 1.0) * g * jax.nn.sigmoid(SWIGLU_ALPHA * g)
        ys_ref[...] = jnp.dot(act.astype(BF16), wd_ref[0], preferred_element_type=F32) + bd_ref[0]

    @pl.when(i >= nu_ref[0])
    def _():
        ys_ref[...] = jnp.zeros_like(ys_ref)


def _experts(block_e, n_used, xs, wgu, bgu, wd, bd):
    slots = xs.shape[0]
    nb = slots // EXPERT_BLOCK
    grid_spec = pltpu.PrefetchScalarGridSpec(
        num_scalar_prefetch=2,
        grid=(nb,),
        in_specs=[pl.BlockSpec((EXPERT_BLOCK, D_MODEL), lambda i, be, nu: (i, 0)),
                  pl.BlockSpec((1, D_MODEL, 2 * D_FF), lambda i, be, nu: (be[i], 0, 0)),
                  pl.BlockSpec((1, 1, 2 * D_FF), lambda i, be, nu: (be[i], 0, 0)),
                  pl.BlockSpec((1, D_FF, D_MODEL), lambda i, be, nu: (be[i], 0, 0)),
                  pl.BlockSpec((1, 1, D_MODEL), lambda i, be, nu: (be[i], 0, 0))],
        out_specs=pl.BlockSpec((EXPERT_BLOCK, D_MODEL), lambda i, be, nu: (i, 0)),
    )
    return pl.pallas_call(
        _expert_kernel,
        out_shape=jax.ShapeDtypeStruct((slots, D_MODEL), F32),
        grid_spec=grid_spec,
        compiler_params=_params(("arbitrary",)),
        name="moe_experts",
    )(block_e, n_used, xs, wgu, bgu, wd, bd)


def _combine_kernel(dest_hbm, x1_ref, rg_ref, ys_hbm, l2w_ref, l2b_ref, o_ref, idx_s, buf, sem_i, sem_r, *, tm):
    i = pl.program_id(0)
    nsteps = pl.num_programs(0)
    n = TOP_K * tm

    def issue(step, slot):
        cp = pltpu.make_async_copy(dest_hbm.at[pl.ds(pl.multiple_of(step * n, n), n)], idx_s, sem_i)
        cp.start()
        cp.wait()

        def body(g, carry):
            for jj in range(SUBLANES):
                for j in range(TOP_K):
                    d = idx_s[g * (SUBLANES * TOP_K) + jj * TOP_K + j]
                    pltpu.make_async_copy(ys_hbm.at[pl.ds(d, 1)], buf.at[slot, j, g, pl.ds(jj, 1)],
                                          sem_r.at[slot]).start(priority=j % 2)
            return carry

        lax.fori_loop(0, tm // SUBLANES, body, 0)

    @pl.when(i == 0)
    def _():
        issue(0, 0)

    @pl.when(i + 1 < nsteps)
    def _():
        issue(i + 1, (i + 1) % 2)

    slot = i % 2
    for _ in range(TOP_K):
        pltpu.make_async_copy(ys_hbm.at[pl.ds(0, tm)], o_ref, sem_r.at[slot]).wait()
    rg = rg_ref[...]
    moe = rg[:, 0:1] * buf[slot, 0].reshape(tm, D_MODEL)
    for j in range(1, TOP_K):
        moe = moe + rg[:, j:j + 1] * buf[slot, j].reshape(tm, D_MODEL)
    o_ref[...] = _layernorm(DN_ALPHA * x1_ref[...] + moe, l2w_ref[...], l2b_ref[...])


def _combine_rows_kernel(x1_ref, rg_ref, g0_ref, g1_ref, g2_ref, g3_ref, l2w_ref, l2b_ref, o_ref):
    rg = rg_ref[...]
    moe = rg[:, 0:1] * g0_ref[...]
    for j, g_ref in enumerate((g1_ref, g2_ref, g3_ref), start=1):
        moe = moe + rg[:, j:j + 1] * g_ref[...]
    o_ref[...] = _layernorm(DN_ALPHA * x1_ref[...] + moe, l2w_ref[...], l2b_ref[...])


def _combine_rows(x1, rg, rows, l2w, l2b, tm):
    t = x1.shape[0]
    nt = t // tm
    full = lambda shape: pl.BlockSpec(shape, lambda i: (0, 0))
    choice = lambda j: pl.BlockSpec((tm, D_MODEL), lambda i, j=j: (j * nt + i, 0))
    return pl.pallas_call(
        _combine_rows_kernel,
        out_shape=jax.ShapeDtypeStruct((t, D_MODEL), F32),
        grid=(nt,),
        in_specs=[pl.BlockSpec((tm, D_MODEL), lambda i: (i, 0)),
                  pl.BlockSpec((tm, LANES), lambda i: (i, 0)),
                  choice(0), choice(1), choice(2), choice(3),
                  full((1, D_MODEL)), full((1, D_MODEL))],
        out_specs=pl.BlockSpec((tm, D_MODEL), lambda i: (i, 0)),
        compiler_params=_params(("arbitrary",)),
        name="moe_combine_rows",
    )(x1, rg, rows, rows, rows, rows, l2w, l2b)


def _combine(dest, x1, rg, ys, l2w, l2b, tm):
    t = x1.shape[0]
    full = lambda shape: pl.BlockSpec(shape, lambda i: (0, 0))
    return pl.pallas_call(
        functools.partial(_combine_kernel, tm=tm),
        out_shape=jax.ShapeDtypeStruct((t, D_MODEL), F32),
        grid=(t // tm,),
        in_specs=[pl.BlockSpec(memory_space=pl.ANY),
                  pl.BlockSpec((tm, D_MODEL), lambda i: (i, 0)),
                  pl.BlockSpec((tm, LANES), lambda i: (i, 0)),
                  pl.BlockSpec(memory_space=pl.ANY),
                  full((1, D_MODEL)), full((1, D_MODEL))],
        out_specs=pl.BlockSpec((tm, D_MODEL), lambda i: (i, 0)),
        scratch_shapes=[pltpu.SMEM((TOP_K * tm,), jnp.int32),
                        pltpu.VMEM((2, TOP_K, tm // SUBLANES, SUBLANES, D_MODEL), F32),
                        pltpu.SemaphoreType.DMA, pltpu.SemaphoreType.DMA((2,))],
        compiler_params=_params(("arbitrary",)),
        name="moe_combine",
    )(dest, x1, rg, ys, l2w, l2b)


def _prep_w_in(w):
    zeros = lambda n: jnp.zeros((D_MODEL, n), w.dtype)
    kr = w[:, 5760:5824]
    kr_sw = jnp.concatenate([kr[:, MLA_ROPE // 2:], kr[:, :MLA_ROPE // 2]], axis=1)
    parts = [w[:, 6848:9920], w[:, 0:5120], w[:, 5824:6848],
             w[:, 5120:5504], zeros(128),
             w[:, 5504:5760], kr, zeros(64), kr_sw, zeros(64)]
    return jnp.concatenate(parts, axis=1).astype(BF16)


def _prep_w_uq(w):
    w3 = w.reshape(MLA_Q_LORA, MLA_HEADS, MLA_NOPE + MLA_ROPE)
    rope = w3[:, :, MLA_NOPE:]
    rope_sw = jnp.concatenate([rope[:, :, MLA_ROPE // 2:], rope[:, :, :MLA_ROPE // 2]], axis=-1)
    pad = jnp.zeros((MLA_Q_LORA, MLA_HEADS, QK_PAD - MLA_NOPE - MLA_ROPE), w.dtype)
    main = jnp.concatenate([w3, pad], axis=-1).reshape(MLA_Q_LORA, MLA_HEADS * QK_PAD)
    swp = jnp.concatenate([rope_sw, pad], axis=-1).reshape(MLA_Q_LORA, MLA_HEADS * LANES)
    both = jnp.concatenate([main, swp], axis=1)
    return jnp.pad(both, ((0, 512 - MLA_Q_LORA), (0, 0))).astype(BF16)


def _rope_tables(seq):
    inv_freq = ROPE_THETA ** (-jnp.arange(0, MLA_ROPE, 2, dtype=F32) / MLA_ROPE)
    ang = jnp.arange(seq, dtype=F32)[:, None] * inv_freq[None, :]
    cos, sin = jnp.cos(ang), jnp.sin(ang)
    pad = jnp.zeros((seq, LANES - MLA_ROPE), F32)
    return (jnp.concatenate([cos, cos, pad], axis=1), jnp.concatenate([-sin, sin, pad], axis=1))


def _tile(n, pref):
    return pref if n % pref == 0 else n


def _layer(x, mem, wts, sc_combine):
    batch, seq, _ = x.shape
    t = batch * seq
    x2d = x.reshape(t, D_MODEL)

    proj = _matmul(x2d, wts["w_in"], BF16, _tile(t, 1024), 1024, "in_proj")
    o_a = _hgrn(proj, wts["lb"], wts["hgrn_norm_w"], batch, seq, _tile(seq, 512))

    cos, sin = _rope_tables(seq)
    q, k, v = _mla_prep(proj, wts["qnw"], wts["kvnw"], wts["w_uq"], wts["w_uk"], wts["w_uv"],
                        cos, sin, batch, seq, _tile(seq, 512))
    o_b = _flash(q, k, v, batch, seq, _tile(seq, 1024), _tile(seq, 1024))

    nmem = mem.shape[1]
    kvm = _matmul(mem.reshape(batch * nmem, D_MODEL), wts["mem_w_kv"], BF16,
                  _tile(batch * nmem, 512), 1024, "mem_kv")

    tm = _tile(seq, 512)
    x1, ri, rg, cnt = _merge(x2d, proj, o_a, o_b, kvm, wts["w_out"], wts["ln1_w"], wts["ln1_b"],
                             wts["router_w"], wts["router_b"], batch, seq, tm)

    idx = ri[:, :TOP_K]
    rank = ri[:, TOP_K:2 * TOP_K]
    counts = cnt[0, :N_EXPERTS].astype(jnp.int32)
    padded = (counts + EXPERT_BLOCK - 1) // EXPERT_BLOCK * EXPERT_BLOCK
    pad_end = jnp.cumsum(padded)
    pad_start = pad_end - padded
    dest = (pad_start[idx] + rank).reshape(t * TOP_K).astype(jnp.int32)
    nb = t * TOP_K // EXPERT_BLOCK + N_EXPERTS
    blk_start = jnp.arange(nb, dtype=jnp.int32) * EXPERT_BLOCK
    block_e = jnp.minimum(jnp.sum((pad_end[None, :] <= blk_start[:, None]).astype(jnp.int32), axis=1),
                          N_EXPERTS - 1).astype(jnp.int32)
    n_used = (pad_end[-1:] // EXPERT_BLOCK).astype(jnp.int32)
    dest_t = dest.reshape(t, TOP_K).T.reshape(TOP_K * t)
    xs = _sc_dispatch(x1, dest_t, nb * EXPERT_BLOCK)
    ys = _experts(block_e, n_used, xs, wts["exp_w_gu"], wts["exp_b_gu"], wts["exp_w_down"],
                        wts["exp_b_down"])
    if sc_combine:
        rows = _sc_gather(ys, dest_t)
        y = _combine_rows(x1, rg, rows, wts["ln2_w"], wts["ln2_b"], _tile(t, 512))
    else:
        y = _combine(dest, x1, rg, ys, wts["ln2_w"], wts["ln2_b"], _tile(t, 256))
    return y.reshape(batch, seq, D_MODEL)


def kernel(x_prompt, x_sample, mem_prompt, mem_sample, w_in, hgrn_lb_logits, hgrn_norm_w,
           mla_q_norm_w, mla_w_uq, mla_kv_norm_w, mla_w_uk, mla_w_uv, mem_w_kv, w_out,
           ln1_w, ln1_b, router_w, router_b, exp_w_gu, exp_b_gu, exp_w_down, exp_b_down,
           ln2_w, ln2_b):
    depth = w_in.shape[0]
    gamma = jax.nn.softmax(hgrn_lb_logits.astype(F32), axis=1)
    cum = jnp.cumsum(gamma, axis=1)
    lb_all = cum[:, 1:] - cum[:, :1]
    y_prompt, y_sample = x_prompt, x_sample
    for l in range(depth):
        row = lambda a: a[l].reshape(1, -1).astype(F32)
        wts = {
            "w_in": _prep_w_in(w_in[l]),
            "lb": lb_all[:, l],
            "hgrn_norm_w": hgrn_norm_w[l].astype(F32),
            "qnw": jnp.pad(row(mla_q_norm_w), ((0, 0), (0, 512 - MLA_Q_LORA))),
            "kvnw": row(mla_kv_norm_w),
            "w_uq": _prep_w_uq(mla_w_uq[l]),
            "w_uk": mla_w_uk[l].astype(BF16),
            "w_uv": mla_w_uv[l].astype(BF16),
            "mem_w_kv": mem_w_kv[l].astype(BF16),
            "w_out": w_out[l].astype(BF16),
            "ln1_w": row(ln1_w), "ln1_b": row(ln1_b),
            "router_w": jnp.pad(router_w[l].astype(F32), ((0, 0), (0, LANES - N_EXPERTS))),
            "router_b": jnp.pad(row(router_b), ((0, 0), (0, LANES - N_EXPERTS)), constant_values=-jnp.inf),
            "exp_w_gu": exp_w_gu[l].astype(BF16),
            "exp_b_gu": exp_b_gu[l].reshape(N_EXPERTS, 1, 2 * D_FF).astype(F32),
            "exp_w_down": exp_w_down[l].astype(BF16),
            "exp_b_down": exp_b_down[l].reshape(N_EXPERTS, 1, D_MODEL).astype(F32),
            "ln2_w": row(ln2_w), "ln2_b": row(ln2_b),
        }
        y_prompt = _layer(y_prompt, mem_prompt, wts, sc_combine=True)
        y_sample = _layer(y_sample, mem_sample, wts, sc_combine=False)
    return (y_prompt, y_sample)
```

```python
import functools

import numpy as np
import jax
import jax.numpy as jnp
from jax import lax
from jax.experimental import pallas as pl
from jax.experimental.pallas import tpu as pltpu
from jax.experimental.pallas import tpu_sc as plsc

F32 = jnp.float32
BF16 = jnp.bfloat16

D_MODEL = 1024
HGRN_HEADS = 8
HGRN_DK = 128
MLA_HEADS = 8
MLA_Q_LORA = 384
MLA_KV_LORA = 256
MLA_NOPE = 128
MLA_ROPE = 64
MLA_V = 128
ROPE_THETA = 10000.0
MEM_HEADS = 4
MEM_HEAD_DIM = D_MODEL // MEM_HEADS
N_EXPERTS = 32
TOP_K = 4
D_FF = D_MODEL
SWIGLU_LIMIT = 7.0
SWIGLU_ALPHA = 1.702
DN_ALPHA = 2.0 ** 0.25
LN_EPS = 1e-5
RMS_EPS = 1e-6

LANES = 128
SUBLANES = 8
QK_PAD = 256
V_PAD = 256
LOG2E = 1.4426950408889634
HGRN_CHUNK = 64
HGRN_SAFE_RANGE = 160.0
EXPERT_BLOCK = 512
SC_CORES = 2
SC_SUBCORES = 16
SC_WORKERS = SC_CORES * SC_SUBCORES
SC_GATHER_ROWS = 64
VMEM_LIMIT = 56 * 1024 * 1024

COL_GA, COL_GB, COL_GM, COL_Q, COL_ZF, COL_ZB, COL_I, COL_G, COL_MQ = range(9)
COL_DQ, COL_DKV = 18, 19
IN_COLS_PAD = 10240

NT_DIMS = (((1,), (1,)), ((), ()))
TN_DIMS = (((0,), (0,)), ((), ()))


def _params(sem, vmem=VMEM_LIMIT):
    return pltpu.CompilerParams(dimension_semantics=sem, vmem_limit_bytes=vmem)


def _mm_kernel(x_ref, w_ref, o_ref, xb_ref):
    @pl.when(pl.program_id(1) == 0)
    def _():
        xb_ref[...] = x_ref[...].astype(BF16)

    o_ref[...] = jnp.dot(xb_ref[...], w_ref[...], preferred_element_type=F32).astype(o_ref.dtype)


def _matmul(x, w, out_dtype, tm, tn, name):
    m, k = x.shape
    n = w.shape[1]
    return pl.pallas_call(
        _mm_kernel,
        out_shape=jax.ShapeDtypeStruct((m, n), out_dtype),
        grid=(m // tm, n // tn),
        in_specs=[pl.BlockSpec((tm, k), lambda i, j: (i, 0)),
                  pl.BlockSpec((k, tn), lambda i, j: (0, j))],
        out_specs=pl.BlockSpec((tm, tn), lambda i, j: (i, j)),
        scratch_shapes=[pltpu.VMEM((tm, k), BF16)],
        compiler_params=_params(("arbitrary", "arbitrary")),
        name=name,
    )(x, w)


def _hgrn_kernel(*refs, reverse, epilogue, sc):
    if epilogue:
        lb_ref, q_ref, z_ref, v_ref, of_ref, g_ref, nw_ref, o_ref, st_ref, kk_s, b_s, vf_s, os_s = refs
    else:
        lb_ref, q_ref, z_ref, v_ref, o_ref, st_ref, kk_s, b_s, vf_s = refs
    C = HGRN_CHUNK
    nch = sc // C

    @pl.when(pl.program_id(1) == 0)
    def _():
        st_ref[...] = jnp.zeros_like(st_ref)

    lb = lb_ref[...]
    row = lax.broadcasted_iota(jnp.int32, (C, C), 0)
    col = lax.broadcasted_iota(jnp.int32, (C, C), 1)
    tri = (row <= col) if reverse else (row >= col)
    trib = jnp.where(tri, 1.0, 0.0).astype(BF16)

    minb = None
    for c in range(nch):
        rows = slice(c * C, (c + 1) * C)
        z = z_ref[rows, :].astype(F32)
        gate = (1.0 - lb) * jax.nn.sigmoid(z)
        lf = jnp.log(lb + gate)
        kk_s[rows, :] = (1.0 - lb) - gate
        hi = lf.astype(BF16)
        r1 = lf - hi.astype(F32)
        mid = r1.astype(BF16)
        lo = (r1 - mid.astype(F32)).astype(BF16)
        b = (jnp.dot(trib, hi, preferred_element_type=F32)
             + jnp.dot(trib, mid, preferred_element_type=F32)
             + jnp.dot(trib, lo, preferred_element_type=F32))
        b_s[rows, :] = b
        mb = jnp.min(b)
        minb = mb if minb is None else jnp.minimum(minb, mb)

    rid = lax.broadcasted_iota(jnp.int32, (C, 1), 0)

    def chunk(i, carry, fast):
        c = (nch - 1 - i) if reverse else i
        r0 = pl.multiple_of(c * C, C)
        rows = pl.ds(r0, C)
        for h in range(HGRN_HEADS):
            cols = slice(h * HGRN_DK, (h + 1) * HGRN_DK)
            q = q_ref[rows, cols].astype(F32)
            kk = kk_s[rows, cols]
            b = b_s[rows, cols]
            v = v_ref[rows, cols]
            bl = b[0:1, :] if reverse else b[C - 1:C, :]
            if fast:
                bm = 0.5 * bl
                qd = (q * jnp.exp(b - bm)).astype(BF16)
                kd = (kk * jnp.exp(bm - b)).astype(BF16)
                s = lax.dot_general(qd, kd, NT_DIMS, preferred_element_type=F32)
                s = jnp.where(tri, s, 0.0).astype(BF16)
                o = jnp.dot(s, v, preferred_element_type=F32)
            else:
                def sbody(g_i, o_acc):
                    grp = pl.ds(pl.multiple_of(r0 + g_i * SUBLANES, SUBLANES), SUBLANES)
                    b8 = b_s[grp, cols]
                    k8 = kk_s[grp, cols]
                    v8 = vf_s[grp, cols]
                    for jj in range(SUBLANES):
                        s_i = g_i * SUBLANES + jj
                        w = q * k8[jj:jj + 1, :] * jnp.exp(jnp.minimum(b - b8[jj:jj + 1, :], 0.0))
                        scol = jnp.sum(w, axis=-1, keepdims=True)
                        keep = (rid <= s_i) if reverse else (rid >= s_i)
                        o_acc = o_acc + jnp.where(keep, scol, 0.0) * v8[jj:jj + 1, :]
                    return o_acc
                o = lax.fori_loop(0, C // SUBLANES, sbody, jnp.zeros((C, HGRN_DK), F32))
            st = st_ref[h]
            qi = (q * jnp.exp(b)).astype(BF16)
            o = o + lax.dot_general(qi, st.astype(BF16), NT_DIMS, preferred_element_type=F32)
            ke = (kk * jnp.exp(bl - b)).astype(BF16)
            upd = lax.dot_general(v, ke, TN_DIMS, preferred_element_type=F32)
            st_ref[h] = st * jnp.exp(bl) + upd
            if epilogue:
                os_s[rows, cols] = o + of_ref[rows, cols].astype(F32)
            else:
                o_ref[rows, cols] = o.astype(o_ref.dtype)
        return carry

    safe = minb >= -HGRN_SAFE_RANGE

    @pl.when(safe)
    def _():
        lax.fori_loop(0, nch, functools.partial(chunk, fast=True), 0, unroll=2)

    @pl.when(jnp.logical_not(safe))
    def _():
        vf_s[...] = v_ref[...].astype(F32)
        lax.fori_loop(0, nch, functools.partial(chunk, fast=False), 0)

    if epilogue:
        nw = nw_ref[...]
        for h in range(HGRN_HEADS):
            cols = slice(h * HGRN_DK, (h + 1) * HGRN_DK)
            os = os_s[:, cols]
            ms = jnp.mean(os * os, axis=-1, keepdims=True)
            y = os * lax.rsqrt(ms + RMS_EPS) * nw
            g = g_ref[:, cols].astype(F32)
            o_ref[:, cols] = (y * (g * jax.nn.sigmoid(g))).astype(o_ref.dtype)


def _hgrn(proj, lb, norm_w, batch, seq, sc):
    t = batch * seq
    ns = seq // sc
    blk = (sc, D_MODEL)

    def spec(colblk, reverse):
        if reverse:
            return pl.BlockSpec(blk, lambda b, n: (b * ns + ns - 1 - n, colblk))
        return pl.BlockSpec(blk, lambda b, n: (b * ns + n, colblk))

    def row_spec(reverse):
        if reverse:
            return pl.BlockSpec(blk, lambda b, n: (b * ns + ns - 1 - n, 0))
        return pl.BlockSpec(blk, lambda b, n: (b * ns + n, 0))

    lb_spec = pl.BlockSpec((1, D_MODEL), lambda b, n: (0, 0))
    common_scratch = [pltpu.VMEM((HGRN_HEADS, HGRN_DK, HGRN_DK), F32),
                      pltpu.VMEM(blk, F32), pltpu.VMEM(blk, F32), pltpu.VMEM(blk, F32)]
    o_f = pl.pallas_call(
        functools.partial(_hgrn_kernel, reverse=False, epilogue=False, sc=sc),
        out_shape=jax.ShapeDtypeStruct((t, D_MODEL), BF16),
        grid=(batch, ns),
        in_specs=[lb_spec, spec(COL_Q, False), spec(COL_ZF, False), spec(COL_I, False)],
        out_specs=row_spec(False),
        scratch_shapes=common_scratch,
        compiler_params=_params(("arbitrary", "arbitrary")),
        name="hgrn_fwd",
    )(lb[0:1], proj, proj, proj)
    o_a = pl.pallas_call(
        functools.partial(_hgrn_kernel, reverse=True, epilogue=True, sc=sc),
        out_shape=jax.ShapeDtypeStruct((t, D_MODEL), BF16),
        grid=(batch, ns),
        in_specs=[lb_spec, spec(COL_Q, True), spec(COL_ZB, True), spec(COL_I, True),
                  row_spec(True), spec(COL_G, True),
                  pl.BlockSpec((1, HGRN_DK), lambda b, n: (0, 0))],
        out_specs=row_spec(True),
        scratch_shapes=common_scratch + [pltpu.VMEM(blk, F32)],
        compiler_params=_params(("arbitrary", "arbitrary")),
        name="hgrn_bwd",
    )(lb[1:2], proj, proj, proj, o_f, proj, norm_w.reshape(1, HGRN_DK))
    return o_a


def _mla_prep_kernel(dq_ref, dkv_ref, qnw_ref, kvnw_ref, wq_ref, wk_ref, wv_ref, cos_ref, sin_ref,
                     q_ref, k_ref, v_ref):
    scale = (MLA_NOPE + MLA_ROPE) ** -0.5 * LOG2E
    cos = cos_ref[...]
    sin = sin_ref[...]
    dq = dq_ref[...].astype(F32)
    ms = jnp.sum(dq * dq, axis=-1, keepdims=True) * (1.0 / MLA_Q_LORA)
    cq = (dq * lax.rsqrt(ms + RMS_EPS) * qnw_ref[...]).astype(BF16)
    qa = jnp.dot(cq, wq_ref[...], preferred_element_type=F32)
    for h in range(MLA_HEADS):
        base = h * QK_PAD
        q_ref[:, base:base + MLA_NOPE] = (qa[:, base:base + MLA_NOPE] * scale).astype(BF16)
        rp = qa[:, base + MLA_NOPE:base + QK_PAD]
        sw = qa[:, MLA_HEADS * QK_PAD + h * LANES:MLA_HEADS * QK_PAD + (h + 1) * LANES]
        q_ref[:, base + MLA_NOPE:base + QK_PAD] = ((rp * cos + sw * sin) * scale).astype(BF16)
    dkv = dkv_ref[...].astype(F32)
    ckv = dkv[:, :MLA_KV_LORA]
    msk = jnp.mean(ckv * ckv, axis=-1, keepdims=True)
    cn = (ckv * lax.rsqrt(msk + RMS_EPS) * kvnw_ref[...]).astype(BF16)
    kn = jnp.dot(cn, wk_ref[...], preferred_element_type=F32)
    vv = jnp.dot(cn, wv_ref[...], preferred_element_type=F32).astype(BF16)
    ones_col = jnp.where(lax.broadcasted_iota(jnp.int32, (vv.shape[0], LANES), 1) == 0, 1.0, 0.0).astype(BF16)
    for h in range(MLA_HEADS):
        v_ref[:, h * V_PAD:h * V_PAD + MLA_V] = vv[:, h * MLA_V:(h + 1) * MLA_V]
        v_ref[:, h * V_PAD + MLA_V:(h + 1) * V_PAD] = ones_col
    kr = (dkv[:, MLA_KV_LORA:MLA_KV_LORA + LANES] * cos
          + dkv[:, MLA_KV_LORA + LANES:MLA_KV_LORA + 2 * LANES] * sin).astype(BF16)
    for h in range(MLA_HEADS):
        base = h * QK_PAD
        k_ref[:, base:base + MLA_NOPE] = kn[:, h * MLA_NOPE:(h + 1) * MLA_NOPE].astype(BF16)
        k_ref[:, base + MLA_NOPE:base + QK_PAD] = kr


def _mla_prep(proj, qnw, kvnw, wq, wk, wv, cos, sin, batch, seq, tm):
    t = batch * seq
    npos = seq // tm
    full = lambda shape: pl.BlockSpec(shape, lambda i: (0, 0))
    return pl.pallas_call(
        _mla_prep_kernel,
        out_shape=(jax.ShapeDtypeStruct((t, MLA_HEADS * QK_PAD), BF16),
                   jax.ShapeDtypeStruct((t, MLA_HEADS * QK_PAD), BF16),
                   jax.ShapeDtypeStruct((t, MLA_HEADS * V_PAD), BF16)),
        grid=(t // tm,),
        in_specs=[pl.BlockSpec((tm, 512), lambda i: (i, COL_DQ)),
                  pl.BlockSpec((tm, 512), lambda i: (i, COL_DKV)),
                  full((1, 512)), full((1, MLA_KV_LORA)),
                  full(wq.shape), full(wk.shape), full(wv.shape),
                  pl.BlockSpec((tm, LANES), lambda i: (i % npos, 0)),
                  pl.BlockSpec((tm, LANES), lambda i: (i % npos, 0))],
        out_specs=(pl.BlockSpec((tm, MLA_HEADS * QK_PAD), lambda i: (i, 0)),
                   pl.BlockSpec((tm, MLA_HEADS * QK_PAD), lambda i: (i, 0)),
                   pl.BlockSpec((tm, MLA_HEADS * V_PAD), lambda i: (i, 0))),
        compiler_params=_params(("arbitrary",)),
        name="mla_prep",
    )(proj, proj, qnw, kvnw, wq, wk, wv, cos, sin)


def _flash_kernel(q_ref, k_ref, v_ref, o_ref, m_s, acc_s, s_buf, *, bk, nk):
    m_s[...] = jnp.full(m_s.shape, -jnp.inf, F32)
    acc_s[...] = jnp.zeros(acc_s.shape, F32)
    nt = bk // LANES
    bq = q_ref.shape[0]
    nsplit = 2 if bq % 256 == 0 else 1
    hq = bq // nsplit
    qrs = [slice(hf * hq, (hf + 1) * hq) for hf in range(nsplit)]

    def scores(j, slot):
        rows = pl.ds(pl.multiple_of(j * bk, bk), bk)
        for qr in qrs:
            s_buf[slot, qr, :] = lax.dot_general(q_ref[qr, :], k_ref[rows, :], NT_DIMS,
                                                 preferred_element_type=F32)

    def consume(j, slot):
        rows = pl.ds(pl.multiple_of(j * bk, bk), bk)
        ps, alphas = [], []
        for qr in qrs:
            tiles = [s_buf[slot, qr, t * LANES:(t + 1) * LANES] for t in range(nt)]
            tmax = tiles[0]
            for t in range(1, nt):
                tmax = jnp.maximum(tmax, tiles[t])
            m_prev = m_s[qr, :]
            m_new = jnp.maximum(m_prev, jnp.max(tmax, axis=-1, keepdims=True))
            alphas.append(jnp.exp2(m_prev - m_new))
            ps.append(jnp.concatenate([jnp.exp2(tl - m_new).astype(BF16) for tl in tiles], axis=1))
            m_s[qr, :] = m_new
        for qr, p, a in zip(qrs, ps, alphas):
            pv = jnp.dot(p, v_ref[rows, :], preferred_element_type=F32)
            acc_s[qr, :] = jnp.concatenate([a, a], axis=1) * acc_s[qr, :] + pv

    scores(0, 0)
    npairs = (nk - 1) // 2

    def body(i, carry):
        j = 2 * i
        scores(j + 1, 1)
        consume(j, 0)
        scores(j + 2, 0)
        consume(j + 1, 1)
        return carry

    lax.fori_loop(0, npairs, body, 0)
    if nk - 2 * npairs == 2:
        scores(nk - 1, 1)
        consume(nk - 2, 0)
        consume(nk - 1, 1)
    else:
        consume(nk - 1, 0)
    acc = acc_s[...]
    o_ref[...] = (acc[:, :MLA_V] / acc[:, MLA_V:MLA_V + 1]).astype(o_ref.dtype)


def _flash(q, k, v, batch, seq, bq, bk):
    t = batch * seq
    nq = seq // bq
    return pl.pallas_call(
        functools.partial(_flash_kernel, bk=bk, nk=seq // bk),
        out_shape=jax.ShapeDtypeStruct((t, MLA_HEADS * MLA_V), BF16),
        grid=(batch, MLA_HEADS, nq),
        in_specs=[pl.BlockSpec((bq, QK_PAD), lambda b, h, i: (b * nq + i, h)),
                  pl.BlockSpec((seq, QK_PAD), lambda b, h, i: (b, h)),
                  pl.BlockSpec((seq, V_PAD), lambda b, h, i: (b, h))],
        out_specs=pl.BlockSpec((bq, MLA_V), lambda b, h, i: (b * nq + i, h)),
        scratch_shapes=[pltpu.VMEM((bq, LANES), F32), pltpu.VMEM((bq, V_PAD), F32),
                        pltpu.VMEM((2, bq, bk), F32)],
        compiler_params=_params(("arbitrary", "arbitrary", "arbitrary")),
        name="mla_flash",
    )(q, k, v)


def _layernorm(y, w, b):
    mu = jnp.mean(y, axis=-1, keepdims=True)
    yc = y - mu
    var = jnp.mean(yc * yc, axis=-1, keepdims=True)
    return yc * lax.rsqrt(var + LN_EPS) * w + b


def _merge_kernel(x_ref, ga_ref, gb_ref, gm_ref, mq_ref, oa_ref, ob_ref, kvm_ref, wout_ref,
                  l1w_ref, l1b_ref, rw_ref, rb_ref,
                  x1_ref, ri_ref, rg_ref, cnt_ref, carry_s, *, tm):
    @pl.when(pl.program_id(0) == 0)
    def _():
        carry_s[...] = jnp.zeros_like(carry_s)

    parts = []
    for h in range(MEM_HEADS):
        cols = slice(h * MEM_HEAD_DIM, (h + 1) * MEM_HEAD_DIM)
        kh = kvm_ref[:, cols]
        vh = kvm_ref[:, D_MODEL + h * MEM_HEAD_DIM:D_MODEL + (h + 1) * MEM_HEAD_DIM]
        s = lax.dot_general(mq_ref[:, cols], kh, NT_DIMS, preferred_element_type=F32) * (MEM_HEAD_DIM ** -0.5)
        s = s - jnp.max(s, axis=-1, keepdims=True)
        p = jnp.exp(s)
        p = p / jnp.sum(p, axis=-1, keepdims=True)
        parts.append(jnp.dot(p.astype(BF16), vh, preferred_element_type=F32))
    om = jnp.concatenate(parts, axis=1)

    merged = (jax.nn.sigmoid(ga_ref[...].astype(F32)) * oa_ref[...].astype(F32)
              + jax.nn.sigmoid(gb_ref[...].astype(F32)) * ob_ref[...].astype(F32)
              + jax.nn.sigmoid(gm_ref[...].astype(F32)) * om)
    y = DN_ALPHA * x_ref[...] + jnp.dot(merged.astype(BF16), wout_ref[...], preferred_element_type=F32)
    x1 = _layernorm(y, l1w_ref[...], l1b_ref[...])
    x1_ref[...] = x1

    logits = jnp.dot(x1, rw_ref[...], preferred_element_type=F32, precision=lax.Precision.HIGHEST) + rb_ref[...]
    lane_i = lax.broadcasted_iota(jnp.int32, (tm, LANES), 1)
    lane = lane_i.astype(F32)
    work = logits
    idx, val = [], []
    for _ in range(TOP_K):
        mx = jnp.max(work, axis=-1, keepdims=True)
        ix = jnp.min(jnp.where(work == mx, lane, float(LANES)), axis=-1, keepdims=True)
        idx.append(ix)
        val.append(mx)
        work = jnp.where(lane == ix, -jnp.inf, work)
    ex = [jnp.exp(v - val[0]) for v in val]
    tot = ex[0] + ex[1] + ex[2] + ex[3]
    hot = [jnp.where(lane == ix, 1.0, 0.0) for ix in idx]
    multi = hot[0] + hot[1] + hot[2] + hot[3]
    r = lax.broadcasted_iota(jnp.int32, (tm, tm), 0)
    c = lax.broadcasted_iota(jnp.int32, (tm, tm), 1)
    lower = jnp.where(r > c, 1.0, 0.0).astype(BF16)
    before = jnp.dot(lower, multi.astype(BF16), preferred_element_type=F32) + carry_s[0:1, :]
    ri = jnp.zeros((tm, LANES), F32)
    rg = jnp.zeros((tm, LANES), F32)
    for j in range(TOP_K):
        rank = jnp.sum(before * hot[j], axis=-1, keepdims=True)
        ri = ri + jnp.where(lane == float(j), idx[j], 0.0) + jnp.where(lane == float(TOP_K + j), rank, 0.0)
        rg = rg + jnp.where(lane == float(j), ex[j] / tot, 0.0)
    ri_ref[...] = ri.astype(jnp.int32)
    rg_ref[...] = rg
    carry_s[...] = carry_s[...] + jnp.sum(multi, axis=0, keepdims=True)
    cnt_ref[...] = carry_s[...]


def _merge(x2d, proj, o_a, o_b, kvm, wout, l1w, l1b, rw, rb, batch, seq, tm):
    t = batch * seq
    per_b = seq // tm
    nmem = kvm.shape[0] // batch
    tile = lambda colblk: pl.BlockSpec((tm, D_MODEL), lambda i: (i, colblk))
    full = lambda shape: pl.BlockSpec(shape, lambda i: (0, 0))
    return pl.pallas_call(
        functools.partial(_merge_kernel, tm=tm),
        out_shape=(jax.ShapeDtypeStruct((t, D_MODEL), F32),
                   jax.ShapeDtypeStruct((t, LANES), jnp.int32),
                   jax.ShapeDtypeStruct((t, LANES), F32),
                   jax.ShapeDtypeStruct((8, LANES), F32)),
        grid=(t // tm,),
        in_specs=[tile(0), tile(COL_GA), tile(COL_GB), tile(COL_GM), tile(COL_MQ), tile(0), tile(0),
                  pl.BlockSpec((nmem, 2 * D_MODEL), lambda i: (i // per_b, 0)),
                  full((D_MODEL, D_MODEL)), full((1, D_MODEL)), full((1, D_MODEL)),
                  full((D_MODEL, LANES)), full((1, LANES))],
        out_specs=(tile(0), pl.BlockSpec((tm, LANES), lambda i: (i, 0)),
                   pl.BlockSpec((tm, LANES), lambda i: (i, 0)), full((8, LANES))),
        scratch_shapes=[pltpu.VMEM((8, LANES), F32)],
        compiler_params=_params(("arbitrary",)),
        name="merge_router",
    )(x2d, proj, proj, proj, proj, o_a, o_b, kvm, wout, l1w, l1b, rw, rb)


def _sc_gather(table, idx):
    p, d = idx.shape[0], table.shape[1]
    per_w = p // SC_WORKERS
    assert per_w * SC_WORKERS == p and per_w % SC_GATHER_ROWS == 0
    mesh = plsc.VectorSubcoreMesh(core_axis_name="c", subcore_axis_name="s")

    @functools.partial(
        pl.kernel, mesh=mesh,
        out_type=jax.ShapeDtypeStruct((p, d), table.dtype),
        scratch_types=[pltpu.VMEM((SC_GATHER_ROWS,), jnp.int32),
                       pltpu.VMEM((SC_GATHER_ROWS, d), table.dtype),
                       pltpu.SemaphoreType.DMA],
        name="moe_gather_sc",
    )
    def gather_kernel(table_hbm, idx_hbm, out_hbm, idx_v, rows_v, sem):
        wid = lax.axis_index("s") * SC_CORES + lax.axis_index("c")
        base = wid * per_w

        @pl.loop(0, per_w // SC_GATHER_ROWS)
        def _(c):
            off = base + c * SC_GATHER_ROWS
            pltpu.sync_copy(idx_hbm.at[pl.ds(off, SC_GATHER_ROWS)], idx_v)
            pltpu.async_copy(table_hbm.at[idx_v], rows_v, sem).wait()
            pltpu.sync_copy(rows_v, out_hbm.at[pl.ds(off, SC_GATHER_ROWS)])

    return gather_kernel(table, idx)


def _sc_dispatch(x, dest_t, slots):
    t, d = x.shape
    per_w = t // SC_WORKERS
    assert per_w * SC_WORKERS == t and per_w % SC_GATHER_ROWS == 0
    mesh = plsc.VectorSubcoreMesh(core_axis_name="c", subcore_axis_name="s")

    @functools.partial(
        pl.kernel, mesh=mesh,
        out_type=jax.ShapeDtypeStruct((slots, d), x.dtype),
        scratch_types=[pltpu.VMEM((SC_GATHER_ROWS,), jnp.int32),
                       pltpu.VMEM((SC_GATHER_ROWS, d), x.dtype)],
        name="moe_dispatch_sc",
    )
    def scatter_kernel(x_hbm, idx_hbm, out_hbm, idx_v, rows_v):
        wid = lax.axis_index("s") * SC_CORES + lax.axis_index("c")
        base = wid * per_w

        @pl.loop(0, per_w // SC_GATHER_ROWS)
        def _(c):
            off = base + c * SC_GATHER_ROWS
            pltpu.sync_copy(x_hbm.at[pl.ds(off, SC_GATHER_ROWS)], rows_v)
            for j in range(TOP_K):
                pltpu.sync_copy(idx_hbm.at[pl.ds(j * t + off, SC_GATHER_ROWS)], idx_v)
                pltpu.sync_copy(rows_v, out_hbm.at[idx_v])

    return scatter_kernel(x, dest_t)


def _expert_kernel(be_ref, nu_ref, xs_ref, wgu_ref, bgu_ref, wd_ref, bd_ref, ys_ref):
    del be_ref
    i = pl.program_id(0)

    @pl.when(i < nu_ref[0])
    def _():
        h = jnp.dot(xs_ref[...].astype(BF16), wgu_ref[0], preferred_element_type=F32) + bgu_ref[0]
        g = jnp.minimum(h[:, :D_FF], SWIGLU_LIMIT)
        u = jnp.clip(h[:, D_FF:], -SWIGLU_LIMIT, SWIGLU_LIMIT)
        act = (u + 1.0) * g * jax.nn.sigmoid(SWIGLU_ALPHA * g)
        ys_ref[...] = jnp.dot(act.astype(BF16), wd_ref[0], preferred_element_type=F32) + bd_ref[0]

    @pl.when(i >= nu_ref[0])
    def _():
        ys_ref[...] = jnp.zeros_like(ys_ref)


def _experts(block_e, n_used, xs, wgu, bgu, wd, bd):
    slots = xs.shape[0]
    nb = slots // EXPERT_BLOCK
    grid_spec = pltpu.PrefetchScalarGridSpec(
        num_scalar_prefetch=2,
        grid=(nb,),
        in_specs=[pl.BlockSpec((EXPERT_BLOCK, D_MODEL), lambda i, be, nu: (i, 0)),
                  pl.BlockSpec((1, D_MODEL, 2 * D_FF), lambda i, be, nu: (be[i], 0, 0)),
                  pl.BlockSpec((1, 1, 2 * D_FF), lambda i, be, nu: (be[i], 0, 0)),
                  pl.BlockSpec((1, D_FF, D_MODEL), lambda i, be, nu: (be[i], 0, 0)),
                  pl.BlockSpec((1, 1, D_MODEL), lambda i, be, nu: (be[i], 0, 0))],
        out_specs=pl.BlockSpec((EXPERT_BLOCK, D_MODEL), lambda i, be, nu: (i, 0)),
    )
    return pl.pallas_call(
        _expert_kernel,
        out_shape=jax.ShapeDtypeStruct((slots, D_MODEL), F32),
        grid_spec=grid_spec,
        compiler_params=_params(("arbitrary",)),
        name="moe_experts",
    )(block_e, n_used, xs, wgu, bgu, wd, bd)


def _combine_kernel(dest_hbm, x1_ref, rg_ref, ys_hbm, l2w_ref, l2b_ref, o_ref, idx_s, buf, sem_i, sem_r, *,
                    tm, first_tile):
    i = pl.program_id(0)
    nsteps = pl.num_programs(0)
    n = TOP_K * tm

    def issue(step, slot):
        cp = pltpu.make_async_copy(dest_hbm.at[pl.ds(pl.multiple_of((first_tile + step) * n, n), n)], idx_s, sem_i)
        cp.start()
        cp.wait()

        def body(g, carry):
            for jj in range(SUBLANES):
                for j in range(TOP_K):
                    d = idx_s[g * (SUBLANES * TOP_K) + jj * TOP_K + j]
                    pltpu.make_async_copy(ys_hbm.at[pl.ds(d, 1)], buf.at[slot, j, g, pl.ds(jj, 1)],
                                          sem_r.at[slot]).start(priority=j % 2)
            return carry

        lax.fori_loop(0, tm // SUBLANES, body, 0)

    @pl.when(i == 0)
    def _():
        issue(0, 0)

    @pl.when(i + 1 < nsteps)
    def _():
        issue(i + 1, (i + 1) % 2)

    slot = i % 2
    for _ in range(TOP_K):
        pltpu.make_async_copy(ys_hbm.at[pl.ds(0, tm)], o_ref, sem_r.at[slot]).wait()
    rg = rg_ref[...]
    moe = rg[:, 0:1] * buf[slot, 0].reshape(tm, D_MODEL)
    for j in range(1, TOP_K):
        moe = moe + rg[:, j:j + 1] * buf[slot, j].reshape(tm, D_MODEL)
    o_ref[...] = _layernorm(DN_ALPHA * x1_ref[...] + moe, l2w_ref[...], l2b_ref[...])


def _combine_rows_kernel(x1_ref, rg_ref, g0_ref, g1_ref, g2_ref, g3_ref, l2w_ref, l2b_ref, *rest):
    o_ref = rest[-1]
    rg = rg_ref[...]
    moe = rg[:, 0:1] * g0_ref[...]
    for j, g_ref in enumerate((g1_ref, g2_ref, g3_ref), start=1):
        moe = moe + rg[:, j:j + 1] * g_ref[...]
    o_ref[...] = _layernorm(DN_ALPHA * x1_ref[...] + moe, l2w_ref[...], l2b_ref[...])


def _combine_rows(x1, rg, rows, l2w, l2b, tm, y_rest):
    t = x1.shape[0]
    nt = rows.shape[0] // TOP_K // tm
    full = lambda shape: pl.BlockSpec(shape, lambda i: (0, 0))
    choice = lambda j: pl.BlockSpec((tm, D_MODEL), lambda i, j=j: (j * nt + i, 0))
    in_specs = [pl.BlockSpec((tm, D_MODEL), lambda i: (i, 0)),
                pl.BlockSpec((tm, LANES), lambda i: (i, 0)),
                choice(0), choice(1), choice(2), choice(3),
                full((1, D_MODEL)), full((1, D_MODEL))]
    args = [x1, rg, rows, rows, rows, rows, l2w, l2b]
    aliases = {}
    if y_rest is not None:
        in_specs.append(pl.BlockSpec(memory_space=pl.ANY))
        args.append(y_rest)
        aliases = {len(args) - 1: 0}
    return pl.pallas_call(
        _combine_rows_kernel,
        out_shape=jax.ShapeDtypeStruct((t, D_MODEL), F32),
        grid=(nt,),
        in_specs=in_specs,
        out_specs=pl.BlockSpec((tm, D_MODEL), lambda i: (i, 0)),
        input_output_aliases=aliases,
        compiler_params=_params(("arbitrary",)),
        name="moe_combine_rows",
    )(*args)


def _combine(dest, x1, rg, ys, l2w, l2b, tm, tok0):
    t = x1.shape[0]
    first = tok0 // tm
    full = lambda shape: pl.BlockSpec(shape, lambda i: (0, 0))
    return pl.pallas_call(
        functools.partial(_combine_kernel, tm=tm, first_tile=first),
        out_shape=jax.ShapeDtypeStruct((t, D_MODEL), F32),
        grid=((t - tok0) // tm,),
        in_specs=[pl.BlockSpec(memory_space=pl.ANY),
                  pl.BlockSpec((tm, D_MODEL), lambda i: (first + i, 0)),
                  pl.BlockSpec((tm, LANES), lambda i: (first + i, 0)),
                  pl.BlockSpec(memory_space=pl.ANY),
                  full((1, D_MODEL)), full((1, D_MODEL))],
        out_specs=pl.BlockSpec((tm, D_MODEL), lambda i: (first + i, 0)),
        scratch_shapes=[pltpu.SMEM((TOP_K * tm,), jnp.int32),
                        pltpu.VMEM((2, TOP_K, tm // SUBLANES, SUBLANES, D_MODEL), F32),
                        pltpu.SemaphoreType.DMA, pltpu.SemaphoreType.DMA((2,))],
        compiler_params=_params(("arbitrary",)),
        name="moe_combine",
    )(dest, x1, rg, ys, l2w, l2b)


def _prep_w_in(w):
    zeros = lambda n: jnp.zeros((D_MODEL, n), w.dtype)
    kr = w[:, 5760:5824]
    kr_sw = jnp.concatenate([kr[:, MLA_ROPE // 2:], kr[:, :MLA_ROPE // 2]], axis=1)
    parts = [w[:, 6848:9920], w[:, 0:5120], w[:, 5824:6848],
             w[:, 5120:5504], zeros(128),
             w[:, 5504:5760], kr, zeros(64), kr_sw, zeros(64)]
    return jnp.concatenate(parts, axis=1).astype(BF16)


def _prep_w_uq(w):
    w3 = w.reshape(MLA_Q_LORA, MLA_HEADS, MLA_NOPE + MLA_ROPE)
    rope = w3[:, :, MLA_NOPE:]
    rope_sw = jnp.concatenate([rope[:, :, MLA_ROPE // 2:], rope[:, :, :MLA_ROPE // 2]], axis=-1)
    pad = jnp.zeros((MLA_Q_LORA, MLA_HEADS, QK_PAD - MLA_NOPE - MLA_ROPE), w.dtype)
    main = jnp.concatenate([w3, pad], axis=-1).reshape(MLA_Q_LORA, MLA_HEADS * QK_PAD)
    swp = jnp.concatenate([rope_sw, pad], axis=-1).reshape(MLA_Q_LORA, MLA_HEADS * LANES)
    both = jnp.concatenate([main, swp], axis=1)
    return jnp.pad(both, ((0, 512 - MLA_Q_LORA), (0, 0))).astype(BF16)


def _rope_tables(seq):
    inv_freq = ROPE_THETA ** (-jnp.arange(0, MLA_ROPE, 2, dtype=F32) / MLA_ROPE)
    ang = jnp.arange(seq, dtype=F32)[:, None] * inv_freq[None, :]
    cos, sin = jnp.cos(ang), jnp.sin(ang)
    pad = jnp.zeros((seq, LANES - MLA_ROPE), F32)
    return (jnp.concatenate([cos, cos, pad], axis=1), jnp.concatenate([-sin, sin, pad], axis=1))


def _tile(n, pref):
    return pref if n % pref == 0 else n


def _layer(x, mem, wts, sc_share):
    batch, seq, _ = x.shape
    t = batch * seq
    x2d = x.reshape(t, D_MODEL)

    proj = _matmul(x2d, wts["w_in"], BF16, _tile(t, 1024), 1024, "in_proj")
    o_a = _hgrn(proj, wts["lb"], wts["hgrn_norm_w"], batch, seq, _tile(seq, 512))

    cos, sin = _rope_tables(seq)
    q, k, v = _mla_prep(proj, wts["qnw"], wts["kvnw"], wts["w_uq"], wts["w_uk"], wts["w_uv"],
                        cos, sin, batch, seq, _tile(seq, 512))
    o_b = _flash(q, k, v, batch, seq, _tile(seq, 1024), _tile(seq, 1024))

    nmem = mem.shape[1]
    kvm = _matmul(mem.reshape(batch * nmem, D_MODEL), wts["mem_w_kv"], BF16,
                  _tile(batch * nmem, 512), 1024, "mem_kv")

    tm = _tile(seq, 512)
    x1, ri, rg, cnt = _merge(x2d, proj, o_a, o_b, kvm, wts["w_out"], wts["ln1_w"], wts["ln1_b"],
                             wts["router_w"], wts["router_b"], batch, seq, tm)

    idx = ri[:, :TOP_K]
    rank = ri[:, TOP_K:2 * TOP_K]
    counts = cnt[0, :N_EXPERTS].astype(jnp.int32)
    padded = (counts + EXPERT_BLOCK - 1) // EXPERT_BLOCK * EXPERT_BLOCK
    pad_end = jnp.cumsum(padded)
    pad_start = pad_end - padded
    dest = (pad_start[idx] + rank).reshape(t * TOP_K).astype(jnp.int32)
    nb = t * TOP_K // EXPERT_BLOCK + N_EXPERTS
    blk_start = jnp.arange(nb, dtype=jnp.int32) * EXPERT_BLOCK
    block_e = jnp.minimum(jnp.sum((pad_end[None, :] <= blk_start[:, None]).astype(jnp.int32), axis=1),
                          N_EXPERTS - 1).astype(jnp.int32)
    n_used = (pad_end[-1:] // EXPERT_BLOCK).astype(jnp.int32)
    dest_t = dest.reshape(t, TOP_K).T.reshape(TOP_K * t)
    xs = _sc_dispatch(x1, dest_t, nb * EXPERT_BLOCK)
    ys = _experts(block_e, n_used, xs, wts["exp_w_gu"], wts["exp_b_gu"], wts["exp_w_down"],
                        wts["exp_b_down"])
    n_sc = int(t * sc_share) // 2048 * 2048
    y_rest = None
    if n_sc < t:
        y_rest = _combine(dest, x1, rg, ys, wts["ln2_w"], wts["ln2_b"], _tile(t, 256), n_sc)
    if n_sc > 0:
        rows = _sc_gather(ys, dest.reshape(t, TOP_K)[:n_sc].T.reshape(TOP_K * n_sc))
        y = _combine_rows(x1, rg, rows, wts["ln2_w"], wts["ln2_b"], 512, y_rest)
    else:
        y = y_rest
    return y.reshape(batch, seq, D_MODEL)


def kernel(x_prompt, x_sample, mem_prompt, mem_sample, w_in, hgrn_lb_logits, hgrn_norm_w,
           mla_q_norm_w, mla_w_uq, mla_kv_norm_w, mla_w_uk, mla_w_uv, mem_w_kv, w_out,
           ln1_w, ln1_b, router_w, router_b, exp_w_gu, exp_b_gu, exp_w_down, exp_b_down,
           ln2_w, ln2_b):
    depth = w_in.shape[0]
    gamma = jax.nn.softmax(hgrn_lb_logits.astype(F32), axis=1)
    cum = jnp.cumsum(gamma, axis=1)
    lb_all = cum[:, 1:] - cum[:, :1]
    y_prompt, y_sample = x_prompt, x_sample
    for l in range(depth):
        row = lambda a: a[l].reshape(1, -1).astype(F32)
        wts = {
            "w_in": _prep_w_in(w_in[l]),
            "lb": lb_all[:, l],
            "hgrn_norm_w": hgrn_norm_w[l].astype(F32),
            "qnw": jnp.pad(row(mla_q_norm_w), ((0, 0), (0, 512 - MLA_Q_LORA))),
            "kvnw": row(mla_kv_norm_w),
            "w_uq": _prep_w_uq(mla_w_uq[l]),
            "w_uk": mla_w_uk[l].astype(BF16),
            "w_uv": mla_w_uv[l].astype(BF16),
            "mem_w_kv": mem_w_kv[l].astype(BF16),
            "w_out": w_out[l].astype(BF16),
            "ln1_w": row(ln1_w), "ln1_b": row(ln1_b),
            "router_w": jnp.pad(router_w[l].astype(F32), ((0, 0), (0, LANES - N_EXPERTS))),
            "router_b": jnp.pad(row(router_b), ((0, 0), (0, LANES - N_EXPERTS)), constant_values=-jnp.inf),
            "exp_w_gu": exp_w_gu[l].astype(BF16),
            "exp_b_gu": exp_b_gu[l].reshape(N_EXPERTS, 1, 2 * D_FF).astype(F32),
            "exp_w_down": exp_w_down[l].astype(BF16),
            "exp_b_down": exp_b_down[l].reshape(N_EXPERTS, 1, D_MODEL).astype(F32),
            "ln2_w": row(ln2_w), "ln2_b": row(ln2_b),
        }
        y_prompt = _layer(y_prompt, mem_prompt, wts, sc_share=1.0)
        y_sample = _layer(y_sample, mem_sample, wts, sc_share=0.75)
    return (y_prompt, y_sample)
```

```python
import functools

import numpy as np
import jax
import jax.numpy as jnp
from jax import lax
from jax.experimental import pallas as pl
from jax.experimental.pallas import tpu as pltpu
from jax.experimental.pallas import tpu_sc as plsc

F32 = jnp.float32
BF16 = jnp.bfloat16

D_MODEL = 1024
HGRN_HEADS = 8
HGRN_DK = 128
MLA_HEADS = 8
MLA_Q_LORA = 384
MLA_KV_LORA = 256
MLA_NOPE = 128
MLA_ROPE = 64
MLA_V = 128
ROPE_THETA = 10000.0
MEM_HEADS = 4
MEM_HEAD_DIM = D_MODEL // MEM_HEADS
N_EXPERTS = 32
TOP_K = 4
D_FF = D_MODEL
SWIGLU_LIMIT = 7.0
SWIGLU_ALPHA = 1.702
DN_ALPHA = 2.0 ** 0.25
LN_EPS = 1e-5
RMS_EPS = 1e-6

LANES = 128
SUBLANES = 8
QK_PAD = 256
V_PAD = 256
LOG2E = 1.4426950408889634
HGRN_CHUNK = 64
HGRN_SAFE_RANGE = 160.0
EXPERT_BLOCK = 512
SC_CORES = 2
SC_SUBCORES = 16
SC_WORKERS = SC_CORES * SC_SUBCORES
SC_GATHER_ROWS = 64
VMEM_LIMIT = 56 * 1024 * 1024

COL_GA, COL_GB, COL_GM, COL_Q, COL_ZF, COL_ZB, COL_I, COL_G, COL_MQ = range(9)
COL_DQ, COL_DKV = 18, 19
IN_COLS_PAD = 10240

NT_DIMS = (((1,), (1,)), ((), ()))
TN_DIMS = (((0,), (0,)), ((), ()))


def _params(sem, vmem=VMEM_LIMIT):
    return pltpu.CompilerParams(dimension_semantics=sem, vmem_limit_bytes=vmem)


def _mm_kernel(x_ref, w_ref, o_ref, xb_ref):
    @pl.when(pl.program_id(1) == 0)
    def _():
        xb_ref[...] = x_ref[...].astype(BF16)

    o_ref[...] = jnp.dot(xb_ref[...], w_ref[...], preferred_element_type=F32).astype(o_ref.dtype)


def _matmul(x, w, out_dtype, tm, tn, name):
    m, k = x.shape
    n = w.shape[1]
    return pl.pallas_call(
        _mm_kernel,
        out_shape=jax.ShapeDtypeStruct((m, n), out_dtype),
        grid=(m // tm, n // tn),
        in_specs=[pl.BlockSpec((tm, k), lambda i, j: (i, 0)),
                  pl.BlockSpec((k, tn), lambda i, j: (0, j))],
        out_specs=pl.BlockSpec((tm, tn), lambda i, j: (i, j)),
        scratch_shapes=[pltpu.VMEM((tm, k), BF16)],
        compiler_params=_params(("arbitrary", "arbitrary")),
        name=name,
    )(x, w)


def _hgrn_kernel(*refs, reverse, epilogue, sc):
    if epilogue:
        lb_ref, q_ref, z_ref, v_ref, of_ref, g_ref, nw_ref, o_ref, st_ref, kk_s, b_s, vf_s, os_s = refs
    else:
        lb_ref, q_ref, z_ref, v_ref, o_ref, st_ref, kk_s, b_s, vf_s = refs
    C = HGRN_CHUNK
    nch = sc // C

    @pl.when(pl.program_id(1) == 0)
    def _():
        st_ref[...] = jnp.zeros_like(st_ref)

    lb = lb_ref[...]
    row = lax.broadcasted_iota(jnp.int32, (C, C), 0)
    col = lax.broadcasted_iota(jnp.int32, (C, C), 1)
    tri = (row <= col) if reverse else (row >= col)
    trib = jnp.where(tri, 1.0, 0.0).astype(BF16)

    minb = None
    for c in range(nch):
        rows = slice(c * C, (c + 1) * C)
        z = z_ref[rows, :].astype(F32)
        gate = (1.0 - lb) * jax.nn.sigmoid(z)
        lf = jnp.log(lb + gate)
        kk_s[rows, :] = (1.0 - lb) - gate
        hi = lf.astype(BF16)
        r1 = lf - hi.astype(F32)
        mid = r1.astype(BF16)
        lo = (r1 - mid.astype(F32)).astype(BF16)
        b = (jnp.dot(trib, hi, preferred_element_type=F32)
             + jnp.dot(trib, mid, preferred_element_type=F32)
             + jnp.dot(trib, lo, preferred_element_type=F32))
        b_s[rows, :] = b
        mb = jnp.min(b)
        minb = mb if minb is None else jnp.minimum(minb, mb)

    rid = lax.broadcasted_iota(jnp.int32, (C, 1), 0)

    def chunk(i, carry, fast):
        c = (nch - 1 - i) if reverse else i
        r0 = pl.multiple_of(c * C, C)
        rows = pl.ds(r0, C)
        for h in range(HGRN_HEADS):
            cols = slice(h * HGRN_DK, (h + 1) * HGRN_DK)
            q = q_ref[rows, cols].astype(F32)
            kk = kk_s[rows, cols]
            b = b_s[rows, cols]
            v = v_ref[rows, cols]
            bl = b[0:1, :] if reverse else b[C - 1:C, :]
            if fast:
                bm = 0.5 * bl
                qd = (q * jnp.exp(b - bm)).astype(BF16)
                kd = (kk * jnp.exp(bm - b)).astype(BF16)
                s = lax.dot_general(qd, kd, NT_DIMS, preferred_element_type=F32)
                s = jnp.where(tri, s, 0.0).astype(BF16)
                o = jnp.dot(s, v, preferred_element_type=F32)
            else:
                def sbody(g_i, o_acc):
                    grp = pl.ds(pl.multiple_of(r0 + g_i * SUBLANES, SUBLANES), SUBLANES)
                    b8 = b_s[grp, cols]
                    k8 = kk_s[grp, cols]
                    v8 = vf_s[grp, cols]
                    for jj in range(SUBLANES):
                        s_i = g_i * SUBLANES + jj
                        w = q * k8[jj:jj + 1, :] * jnp.exp(jnp.minimum(b - b8[jj:jj + 1, :], 0.0))
                        scol = jnp.sum(w, axis=-1, keepdims=True)
                        keep = (rid <= s_i) if reverse else (rid >= s_i)
                        o_acc = o_acc + jnp.where(keep, scol, 0.0) * v8[jj:jj + 1, :]
                    return o_acc
                o = lax.fori_loop(0, C // SUBLANES, sbody, jnp.zeros((C, HGRN_DK), F32))
            st = st_ref[h]
            qi = (q * jnp.exp(b)).astype(BF16)
            o = o + lax.dot_general(qi, st.astype(BF16), NT_DIMS, preferred_element_type=F32)
            ke = (kk * jnp.exp(bl - b)).astype(BF16)
            upd = lax.dot_general(v, ke, TN_DIMS, preferred_element_type=F32)
            st_ref[h] = st * jnp.exp(bl) + upd
            if epilogue:
                os_s[rows, cols] = o + of_ref[rows, cols].astype(F32)
            else:
                o_ref[rows, cols] = o.astype(o_ref.dtype)
        return carry

    safe = minb >= -HGRN_SAFE_RANGE

    @pl.when(safe)
    def _():
        lax.fori_loop(0, nch, functools.partial(chunk, fast=True), 0, unroll=2)

    @pl.when(jnp.logical_not(safe))
    def _():
        vf_s[...] = v_ref[...].astype(F32)
        lax.fori_loop(0, nch, functools.partial(chunk, fast=False), 0)

    if epilogue:
        nw = nw_ref[...]
        for h in range(HGRN_HEADS):
            cols = slice(h * HGRN_DK, (h + 1) * HGRN_DK)
            os = os_s[:, cols]
            ms = jnp.mean(os * os, axis=-1, keepdims=True)
            y = os * lax.rsqrt(ms + RMS_EPS) * nw
            g = g_ref[:, cols].astype(F32)
            o_ref[:, cols] = (y * (g * jax.nn.sigmoid(g))).astype(o_ref.dtype)


def _hgrn(proj, lb, norm_w, batch, seq, sc):
    t = batch * seq
    ns = seq // sc
    blk = (sc, D_MODEL)

    def spec(colblk, reverse):
        if reverse:
            return pl.BlockSpec(blk, lambda b, n: (b * ns + ns - 1 - n, colblk))
        return pl.BlockSpec(blk, lambda b, n: (b * ns + n, colblk))

    def row_spec(reverse):
        if reverse:
            return pl.BlockSpec(blk, lambda b, n: (b * ns + ns - 1 - n, 0))
        return pl.BlockSpec(blk, lambda b, n: (b * ns + n, 0))

    lb_spec = pl.BlockSpec((1, D_MODEL), lambda b, n: (0, 0))
    common_scratch = [pltpu.VMEM((HGRN_HEADS, HGRN_DK, HGRN_DK), F32),
                      pltpu.VMEM(blk, F32), pltpu.VMEM(blk, F32), pltpu.VMEM(blk, F32)]
    o_f = pl.pallas_call(
        functools.partial(_hgrn_kernel, reverse=False, epilogue=False, sc=sc),
        out_shape=jax.ShapeDtypeStruct((t, D_MODEL), BF16),
        grid=(batch, ns),
        in_specs=[lb_spec, spec(COL_Q, False), spec(COL_ZF, False), spec(COL_I, False)],
        out_specs=row_spec(False),
        scratch_shapes=common_scratch,
        compiler_params=_params(("arbitrary", "arbitrary")),
        name="hgrn_fwd",
    )(lb[0:1], proj, proj, proj)
    o_a = pl.pallas_call(
        functools.partial(_hgrn_kernel, reverse=True, epilogue=True, sc=sc),
        out_shape=jax.ShapeDtypeStruct((t, D_MODEL), BF16),
        grid=(batch, ns),
        in_specs=[lb_spec, spec(COL_Q, True), spec(COL_ZB, True), spec(COL_I, True),
                  row_spec(True), spec(COL_G, True),
                  pl.BlockSpec((1, HGRN_DK), lambda b, n: (0, 0))],
        out_specs=row_spec(True),
        scratch_shapes=common_scratch + [pltpu.VMEM(blk, F32)],
        compiler_params=_params(("arbitrary", "arbitrary")),
        name="hgrn_bwd",
    )(lb[1:2], proj, proj, proj, o_f, proj, norm_w.reshape(1, HGRN_DK))
    return o_a


def _mla_prep_kernel(dq_ref, dkv_ref, qnw_ref, kvnw_ref, wq_ref, wk_ref, wv_ref, cos_ref, sin_ref,
                     q_ref, k_ref, v_ref):
    scale = (MLA_NOPE + MLA_ROPE) ** -0.5 * LOG2E
    cos = cos_ref[...]
    sin = sin_ref[...]
    dq = dq_ref[...].astype(F32)
    ms = jnp.sum(dq * dq, axis=-1, keepdims=True) * (1.0 / MLA_Q_LORA)
    cq = (dq * lax.rsqrt(ms + RMS_EPS) * qnw_ref[...]).astype(BF16)
    qa = jnp.dot(cq, wq_ref[...], preferred_element_type=F32)
    for h in range(MLA_HEADS):
        base = h * QK_PAD
        q_ref[:, base:base + MLA_NOPE] = (qa[:, base:base + MLA_NOPE] * scale).astype(BF16)
        rp = qa[:, base + MLA_NOPE:base + QK_PAD]
        sw = qa[:, MLA_HEADS * QK_PAD + h * LANES:MLA_HEADS * QK_PAD + (h + 1) * LANES]
        q_ref[:, base + MLA_NOPE:base + QK_PAD] = ((rp * cos + sw * sin) * scale).astype(BF16)
    dkv = dkv_ref[...].astype(F32)
    ckv = dkv[:, :MLA_KV_LORA]
    msk = jnp.mean(ckv * ckv, axis=-1, keepdims=True)
    cn = (ckv * lax.rsqrt(msk + RMS_EPS) * kvnw_ref[...]).astype(BF16)
    kn = jnp.dot(cn, wk_ref[...], preferred_element_type=F32)
    vv = jnp.dot(cn, wv_ref[...], preferred_element_type=F32).astype(BF16)
    ones_col = jnp.where(lax.broadcasted_iota(jnp.int32, (vv.shape[0], LANES), 1) == 0, 1.0, 0.0).astype(BF16)
    for h in range(MLA_HEADS):
        v_ref[:, h * V_PAD:h * V_PAD + MLA_V] = vv[:, h * MLA_V:(h + 1) * MLA_V]
        v_ref[:, h * V_PAD + MLA_V:(h + 1) * V_PAD] = ones_col
    kr = (dkv[:, MLA_KV_LORA:MLA_KV_LORA + LANES] * cos
          + dkv[:, MLA_KV_LORA + LANES:MLA_KV_LORA + 2 * LANES] * sin).astype(BF16)
    for h in range(MLA_HEADS):
        base = h * QK_PAD
        k_ref[:, base:base + MLA_NOPE] = kn[:, h * MLA_NOPE:(h + 1) * MLA_NOPE].astype(BF16)
        k_ref[:, base + MLA_NOPE:base + QK_PAD] = kr


def _mla_prep(proj, qnw, kvnw, wq, wk, wv, cos, sin, batch, seq, tm):
    t = batch * seq
    npos = seq // tm
    full = lambda shape: pl.BlockSpec(shape, lambda i: (0, 0))
    return pl.pallas_call(
        _mla_prep_kernel,
        out_shape=(jax.ShapeDtypeStruct((t, MLA_HEADS * QK_PAD), BF16),
                   jax.ShapeDtypeStruct((t, MLA_HEADS * QK_PAD), BF16),
                   jax.ShapeDtypeStruct((t, MLA_HEADS * V_PAD), BF16)),
        grid=(t // tm,),
        in_specs=[pl.BlockSpec((tm, 512), lambda i: (i, COL_DQ)),
                  pl.BlockSpec((tm, 512), lambda i: (i, COL_DKV)),
                  full((1, 512)), full((1, MLA_KV_LORA)),
                  full(wq.shape), full(wk.shape), full(wv.shape),
                  pl.BlockSpec((tm, LANES), lambda i: (i % npos, 0)),
                  pl.BlockSpec((tm, LANES), lambda i: (i % npos, 0))],
        out_specs=(pl.BlockSpec((tm, MLA_HEADS * QK_PAD), lambda i: (i, 0)),
                   pl.BlockSpec((tm, MLA_HEADS * QK_PAD), lambda i: (i, 0)),
                   pl.BlockSpec((tm, MLA_HEADS * V_PAD), lambda i: (i, 0))),
        compiler_params=_params(("arbitrary",)),
        name="mla_prep",
    )(proj, proj, qnw, kvnw, wq, wk, wv, cos, sin)


def _flash_kernel(q_ref, k_ref, v_ref, o_ref, m_s, acc_s, s_buf, *, bk, nk):
    m_s[...] = jnp.full(m_s.shape, -jnp.inf, F32)
    acc_s[...] = jnp.zeros(acc_s.shape, F32)
    nt = bk // LANES
    bq = q_ref.shape[0]
    nsplit = 2 if bq % 256 == 0 else 1
    hq = bq // nsplit
    qrs = [slice(hf * hq, (hf + 1) * hq) for hf in range(nsplit)]

    def scores(j, slot):
        rows = pl.ds(pl.multiple_of(j * bk, bk), bk)
        for qr in qrs:
            s_buf[slot, qr, :] = lax.dot_general(q_ref[qr, :], k_ref[rows, :], NT_DIMS,
                                                 preferred_element_type=F32)

    def consume(j, slot):
        rows = pl.ds(pl.multiple_of(j * bk, bk), bk)
        ps, alphas = [], []
        for qr in qrs:
            tiles = [s_buf[slot, qr, t * LANES:(t + 1) * LANES] for t in range(nt)]
            tmax = tiles[0]
            for t in range(1, nt):
                tmax = jnp.maximum(tmax, tiles[t])
            m_prev = m_s[qr, :]
            m_new = jnp.maximum(m_prev, jnp.max(tmax, axis=-1, keepdims=True))
            alphas.append(jnp.exp2(m_prev - m_new))
            ps.append(jnp.concatenate([jnp.exp2(tl - m_new).astype(BF16) for tl in tiles], axis=1))
            m_s[qr, :] = m_new
        for qr, p, a in zip(qrs, ps, alphas):
            pv = jnp.dot(p, v_ref[rows, :], preferred_element_type=F32)
            acc_s[qr, :] = jnp.concatenate([a, a], axis=1) * acc_s[qr, :] + pv

    scores(0, 0)
    npairs = (nk - 1) // 2

    def body(i, carry):
        j = 2 * i
        scores(j + 1, 1)
        consume(j, 0)
        scores(j + 2, 0)
        consume(j + 1, 1)
        return carry

    lax.fori_loop(0, npairs, body, 0)
    if nk - 2 * npairs == 2:
        scores(nk - 1, 1)
        consume(nk - 2, 0)
        consume(nk - 1, 1)
    else:
        consume(nk - 1, 0)
    acc = acc_s[...]
    o_ref[...] = (acc[:, :MLA_V] / acc[:, MLA_V:MLA_V + 1]).astype(o_ref.dtype)


def _flash(q, k, v, batch, seq, bq, bk):
    t = batch * seq
    nq = seq // bq
    return pl.pallas_call(
        functools.partial(_flash_kernel, bk=bk, nk=seq // bk),
        out_shape=jax.ShapeDtypeStruct((t, MLA_HEADS * MLA_V), BF16),
        grid=(batch, MLA_HEADS, nq),
        in_specs=[pl.BlockSpec((bq, QK_PAD), lambda b, h, i: (b * nq + i, h)),
                  pl.BlockSpec((seq, QK_PAD), lambda b, h, i: (b, h)),
                  pl.BlockSpec((seq, V_PAD), lambda b, h, i: (b, h))],
        out_specs=pl.BlockSpec((bq, MLA_V), lambda b, h, i: (b * nq + i, h)),
        scratch_shapes=[pltpu.VMEM((bq, LANES), F32), pltpu.VMEM((bq, V_PAD), F32),
                        pltpu.VMEM((2, bq, bk), F32)],
        compiler_params=_params(("arbitrary", "arbitrary", "arbitrary")),
        name="mla_flash",
    )(q, k, v)


def _layernorm(y, w, b):
    mu = jnp.mean(y, axis=-1, keepdims=True)
    yc = y - mu
    var = jnp.mean(yc * yc, axis=-1, keepdims=True)
    return yc * lax.rsqrt(var + LN_EPS) * w + b


def _merge_kernel(x_ref, ga_ref, gb_ref, gm_ref, mq_ref, oa_ref, ob_ref, kvm_ref, wout_ref,
                  l1w_ref, l1b_ref, rw_ref, rb_ref,
                  x1_ref, ri_ref, rg_ref, cnt_ref, carry_s, *, tm):
    @pl.when(pl.program_id(0) == 0)
    def _():
        carry_s[...] = jnp.zeros_like(carry_s)

    parts = []
    for h in range(MEM_HEADS):
        cols = slice(h * MEM_HEAD_DIM, (h + 1) * MEM_HEAD_DIM)
        kh = kvm_ref[:, cols]
        vh = kvm_ref[:, D_MODEL + h * MEM_HEAD_DIM:D_MODEL + (h + 1) * MEM_HEAD_DIM]
        s = lax.dot_general(mq_ref[:, cols], kh, NT_DIMS, preferred_element_type=F32) * (MEM_HEAD_DIM ** -0.5)
        s = s - jnp.max(s, axis=-1, keepdims=True)
        p = jnp.exp(s)
        p = p / jnp.sum(p, axis=-1, keepdims=True)
        parts.append(jnp.dot(p.astype(BF16), vh, preferred_element_type=F32))
    om = jnp.concatenate(parts, axis=1)

    merged = (jax.nn.sigmoid(ga_ref[...].astype(F32)) * oa_ref[...].astype(F32)
              + jax.nn.sigmoid(gb_ref[...].astype(F32)) * ob_ref[...].astype(F32)
              + jax.nn.sigmoid(gm_ref[...].astype(F32)) * om)
    y = DN_ALPHA * x_ref[...] + jnp.dot(merged.astype(BF16), wout_ref[...], preferred_element_type=F32)
    x1 = _layernorm(y, l1w_ref[...], l1b_ref[...])
    x1_ref[...] = x1

    x_hi = x1.astype(BF16)
    x_lo = (x1 - x_hi.astype(F32)).astype(BF16)
    hi_part = jnp.dot(x_hi, rw_ref[...], preferred_element_type=F32)
    lo_part = jnp.dot(x_lo, rw_ref[:, :LANES], preferred_element_type=F32)
    logits = hi_part[:, :LANES] + hi_part[:, LANES:] + lo_part + rb_ref[...]
    lane_i = lax.broadcasted_iota(jnp.int32, (tm, LANES), 1)
    lane = lane_i.astype(F32)
    work = logits
    idx, val = [], []
    for _ in range(TOP_K):
        mx = jnp.max(work, axis=-1, keepdims=True)
        ix = jnp.min(jnp.where(work == mx, lane, float(LANES)), axis=-1, keepdims=True)
        idx.append(ix)
        val.append(mx)
        work = jnp.where(lane == ix, -jnp.inf, work)
    ex = [jnp.exp(v - val[0]) for v in val]
    tot = ex[0] + ex[1] + ex[2] + ex[3]
    hot = [jnp.where(lane == ix, 1.0, 0.0) for ix in idx]
    multi = hot[0] + hot[1] + hot[2] + hot[3]
    r = lax.broadcasted_iota(jnp.int32, (tm, tm), 0)
    c = lax.broadcasted_iota(jnp.int32, (tm, tm), 1)
    lower = jnp.where(r > c, 1.0, 0.0).astype(BF16)
    before = jnp.dot(lower, multi.astype(BF16), preferred_element_type=F32) + carry_s[0:1, :]
    ri = jnp.zeros((tm, LANES), F32)
    rg = jnp.zeros((tm, LANES), F32)
    for j in range(TOP_K):
        rank = jnp.sum(before * hot[j], axis=-1, keepdims=True)
        ri = ri + jnp.where(lane == float(j), idx[j], 0.0) + jnp.where(lane == float(TOP_K + j), rank, 0.0)
        rg = rg + jnp.where(lane == float(j), ex[j] / tot, 0.0)
    ri_ref[...] = ri.astype(jnp.int32)
    rg_ref[...] = rg
    carry_s[...] = carry_s[...] + jnp.sum(multi, axis=0, keepdims=True)
    cnt_ref[...] = carry_s[...]


def _merge(x2d, proj, o_a, o_b, kvm, wout, l1w, l1b, rw, rb, batch, seq, tm):
    t = batch * seq
    per_b = seq // tm
    nmem = kvm.shape[0] // batch
    tile = lambda colblk: pl.BlockSpec((tm, D_MODEL), lambda i: (i, colblk))
    full = lambda shape: pl.BlockSpec(shape, lambda i: (0, 0))
    return pl.pallas_call(
        functools.partial(_merge_kernel, tm=tm),
        out_shape=(jax.ShapeDtypeStruct((t, D_MODEL), F32),
                   jax.ShapeDtypeStruct((t, LANES), jnp.int32),
                   jax.ShapeDtypeStruct((t, LANES), F32),
                   jax.ShapeDtypeStruct((8, LANES), F32)),
        grid=(t // tm,),
        in_specs=[tile(0), tile(COL_GA), tile(COL_GB), tile(COL_GM), tile(COL_MQ), tile(0), tile(0),
                  pl.BlockSpec((nmem, 2 * D_MODEL), lambda i: (i // per_b, 0)),
                  full((D_MODEL, D_MODEL)), full((1, D_MODEL)), full((1, D_MODEL)),
                  full((D_MODEL, 2 * LANES)), full((1, LANES))],
        out_specs=(tile(0), pl.BlockSpec((tm, LANES), lambda i: (i, 0)),
                   pl.BlockSpec((tm, LANES), lambda i: (i, 0)), full((8, LANES))),
        scratch_shapes=[pltpu.VMEM((8, LANES), F32)],
        compiler_params=_params(("arbitrary",)),
        name="merge_router",
    )(x2d, proj, proj, proj, proj, o_a, o_b, kvm, wout, l1w, l1b, rw, rb)


def _sc_gather(table, idx):
    p, d = idx.shape[0], table.shape[1]
    per_w = p // SC_WORKERS
    assert per_w * SC_WORKERS == p and per_w % SC_GATHER_ROWS == 0
    mesh = plsc.VectorSubcoreMesh(core_axis_name="c", subcore_axis_name="s")

    @functools.partial(
        pl.kernel, mesh=mesh,
        out_type=jax.ShapeDtypeStruct((p, d), table.dtype),
        scratch_types=[pltpu.VMEM((SC_GATHER_ROWS,), jnp.int32),
                       pltpu.VMEM((SC_GATHER_ROWS, d), table.dtype),
                       pltpu.SemaphoreType.DMA],
        name="moe_gather_sc",
    )
    def gather_kernel(table_hbm, idx_hbm, out_hbm, idx_v, rows_v, sem):
        wid = lax.axis_index("s") * SC_CORES + lax.axis_index("c")
        base = wid * per_w

        @pl.loop(0, per_w // SC_GATHER_ROWS)
        def _(c):
            off = base + c * SC_GATHER_ROWS
            pltpu.sync_copy(idx_hbm.at[pl.ds(off, SC_GATHER_ROWS)], idx_v)
            pltpu.async_copy(table_hbm.at[idx_v], rows_v, sem).wait()
            pltpu.sync_copy(rows_v, out_hbm.at[pl.ds(off, SC_GATHER_ROWS)])

    return gather_kernel(table, idx)


def _sc_dispatch(x, dest_t, slots):
    t, d = x.shape
    per_w = t // SC_WORKERS
    assert per_w * SC_WORKERS == t and per_w % SC_GATHER_ROWS == 0
    mesh = plsc.VectorSubcoreMesh(core_axis_name="c", subcore_axis_name="s")

    @functools.partial(
        pl.kernel, mesh=mesh,
        out_type=jax.ShapeDtypeStruct((slots, d), x.dtype),
        scratch_types=[pltpu.VMEM((SC_GATHER_ROWS,), jnp.int32),
                       pltpu.VMEM((SC_GATHER_ROWS, d), x.dtype)],
        name="moe_dispatch_sc",
    )
    def scatter_kernel(x_hbm, idx_hbm, out_hbm, idx_v, rows_v):
        wid = lax.axis_index("s") * SC_CORES + lax.axis_index("c")
        base = wid * per_w

        @pl.loop(0, per_w // SC_GATHER_ROWS)
        def _(c):
            off = base + c * SC_GATHER_ROWS
            pltpu.sync_copy(x_hbm.at[pl.ds(off, SC_GATHER_ROWS)], rows_v)
            for j in range(TOP_K):
                pltpu.sync_copy(idx_hbm.at[pl.ds(j * t + off, SC_GATHER_ROWS)], idx_v)
                pltpu.sync_copy(rows_v, out_hbm.at[idx_v])

    return scatter_kernel(x, dest_t)


def _expert_kernel(be_ref, nu_ref, xs_ref, wgu_ref, bgu_ref, wd_ref, bd_ref, ys_ref):
    del be_ref
    i = pl.program_id(0)

    @pl.when(i < nu_ref[0])
    def _():
        h = jnp.dot(xs_ref[...].astype(BF16), wgu_ref[0], preferred_element_type=F32) + bgu_ref[0]
        g = jnp.minimum(h[:, :D_FF], SWIGLU_LIMIT)
        u = jnp.clip(h[:, D_FF:], -SWIGLU_LIMIT, SWIGLU_LIMIT)
        act = (u + 1.0) * g * jax.nn.sigmoid(SWIGLU_ALPHA * g)
        ys_ref[...] = jnp.dot(act.astype(BF16), wd_ref[0], preferred_element_type=F32) + bd_ref[0]

    @pl.when(i >= nu_ref[0])
    def _():
        ys_ref[...] = jnp.zeros_like(ys_ref)


def _experts(block_e, n_used, xs, wgu, bgu, wd, bd):
    slots = xs.shape[0]
    nb = slots // EXPERT_BLOCK
    grid_spec = pltpu.PrefetchScalarGridSpec(
        num_scalar_prefetch=2,
        grid=(nb,),
        in_specs=[pl.BlockSpec((EXPERT_BLOCK, D_MODEL), lambda i, be, nu: (i, 0)),
                  pl.BlockSpec((1, D_MODEL, 2 * D_FF), lambda i, be, nu: (be[i], 0, 0)),
                  pl.BlockSpec((1, 1, 2 * D_FF), lambda i, be, nu: (be[i], 0, 0)),
                  pl.BlockSpec((1, D_FF, D_MODEL), lambda i, be, nu: (be[i], 0, 0)),
                  pl.BlockSpec((1, 1, D_MODEL), lambda i, be, nu: (be[i], 0, 0))],
        out_specs=pl.BlockSpec((EXPERT_BLOCK, D_MODEL), lambda i, be, nu: (i, 0)),
    )
    return pl.pallas_call(
        _expert_kernel,
        out_shape=jax.ShapeDtypeStruct((slots, D_MODEL), F32),
        grid_spec=grid_spec,
        compiler_params=_params(("arbitrary",)),
        name="moe_experts",
    )(block_e, n_used, xs, wgu, bgu, wd, bd)


def _combine_kernel(dest_hbm, x1_ref, rg_ref, ys_hbm, l2w_ref, l2b_ref, o_ref, idx_s, buf, sem_i, sem_r, *,
                    tm, first_tile):
    i = pl.program_id(0)
    nsteps = pl.num_programs(0)
    n = TOP_K * tm

    def issue(step, slot):
        cp = pltpu.make_async_copy(dest_hbm.at[pl.ds(pl.multiple_of((first_tile + step) * n, n), n)], idx_s, sem_i)
        cp.start()
        cp.wait()

        def body(g, carry):
            for jj in range(SUBLANES):
                for j in range(TOP_K):
                    d = idx_s[g * (SUBLANES * TOP_K) + jj * TOP_K + j]
                    pltpu.make_async_copy(ys_hbm.at[pl.ds(d, 1)], buf.at[slot, j, g, pl.ds(jj, 1)],
                                          sem_r.at[slot]).start(priority=j % 2)
            return carry

        lax.fori_loop(0, tm // SUBLANES, body, 0)

    @pl.when(i == 0)
    def _():
        issue(0, 0)

    @pl.when(i + 1 < nsteps)
    def _():
        issue(i + 1, (i + 1) % 2)

    slot = i % 2
    for _ in range(TOP_K):
        pltpu.make_async_copy(ys_hbm.at[pl.ds(0, tm)], o_ref, sem_r.at[slot]).wait()
    rg = rg_ref[...]
    moe = rg[:, 0:1] * buf[slot, 0].reshape(tm, D_MODEL)
    for j in range(1, TOP_K):
        moe = moe + rg[:, j:j + 1] * buf[slot, j].reshape(tm, D_MODEL)
    o_ref[...] = _layernorm(DN_ALPHA * x1_ref[...] + moe, l2w_ref[...], l2b_ref[...])


def _combine_rows_kernel(x1_ref, rg_ref, g0_ref, g1_ref, g2_ref, g3_ref, l2w_ref, l2b_ref, *rest):
    o_ref = rest[-1]
    rg = rg_ref[...]
    moe = rg[:, 0:1] * g0_ref[...]
    for j, g_ref in enumerate((g1_ref, g2_ref, g3_ref), start=1):
        moe = moe + rg[:, j:j + 1] * g_ref[...]
    o_ref[...] = _layernorm(DN_ALPHA * x1_ref[...] + moe, l2w_ref[...], l2b_ref[...])


def _combine_rows(x1, rg, rows, l2w, l2b, tm, y_rest):
    t = x1.shape[0]
    nt = rows.shape[0] // TOP_K // tm
    full = lambda shape: pl.BlockSpec(shape, lambda i: (0, 0))
    choice = lambda j: pl.BlockSpec((tm, D_MODEL), lambda i, j=j: (j * nt + i, 0))
    in_specs = [pl.BlockSpec((tm, D_MODEL), lambda i: (i, 0)),
                pl.BlockSpec((tm, LANES), lambda i: (i, 0)),
                choice(0), choice(1), choice(2), choice(3),
                full((1, D_MODEL)), full((1, D_MODEL))]
    args = [x1, rg, rows, rows, rows, rows, l2w, l2b]
    aliases = {}
    if y_rest is not None:
        in_specs.append(pl.BlockSpec(memory_space=pl.ANY))
        args.append(y_rest)
        aliases = {len(args) - 1: 0}
    return pl.pallas_call(
        _combine_rows_kernel,
        out_shape=jax.ShapeDtypeStruct((t, D_MODEL), F32),
        grid=(nt,),
        in_specs=in_specs,
        out_specs=pl.BlockSpec((tm, D_MODEL), lambda i: (i, 0)),
        input_output_aliases=aliases,
        compiler_params=_params(("arbitrary",)),
        name="moe_combine_rows",
    )(*args)


def _combine(dest, x1, rg, ys, l2w, l2b, tm, tok0):
    t = x1.shape[0]
    first = tok0 // tm
    full = lambda shape: pl.BlockSpec(shape, lambda i: (0, 0))
    return pl.pallas_call(
        functools.partial(_combine_kernel, tm=tm, first_tile=first),
        out_shape=jax.ShapeDtypeStruct((t, D_MODEL), F32),
        grid=((t - tok0) // tm,),
        in_specs=[pl.BlockSpec(memory_space=pl.ANY),
                  pl.BlockSpec((tm, D_MODEL), lambda i: (first + i, 0)),
                  pl.BlockSpec((tm, LANES), lambda i: (first + i, 0)),
                  pl.BlockSpec(memory_space=pl.ANY),
                  full((1, D_MODEL)), full((1, D_MODEL))],
        out_specs=pl.BlockSpec((tm, D_MODEL), lambda i: (first + i, 0)),
        scratch_shapes=[pltpu.SMEM((TOP_K * tm,), jnp.int32),
                        pltpu.VMEM((2, TOP_K, tm // SUBLANES, SUBLANES, D_MODEL), F32),
                        pltpu.SemaphoreType.DMA, pltpu.SemaphoreType.DMA((2,))],
        compiler_params=_params(("arbitrary",)),
        name="moe_combine",
    )(dest, x1, rg, ys, l2w, l2b)


def _prep_w_in(w):
    zeros = lambda n: jnp.zeros((D_MODEL, n), w.dtype)
    kr = w[:, 5760:5824]
    kr_sw = jnp.concatenate([kr[:, MLA_ROPE // 2:], kr[:, :MLA_ROPE // 2]], axis=1)
    parts = [w[:, 6848:9920], w[:, 0:5120], w[:, 5824:6848],
             w[:, 5120:5504], zeros(128),
             w[:, 5504:5760], kr, zeros(64), kr_sw, zeros(64)]
    return jnp.concatenate(parts, axis=1).astype(BF16)


def _prep_w_uq(w):
    w3 = w.reshape(MLA_Q_LORA, MLA_HEADS, MLA_NOPE + MLA_ROPE)
    rope = w3[:, :, MLA_NOPE:]
    rope_sw = jnp.concatenate([rope[:, :, MLA_ROPE // 2:], rope[:, :, :MLA_ROPE // 2]], axis=-1)
    pad = jnp.zeros((MLA_Q_LORA, MLA_HEADS, QK_PAD - MLA_NOPE - MLA_ROPE), w.dtype)
    main = jnp.concatenate([w3, pad], axis=-1).reshape(MLA_Q_LORA, MLA_HEADS * QK_PAD)
    swp = jnp.concatenate([rope_sw, pad], axis=-1).reshape(MLA_Q_LORA, MLA_HEADS * LANES)
    both = jnp.concatenate([main, swp], axis=1)
    return jnp.pad(both, ((0, 512 - MLA_Q_LORA), (0, 0))).astype(BF16)


def _rope_tables(seq):
    inv_freq = ROPE_THETA ** (-jnp.arange(0, MLA_ROPE, 2, dtype=F32) / MLA_ROPE)
    ang = jnp.arange(seq, dtype=F32)[:, None] * inv_freq[None, :]
    cos, sin = jnp.cos(ang), jnp.sin(ang)
    pad = jnp.zeros((seq, LANES - MLA_ROPE), F32)
    return (jnp.concatenate([cos, cos, pad], axis=1), jnp.concatenate([-sin, sin, pad], axis=1))


def _split_bf16(w):
    hi = w.astype(BF16)
    lo = (w - hi.astype(F32)).astype(BF16)
    return jnp.concatenate([hi, lo], axis=1)


def _tile(n, pref):
    return pref if n % pref == 0 else n


def _layer(x, mem, wts, sc_share):
    batch, seq, _ = x.shape
    t = batch * seq
    x2d = x.reshape(t, D_MODEL)

    proj = _matmul(x2d, wts["w_in"], BF16, _tile(t, 1024), 2048, "in_proj")
    o_a = _hgrn(proj, wts["lb"], wts["hgrn_norm_w"], batch, seq, _tile(seq, 512))

    cos, sin = _rope_tables(seq)
    q, k, v = _mla_prep(proj, wts["qnw"], wts["kvnw"], wts["w_uq"], wts["w_uk"], wts["w_uv"],
                        cos, sin, batch, seq, _tile(seq, 512))
    o_b = _flash(q, k, v, batch, seq, _tile(seq, 1024), _tile(seq, 1024))

    nmem = mem.shape[1]
    kvm = _matmul(mem.reshape(batch * nmem, D_MODEL), wts["mem_w_kv"], BF16,
                  _tile(batch * nmem, 512), 1024, "mem_kv")

    tm = _tile(seq, 512)
    x1, ri, rg, cnt = _merge(x2d, proj, o_a, o_b, kvm, wts["w_out"], wts["ln1_w"], wts["ln1_b"],
                             wts["router_w"], wts["router_b"], batch, seq, tm)

    idx = ri[:, :TOP_K]
    rank = ri[:, TOP_K:2 * TOP_K]
    counts = cnt[0, :N_EXPERTS].astype(jnp.int32)
    padded = (counts + EXPERT_BLOCK - 1) // EXPERT_BLOCK * EXPERT_BLOCK
    pad_end = jnp.cumsum(padded)
    pad_start = pad_end - padded
    dest = (pad_start[idx] + rank).reshape(t * TOP_K).astype(jnp.int32)
    nb = t * TOP_K // EXPERT_BLOCK + N_EXPERTS
    blk_start = jnp.arange(nb, dtype=jnp.int32) * EXPERT_BLOCK
    block_e = jnp.minimum(jnp.sum((pad_end[None, :] <= blk_start[:, None]).astype(jnp.int32), axis=1),
                          N_EXPERTS - 1).astype(jnp.int32)
    n_used = (pad_end[-1:] // EXPERT_BLOCK).astype(jnp.int32)
    dest_t = dest.reshape(t, TOP_K).T.reshape(TOP_K * t)
    xs = _sc_dispatch(x1, dest_t, nb * EXPERT_BLOCK)
    ys = _experts(block_e, n_used, xs, wts["exp_w_gu"], wts["exp_b_gu"], wts["exp_w_down"],
                        wts["exp_b_down"])
    n_sc = int(t * sc_share) // 2048 * 2048
    y_rest = None
    if n_sc < t:
        y_rest = _combine(dest, x1, rg, ys, wts["ln2_w"], wts["ln2_b"], _tile(t, 256), n_sc)
    if n_sc > 0:
        rows = _sc_gather(ys, dest.reshape(t, TOP_K)[:n_sc].T.reshape(TOP_K * n_sc))
        y = _combine_rows(x1, rg, rows, wts["ln2_w"], wts["ln2_b"], 512, y_rest)
    else:
        y = y_rest
    return y.reshape(batch, seq, D_MODEL)


def kernel(x_prompt, x_sample, mem_prompt, mem_sample, w_in, hgrn_lb_logits, hgrn_norm_w,
           mla_q_norm_w, mla_w_uq, mla_kv_norm_w, mla_w_uk, mla_w_uv, mem_w_kv, w_out,
           ln1_w, ln1_b, router_w, router_b, exp_w_gu, exp_b_gu, exp_w_down, exp_b_down,
           ln2_w, ln2_b):
    depth = w_in.shape[0]
    gamma = jax.nn.softmax(hgrn_lb_logits.astype(F32), axis=1)
    cum = jnp.cumsum(gamma, axis=1)
    lb_all = cum[:, 1:] - cum[:, :1]
    y_prompt, y_sample = x_prompt, x_sample
    for l in range(depth):
        row = lambda a: a[l].reshape(1, -1).astype(F32)
        wts = {
            "w_in": _prep_w_in(w_in[l]),
            "lb": lb_all[:, l],
            "hgrn_norm_w": hgrn_norm_w[l].astype(F32),
            "qnw": jnp.pad(row(mla_q_norm_w), ((0, 0), (0, 512 - MLA_Q_LORA))),
            "kvnw": row(mla_kv_norm_w),
            "w_uq": _prep_w_uq(mla_w_uq[l]),
            "w_uk": mla_w_uk[l].astype(BF16),
            "w_uv": mla_w_uv[l].astype(BF16),
            "mem_w_kv": mem_w_kv[l].astype(BF16),
            "w_out": w_out[l].astype(BF16),
            "ln1_w": row(ln1_w), "ln1_b": row(ln1_b),
            "router_w": _split_bf16(jnp.pad(router_w[l].astype(F32), ((0, 0), (0, LANES - N_EXPERTS)))),
            "router_b": jnp.pad(row(router_b), ((0, 0), (0, LANES - N_EXPERTS)), constant_values=-jnp.inf),
            "exp_w_gu": exp_w_gu[l].astype(BF16),
            "exp_b_gu": exp_b_gu[l].reshape(N_EXPERTS, 1, 2 * D_FF).astype(F32),
            "exp_w_down": exp_w_down[l].astype(BF16),
            "exp_b_down": exp_b_down[l].reshape(N_EXPERTS, 1, D_MODEL).astype(F32),
            "ln2_w": row(ln2_w), "ln2_b": row(ln2_b),
        }
        y_prompt = _layer(y_prompt, mem_prompt, wts, sc_share=1.0)
        y_sample = _layer(y_sample, mem_sample, wts, sc_share=0.875)
    return (y_prompt, y_sample)
```

```python
import functools

import numpy as np
import jax
import jax.numpy as jnp
from jax import lax
from jax.experimental import pallas as pl
from jax.experimental.pallas import tpu as pltpu
from jax.experimental.pallas import tpu_sc as plsc

F32 = jnp.float32
BF16 = jnp.bfloat16

D_MODEL = 1024
HGRN_HEADS = 8
HGRN_DK = 128
MLA_HEADS = 8
MLA_Q_LORA = 384
MLA_KV_LORA = 256
MLA_NOPE = 128
MLA_ROPE = 64
MLA_V = 128
ROPE_THETA = 10000.0
MEM_HEADS = 4
MEM_HEAD_DIM = D_MODEL // MEM_HEADS
N_EXPERTS = 32
TOP_K = 4
D_FF = D_MODEL
SWIGLU_LIMIT = 7.0
SWIGLU_ALPHA = 1.702
DN_ALPHA = 2.0 ** 0.25
LN_EPS = 1e-5
RMS_EPS = 1e-6

LANES = 128
SUBLANES = 8
QK_PAD = 256
V_PAD = 256
LOG2E = 1.4426950408889634
HGRN_CHUNK = 64
HGRN_SAFE_RANGE = 160.0
EXPERT_BLOCK = 512
SC_CORES = 2
SC_SUBCORES = 16
SC_WORKERS = SC_CORES * SC_SUBCORES
SC_GATHER_ROWS = 64
VMEM_LIMIT = 56 * 1024 * 1024

COL_GA, COL_GB, COL_GM, COL_Q, COL_ZF, COL_ZB, COL_I, COL_G, COL_MQ = range(9)
COL_DQ, COL_DKV = 18, 19
IN_COLS_PAD = 10240

NT_DIMS = (((1,), (1,)), ((), ()))
TN_DIMS = (((0,), (0,)), ((), ()))


def _params(sem, vmem=VMEM_LIMIT):
    return pltpu.CompilerParams(dimension_semantics=sem, vmem_limit_bytes=vmem)


def _mm_kernel(x_ref, w_ref, o_ref, xb_ref):
    @pl.when(pl.program_id(1) == 0)
    def _():
        xb_ref[...] = x_ref[...].astype(BF16)

    o_ref[...] = jnp.dot(xb_ref[...], w_ref[...], preferred_element_type=F32).astype(o_ref.dtype)


def _matmul(x, w, out_dtype, tm, tn, name):
    m, k = x.shape
    n = w.shape[1]
    return pl.pallas_call(
        _mm_kernel,
        out_shape=jax.ShapeDtypeStruct((m, n), out_dtype),
        grid=(m // tm, n // tn),
        in_specs=[pl.BlockSpec((tm, k), lambda i, j: (i, 0)),
                  pl.BlockSpec((k, tn), lambda i, j: (0, j))],
        out_specs=pl.BlockSpec((tm, tn), lambda i, j: (i, j)),
        scratch_shapes=[pltpu.VMEM((tm, k), BF16)],
        compiler_params=_params(("arbitrary", "arbitrary")),
        name=name,
    )(x, w)


def _hgrn_kernel(*refs, reverse, epilogue, sc):
    if epilogue:
        lb_ref, q_ref, z_ref, v_ref, of_ref, g_ref, nw_ref, o_ref, st_ref, kk_s, b_s, vf_s, os_s = refs
    else:
        lb_ref, q_ref, z_ref, v_ref, o_ref, st_ref, kk_s, b_s, vf_s = refs
    C = HGRN_CHUNK
    nch = sc // C

    @pl.when(pl.program_id(1) == 0)
    def _():
        st_ref[...] = jnp.zeros_like(st_ref)

    lb = lb_ref[...]
    row = lax.broadcasted_iota(jnp.int32, (C, C), 0)
    col = lax.broadcasted_iota(jnp.int32, (C, C), 1)
    tri = (row <= col) if reverse else (row >= col)
    trib = jnp.where(tri, 1.0, 0.0).astype(BF16)

    minb = None
    for c in range(nch):
        rows = slice(c * C, (c + 1) * C)
        z = z_ref[rows, :].astype(F32)
        gate = (1.0 - lb) * jax.nn.sigmoid(z)
        lf = jnp.log(lb + gate)
        kk_s[rows, :] = (1.0 - lb) - gate
        hi = lf.astype(BF16)
        r1 = lf - hi.astype(F32)
        mid = r1.astype(BF16)
        lo = (r1 - mid.astype(F32)).astype(BF16)
        b = (jnp.dot(trib, hi, preferred_element_type=F32)
             + jnp.dot(trib, mid, preferred_element_type=F32)
             + jnp.dot(trib, lo, preferred_element_type=F32))
        b_s[rows, :] = b
        mb = jnp.min(b)
        minb = mb if minb is None else jnp.minimum(minb, mb)

    rid = lax.broadcasted_iota(jnp.int32, (C, 1), 0)

    def chunk(i, carry, fast):
        c = (nch - 1 - i) if reverse else i
        r0 = pl.multiple_of(c * C, C)
        rows = pl.ds(r0, C)
        for h in range(HGRN_HEADS):
            cols = slice(h * HGRN_DK, (h + 1) * HGRN_DK)
            q = q_ref[rows, cols].astype(F32)
            kk = kk_s[rows, cols]
            b = b_s[rows, cols]
            v = v_ref[rows, cols]
            bl = b[0:1, :] if reverse else b[C - 1:C, :]
            if fast:
                bm = 0.5 * bl
                qd = (q * jnp.exp(b - bm)).astype(BF16)
                kd = (kk * jnp.exp(bm - b)).astype(BF16)
                s = lax.dot_general(qd, kd, NT_DIMS, preferred_element_type=F32)
                s = jnp.where(tri, s, 0.0).astype(BF16)
                o = jnp.dot(s, v, preferred_element_type=F32)
            else:
                def sbody(g_i, o_acc):
                    grp = pl.ds(pl.multiple_of(r0 + g_i * SUBLANES, SUBLANES), SUBLANES)
                    b8 = b_s[grp, cols]
                    k8 = kk_s[grp, cols]
                    v8 = vf_s[grp, cols]
                    for jj in range(SUBLANES):
                        s_i = g_i * SUBLANES + jj
                        w = q * k8[jj:jj + 1, :] * jnp.exp(jnp.minimum(b - b8[jj:jj + 1, :], 0.0))
                        scol = jnp.sum(w, axis=-1, keepdims=True)
                        keep = (rid <= s_i) if reverse else (rid >= s_i)
                        o_acc = o_acc + jnp.where(keep, scol, 0.0) * v8[jj:jj + 1, :]
                    return o_acc
                o = lax.fori_loop(0, C // SUBLANES, sbody, jnp.zeros((C, HGRN_DK), F32))
            st = st_ref[h]
            qi = (q * jnp.exp(b)).astype(BF16)
            o = o + lax.dot_general(qi, st.astype(BF16), NT_DIMS, preferred_element_type=F32)
            ke = (kk * jnp.exp(bl - b)).astype(BF16)
            upd = lax.dot_general(v, ke, TN_DIMS, preferred_element_type=F32)
            st_ref[h] = st * jnp.exp(bl) + upd
            if epilogue:
                os_s[rows, cols] = o + of_ref[rows, cols].astype(F32)
            else:
                o_ref[rows, cols] = o.astype(o_ref.dtype)
        return carry

    safe = minb >= -HGRN_SAFE_RANGE

    @pl.when(safe)
    def _():
        lax.fori_loop(0, nch, functools.partial(chunk, fast=True), 0, unroll=2)

    @pl.when(jnp.logical_not(safe))
    def _():
        vf_s[...] = v_ref[...].astype(F32)
        lax.fori_loop(0, nch, functools.partial(chunk, fast=False), 0)

    if epilogue:
        nw = nw_ref[...]
        for h in range(HGRN_HEADS):
            cols = slice(h * HGRN_DK, (h + 1) * HGRN_DK)
            os = os_s[:, cols]
            ms = jnp.mean(os * os, axis=-1, keepdims=True)
            y = os * lax.rsqrt(ms + RMS_EPS) * nw
            g = g_ref[:, cols].astype(F32)
            o_ref[:, cols] = (y * (g * jax.nn.sigmoid(g))).astype(o_ref.dtype)


def _hgrn(proj, lb, norm_w, batch, seq, sc):
    t = batch * seq
    ns = seq // sc
    blk = (sc, D_MODEL)

    def spec(colblk, reverse):
        if reverse:
            return pl.BlockSpec(blk, lambda b, n: (b * ns + ns - 1 - n, colblk))
        return pl.BlockSpec(blk, lambda b, n: (b * ns + n, colblk))

    def row_spec(reverse):
        if reverse:
            return pl.BlockSpec(blk, lambda b, n: (b * ns + ns - 1 - n, 0))
        return pl.BlockSpec(blk, lambda b, n: (b * ns + n, 0))

    lb_spec = pl.BlockSpec((1, D_MODEL), lambda b, n: (0, 0))
    common_scratch = [pltpu.VMEM((HGRN_HEADS, HGRN_DK, HGRN_DK), F32),
                      pltpu.VMEM(blk, F32), pltpu.VMEM(blk, F32), pltpu.VMEM(blk, F32)]
    o_f = pl.pallas_call(
        functools.partial(_hgrn_kernel, reverse=False, epilogue=False, sc=sc),
        out_shape=jax.ShapeDtypeStruct((t, D_MODEL), BF16),
        grid=(batch, ns),
        in_specs=[lb_spec, spec(COL_Q, False), spec(COL_ZF, False), spec(COL_I, False)],
        out_specs=row_spec(False),
        scratch_shapes=common_scratch,
        compiler_params=_params(("arbitrary", "arbitrary")),
        name="hgrn_fwd",
    )(lb[0:1], proj, proj, proj)
    o_a = pl.pallas_call(
        functools.partial(_hgrn_kernel, reverse=True, epilogue=True, sc=sc),
        out_shape=jax.ShapeDtypeStruct((t, D_MODEL), BF16),
        grid=(batch, ns),
        in_specs=[lb_spec, spec(COL_Q, True), spec(COL_ZB, True), spec(COL_I, True),
                  row_spec(True), spec(COL_G, True),
                  pl.BlockSpec((1, HGRN_DK), lambda b, n: (0, 0))],
        out_specs=row_spec(True),
        scratch_shapes=common_scratch + [pltpu.VMEM(blk, F32)],
        compiler_params=_params(("arbitrary", "arbitrary")),
        name="hgrn_bwd",
    )(lb[1:2], proj, proj, proj, o_f, proj, norm_w.reshape(1, HGRN_DK))
    return o_a


def _mla_prep_kernel(dq_ref, dkv_ref, qnw_ref, kvnw_ref, wq_ref, wk_ref, wv_ref, cos_ref, sin_ref,
                     q_ref, k_ref, v_ref):
    scale = (MLA_NOPE + MLA_ROPE) ** -0.5 * LOG2E
    cos = cos_ref[...]
    sin = sin_ref[...]
    dq = dq_ref[...].astype(F32)
    ms = jnp.sum(dq * dq, axis=-1, keepdims=True) * (1.0 / MLA_Q_LORA)
    cq = (dq * lax.rsqrt(ms + RMS_EPS) * qnw_ref[...]).astype(BF16)
    qa = jnp.dot(cq, wq_ref[...], preferred_element_type=F32)
    for h in range(MLA_HEADS):
        base = h * QK_PAD
        q_ref[:, base:base + MLA_NOPE] = (qa[:, base:base + MLA_NOPE] * scale).astype(BF16)
        rp = qa[:, base + MLA_NOPE:base + QK_PAD]
        sw = qa[:, MLA_HEADS * QK_PAD + h * LANES:MLA_HEADS * QK_PAD + (h + 1) * LANES]
        q_ref[:, base + MLA_NOPE:base + QK_PAD] = ((rp * cos + sw * sin) * scale).astype(BF16)
    dkv = dkv_ref[...].astype(F32)
    ckv = dkv[:, :MLA_KV_LORA]
    msk = jnp.mean(ckv * ckv, axis=-1, keepdims=True)
    cn = (ckv * lax.rsqrt(msk + RMS_EPS) * kvnw_ref[...]).astype(BF16)
    kn = jnp.dot(cn, wk_ref[...], preferred_element_type=F32)
    vv = jnp.dot(cn, wv_ref[...], preferred_element_type=F32).astype(BF16)
    ones_col = jnp.where(lax.broadcasted_iota(jnp.int32, (vv.shape[0], LANES), 1) == 0, 1.0, 0.0).astype(BF16)
    for h in range(MLA_HEADS):
        v_ref[:, h * V_PAD:h * V_PAD + MLA_V] = vv[:, h * MLA_V:(h + 1) * MLA_V]
        v_ref[:, h * V_PAD + MLA_V:(h + 1) * V_PAD] = ones_col
    kr = (dkv[:, MLA_KV_LORA:MLA_KV_LORA + LANES] * cos
          + dkv[:, MLA_KV_LORA + LANES:MLA_KV_LORA + 2 * LANES] * sin).astype(BF16)
    for h in range(MLA_HEADS):
        base = h * QK_PAD
        k_ref[:, base:base + MLA_NOPE] = kn[:, h * MLA_NOPE:(h + 1) * MLA_NOPE].astype(BF16)
        k_ref[:, base + MLA_NOPE:base + QK_PAD] = kr


def _mla_prep(proj, qnw, kvnw, wq, wk, wv, cos, sin, batch, seq, tm):
    t = batch * seq
    npos = seq // tm
    full = lambda shape: pl.BlockSpec(shape, lambda i: (0, 0))
    return pl.pallas_call(
        _mla_prep_kernel,
        out_shape=(jax.ShapeDtypeStruct((t, MLA_HEADS * QK_PAD), BF16),
                   jax.ShapeDtypeStruct((t, MLA_HEADS * QK_PAD), BF16),
                   jax.ShapeDtypeStruct((t, MLA_HEADS * V_PAD), BF16)),
        grid=(t // tm,),
        in_specs=[pl.BlockSpec((tm, 512), lambda i: (i, COL_DQ)),
                  pl.BlockSpec((tm, 512), lambda i: (i, COL_DKV)),
                  full((1, 512)), full((1, MLA_KV_LORA)),
                  full(wq.shape), full(wk.shape), full(wv.shape),
                  pl.BlockSpec((tm, LANES), lambda i: (i % npos, 0)),
                  pl.BlockSpec((tm, LANES), lambda i: (i % npos, 0))],
        out_specs=(pl.BlockSpec((tm, MLA_HEADS * QK_PAD), lambda i: (i, 0)),
                   pl.BlockSpec((tm, MLA_HEADS * QK_PAD), lambda i: (i, 0)),
                   pl.BlockSpec((tm, MLA_HEADS * V_PAD), lambda i: (i, 0))),
        compiler_params=_params(("arbitrary",)),
        name="mla_prep",
    )(proj, proj, qnw, kvnw, wq, wk, wv, cos, sin)


def _flash_kernel(q_ref, k_ref, v_ref, o_ref, m_s, acc_s, s_buf, *, bk, nk):
    m_s[...] = jnp.full(m_s.shape, -jnp.inf, F32)
    acc_s[...] = jnp.zeros(acc_s.shape, F32)
    nt = bk // LANES
    bq = q_ref.shape[0]
    nsplit = 2 if bq % 256 == 0 else 1
    hq = bq // nsplit
    qrs = [slice(hf * hq, (hf + 1) * hq) for hf in range(nsplit)]

    def scores(j, slot):
        rows = pl.ds(pl.multiple_of(j * bk, bk), bk)
        for qr in qrs:
            s_buf[slot, qr, :] = lax.dot_general(q_ref[qr, :], k_ref[rows, :], NT_DIMS,
                                                 preferred_element_type=F32)

    def consume(j, slot):
        rows = pl.ds(pl.multiple_of(j * bk, bk), bk)
        ps, alphas = [], []
        for qr in qrs:
            tiles = [s_buf[slot, qr, t * LANES:(t + 1) * LANES] for t in range(nt)]
            tmax = tiles[0]
            for t in range(1, nt):
                tmax = jnp.maximum(tmax, tiles[t])
            m_prev = m_s[qr, :]
            m_new = jnp.maximum(m_prev, jnp.max(tmax, axis=-1, keepdims=True))
            alphas.append(jnp.exp2(m_prev - m_new))
            ps.append(jnp.concatenate([jnp.exp2(tl - m_new).astype(BF16) for tl in tiles], axis=1))
            m_s[qr, :] = m_new
        for qr, p, a in zip(qrs, ps, alphas):
            pv = jnp.dot(p, v_ref[rows, :], preferred_element_type=F32)
            acc_s[qr, :] = jnp.concatenate([a, a], axis=1) * acc_s[qr, :] + pv

    scores(0, 0)
    npairs = (nk - 1) // 2

    def body(i, carry):
        j = 2 * i
        scores(j + 1, 1)
        consume(j, 0)
        scores(j + 2, 0)
        consume(j + 1, 1)
        return carry

    lax.fori_loop(0, npairs, body, 0)
    if nk - 2 * npairs == 2:
        scores(nk - 1, 1)
        consume(nk - 2, 0)
        consume(nk - 1, 1)
    else:
        consume(nk - 1, 0)
    acc = acc_s[...]
    o_ref[...] = (acc[:, :MLA_V] / acc[:, MLA_V:MLA_V + 1]).astype(o_ref.dtype)


def _flash(q, k, v, batch, seq, bq, bk):
    t = batch * seq
    nq = seq // bq
    return pl.pallas_call(
        functools.partial(_flash_kernel, bk=bk, nk=seq // bk),
        out_shape=jax.ShapeDtypeStruct((t, MLA_HEADS * MLA_V), BF16),
        grid=(batch, MLA_HEADS, nq),
        in_specs=[pl.BlockSpec((bq, QK_PAD), lambda b, h, i: (b * nq + i, h)),
                  pl.BlockSpec((seq, QK_PAD), lambda b, h, i: (b, h)),
                  pl.BlockSpec((seq, V_PAD), lambda b, h, i: (b, h))],
        out_specs=pl.BlockSpec((bq, MLA_V), lambda b, h, i: (b * nq + i, h)),
        scratch_shapes=[pltpu.VMEM((bq, LANES), F32), pltpu.VMEM((bq, V_PAD), F32),
                        pltpu.VMEM((2, bq, bk), F32)],
        compiler_params=_params(("arbitrary", "arbitrary", "arbitrary")),
        name="mla_flash",
    )(q, k, v)


def _layernorm(y, w, b):
    mu = jnp.mean(y, axis=-1, keepdims=True)
    yc = y - mu
    var = jnp.mean(yc * yc, axis=-1, keepdims=True)
    return yc * lax.rsqrt(var + LN_EPS) * w + b


def _merge_kernel(x_ref, ga_ref, gb_ref, gm_ref, mq_ref, oa_ref, ob_ref, kvm_ref, wout_ref,
                  l1w_ref, l1b_ref, rw_ref, rb_ref,
                  x1_ref, ri_ref, rg_ref, cnt_ref, carry_s, *, tm):
    @pl.when(pl.program_id(0) == 0)
    def _():
        carry_s[...] = jnp.zeros_like(carry_s)

    parts = []
    for h in range(MEM_HEADS):
        cols = slice(h * MEM_HEAD_DIM, (h + 1) * MEM_HEAD_DIM)
        kh = kvm_ref[:, cols]
        vh = kvm_ref[:, D_MODEL + h * MEM_HEAD_DIM:D_MODEL + (h + 1) * MEM_HEAD_DIM]
        s = lax.dot_general(mq_ref[:, cols], kh, NT_DIMS, preferred_element_type=F32) * (MEM_HEAD_DIM ** -0.5)
        s = s - jnp.max(s, axis=-1, keepdims=True)
        p = jnp.exp(s)
        p = p / jnp.sum(p, axis=-1, keepdims=True)
        parts.append(jnp.dot(p.astype(BF16), vh, preferred_element_type=F32))
    om = jnp.concatenate(parts, axis=1)

    merged = (jax.nn.sigmoid(ga_ref[...].astype(F32)) * oa_ref[...].astype(F32)
              + jax.nn.sigmoid(gb_ref[...].astype(F32)) * ob_ref[...].astype(F32)
              + jax.nn.sigmoid(gm_ref[...].astype(F32)) * om)
    y = DN_ALPHA * x_ref[...] + jnp.dot(merged.astype(BF16), wout_ref[...], preferred_element_type=F32)
    x1 = _layernorm(y, l1w_ref[...], l1b_ref[...])
    x1_ref[...] = x1

    x_hi = x1.astype(BF16)
    x_lo = (x1 - x_hi.astype(F32)).astype(BF16)
    hi_part = jnp.dot(x_hi, rw_ref[...], preferred_element_type=F32)
    lo_part = jnp.dot(x_lo, rw_ref[:, :LANES], preferred_element_type=F32)
    logits = hi_part[:, :LANES] + hi_part[:, LANES:] + lo_part + rb_ref[...]
    lane_i = lax.broadcasted_iota(jnp.int32, (tm, LANES), 1)
    lane = lane_i.astype(F32)
    work = logits
    idx, val = [], []
    for _ in range(TOP_K):
        mx = jnp.max(work, axis=-1, keepdims=True)
        ix = jnp.min(jnp.where(work == mx, lane, float(LANES)), axis=-1, keepdims=True)
        idx.append(ix)
        val.append(mx)
        work = jnp.where(lane == ix, -jnp.inf, work)
    ex = [jnp.exp(v - val[0]) for v in val]
    tot = ex[0] + ex[1] + ex[2] + ex[3]
    hot = [jnp.where(lane == ix, 1.0, 0.0) for ix in idx]
    multi = hot[0] + hot[1] + hot[2] + hot[3]
    r = lax.broadcasted_iota(jnp.int32, (tm, tm), 0)
    c = lax.broadcasted_iota(jnp.int32, (tm, tm), 1)
    lower = jnp.where(r > c, 1.0, 0.0).astype(BF16)
    before = jnp.dot(lower, multi.astype(BF16), preferred_element_type=F32) + carry_s[0:1, :]
    ri = jnp.zeros((tm, LANES), F32)
    rg = jnp.zeros((tm, LANES), F32)
    for j in range(TOP_K):
        rank = jnp.sum(before * hot[j], axis=-1, keepdims=True)
        ri = ri + jnp.where(lane == float(j), idx[j], 0.0) + jnp.where(lane == float(TOP_K + j), rank, 0.0)
        rg = rg + jnp.where(lane == float(j), ex[j] / tot, 0.0)
    ri_ref[...] = ri.astype(jnp.int32)
    rg_ref[...] = rg
    carry_s[...] = carry_s[...] + jnp.sum(multi, axis=0, keepdims=True)
    cnt_ref[...] = carry_s[...]


def _merge(x2d, proj, o_a, o_b, kvm, wout, l1w, l1b, rw, rb, batch, seq, tm):
    t = batch * seq
    per_b = seq // tm
    nmem = kvm.shape[0] // batch
    tile = lambda colblk: pl.BlockSpec((tm, D_MODEL), lambda i: (i, colblk))
    full = lambda shape: pl.BlockSpec(shape, lambda i: (0, 0))
    return pl.pallas_call(
        functools.partial(_merge_kernel, tm=tm),
        out_shape=(jax.ShapeDtypeStruct((t, D_MODEL), F32),
                   jax.ShapeDtypeStruct((t, LANES), jnp.int32),
                   jax.ShapeDtypeStruct((t, LANES), F32),
                   jax.ShapeDtypeStruct((8, LANES), F32)),
        grid=(t // tm,),
        in_specs=[tile(0), tile(COL_GA), tile(COL_GB), tile(COL_GM), tile(COL_MQ), tile(0), tile(0),
                  pl.BlockSpec((nmem, 2 * D_MODEL), lambda i: (i // per_b, 0)),
                  full((D_MODEL, D_MODEL)), full((1, D_MODEL)), full((1, D_MODEL)),
                  full((D_MODEL, 2 * LANES)), full((1, LANES))],
        out_specs=(tile(0), pl.BlockSpec((tm, LANES), lambda i: (i, 0)),
                   pl.BlockSpec((tm, LANES), lambda i: (i, 0)), full((8, LANES))),
        scratch_shapes=[pltpu.VMEM((8, LANES), F32)],
        compiler_params=_params(("arbitrary",)),
        name="merge_router",
    )(x2d, proj, proj, proj, proj, o_a, o_b, kvm, wout, l1w, l1b, rw, rb)


def _sc_gather(table, idx):
    p, d = idx.shape[0], table.shape[1]
    per_w = p // SC_WORKERS
    assert per_w * SC_WORKERS == p and per_w % SC_GATHER_ROWS == 0
    mesh = plsc.VectorSubcoreMesh(core_axis_name="c", subcore_axis_name="s")

    @functools.partial(
        pl.kernel, mesh=mesh,
        out_type=jax.ShapeDtypeStruct((p, d), table.dtype),
        scratch_types=[pltpu.VMEM((SC_GATHER_ROWS,), jnp.int32),
                       pltpu.VMEM((SC_GATHER_ROWS, d), table.dtype),
                       pltpu.SemaphoreType.DMA],
        name="moe_gather_sc",
    )
    def gather_kernel(table_hbm, idx_hbm, out_hbm, idx_v, rows_v, sem):
        wid = lax.axis_index("s") * SC_CORES + lax.axis_index("c")
        base = wid * per_w

        @pl.loop(0, per_w // SC_GATHER_ROWS)
        def _(c):
            off = base + c * SC_GATHER_ROWS
            pltpu.sync_copy(idx_hbm.at[pl.ds(off, SC_GATHER_ROWS)], idx_v)
            pltpu.async_copy(table_hbm.at[idx_v], rows_v, sem).wait()
            pltpu.sync_copy(rows_v, out_hbm.at[pl.ds(off, SC_GATHER_ROWS)])

    return gather_kernel(table, idx)


def _sc_dispatch(x, dest_t, slots):
    t, d = x.shape
    per_w = t // SC_WORKERS
    assert per_w * SC_WORKERS == t and per_w % SC_GATHER_ROWS == 0
    mesh = plsc.VectorSubcoreMesh(core_axis_name="c", subcore_axis_name="s")

    @functools.partial(
        pl.kernel, mesh=mesh,
        out_type=jax.ShapeDtypeStruct((slots, d), x.dtype),
        scratch_types=[pltpu.VMEM((SC_GATHER_ROWS,), jnp.int32),
                       pltpu.VMEM((SC_GATHER_ROWS, d), x.dtype)],
        name="moe_dispatch_sc",
    )
    def scatter_kernel(x_hbm, idx_hbm, out_hbm, idx_v, rows_v):
        wid = lax.axis_index("s") * SC_CORES + lax.axis_index("c")
        base = wid * per_w

        @pl.loop(0, per_w // SC_GATHER_ROWS)
        def _(c):
            off = base + c * SC_GATHER_ROWS
            pltpu.sync_copy(x_hbm.at[pl.ds(off, SC_GATHER_ROWS)], rows_v)
            for j in range(TOP_K):
                pltpu.sync_copy(idx_hbm.at[pl.ds(j * t + off, SC_GATHER_ROWS)], idx_v)
                pltpu.sync_copy(rows_v, out_hbm.at[idx_v])

    return scatter_kernel(x, dest_t)


HALF_D = D_MODEL // 2
HI16 = -65536


def _pack_bf16_pair(a, b):
    wa = lax.bitcast_convert_type(a.astype(BF16).astype(F32), jnp.int32)
    wb = lax.bitcast_convert_type(b.astype(BF16).astype(F32), jnp.int32)
    return wa | lax.shift_right_logical(wb, 16)


def _unpack_bf16_pair(w):
    a = lax.bitcast_convert_type(w & HI16, F32)
    b = lax.bitcast_convert_type(lax.shift_left(w, 16), F32)
    return a, b


def _expert_kernel(be_ref, nu_ref, xs_ref, wgu_ref, bgu_ref, wd_ref, bd_ref, ys_ref):
    del be_ref
    i = pl.program_id(0)

    @pl.when(i < nu_ref[0])
    def _():
        h = jnp.dot(xs_ref[...].astype(BF16), wgu_ref[0], preferred_element_type=F32) + bgu_ref[0]
        g = jnp.minimum(h[:, :D_FF], SWIGLU_LIMIT)
        u = jnp.clip(h[:, D_FF:], -SWIGLU_LIMIT, SWIGLU_LIMIT)
        act = (u + 1.0) * g * jax.nn.sigmoid(SWIGLU_ALPHA * g)
        y = jnp.dot(act.astype(BF16), wd_ref[0], preferred_element_type=F32) + bd_ref[0]
        ys_ref[...] = _pack_bf16_pair(y[:, :HALF_D], y[:, HALF_D:])

    @pl.when(i >= nu_ref[0])
    def _():
        ys_ref[...] = jnp.zeros_like(ys_ref)


def _experts(block_e, n_used, xs, wgu, bgu, wd, bd):
    slots = xs.shape[0]
    nb = slots // EXPERT_BLOCK
    grid_spec = pltpu.PrefetchScalarGridSpec(
        num_scalar_prefetch=2,
        grid=(nb,),
        in_specs=[pl.BlockSpec((EXPERT_BLOCK, D_MODEL), lambda i, be, nu: (i, 0)),
                  pl.BlockSpec((1, D_MODEL, 2 * D_FF), lambda i, be, nu: (be[i], 0, 0)),
                  pl.BlockSpec((1, 1, 2 * D_FF), lambda i, be, nu: (be[i], 0, 0)),
                  pl.BlockSpec((1, D_FF, D_MODEL), lambda i, be, nu: (be[i], 0, 0)),
                  pl.BlockSpec((1, 1, D_MODEL), lambda i, be, nu: (be[i], 0, 0))],
        out_specs=pl.BlockSpec((EXPERT_BLOCK, HALF_D), lambda i, be, nu: (i, 0)),
    )
    return pl.pallas_call(
        _expert_kernel,
        out_shape=jax.ShapeDtypeStruct((slots, HALF_D), jnp.int32),
        grid_spec=grid_spec,
        compiler_params=_params(("arbitrary",)),
        name="moe_experts",
    )(block_e, n_used, xs, wgu, bgu, wd, bd)


def _weighted_rows(rg, packed):
    lo = hi = None
    for j, w in enumerate(packed):
        a, b = _unpack_bf16_pair(w)
        gj = rg[:, j:j + 1]
        lo = gj * a if lo is None else lo + gj * a
        hi = gj * b if hi is None else hi + gj * b
    return jnp.concatenate([lo, hi], axis=1)


def _combine_kernel(dest_hbm, x1_ref, rg_ref, ys_hbm, ys_tiles_hbm, l2w_ref, l2b_ref, o_ref, idx_s, buf,
                    sem_i, sem_r, *, tm, first_tile):
    i = pl.program_id(0)
    nsteps = pl.num_programs(0)
    n = TOP_K * tm

    def issue(step, slot):
        cp = pltpu.make_async_copy(dest_hbm.at[pl.ds(pl.multiple_of((first_tile + step) * n, n), n)], idx_s, sem_i)
        cp.start()
        cp.wait()

        def body(g, carry):
            for jj in range(SUBLANES):
                for j in range(TOP_K):
                    d = idx_s[g * (SUBLANES * TOP_K) + jj * TOP_K + j]
                    pltpu.make_async_copy(ys_hbm.at[pl.ds(d, 1)], buf.at[slot, j, g, pl.ds(jj, 1)],
                                          sem_r.at[slot]).start(priority=j % 2)
            return carry

        lax.fori_loop(0, tm // SUBLANES, body, 0)

    @pl.when(i == 0)
    def _():
        issue(0, 0)

    @pl.when(i + 1 < nsteps)
    def _():
        issue(i + 1, (i + 1) % 2)

    slot = i % 2
    for j in range(TOP_K):
        pltpu.make_async_copy(ys_tiles_hbm.at[pl.ds(0, tm // SUBLANES)], buf.at[slot, j], sem_r.at[slot]).wait()
    moe = _weighted_rows(rg_ref[...], [buf[slot, j].reshape(tm, HALF_D) for j in range(TOP_K)])
    o_ref[...] = _layernorm(DN_ALPHA * x1_ref[...] + moe, l2w_ref[...], l2b_ref[...])


def _combine_rows_kernel(x1_ref, rg_ref, g0_ref, g1_ref, g2_ref, g3_ref, l2w_ref, l2b_ref, *rest):
    o_ref = rest[-1]
    moe = _weighted_rows(rg_ref[...], [g0_ref[...], g1_ref[...], g2_ref[...], g3_ref[...]])
    o_ref[...] = _layernorm(DN_ALPHA * x1_ref[...] + moe, l2w_ref[...], l2b_ref[...])


def _combine_rows(x1, rg, rows, l2w, l2b, tm, y_rest):
    t = x1.shape[0]
    nt = rows.shape[0] // TOP_K // tm
    full = lambda shape: pl.BlockSpec(shape, lambda i: (0, 0))
    choice = lambda j: pl.BlockSpec((tm, HALF_D), lambda i, j=j: (j * nt + i, 0))
    in_specs = [pl.BlockSpec((tm, D_MODEL), lambda i: (i, 0)),
                pl.BlockSpec((tm, LANES), lambda i: (i, 0)),
                choice(0), choice(1), choice(2), choice(3),
                full((1, D_MODEL)), full((1, D_MODEL))]
    args = [x1, rg, rows, rows, rows, rows, l2w, l2b]
    aliases = {}
    if y_rest is not None:
        in_specs.append(pl.BlockSpec(memory_space=pl.ANY))
        args.append(y_rest)
        aliases = {len(args) - 1: 0}
    return pl.pallas_call(
        _combine_rows_kernel,
        out_shape=jax.ShapeDtypeStruct((t, D_MODEL), F32),
        grid=(nt,),
        in_specs=in_specs,
        out_specs=pl.BlockSpec((tm, D_MODEL), lambda i: (i, 0)),
        input_output_aliases=aliases,
        compiler_params=_params(("arbitrary",)),
        name="moe_combine_rows",
    )(*args)


def _combine(dest, x1, rg, ys, l2w, l2b, tm, tok0):
    t = x1.shape[0]
    first = tok0 // tm
    full = lambda shape: pl.BlockSpec(shape, lambda i: (0, 0))
    return pl.pallas_call(
        functools.partial(_combine_kernel, tm=tm, first_tile=first),
        out_shape=jax.ShapeDtypeStruct((t, D_MODEL), F32),
        grid=((t - tok0) // tm,),
        in_specs=[pl.BlockSpec(memory_space=pl.ANY),
                  pl.BlockSpec((tm, D_MODEL), lambda i: (first + i, 0)),
                  pl.BlockSpec((tm, LANES), lambda i: (first + i, 0)),
                  pl.BlockSpec(memory_space=pl.ANY), pl.BlockSpec(memory_space=pl.ANY),
                  full((1, D_MODEL)), full((1, D_MODEL))],
        out_specs=pl.BlockSpec((tm, D_MODEL), lambda i: (first + i, 0)),
        scratch_shapes=[pltpu.SMEM((TOP_K * tm,), jnp.int32),
                        pltpu.VMEM((2, TOP_K, tm // SUBLANES, SUBLANES, HALF_D), jnp.int32),
                        pltpu.SemaphoreType.DMA, pltpu.SemaphoreType.DMA((2,))],
        compiler_params=_params(("arbitrary",)),
        name="moe_combine",
    )(dest, x1, rg, ys, ys.reshape(ys.shape[0] // SUBLANES, SUBLANES, HALF_D), l2w, l2b)


def _prep_w_in(w):
    zeros = lambda n: jnp.zeros((D_MODEL, n), w.dtype)
    kr = w[:, 5760:5824]
    kr_sw = jnp.concatenate([kr[:, MLA_ROPE // 2:], kr[:, :MLA_ROPE // 2]], axis=1)
    parts = [w[:, 6848:9920], w[:, 0:5120], w[:, 5824:6848],
             w[:, 5120:5504], zeros(128),
             w[:, 5504:5760], kr, zeros(64), kr_sw, zeros(64)]
    return jnp.concatenate(parts, axis=1).astype(BF16)


def _prep_w_uq(w):
    w3 = w.reshape(MLA_Q_LORA, MLA_HEADS, MLA_NOPE + MLA_ROPE)
    rope = w3[:, :, MLA_NOPE:]
    rope_sw = jnp.concatenate([rope[:, :, MLA_ROPE // 2:], rope[:, :, :MLA_ROPE // 2]], axis=-1)
    pad = jnp.zeros((MLA_Q_LORA, MLA_HEADS, QK_PAD - MLA_NOPE - MLA_ROPE), w.dtype)
    main = jnp.concatenate([w3, pad], axis=-1).reshape(MLA_Q_LORA, MLA_HEADS * QK_PAD)
    swp = jnp.concatenate([rope_sw, pad], axis=-1).reshape(MLA_Q_LORA, MLA_HEADS * LANES)
    both = jnp.concatenate([main, swp], axis=1)
    return jnp.pad(both, ((0, 512 - MLA_Q_LORA), (0, 0))).astype(BF16)


def _rope_tables(seq):
    inv_freq = ROPE_THETA ** (-jnp.arange(0, MLA_ROPE, 2, dtype=F32) / MLA_ROPE)
    ang = jnp.arange(seq, dtype=F32)[:, None] * inv_freq[None, :]
    cos, sin = jnp.cos(ang), jnp.sin(ang)
    pad = jnp.zeros((seq, LANES - MLA_ROPE), F32)
    return (jnp.concatenate([cos, cos, pad], axis=1), jnp.concatenate([-sin, sin, pad], axis=1))


def _split_bf16(w):
    hi = w.astype(BF16)
    lo = (w - hi.astype(F32)).astype(BF16)
    return jnp.concatenate([hi, lo], axis=1)


def _tile(n, pref):
    return pref if n % pref == 0 else n


def _layer(x, mem, wts, sc_share):
    batch, seq, _ = x.shape
    t = batch * seq
    x2d = x.reshape(t, D_MODEL)

    proj = _matmul(x2d, wts["w_in"], BF16, _tile(t, 1024), 2048, "in_proj")
    o_a = _hgrn(proj, wts["lb"], wts["hgrn_norm_w"], batch, seq, _tile(seq, 512))

    cos, sin = _rope_tables(seq)
    q, k, v = _mla_prep(proj, wts["qnw"], wts["kvnw"], wts["w_uq"], wts["w_uk"], wts["w_uv"],
                        cos, sin, batch, seq, _tile(seq, 512))
    o_b = _flash(q, k, v, batch, seq, _tile(seq, 1024), _tile(seq, 1024))

    nmem = mem.shape[1]
    kvm = _matmul(mem.reshape(batch * nmem, D_MODEL), wts["mem_w_kv"], BF16,
                  _tile(batch * nmem, 512), 1024, "mem_kv")

    tm = _tile(seq, 512)
    x1, ri, rg, cnt = _merge(x2d, proj, o_a, o_b, kvm, wts["w_out"], wts["ln1_w"], wts["ln1_b"],
                             wts["router_w"], wts["router_b"], batch, seq, tm)

    idx = ri[:, :TOP_K]
    rank = ri[:, TOP_K:2 * TOP_K]
    counts = cnt[0, :N_EXPERTS].astype(jnp.int32)
    padded = (counts + EXPERT_BLOCK - 1) // EXPERT_BLOCK * EXPERT_BLOCK
    pad_end = jnp.cumsum(padded)
    pad_start = pad_end - padded
    dest = (pad_start[idx] + rank).reshape(t * TOP_K).astype(jnp.int32)
    nb = t * TOP_K // EXPERT_BLOCK + N_EXPERTS
    blk_start = jnp.arange(nb, dtype=jnp.int32) * EXPERT_BLOCK
    block_e = jnp.minimum(jnp.sum((pad_end[None, :] <= blk_start[:, None]).astype(jnp.int32), axis=1),
                          N_EXPERTS - 1).astype(jnp.int32)
    n_used = (pad_end[-1:] // EXPERT_BLOCK).astype(jnp.int32)
    dest_t = dest.reshape(t, TOP_K).T.reshape(TOP_K * t)
    xs = _sc_dispatch(x1, dest_t, nb * EXPERT_BLOCK)
    ys = _experts(block_e, n_used, xs, wts["exp_w_gu"], wts["exp_b_gu"], wts["exp_w_down"],
                        wts["exp_b_down"])
    n_sc = int(t * sc_share) // 2048 * 2048
    y_rest = None
    if n_sc < t:
        y_rest = _combine(dest, x1, rg, ys, wts["ln2_w"], wts["ln2_b"], _tile(t, 256), n_sc)
    if n_sc > 0:
        rows = _sc_gather(ys, dest.reshape(t, TOP_K)[:n_sc].T.reshape(TOP_K * n_sc))
        y = _combine_rows(x1, rg, rows, wts["ln2_w"], wts["ln2_b"], 512, y_rest)
    else:
        y = y_rest
    return y.reshape(batch, seq, D_MODEL)


def kernel(x_prompt, x_sample, mem_prompt, mem_sample, w_in, hgrn_lb_logits, hgrn_norm_w,
           mla_q_norm_w, mla_w_uq, mla_kv_norm_w, mla_w_uk, mla_w_uv, mem_w_kv, w_out,
           ln1_w, ln1_b, router_w, router_b, exp_w_gu, exp_b_gu, exp_w_down, exp_b_down,
           ln2_w, ln2_b):
    depth = w_in.shape[0]
    gamma = jax.nn.softmax(hgrn_lb_logits.astype(F32), axis=1)
    cum = jnp.cumsum(gamma, axis=1)
    lb_all = cum[:, 1:] - cum[:, :1]
    y_prompt, y_sample = x_prompt, x_sample
    for l in range(depth):
        row = lambda a: a[l].reshape(1, -1).astype(F32)
        wts = {
            "w_in": _prep_w_in(w_in[l]),
            "lb": lb_all[:, l],
            "hgrn_norm_w": hgrn_norm_w[l].astype(F32),
            "qnw": jnp.pad(row(mla_q_norm_w), ((0, 0), (0, 512 - MLA_Q_LORA))),
            "kvnw": row(mla_kv_norm_w),
            "w_uq": _prep_w_uq(mla_w_uq[l]),
            "w_uk": mla_w_uk[l].astype(BF16),
            "w_uv": mla_w_uv[l].astype(BF16),
            "mem_w_kv": mem_w_kv[l].astype(BF16),
            "w_out": w_out[l].astype(BF16),
            "ln1_w": row(ln1_w), "ln1_b": row(ln1_b),
            "router_w": _split_bf16(jnp.pad(router_w[l].astype(F32), ((0, 0), (0, LANES - N_EXPERTS)))),
            "router_b": jnp.pad(row(router_b), ((0, 0), (0, LANES - N_EXPERTS)), constant_values=-jnp.inf),
            "exp_w_gu": exp_w_gu[l].astype(BF16),
            "exp_b_gu": exp_b_gu[l].reshape(N_EXPERTS, 1, 2 * D_FF).astype(F32),
            "exp_w_down": exp_w_down[l].astype(BF16),
            "exp_b_down": exp_b_down[l].reshape(N_EXPERTS, 1, D_MODEL).astype(F32),
            "ln2_w": row(ln2_w), "ln2_b": row(ln2_b),
        }
        y_prompt = _layer(y_prompt, mem_prompt, wts, sc_share=1.0)
        y_sample = _layer(y_sample, mem_sample, wts, sc_share=0.875)
    return (y_prompt, y_sample)
```

```python
import functools

import numpy as np
import jax
import jax.numpy as jnp
from jax import lax
from jax.experimental import pallas as pl
from jax.experimental.pallas import tpu as pltpu
from jax.experimental.pallas import tpu_sc as plsc

F32 = jnp.float32
BF16 = jnp.bfloat16

D_MODEL = 1024
HGRN_HEADS = 8
HGRN_DK = 128
MLA_HEADS = 8
MLA_Q_LORA = 384
MLA_KV_LORA = 256
MLA_NOPE = 128
MLA_ROPE = 64
MLA_V = 128
ROPE_THETA = 10000.0
MEM_HEADS = 4
MEM_HEAD_DIM = D_MODEL // MEM_HEADS
N_EXPERTS = 32
TOP_K = 4
D_FF = D_MODEL
SWIGLU_LIMIT = 7.0
SWIGLU_ALPHA = 1.702
DN_ALPHA = 2.0 ** 0.25
LN_EPS = 1e-5
RMS_EPS = 1e-6

LANES = 128
SUBLANES = 8
QK_PAD = 256
V_PAD = 256
LOG2E = 1.4426950408889634
HGRN_CHUNK = 64
HGRN_SAFE_RANGE = 160.0
EXPERT_BLOCK = 512
SC_CORES = 2
SC_SUBCORES = 16
SC_WORKERS = SC_CORES * SC_SUBCORES
SC_GATHER_ROWS = 64
VMEM_LIMIT = 56 * 1024 * 1024

COL_GA, COL_GB, COL_GM, COL_Q, COL_ZF, COL_ZB, COL_I, COL_G, COL_MQ = range(9)
COL_DQ, COL_DKV = 18, 19
IN_COLS_PAD = 10240

NT_DIMS = (((1,), (1,)), ((), ()))
TN_DIMS = (((0,), (0,)), ((), ()))


def _params(sem, vmem=VMEM_LIMIT):
    return pltpu.CompilerParams(dimension_semantics=sem, vmem_limit_bytes=vmem)


def _mm_kernel(x_ref, w_ref, o_ref, xb_ref):
    @pl.when(pl.program_id(1) == 0)
    def _():
        xb_ref[...] = x_ref[...].astype(BF16)

    o_ref[...] = jnp.dot(xb_ref[...], w_ref[...], preferred_element_type=F32).astype(o_ref.dtype)


def _matmul(x, w, out_dtype, tm, tn, name):
    m, k = x.shape
    n = w.shape[1]
    return pl.pallas_call(
        _mm_kernel,
        out_shape=jax.ShapeDtypeStruct((m, n), out_dtype),
        grid=(m // tm, n // tn),
        in_specs=[pl.BlockSpec((tm, k), lambda i, j: (i, 0)),
                  pl.BlockSpec((k, tn), lambda i, j: (0, j))],
        out_specs=pl.BlockSpec((tm, tn), lambda i, j: (i, j)),
        scratch_shapes=[pltpu.VMEM((tm, k), BF16)],
        compiler_params=_params(("arbitrary", "arbitrary")),
        name=name,
    )(x, w)


def _hgrn_kernel(*refs, reverse, epilogue, sc):
    if epilogue:
        lb_ref, q_ref, z_ref, v_ref, of_ref, g_ref, nw_ref, o_ref, st_ref, kk_s, b_s, vf_s, os_s = refs
    else:
        lb_ref, q_ref, z_ref, v_ref, o_ref, st_ref, kk_s, b_s, vf_s = refs
    C = HGRN_CHUNK
    nch = sc // C

    @pl.when(pl.program_id(1) == 0)
    def _():
        st_ref[...] = jnp.zeros_like(st_ref)

    lb = lb_ref[...]
    row = lax.broadcasted_iota(jnp.int32, (C, C), 0)
    col = lax.broadcasted_iota(jnp.int32, (C, C), 1)
    tri = (row <= col) if reverse else (row >= col)
    trib = jnp.where(tri, 1.0, 0.0).astype(BF16)

    minb = None
    for c in range(nch):
        rows = slice(c * C, (c + 1) * C)
        z = z_ref[rows, :].astype(F32)
        gate = (1.0 - lb) * jax.nn.sigmoid(z)
        lf = jnp.log(lb + gate)
        kk_s[rows, :] = (1.0 - lb) - gate
        hi = lf.astype(BF16)
        r1 = lf - hi.astype(F32)
        mid = r1.astype(BF16)
        lo = (r1 - mid.astype(F32)).astype(BF16)
        b = (jnp.dot(trib, hi, preferred_element_type=F32)
             + jnp.dot(trib, mid, preferred_element_type=F32)
             + jnp.dot(trib, lo, preferred_element_type=F32))
        b_s[rows, :] = b
        mb = jnp.min(b)
        minb = mb if minb is None else jnp.minimum(minb, mb)

    rid = lax.broadcasted_iota(jnp.int32, (C, 1), 0)

    def chunk(i, carry, fast):
        c = (nch - 1 - i) if reverse else i
        r0 = pl.multiple_of(c * C, C)
        rows = pl.ds(r0, C)
        for h in range(HGRN_HEADS):
            cols = slice(h * HGRN_DK, (h + 1) * HGRN_DK)
            q = q_ref[rows, cols].astype(F32)
            kk = kk_s[rows, cols]
            b = b_s[rows, cols]
            v = v_ref[rows, cols]
            bl = b[0:1, :] if reverse else b[C - 1:C, :]
            if fast:
                bm = 0.5 * bl
                qd = (q * jnp.exp(b - bm)).astype(BF16)
                kd = (kk * jnp.exp(bm - b)).astype(BF16)
                s = lax.dot_general(qd, kd, NT_DIMS, preferred_element_type=F32)
                s = jnp.where(tri, s, 0.0).astype(BF16)
                o = jnp.dot(s, v, preferred_element_type=F32)
            else:
                def sbody(g_i, o_acc):
                    grp = pl.ds(pl.multiple_of(r0 + g_i * SUBLANES, SUBLANES), SUBLANES)
                    b8 = b_s[grp, cols]
                    k8 = kk_s[grp, cols]
                    v8 = vf_s[grp, cols]
                    for jj in range(SUBLANES):
                        s_i = g_i * SUBLANES + jj
                        w = q * k8[jj:jj + 1, :] * jnp.exp(jnp.minimum(b - b8[jj:jj + 1, :], 0.0))
                        scol = jnp.sum(w, axis=-1, keepdims=True)
                        keep = (rid <= s_i) if reverse else (rid >= s_i)
                        o_acc = o_acc + jnp.where(keep, scol, 0.0) * v8[jj:jj + 1, :]
                    return o_acc
                o = lax.fori_loop(0, C // SUBLANES, sbody, jnp.zeros((C, HGRN_DK), F32))
            st = st_ref[h]
            qi = (q * jnp.exp(b)).astype(BF16)
            o = o + lax.dot_general(qi, st.astype(BF16), NT_DIMS, preferred_element_type=F32)
            ke = (kk * jnp.exp(bl - b)).astype(BF16)
            upd = lax.dot_general(v, ke, TN_DIMS, preferred_element_type=F32)
            st_ref[h] = st * jnp.exp(bl) + upd
            if epilogue:
                os_s[rows, cols] = o + of_ref[rows, cols].astype(F32)
            else:
                o_ref[rows, cols] = o.astype(o_ref.dtype)
        return carry

    safe = minb >= -HGRN_SAFE_RANGE

    @pl.when(safe)
    def _():
        lax.fori_loop(0, nch, functools.partial(chunk, fast=True), 0, unroll=2)

    @pl.when(jnp.logical_not(safe))
    def _():
        vf_s[...] = v_ref[...].astype(F32)
        lax.fori_loop(0, nch, functools.partial(chunk, fast=False), 0)

    if epilogue:
        nw = nw_ref[...]
        for h in range(HGRN_HEADS):
            cols = slice(h * HGRN_DK, (h + 1) * HGRN_DK)
            os = os_s[:, cols]
            ms = jnp.mean(os * os, axis=-1, keepdims=True)
            y = os * lax.rsqrt(ms + RMS_EPS) * nw
            g = g_ref[:, cols].astype(F32)
            o_ref[:, cols] = (y * (g * jax.nn.sigmoid(g))).astype(o_ref.dtype)


def _hgrn(proj, lb, norm_w, batch, seq, sc):
    t = batch * seq
    ns = seq // sc
    blk = (sc, D_MODEL)

    def spec(colblk, reverse):
        if reverse:
            return pl.BlockSpec(blk, lambda b, n: (b * ns + ns - 1 - n, colblk))
        return pl.BlockSpec(blk, lambda b, n: (b * ns + n, colblk))

    def row_spec(reverse):
        if reverse:
            return pl.BlockSpec(blk, lambda b, n: (b * ns + ns - 1 - n, 0))
        return pl.BlockSpec(blk, lambda b, n: (b * ns + n, 0))

    lb_spec = pl.BlockSpec((1, D_MODEL), lambda b, n: (0, 0))
    common_scratch = [pltpu.VMEM((HGRN_HEADS, HGRN_DK, HGRN_DK), F32),
                      pltpu.VMEM(blk, F32), pltpu.VMEM(blk, F32), pltpu.VMEM(blk, F32)]
    o_f = pl.pallas_call(
        functools.partial(_hgrn_kernel, reverse=False, epilogue=False, sc=sc),
        out_shape=jax.ShapeDtypeStruct((t, D_MODEL), BF16),
        grid=(batch, ns),
        in_specs=[lb_spec, spec(COL_Q, False), spec(COL_ZF, False), spec(COL_I, False)],
        out_specs=row_spec(False),
        scratch_shapes=common_scratch,
        compiler_params=_params(("arbitrary", "arbitrary")),
        name="hgrn_fwd",
    )(lb[0:1], proj, proj, proj)
    o_a = pl.pallas_call(
        functools.partial(_hgrn_kernel, reverse=True, epilogue=True, sc=sc),
        out_shape=jax.ShapeDtypeStruct((t, D_MODEL), BF16),
        grid=(batch, ns),
        in_specs=[lb_spec, spec(COL_Q, True), spec(COL_ZB, True), spec(COL_I, True),
                  row_spec(True), spec(COL_G, True),
                  pl.BlockSpec((1, HGRN_DK), lambda b, n: (0, 0))],
        out_specs=row_spec(True),
        scratch_shapes=common_scratch + [pltpu.VMEM(blk, F32)],
        compiler_params=_params(("arbitrary", "arbitrary")),
        name="hgrn_bwd",
    )(lb[1:2], proj, proj, proj, o_f, proj, norm_w.reshape(1, HGRN_DK))
    return o_a


def _mla_prep_kernel(dq_ref, dkv_ref, qnw_ref, kvnw_ref, wq_ref, wk_ref, wv_ref, cos_ref, sin_ref,
                     q_ref, k_ref, v_ref):
    scale = (MLA_NOPE + MLA_ROPE) ** -0.5 * LOG2E
    cos = cos_ref[...]
    sin = sin_ref[...]
    dq = dq_ref[...].astype(F32)
    ms = jnp.sum(dq * dq, axis=-1, keepdims=True) * (1.0 / MLA_Q_LORA)
    cq = (dq * lax.rsqrt(ms + RMS_EPS) * qnw_ref[...]).astype(BF16)
    qa = jnp.dot(cq, wq_ref[...], preferred_element_type=F32)
    for h in range(MLA_HEADS):
        base = h * QK_PAD
        q_ref[:, base:base + MLA_NOPE] = (qa[:, base:base + MLA_NOPE] * scale).astype(BF16)
        rp = qa[:, base + MLA_NOPE:base + QK_PAD]
        sw = qa[:, MLA_HEADS * QK_PAD + h * LANES:MLA_HEADS * QK_PAD + (h + 1) * LANES]
        q_ref[:, base + MLA_NOPE:base + QK_PAD] = ((rp * cos + sw * sin) * scale).astype(BF16)
    dkv = dkv_ref[...].astype(F32)
    ckv = dkv[:, :MLA_KV_LORA]
    msk = jnp.mean(ckv * ckv, axis=-1, keepdims=True)
    cn = (ckv * lax.rsqrt(msk + RMS_EPS) * kvnw_ref[...]).astype(BF16)
    kn = jnp.dot(cn, wk_ref[...], preferred_element_type=F32)
    vv = jnp.dot(cn, wv_ref[...], preferred_element_type=F32).astype(BF16)
    ones_col = jnp.where(lax.broadcasted_iota(jnp.int32, (vv.shape[0], LANES), 1) == 0, 1.0, 0.0).astype(BF16)
    for h in range(MLA_HEADS):
        v_ref[:, h * V_PAD:h * V_PAD + MLA_V] = vv[:, h * MLA_V:(h + 1) * MLA_V]
        v_ref[:, h * V_PAD + MLA_V:(h + 1) * V_PAD] = ones_col
    kr = (dkv[:, MLA_KV_LORA:MLA_KV_LORA + LANES] * cos
          + dkv[:, MLA_KV_LORA + LANES:MLA_KV_LORA + 2 * LANES] * sin).astype(BF16)
    for h in range(MLA_HEADS):
        base = h * QK_PAD
        k_ref[:, base:base + MLA_NOPE] = kn[:, h * MLA_NOPE:(h + 1) * MLA_NOPE].astype(BF16)
        k_ref[:, base + MLA_NOPE:base + QK_PAD] = kr


def _mla_prep(proj, qnw, kvnw, wq, wk, wv, cos, sin, batch, seq, tm):
    t = batch * seq
    npos = seq // tm
    full = lambda shape: pl.BlockSpec(shape, lambda i: (0, 0))
    return pl.pallas_call(
        _mla_prep_kernel,
        out_shape=(jax.ShapeDtypeStruct((t, MLA_HEADS * QK_PAD), BF16),
                   jax.ShapeDtypeStruct((t, MLA_HEADS * QK_PAD), BF16),
                   jax.ShapeDtypeStruct((t, MLA_HEADS * V_PAD), BF16)),
        grid=(t // tm,),
        in_specs=[pl.BlockSpec((tm, 512), lambda i: (i, COL_DQ)),
                  pl.BlockSpec((tm, 512), lambda i: (i, COL_DKV)),
                  full((1, 512)), full((1, MLA_KV_LORA)),
                  full(wq.shape), full(wk.shape), full(wv.shape),
                  pl.BlockSpec((tm, LANES), lambda i: (i % npos, 0)),
                  pl.BlockSpec((tm, LANES), lambda i: (i % npos, 0))],
        out_specs=(pl.BlockSpec((tm, MLA_HEADS * QK_PAD), lambda i: (i, 0)),
                   pl.BlockSpec((tm, MLA_HEADS * QK_PAD), lambda i: (i, 0)),
                   pl.BlockSpec((tm, MLA_HEADS * V_PAD), lambda i: (i, 0))),
        compiler_params=_params(("arbitrary",)),
        name="mla_prep",
    )(proj, proj, qnw, kvnw, wq, wk, wv, cos, sin)


def _flash_kernel(q_ref, k_ref, v_ref, o_ref, m_s, acc_s, s_buf, *, bk, nk):
    m_s[...] = jnp.full(m_s.shape, -jnp.inf, F32)
    acc_s[...] = jnp.zeros(acc_s.shape, F32)
    nt = bk // LANES
    bq = q_ref.shape[0]
    nsplit = 2 if bq % 256 == 0 else 1
    hq = bq // nsplit
    qrs = [slice(hf * hq, (hf + 1) * hq) for hf in range(nsplit)]

    def scores(j, slot):
        rows = pl.ds(pl.multiple_of(j * bk, bk), bk)
        for qr in qrs:
            s_buf[slot, qr, :] = lax.dot_general(q_ref[qr, :], k_ref[rows, :], NT_DIMS,
                                                 preferred_element_type=F32)

    def consume(j, slot):
        rows = pl.ds(pl.multiple_of(j * bk, bk), bk)
        ps, alphas = [], []
        for qr in qrs:
            tiles = [s_buf[slot, qr, t * LANES:(t + 1) * LANES] for t in range(nt)]
            tmax = tiles[0]
            for t in range(1, nt):
                tmax = jnp.maximum(tmax, tiles[t])
            m_prev = m_s[qr, :]
            m_new = jnp.maximum(m_prev, jnp.max(tmax, axis=-1, keepdims=True))
            alphas.append(jnp.exp2(m_prev - m_new))
            ps.append(jnp.concatenate([jnp.exp2(tl - m_new).astype(BF16) for tl in tiles], axis=1))
            m_s[qr, :] = m_new
        for qr, p, a in zip(qrs, ps, alphas):
            pv = jnp.dot(p, v_ref[rows, :], preferred_element_type=F32)
            acc_s[qr, :] = jnp.concatenate([a, a], axis=1) * acc_s[qr, :] + pv

    scores(0, 0)
    npairs = (nk - 1) // 2

    def body(i, carry):
        j = 2 * i
        scores(j + 1, 1)
        consume(j, 0)
        scores(j + 2, 0)
        consume(j + 1, 1)
        return carry

    lax.fori_loop(0, npairs, body, 0)
    if nk - 2 * npairs == 2:
        scores(nk - 1, 1)
        consume(nk - 2, 0)
        consume(nk - 1, 1)
    else:
        consume(nk - 1, 0)
    acc = acc_s[...]
    o_ref[...] = (acc[:, :MLA_V] / acc[:, MLA_V:MLA_V + 1]).astype(o_ref.dtype)


def _flash(q, k, v, batch, seq, bq, bk):
    t = batch * seq
    nq = seq // bq
    return pl.pallas_call(
        functools.partial(_flash_kernel, bk=bk, nk=seq // bk),
        out_shape=jax.ShapeDtypeStruct((t, MLA_HEADS * MLA_V), BF16),
        grid=(batch, MLA_HEADS, nq),
        in_specs=[pl.BlockSpec((bq, QK_PAD), lambda b, h, i: (b * nq + i, h)),
                  pl.BlockSpec((seq, QK_PAD), lambda b, h, i: (b, h)),
                  pl.BlockSpec((seq, V_PAD), lambda b, h, i: (b, h))],
        out_specs=pl.BlockSpec((bq, MLA_V), lambda b, h, i: (b * nq + i, h)),
        scratch_shapes=[pltpu.VMEM((bq, LANES), F32), pltpu.VMEM((bq, V_PAD), F32),
                        pltpu.VMEM((2, bq, bk), F32)],
        compiler_params=_params(("arbitrary", "arbitrary", "arbitrary")),
        name="mla_flash",
    )(q, k, v)


def _layernorm(y, w, b):
    mu = jnp.mean(y, axis=-1, keepdims=True)
    yc = y - mu
    var = jnp.mean(yc * yc, axis=-1, keepdims=True)
    return yc * lax.rsqrt(var + LN_EPS) * w + b


def _merge_kernel(x_ref, ga_ref, gb_ref, gm_ref, mq_ref, oa_ref, ob_ref, kvm_ref, wout_ref,
                  l1w_ref, l1b_ref, rw_ref, rb_ref,
                  x1_ref, ri_ref, rg_ref, cnt_ref, carry_s, *, tm):
    @pl.when(pl.program_id(0) == 0)
    def _():
        carry_s[...] = jnp.zeros_like(carry_s)

    parts = []
    for h in range(MEM_HEADS):
        cols = slice(h * MEM_HEAD_DIM, (h + 1) * MEM_HEAD_DIM)
        kh = kvm_ref[:, cols]
        vh = kvm_ref[:, D_MODEL + h * MEM_HEAD_DIM:D_MODEL + (h + 1) * MEM_HEAD_DIM]
        s = lax.dot_general(mq_ref[:, cols], kh, NT_DIMS, preferred_element_type=F32) * (MEM_HEAD_DIM ** -0.5)
        s = s - jnp.max(s, axis=-1, keepdims=True)
        p = jnp.exp(s)
        p = p / jnp.sum(p, axis=-1, keepdims=True)
        parts.append(jnp.dot(p.astype(BF16), vh, preferred_element_type=F32))
    om = jnp.concatenate(parts, axis=1)

    merged = (jax.nn.sigmoid(ga_ref[...].astype(F32)) * oa_ref[...].astype(F32)
              + jax.nn.sigmoid(gb_ref[...].astype(F32)) * ob_ref[...].astype(F32)
              + jax.nn.sigmoid(gm_ref[...].astype(F32)) * om)
    y = DN_ALPHA * x_ref[...] + jnp.dot(merged.astype(BF16), wout_ref[...], preferred_element_type=F32)
    x1 = _layernorm(y, l1w_ref[...], l1b_ref[...])
    x1_ref[...] = x1

    x_hi = x1.astype(BF16)
    x_lo = (x1 - x_hi.astype(F32)).astype(BF16)
    hi_part = jnp.dot(x_hi, rw_ref[...], preferred_element_type=F32)
    lo_part = jnp.dot(x_lo, rw_ref[:, :LANES], preferred_element_type=F32)
    logits = hi_part[:, :LANES] + hi_part[:, LANES:] + lo_part + rb_ref[...]
    lane_i = lax.broadcasted_iota(jnp.int32, (tm, LANES), 1)
    lane = lane_i.astype(F32)
    work = logits
    idx, val = [], []
    for _ in range(TOP_K):
        mx = jnp.max(work, axis=-1, keepdims=True)
        ix = jnp.min(jnp.where(work == mx, lane, float(LANES)), axis=-1, keepdims=True)
        idx.append(ix)
        val.append(mx)
        work = jnp.where(lane == ix, -jnp.inf, work)
    ex = [jnp.exp(v - val[0]) for v in val]
    tot = ex[0] + ex[1] + ex[2] + ex[3]
    hot = [jnp.where(lane == ix, 1.0, 0.0) for ix in idx]
    multi = hot[0] + hot[1] + hot[2] + hot[3]
    r = lax.broadcasted_iota(jnp.int32, (tm, tm), 0)
    c = lax.broadcasted_iota(jnp.int32, (tm, tm), 1)
    lower = jnp.where(r > c, 1.0, 0.0).astype(BF16)
    before = jnp.dot(lower, multi.astype(BF16), preferred_element_type=F32) + carry_s[0:1, :]
    ri = jnp.zeros((tm, LANES), F32)
    rg = jnp.zeros((tm, LANES), F32)
    for j in range(TOP_K):
        rank = jnp.sum(before * hot[j], axis=-1, keepdims=True)
        ri = ri + jnp.where(lane == float(j), idx[j], 0.0) + jnp.where(lane == float(TOP_K + j), rank, 0.0)
        rg = rg + jnp.where(lane == float(j), ex[j] / tot, 0.0)
    ri_ref[...] = ri.astype(jnp.int32)
    rg_ref[...] = rg
    carry_s[...] = carry_s[...] + jnp.sum(multi, axis=0, keepdims=True)
    cnt_ref[...] = carry_s[...]


def _merge(x2d, proj, o_a, o_b, kvm, wout, l1w, l1b, rw, rb, batch, seq, tm):
    t = batch * seq
    per_b = seq // tm
    nmem = kvm.shape[0] // batch
    tile = lambda colblk: pl.BlockSpec((tm, D_MODEL), lambda i: (i, colblk))
    full = lambda shape: pl.BlockSpec(shape, lambda i: (0, 0))
    return pl.pallas_call(
        functools.partial(_merge_kernel, tm=tm),
        out_shape=(jax.ShapeDtypeStruct((t, D_MODEL), F32),
                   jax.ShapeDtypeStruct((t, LANES), jnp.int32),
                   jax.ShapeDtypeStruct((t, LANES), F32),
                   jax.ShapeDtypeStruct((8, LANES), F32)),
        grid=(t // tm,),
        in_specs=[tile(0), tile(COL_GA), tile(COL_GB), tile(COL_GM), tile(COL_MQ), tile(0), tile(0),
                  pl.BlockSpec((nmem, 2 * D_MODEL), lambda i: (i // per_b, 0)),
                  full((D_MODEL, D_MODEL)), full((1, D_MODEL)), full((1, D_MODEL)),
                  full((D_MODEL, 2 * LANES)), full((1, LANES))],
        out_specs=(tile(0), pl.BlockSpec((tm, LANES), lambda i: (i, 0)),
                   pl.BlockSpec((tm, LANES), lambda i: (i, 0)), full((8, LANES))),
        scratch_shapes=[pltpu.VMEM((8, LANES), F32)],
        compiler_params=_params(("arbitrary",)),
        name="merge_router",
    )(x2d, proj, proj, proj, proj, o_a, o_b, kvm, wout, l1w, l1b, rw, rb)


def _sc_gather(table, idx):
    p, d = idx.shape[0], table.shape[1]
    per_w = p // SC_WORKERS
    assert per_w * SC_WORKERS == p and per_w % SC_GATHER_ROWS == 0
    mesh = plsc.VectorSubcoreMesh(core_axis_name="c", subcore_axis_name="s")

    @functools.partial(
        pl.kernel, mesh=mesh,
        out_type=jax.ShapeDtypeStruct((p, d), table.dtype),
        scratch_types=[pltpu.VMEM((SC_GATHER_ROWS,), jnp.int32),
                       pltpu.VMEM((SC_GATHER_ROWS, d), table.dtype),
                       pltpu.SemaphoreType.DMA],
        name="moe_gather_sc",
    )
    def gather_kernel(table_hbm, idx_hbm, out_hbm, idx_v, rows_v, sem):
        wid = lax.axis_index("s") * SC_CORES + lax.axis_index("c")
        base = wid * per_w

        @pl.loop(0, per_w // SC_GATHER_ROWS)
        def _(c):
            off = base + c * SC_GATHER_ROWS
            pltpu.sync_copy(idx_hbm.at[pl.ds(off, SC_GATHER_ROWS)], idx_v)
            pltpu.async_copy(table_hbm.at[idx_v], rows_v, sem).wait()
            pltpu.sync_copy(rows_v, out_hbm.at[pl.ds(off, SC_GATHER_ROWS)])

    return gather_kernel(table, idx)


def _sc_dispatch(x, dest_t, slots):
    t, d = x.shape
    per_w = t // SC_WORKERS
    assert per_w * SC_WORKERS == t and per_w % SC_GATHER_ROWS == 0
    mesh = plsc.VectorSubcoreMesh(core_axis_name="c", subcore_axis_name="s")

    @functools.partial(
        pl.kernel, mesh=mesh,
        out_type=jax.ShapeDtypeStruct((slots, d), x.dtype),
        scratch_types=[pltpu.VMEM((SC_GATHER_ROWS,), jnp.int32),
                       pltpu.VMEM((SC_GATHER_ROWS, d), x.dtype)],
        name="moe_dispatch_sc",
    )
    def scatter_kernel(x_hbm, idx_hbm, out_hbm, idx_v, rows_v):
        wid = lax.axis_index("s") * SC_CORES + lax.axis_index("c")
        base = wid * per_w

        @pl.loop(0, per_w // SC_GATHER_ROWS)
        def _(c):
            off = base + c * SC_GATHER_ROWS
            pltpu.sync_copy(x_hbm.at[pl.ds(off, SC_GATHER_ROWS)], rows_v)
            for j in range(TOP_K):
                pltpu.sync_copy(idx_hbm.at[pl.ds(j * t + off, SC_GATHER_ROWS)], idx_v)
                pltpu.sync_copy(rows_v, out_hbm.at[idx_v])

    return scatter_kernel(x, dest_t)


HALF_D = D_MODEL // 2
HI16 = -65536


def _pack_bf16_pair(a, b):
    wa = lax.bitcast_convert_type(a.astype(BF16).astype(F32), jnp.int32)
    wb = lax.bitcast_convert_type(b.astype(BF16).astype(F32), jnp.int32)
    return wa | lax.shift_right_logical(wb, 16)


def _unpack_bf16_pair(w):
    a = lax.bitcast_convert_type(w & HI16, F32)
    b = lax.bitcast_convert_type(lax.shift_left(w, 16), F32)
    return a, b


def _expert_kernel(be_ref, nu_ref, xs_ref, wgu_ref, bgu_ref, wd_ref, bd_ref, ys_ref, wgu_b, wd_b):
    i = pl.program_id(0)

    @pl.when(jnp.logical_or(i == 0, be_ref[i] != be_ref[jnp.maximum(i - 1, 0)]))
    def _():
        wgu_b[...] = wgu_ref[0].astype(BF16)
        wd_b[...] = wd_ref[0].astype(BF16)

    @pl.when(i < nu_ref[0])
    def _():
        h = jnp.dot(xs_ref[...].astype(BF16), wgu_b[...], preferred_element_type=F32) + bgu_ref[0]
        g = jnp.minimum(h[:, :D_FF], SWIGLU_LIMIT)
        u = jnp.clip(h[:, D_FF:], -SWIGLU_LIMIT, SWIGLU_LIMIT)
        act = (u + 1.0) * g * jax.nn.sigmoid(SWIGLU_ALPHA * g)
        y = jnp.dot(act.astype(BF16), wd_b[...], preferred_element_type=F32) + bd_ref[0]
        ys_ref[...] = _pack_bf16_pair(y[:, :HALF_D], y[:, HALF_D:])

    @pl.when(i >= nu_ref[0])
    def _():
        ys_ref[...] = jnp.zeros_like(ys_ref)


def _experts(block_e, n_used, xs, wgu, bgu, wd, bd):
    slots = xs.shape[0]
    nb = slots // EXPERT_BLOCK
    grid_spec = pltpu.PrefetchScalarGridSpec(
        num_scalar_prefetch=2,
        grid=(nb,),
        in_specs=[pl.BlockSpec((EXPERT_BLOCK, D_MODEL), lambda i, be, nu: (i, 0)),
                  pl.BlockSpec((1, D_MODEL, 2 * D_FF), lambda i, be, nu: (be[i], 0, 0)),
                  pl.BlockSpec((1, 1, 2 * D_FF), lambda i, be, nu: (be[i], 0, 0)),
                  pl.BlockSpec((1, D_FF, D_MODEL), lambda i, be, nu: (be[i], 0, 0)),
                  pl.BlockSpec((1, 1, D_MODEL), lambda i, be, nu: (be[i], 0, 0))],
        out_specs=pl.BlockSpec((EXPERT_BLOCK, HALF_D), lambda i, be, nu: (i, 0)),
        scratch_shapes=[pltpu.VMEM((D_MODEL, 2 * D_FF), BF16), pltpu.VMEM((D_FF, D_MODEL), BF16)],
    )
    return pl.pallas_call(
        _expert_kernel,
        out_shape=jax.ShapeDtypeStruct((slots, HALF_D), jnp.int32),
        grid_spec=grid_spec,
        compiler_params=_params(("arbitrary",)),
        name="moe_experts",
    )(block_e, n_used, xs, wgu, bgu, wd, bd)


def _weighted_rows(rg, packed):
    lo = hi = None
    for j, w in enumerate(packed):
        a, b = _unpack_bf16_pair(w)
        gj = rg[:, j:j + 1]
        lo = gj * a if lo is None else lo + gj * a
        hi = gj * b if hi is None else hi + gj * b
    return jnp.concatenate([lo, hi], axis=1)


def _combine_kernel(dest_hbm, x1_ref, rg_ref, ys_hbm, ys_tiles_hbm, l2w_ref, l2b_ref, o_ref, idx_s, buf,
                    sem_i, sem_r, *, tm, first_tile):
    i = pl.program_id(0)
    nsteps = pl.num_programs(0)
    n = TOP_K * tm

    def issue(step, slot):
        cp = pltpu.make_async_copy(dest_hbm.at[pl.ds(pl.multiple_of((first_tile + step) * n, n), n)], idx_s, sem_i)
        cp.start()
        cp.wait()

        def body(g, carry):
            for jj in range(SUBLANES):
                for j in range(TOP_K):
                    d = idx_s[g * (SUBLANES * TOP_K) + jj * TOP_K + j]
                    pltpu.make_async_copy(ys_hbm.at[pl.ds(d, 1)], buf.at[slot, j, g, pl.ds(jj, 1)],
                                          sem_r.at[slot]).start(priority=j % 2)
            return carry

        lax.fori_loop(0, tm // SUBLANES, body, 0)

    @pl.when(i == 0)
    def _():
        issue(0, 0)

    @pl.when(i + 1 < nsteps)
    def _():
        issue(i + 1, (i + 1) % 2)

    slot = i % 2
    for j in range(TOP_K):
        pltpu.make_async_copy(ys_tiles_hbm.at[pl.ds(0, tm // SUBLANES)], buf.at[slot, j], sem_r.at[slot]).wait()
    moe = _weighted_rows(rg_ref[...], [buf[slot, j].reshape(tm, HALF_D) for j in range(TOP_K)])
    o_ref[...] = _layernorm(DN_ALPHA * x1_ref[...] + moe, l2w_ref[...], l2b_ref[...])


def _combine_rows_kernel(x1_ref, rg_ref, g0_ref, g1_ref, g2_ref, g3_ref, l2w_ref, l2b_ref, *rest):
    o_ref = rest[-1]
    moe = _weighted_rows(rg_ref[...], [g0_ref[...], g1_ref[...], g2_ref[...], g3_ref[...]])
    o_ref[...] = _layernorm(DN_ALPHA * x1_ref[...] + moe, l2w_ref[...], l2b_ref[...])


def _combine_rows(x1, rg, rows, l2w, l2b, tm, y_rest):
    t = x1.shape[0]
    nt = rows.shape[0] // TOP_K // tm
    full = lambda shape: pl.BlockSpec(shape, lambda i: (0, 0))
    choice = lambda j: pl.BlockSpec((tm, HALF_D), lambda i, j=j: (j * nt + i, 0))
    in_specs = [pl.BlockSpec((tm, D_MODEL), lambda i: (i, 0)),
                pl.BlockSpec((tm, LANES), lambda i: (i, 0)),
                choice(0), choice(1), choice(2), choice(3),
                full((1, D_MODEL)), full((1, D_MODEL))]
    args = [x1, rg, rows, rows, rows, rows, l2w, l2b]
    aliases = {}
    if y_rest is not None:
        in_specs.append(pl.BlockSpec(memory_space=pl.ANY))
        args.append(y_rest)
        aliases = {len(args) - 1: 0}
    return pl.pallas_call(
        _combine_rows_kernel,
        out_shape=jax.ShapeDtypeStruct((t, D_MODEL), F32),
        grid=(nt,),
        in_specs=in_specs,
        out_specs=pl.BlockSpec((tm, D_MODEL), lambda i: (i, 0)),
        input_output_aliases=aliases,
        compiler_params=_params(("arbitrary",)),
        name="moe_combine_rows",
    )(*args)


def _combine(dest, x1, rg, ys, l2w, l2b, tm, tok0):
    t = x1.shape[0]
    first = tok0 // tm
    full = lambda shape: pl.BlockSpec(shape, lambda i: (0, 0))
    return pl.pallas_call(
        functools.partial(_combine_kernel, tm=tm, first_tile=first),
        out_shape=jax.ShapeDtypeStruct((t, D_MODEL), F32),
        grid=((t - tok0) // tm,),
        in_specs=[pl.BlockSpec(memory_space=pl.ANY),
                  pl.BlockSpec((tm, D_MODEL), lambda i: (first + i, 0)),
                  pl.BlockSpec((tm, LANES), lambda i: (first + i, 0)),
                  pl.BlockSpec(memory_space=pl.ANY), pl.BlockSpec(memory_space=pl.ANY),
                  full((1, D_MODEL)), full((1, D_MODEL))],
        out_specs=pl.BlockSpec((tm, D_MODEL), lambda i: (first + i, 0)),
        scratch_shapes=[pltpu.SMEM((TOP_K * tm,), jnp.int32),
                        pltpu.VMEM((2, TOP_K, tm // SUBLANES, SUBLANES, HALF_D), jnp.int32),
                        pltpu.SemaphoreType.DMA, pltpu.SemaphoreType.DMA((2,))],
        compiler_params=_params(("arbitrary",)),
        name="moe_combine",
    )(dest, x1, rg, ys, ys.reshape(ys.shape[0] // SUBLANES, SUBLANES, HALF_D), l2w, l2b)


def _prep_w_in(w):
    zeros = lambda n: jnp.zeros((D_MODEL, n), w.dtype)
    kr = w[:, 5760:5824]
    kr_sw = jnp.concatenate([kr[:, MLA_ROPE // 2:], kr[:, :MLA_ROPE // 2]], axis=1)
    parts = [w[:, 6848:9920], w[:, 0:5120], w[:, 5824:6848],
             w[:, 5120:5504], zeros(128),
             w[:, 5504:5760], kr, zeros(64), kr_sw, zeros(64)]
    return jnp.concatenate(parts, axis=1).astype(BF16)


def _prep_w_uq(w):
    w3 = w.reshape(MLA_Q_LORA, MLA_HEADS, MLA_NOPE + MLA_ROPE)
    rope = w3[:, :, MLA_NOPE:]
    rope_sw = jnp.concatenate([rope[:, :, MLA_ROPE // 2:], rope[:, :, :MLA_ROPE // 2]], axis=-1)
    pad = jnp.zeros((MLA_Q_LORA, MLA_HEADS, QK_PAD - MLA_NOPE - MLA_ROPE), w.dtype)
    main = jnp.concatenate([w3, pad], axis=-1).reshape(MLA_Q_LORA, MLA_HEADS * QK_PAD)
    swp = jnp.concatenate([rope_sw, pad], axis=-1).reshape(MLA_Q_LORA, MLA_HEADS * LANES)
    both = jnp.concatenate([main, swp], axis=1)
    return jnp.pad(both, ((0, 512 - MLA_Q_LORA), (0, 0))).astype(BF16)


def _rope_tables(seq):
    inv_freq = ROPE_THETA ** (-jnp.arange(0, MLA_ROPE, 2, dtype=F32) / MLA_ROPE)
    ang = jnp.arange(seq, dtype=F32)[:, None] * inv_freq[None, :]
    cos, sin = jnp.cos(ang), jnp.sin(ang)
    pad = jnp.zeros((seq, LANES - MLA_ROPE), F32)
    return (jnp.concatenate([cos, cos, pad], axis=1), jnp.concatenate([-sin, sin, pad], axis=1))


def _split_bf16(w):
    hi = w.astype(BF16)
    lo = (w - hi.astype(F32)).astype(BF16)
    return jnp.concatenate([hi, lo], axis=1)


def _tile(n, pref):
    return pref if n % pref == 0 else n


def _layer(x, mem, wts, sc_share):
    batch, seq, _ = x.shape
    t = batch * seq
    x2d = x.reshape(t, D_MODEL)

    proj = _matmul(x2d, wts["w_in"], BF16, _tile(t, 1024), 2048, "in_proj")
    o_a = _hgrn(proj, wts["lb"], wts["hgrn_norm_w"], batch, seq, _tile(seq, 512))

    cos, sin = _rope_tables(seq)
    q, k, v = _mla_prep(proj, wts["qnw"], wts["kvnw"], wts["w_uq"], wts["w_uk"], wts["w_uv"],
                        cos, sin, batch, seq, _tile(seq, 512))
    o_b = _flash(q, k, v, batch, seq, _tile(seq, 1024), _tile(seq, 1024))

    nmem = mem.shape[1]
    kvm = _matmul(mem.reshape(batch * nmem, D_MODEL), wts["mem_w_kv"], BF16,
                  _tile(batch * nmem, 512), 1024, "mem_kv")

    tm = _tile(seq, 512)
    x1, ri, rg, cnt = _merge(x2d, proj, o_a, o_b, kvm, wts["w_out"], wts["ln1_w"], wts["ln1_b"],
                             wts["router_w"], wts["router_b"], batch, seq, tm)

    idx = ri[:, :TOP_K]
    rank = ri[:, TOP_K:2 * TOP_K]
    counts = cnt[0, :N_EXPERTS].astype(jnp.int32)
    padded = (counts + EXPERT_BLOCK - 1) // EXPERT_BLOCK * EXPERT_BLOCK
    pad_end = jnp.cumsum(padded)
    pad_start = pad_end - padded
    dest = (pad_start[idx] + rank).reshape(t * TOP_K).astype(jnp.int32)
    nb = t * TOP_K // EXPERT_BLOCK + N_EXPERTS
    blk_start = jnp.arange(nb, dtype=jnp.int32) * EXPERT_BLOCK
    block_e = jnp.minimum(jnp.sum((pad_end[None, :] <= blk_start[:, None]).astype(jnp.int32), axis=1),
                          N_EXPERTS - 1).astype(jnp.int32)
    n_used = (pad_end[-1:] // EXPERT_BLOCK).astype(jnp.int32)
    dest_t = dest.reshape(t, TOP_K).T.reshape(TOP_K * t)
    xs = _sc_dispatch(x1, dest_t, nb * EXPERT_BLOCK)
    ys = _experts(block_e, n_used, xs, wts["exp_w_gu"], wts["exp_b_gu"], wts["exp_w_down"],
                        wts["exp_b_down"])
    n_sc = int(t * sc_share) // 2048 * 2048
    y_rest = None
    if n_sc < t:
        y_rest = _combine(dest, x1, rg, ys, wts["ln2_w"], wts["ln2_b"], _tile(t, 256), n_sc)
    if n_sc > 0:
        rows = _sc_gather(ys, dest.reshape(t, TOP_K)[:n_sc].T.reshape(TOP_K * n_sc))
        y = _combine_rows(x1, rg, rows, wts["ln2_w"], wts["ln2_b"], 512, y_rest)
    else:
        y = y_rest
    return y.reshape(batch, seq, D_MODEL)


def kernel(x_prompt, x_sample, mem_prompt, mem_sample, w_in, hgrn_lb_logits, hgrn_norm_w,
           mla_q_norm_w, mla_w_uq, mla_kv_norm_w, mla_w_uk, mla_w_uv, mem_w_kv, w_out,
           ln1_w, ln1_b, router_w, router_b, exp_w_gu, exp_b_gu, exp_w_down, exp_b_down,
           ln2_w, ln2_b):
    depth = w_in.shape[0]
    gamma = jax.nn.softmax(hgrn_lb_logits.astype(F32), axis=1)
    cum = jnp.cumsum(gamma, axis=1)
    lb_all = cum[:, 1:] - cum[:, :1]
    y_prompt, y_sample = x_prompt, x_sample
    for l in range(depth):
        row = lambda a: a[l].reshape(1, -1).astype(F32)
        wts = {
            "w_in": _prep_w_in(w_in[l]),
            "lb": lb_all[:, l],
            "hgrn_norm_w": hgrn_norm_w[l].astype(F32),
            "qnw": jnp.pad(row(mla_q_norm_w), ((0, 0), (0, 512 - MLA_Q_LORA))),
            "kvnw": row(mla_kv_norm_w),
            "w_uq": _prep_w_uq(mla_w_uq[l]),
            "w_uk": mla_w_uk[l].astype(BF16),
            "w_uv": mla_w_uv[l].astype(BF16),
            "mem_w_kv": mem_w_kv[l].astype(BF16),
            "w_out": w_out[l].astype(BF16),
            "ln1_w": row(ln1_w), "ln1_b": row(ln1_b),
            "router_w": _split_bf16(jnp.pad(router_w[l].astype(F32), ((0, 0), (0, LANES - N_EXPERTS)))),
            "router_b": jnp.pad(row(router_b), ((0, 0), (0, LANES - N_EXPERTS)), constant_values=-jnp.inf),
            "exp_w_gu": exp_w_gu[l].astype(F32),
            "exp_b_gu": exp_b_gu[l].reshape(N_EXPERTS, 1, 2 * D_FF).astype(F32),
            "exp_w_down": exp_w_down[l].astype(F32),
            "exp_b_down": exp_b_down[l].reshape(N_EXPERTS, 1, D_MODEL).astype(F32),
            "ln2_w": row(ln2_w), "ln2_b": row(ln2_b),
        }
        y_prompt = _layer(y_prompt, mem_prompt, wts, sc_share=1.0)
        y_sample = _layer(y_sample, mem_sample, wts, sc_share=0.875)
    return (y_prompt, y_sample)
```

```python
import functools

import numpy as np
import jax
import jax.numpy as jnp
from jax import lax
from jax.experimental import pallas as pl
from jax.experimental.pallas import tpu as pltpu
from jax.experimental.pallas import tpu_sc as plsc

F32 = jnp.float32
BF16 = jnp.bfloat16

D_MODEL = 1024
HGRN_HEADS = 8
HGRN_DK = 128
MLA_HEADS = 8
MLA_Q_LORA = 384
MLA_KV_LORA = 256
MLA_NOPE = 128
MLA_ROPE = 64
MLA_V = 128
ROPE_THETA = 10000.0
MEM_HEADS = 4
MEM_HEAD_DIM = D_MODEL // MEM_HEADS
N_EXPERTS = 32
TOP_K = 4
D_FF = D_MODEL
SWIGLU_LIMIT = 7.0
SWIGLU_ALPHA = 1.702
DN_ALPHA = 2.0 ** 0.25
LN_EPS = 1e-5
RMS_EPS = 1e-6

LANES = 128
SUBLANES = 8
QK_PAD = 256
V_PAD = 256
LOG2E = 1.4426950408889634
HGRN_CHUNK = 64
HGRN_SAFE_RANGE = 160.0
EXPERT_BLOCK = 512
SC_CORES = 2
SC_SUBCORES = 16
SC_WORKERS = SC_CORES * SC_SUBCORES
SC_GATHER_ROWS = 64
VMEM_LIMIT = 56 * 1024 * 1024

COL_GA, COL_GB, COL_GM, COL_Q, COL_ZF, COL_ZB, COL_I, COL_G, COL_MQ = range(9)
COL_DQ, COL_DKV = 18, 19
IN_COLS_PAD = 10240

NT_DIMS = (((1,), (1,)), ((), ()))
TN_DIMS = (((0,), (0,)), ((), ()))


def _params(sem, vmem=VMEM_LIMIT):
    return pltpu.CompilerParams(dimension_semantics=sem, vmem_limit_bytes=vmem)


def _mm_kernel(x_ref, w_ref, o_ref, xb_ref):
    @pl.when(pl.program_id(1) == 0)
    def _():
        xb_ref[...] = x_ref[...].astype(BF16)

    o_ref[...] = jnp.dot(xb_ref[...], w_ref[...], preferred_element_type=F32).astype(o_ref.dtype)


def _matmul(x, w, out_dtype, tm, tn, name):
    m, k = x.shape
    n = w.shape[1]
    return pl.pallas_call(
        _mm_kernel,
        out_shape=jax.ShapeDtypeStruct((m, n), out_dtype),
        grid=(m // tm, n // tn),
        in_specs=[pl.BlockSpec((tm, k), lambda i, j: (i, 0)),
                  pl.BlockSpec((k, tn), lambda i, j: (0, j))],
        out_specs=pl.BlockSpec((tm, tn), lambda i, j: (i, j)),
        scratch_shapes=[pltpu.VMEM((tm, k), BF16)],
        compiler_params=_params(("arbitrary", "arbitrary")),
        name=name,
    )(x, w)


def _hgrn_kernel(*refs, reverse, epilogue, sc):
    if epilogue:
        lb_ref, q_ref, z_ref, v_ref, of_ref, g_ref, nw_ref, o_ref, st_ref, kk_s, b_s, vf_s, os_s = refs
    else:
        lb_ref, q_ref, z_ref, v_ref, o_ref, st_ref, kk_s, b_s, vf_s = refs
    C = HGRN_CHUNK
    nch = sc // C

    @pl.when(pl.program_id(1) == 0)
    def _():
        st_ref[...] = jnp.zeros_like(st_ref)

    lb = lb_ref[...]
    row = lax.broadcasted_iota(jnp.int32, (C, C), 0)
    col = lax.broadcasted_iota(jnp.int32, (C, C), 1)
    tri = (row <= col) if reverse else (row >= col)
    trib = jnp.where(tri, 1.0, 0.0).astype(BF16)

    minb = None
    for c in range(nch):
        rows = slice(c * C, (c + 1) * C)
        z = z_ref[rows, :].astype(F32)
        gate = (1.0 - lb) * jax.nn.sigmoid(z)
        lf = jnp.log(lb + gate)
        kk_s[rows, :] = (1.0 - lb) - gate
        hi = lf.astype(BF16)
        r1 = lf - hi.astype(F32)
        mid = r1.astype(BF16)
        lo = (r1 - mid.astype(F32)).astype(BF16)
        b = (jnp.dot(trib, hi, preferred_element_type=F32)
             + jnp.dot(trib, mid, preferred_element_type=F32)
             + jnp.dot(trib, lo, preferred_element_type=F32))
        b_s[rows, :] = b
        mb = jnp.min(b)
        minb = mb if minb is None else jnp.minimum(minb, mb)

    rid = lax.broadcasted_iota(jnp.int32, (C, 1), 0)

    def chunk(i, carry, fast):
        c = (nch - 1 - i) if reverse else i
        r0 = pl.multiple_of(c * C, C)
        rows = pl.ds(r0, C)
        for h in range(HGRN_HEADS):
            cols = slice(h * HGRN_DK, (h + 1) * HGRN_DK)
            q = q_ref[rows, cols].astype(F32)
            kk = kk_s[rows, cols]
            b = b_s[rows, cols]
            v = v_ref[rows, cols]
            bl = b[0:1, :] if reverse else b[C - 1:C, :]
            if fast:
                bm = 0.5 * bl
                qd = (q * jnp.exp(b - bm)).astype(BF16)
                kd = (kk * jnp.exp(bm - b)).astype(BF16)
                s = lax.dot_general(qd, kd, NT_DIMS, preferred_element_type=F32)
                s = jnp.where(tri, s, 0.0).astype(BF16)
                o = jnp.dot(s, v, preferred_element_type=F32)
            else:
                def sbody(g_i, o_acc):
                    grp = pl.ds(pl.multiple_of(r0 + g_i * SUBLANES, SUBLANES), SUBLANES)
                    b8 = b_s[grp, cols]
                    k8 = kk_s[grp, cols]
                    v8 = vf_s[grp, cols]
                    for jj in range(SUBLANES):
                        s_i = g_i * SUBLANES + jj
                        w = q * k8[jj:jj + 1, :] * jnp.exp(jnp.minimum(b - b8[jj:jj + 1, :], 0.0))
                        scol = jnp.sum(w, axis=-1, keepdims=True)
                        keep = (rid <= s_i) if reverse else (rid >= s_i)
                        o_acc = o_acc + jnp.where(keep, scol, 0.0) * v8[jj:jj + 1, :]
                    return o_acc
                o = lax.fori_loop(0, C // SUBLANES, sbody, jnp.zeros((C, HGRN_DK), F32))
            st = st_ref[h]
            qi = (q * jnp.exp(b)).astype(BF16)
            o = o + lax.dot_general(qi, st.astype(BF16), NT_DIMS, preferred_element_type=F32)
            ke = (kk * jnp.exp(bl - b)).astype(BF16)
            upd = lax.dot_general(v, ke, TN_DIMS, preferred_element_type=F32)
            st_ref[h] = st * jnp.exp(bl) + upd
            if epilogue:
                os_s[rows, cols] = o + of_ref[rows, cols].astype(F32)
            else:
                o_ref[rows, cols] = o.astype(o_ref.dtype)
        return carry

    safe = minb >= -HGRN_SAFE_RANGE

    @pl.when(safe)
    def _():
        lax.fori_loop(0, nch, functools.partial(chunk, fast=True), 0, unroll=2)

    @pl.when(jnp.logical_not(safe))
    def _():
        vf_s[...] = v_ref[...].astype(F32)
        lax.fori_loop(0, nch, functools.partial(chunk, fast=False), 0)

    if epilogue:
        nw = nw_ref[...]
        for h in range(HGRN_HEADS):
            cols = slice(h * HGRN_DK, (h + 1) * HGRN_DK)
            os = os_s[:, cols]
            ms = jnp.mean(os * os, axis=-1, keepdims=True)
            y = os * lax.rsqrt(ms + RMS_EPS) * nw
            g = g_ref[:, cols].astype(F32)
            o_ref[:, cols] = (y * (g * jax.nn.sigmoid(g))).astype(o_ref.dtype)


def _hgrn(proj, lb, norm_w, batch, seq, sc):
    t = batch * seq
    ns = seq // sc
    blk = (sc, D_MODEL)

    def spec(colblk, reverse):
        if reverse:
            return pl.BlockSpec(blk, lambda b, n: (b * ns + ns - 1 - n, colblk))
        return pl.BlockSpec(blk, lambda b, n: (b * ns + n, colblk))

    def row_spec(reverse):
        if reverse:
            return pl.BlockSpec(blk, lambda b, n: (b * ns + ns - 1 - n, 0))
        return pl.BlockSpec(blk, lambda b, n: (b * ns + n, 0))

    lb_spec = pl.BlockSpec((1, D_MODEL), lambda b, n: (0, 0))
    common_scratch = [pltpu.VMEM((HGRN_HEADS, HGRN_DK, HGRN_DK), F32),
                      pltpu.VMEM(blk, F32), pltpu.VMEM(blk, F32), pltpu.VMEM(blk, F32)]
    o_f = pl.pallas_call(
        functools.partial(_hgrn_kernel, reverse=False, epilogue=False, sc=sc),
        out_shape=jax.ShapeDtypeStruct((t, D_MODEL), BF16),
        grid=(batch, ns),
        in_specs=[lb_spec, spec(COL_Q, False), spec(COL_ZF, False), spec(COL_I, False)],
        out_specs=row_spec(False),
        scratch_shapes=common_scratch,
        compiler_params=_params(("arbitrary", "arbitrary")),
        name="hgrn_fwd",
    )(lb[0:1], proj, proj, proj)
    o_a = pl.pallas_call(
        functools.partial(_hgrn_kernel, reverse=True, epilogue=True, sc=sc),
        out_shape=jax.ShapeDtypeStruct((t, D_MODEL), BF16),
        grid=(batch, ns),
        in_specs=[lb_spec, spec(COL_Q, True), spec(COL_ZB, True), spec(COL_I, True),
                  row_spec(True), spec(COL_G, True),
                  pl.BlockSpec((1, HGRN_DK), lambda b, n: (0, 0))],
        out_specs=row_spec(True),
        scratch_shapes=common_scratch + [pltpu.VMEM(blk, F32)],
        compiler_params=_params(("arbitrary", "arbitrary")),
        name="hgrn_bwd",
    )(lb[1:2], proj, proj, proj, o_f, proj, norm_w.reshape(1, HGRN_DK))
    return o_a


def _mla_prep_kernel(dq_ref, dkv_ref, qnw_ref, kvnw_ref, wq_ref, wk_ref, wv_ref, cos_ref, sin_ref,
                     q_ref, k_ref, v_ref):
    scale = (MLA_NOPE + MLA_ROPE) ** -0.5 * LOG2E
    cos = cos_ref[...]
    sin = sin_ref[...]
    dq = dq_ref[...].astype(F32)
    ms = jnp.sum(dq * dq, axis=-1, keepdims=True) * (1.0 / MLA_Q_LORA)
    cq = (dq * lax.rsqrt(ms + RMS_EPS) * qnw_ref[...]).astype(BF16)
    qa = jnp.dot(cq, wq_ref[...], preferred_element_type=F32)
    for h in range(MLA_HEADS):
        base = h * QK_PAD
        q_ref[:, base:base + MLA_NOPE] = (qa[:, base:base + MLA_NOPE] * scale).astype(BF16)
        rp = qa[:, base + MLA_NOPE:base + QK_PAD]
        sw = qa[:, MLA_HEADS * QK_PAD + h * LANES:MLA_HEADS * QK_PAD + (h + 1) * LANES]
        q_ref[:, base + MLA_NOPE:base + QK_PAD] = ((rp * cos + sw * sin) * scale).astype(BF16)
    dkv = dkv_ref[...].astype(F32)
    ckv = dkv[:, :MLA_KV_LORA]
    msk = jnp.mean(ckv * ckv, axis=-1, keepdims=True)
    cn = (ckv * lax.rsqrt(msk + RMS_EPS) * kvnw_ref[...]).astype(BF16)
    kn = jnp.dot(cn, wk_ref[...], preferred_element_type=F32)
    vv = jnp.dot(cn, wv_ref[...], preferred_element_type=F32).astype(BF16)
    ones_col = jnp.where(lax.broadcasted_iota(jnp.int32, (vv.shape[0], LANES), 1) == 0, 1.0, 0.0).astype(BF16)
    for h in range(MLA_HEADS):
        v_ref[:, h * V_PAD:h * V_PAD + MLA_V] = vv[:, h * MLA_V:(h + 1) * MLA_V]
        v_ref[:, h * V_PAD + MLA_V:(h + 1) * V_PAD] = ones_col
    kr = (dkv[:, MLA_KV_LORA:MLA_KV_LORA + LANES] * cos
          + dkv[:, MLA_KV_LORA + LANES:MLA_KV_LORA + 2 * LANES] * sin).astype(BF16)
    for h in range(MLA_HEADS):
        base = h * QK_PAD
        k_ref[:, base:base + MLA_NOPE] = kn[:, h * MLA_NOPE:(h + 1) * MLA_NOPE].astype(BF16)
        k_ref[:, base + MLA_NOPE:base + QK_PAD] = kr


def _mla_prep(proj, qnw, kvnw, wq, wk, wv, cos, sin, batch, seq, tm):
    t = batch * seq
    npos = seq // tm
    full = lambda shape: pl.BlockSpec(shape, lambda i: (0, 0))
    return pl.pallas_call(
        _mla_prep_kernel,
        out_shape=(jax.ShapeDtypeStruct((t, MLA_HEADS * QK_PAD), BF16),
                   jax.ShapeDtypeStruct((t, MLA_HEADS * QK_PAD), BF16),
                   jax.ShapeDtypeStruct((t, MLA_HEADS * V_PAD), BF16)),
        grid=(t // tm,),
        in_specs=[pl.BlockSpec((tm, 512), lambda i: (i, COL_DQ)),
                  pl.BlockSpec((tm, 512), lambda i: (i, COL_DKV)),
                  full((1, 512)), full((1, MLA_KV_LORA)),
                  full(wq.shape), full(wk.shape), full(wv.shape),
                  pl.BlockSpec((tm, LANES), lambda i: (i % npos, 0)),
                  pl.BlockSpec((tm, LANES), lambda i: (i % npos, 0))],
        out_specs=(pl.BlockSpec((tm, MLA_HEADS * QK_PAD), lambda i: (i, 0)),
                   pl.BlockSpec((tm, MLA_HEADS * QK_PAD), lambda i: (i, 0)),
                   pl.BlockSpec((tm, MLA_HEADS * V_PAD), lambda i: (i, 0))),
        compiler_params=_params(("arbitrary",)),
        name="mla_prep",
    )(proj, proj, qnw, kvnw, wq, wk, wv, cos, sin)


def _flash_kernel(q_ref, k_ref, v_ref, o_ref, m_s, acc_s, s_buf, *, bk, nk):
    m_s[...] = jnp.full(m_s.shape, -jnp.inf, F32)
    acc_s[...] = jnp.zeros(acc_s.shape, F32)
    nt = bk // LANES
    bq = q_ref.shape[0]
    nsplit = 2 if bq % 256 == 0 else 1
    hq = bq // nsplit
    qrs = [slice(hf * hq, (hf + 1) * hq) for hf in range(nsplit)]

    def scores(j, slot):
        rows = pl.ds(pl.multiple_of(j * bk, bk), bk)
        for qr in qrs:
            s_buf[slot, qr, :] = lax.dot_general(q_ref[qr, :], k_ref[rows, :], NT_DIMS,
                                                 preferred_element_type=F32)

    def consume(j, slot):
        rows = pl.ds(pl.multiple_of(j * bk, bk), bk)
        ps, alphas = [], []
        for qr in qrs:
            tiles = [s_buf[slot, qr, t * LANES:(t + 1) * LANES] for t in range(nt)]
            tmax = tiles[0]
            for t in range(1, nt):
                tmax = jnp.maximum(tmax, tiles[t])
            m_prev = m_s[qr, :]
            m_new = jnp.maximum(m_prev, jnp.max(tmax, axis=-1, keepdims=True))
            alphas.append(jnp.exp2(m_prev - m_new))
            ps.append(jnp.concatenate([jnp.exp2(tl - m_new).astype(BF16) for tl in tiles], axis=1))
            m_s[qr, :] = m_new
        for qr, p, a in zip(qrs, ps, alphas):
            pv = jnp.dot(p, v_ref[rows, :], preferred_element_type=F32)
            acc_s[qr, :] = jnp.concatenate([a, a], axis=1) * acc_s[qr, :] + pv

    scores(0, 0)
    npairs = (nk - 1) // 2

    def body(i, carry):
        j = 2 * i
        scores(j + 1, 1)
        consume(j, 0)
        scores(j + 2, 0)
        consume(j + 1, 1)
        return carry

    lax.fori_loop(0, npairs, body, 0)
    if nk - 2 * npairs == 2:
        scores(nk - 1, 1)
        consume(nk - 2, 0)
        consume(nk - 1, 1)
    else:
        consume(nk - 1, 0)
    acc = acc_s[...]
    o_ref[...] = (acc[:, :MLA_V] / acc[:, MLA_V:MLA_V + 1]).astype(o_ref.dtype)


def _flash(q, k, v, batch, seq, bq, bk):
    t = batch * seq
    nq = seq // bq
    return pl.pallas_call(
        functools.partial(_flash_kernel, bk=bk, nk=seq // bk),
        out_shape=jax.ShapeDtypeStruct((t, MLA_HEADS * MLA_V), BF16),
        grid=(batch, MLA_HEADS, nq),
        in_specs=[pl.BlockSpec((bq, QK_PAD), lambda b, h, i: (b * nq + i, h)),
                  pl.BlockSpec((seq, QK_PAD), lambda b, h, i: (b, h)),
                  pl.BlockSpec((seq, V_PAD), lambda b, h, i: (b, h))],
        out_specs=pl.BlockSpec((bq, MLA_V), lambda b, h, i: (b * nq + i, h)),
        scratch_shapes=[pltpu.VMEM((bq, LANES), F32), pltpu.VMEM((bq, V_PAD), F32),
                        pltpu.VMEM((2, bq, bk), F32)],
        compiler_params=_params(("arbitrary", "arbitrary", "arbitrary")),
        name="mla_flash",
    )(q, k, v)


def _layernorm(y, w, b):
    mu = jnp.mean(y, axis=-1, keepdims=True)
    yc = y - mu
    var = jnp.mean(yc * yc, axis=-1, keepdims=True)
    return yc * lax.rsqrt(var + LN_EPS) * w + b


def _merge_kernel(x_ref, ga_ref, gb_ref, gm_ref, mq_ref, oa_ref, ob_ref, kvm_ref, wout_ref,
                  l1w_ref, l1b_ref, rw_ref, rb_ref,
                  x1_ref, ri_ref, rg_ref, cnt_ref, carry_s, *, tm):
    @pl.when(pl.program_id(0) == 0)
    def _():
        carry_s[...] = jnp.zeros_like(carry_s)

    parts = []
    for h in range(MEM_HEADS):
        cols = slice(h * MEM_HEAD_DIM, (h + 1) * MEM_HEAD_DIM)
        kh = kvm_ref[:, cols]
        vh = kvm_ref[:, D_MODEL + h * MEM_HEAD_DIM:D_MODEL + (h + 1) * MEM_HEAD_DIM]
        s = lax.dot_general(mq_ref[:, cols], kh, NT_DIMS, preferred_element_type=F32) * (MEM_HEAD_DIM ** -0.5)
        s = s - jnp.max(s, axis=-1, keepdims=True)
        p = jnp.exp(s)
        p = p / jnp.sum(p, axis=-1, keepdims=True)
        parts.append(jnp.dot(p.astype(BF16), vh, preferred_element_type=F32))
    om = jnp.concatenate(parts, axis=1)

    merged = (jax.nn.sigmoid(ga_ref[...].astype(F32)) * oa_ref[...].astype(F32)
              + jax.nn.sigmoid(gb_ref[...].astype(F32)) * ob_ref[...].astype(F32)
              + jax.nn.sigmoid(gm_ref[...].astype(F32)) * om)
    y = DN_ALPHA * x_ref[...] + jnp.dot(merged.astype(BF16), wout_ref[...], preferred_element_type=F32)
    x1 = _layernorm(y, l1w_ref[...], l1b_ref[...])
    x1_ref[...] = x1

    x_hi = x1.astype(BF16)
    x_lo = (x1 - x_hi.astype(F32)).astype(BF16)
    hi_part = jnp.dot(x_hi, rw_ref[...], preferred_element_type=F32)
    lo_part = jnp.dot(x_lo, rw_ref[:, :LANES], preferred_element_type=F32)
    logits = hi_part[:, :LANES] + hi_part[:, LANES:] + lo_part + rb_ref[...]
    lane_i = lax.broadcasted_iota(jnp.int32, (tm, LANES), 1)
    lane = lane_i.astype(F32)
    work = logits
    idx, val = [], []
    for _ in range(TOP_K):
        mx = jnp.max(work, axis=-1, keepdims=True)
        ix = jnp.min(jnp.where(work == mx, lane, float(LANES)), axis=-1, keepdims=True)
        idx.append(ix)
        val.append(mx)
        work = jnp.where(lane == ix, -jnp.inf, work)
    ex = [jnp.exp(v - val[0]) for v in val]
    tot = ex[0] + ex[1] + ex[2] + ex[3]
    hot = [jnp.where(lane == ix, 1.0, 0.0) for ix in idx]
    multi = hot[0] + hot[1] + hot[2] + hot[3]
    r = lax.broadcasted_iota(jnp.int32, (tm, tm), 0)
    c = lax.broadcasted_iota(jnp.int32, (tm, tm), 1)
    lower = jnp.where(r > c, 1.0, 0.0).astype(BF16)
    before = jnp.dot(lower, multi.astype(BF16), preferred_element_type=F32) + carry_s[0:1, :]
    ri = jnp.zeros((tm, LANES), F32)
    rg = jnp.zeros((tm, LANES), F32)
    for j in range(TOP_K):
        rank = jnp.sum(before * hot[j], axis=-1, keepdims=True)
        ri = ri + jnp.where(lane == float(j), idx[j], 0.0) + jnp.where(lane == float(TOP_K + j), rank, 0.0)
        rg = rg + jnp.where(lane == float(j), ex[j] / tot, 0.0)
    ri_ref[...] = ri.T[:2 * TOP_K, :].astype(jnp.int32)
    rg_ref[...] = rg
    carry_s[...] = carry_s[...] + jnp.sum(multi, axis=0, keepdims=True)
    cnt_ref[...] = carry_s[...]


def _merge(x2d, proj, o_a, o_b, kvm, wout, l1w, l1b, rw, rb, batch, seq, tm):
    t = batch * seq
    per_b = seq // tm
    nmem = kvm.shape[0] // batch
    tile = lambda colblk: pl.BlockSpec((tm, D_MODEL), lambda i: (i, colblk))
    full = lambda shape: pl.BlockSpec(shape, lambda i: (0, 0))
    return pl.pallas_call(
        functools.partial(_merge_kernel, tm=tm),
        out_shape=(jax.ShapeDtypeStruct((t, D_MODEL), F32),
                   jax.ShapeDtypeStruct((2 * TOP_K, t), jnp.int32),
                   jax.ShapeDtypeStruct((t, LANES), F32),
                   jax.ShapeDtypeStruct((8, LANES), F32)),
        grid=(t // tm,),
        in_specs=[tile(0), tile(COL_GA), tile(COL_GB), tile(COL_GM), tile(COL_MQ), tile(0), tile(0),
                  pl.BlockSpec((nmem, 2 * D_MODEL), lambda i: (i // per_b, 0)),
                  full((D_MODEL, D_MODEL)), full((1, D_MODEL)), full((1, D_MODEL)),
                  full((D_MODEL, 2 * LANES)), full((1, LANES))],
        out_specs=(tile(0), pl.BlockSpec((2 * TOP_K, tm), lambda i: (0, i)),
                   pl.BlockSpec((tm, LANES), lambda i: (i, 0)), full((8, LANES))),
        scratch_shapes=[pltpu.VMEM((8, LANES), F32)],
        compiler_params=_params(("arbitrary",)),
        name="merge_router",
    )(x2d, proj, proj, proj, proj, o_a, o_b, kvm, wout, l1w, l1b, rw, rb)


def _sc_gather(table, idx):
    p, d = idx.shape[0], table.shape[1]
    per_w = p // SC_WORKERS
    assert per_w * SC_WORKERS == p and per_w % SC_GATHER_ROWS == 0
    mesh = plsc.VectorSubcoreMesh(core_axis_name="c", subcore_axis_name="s")

    @functools.partial(
        pl.kernel, mesh=mesh,
        out_type=jax.ShapeDtypeStruct((p, d), table.dtype),
        scratch_types=[pltpu.VMEM((SC_GATHER_ROWS,), jnp.int32),
                       pltpu.VMEM((SC_GATHER_ROWS, d), table.dtype),
                       pltpu.SemaphoreType.DMA],
        name="moe_gather_sc",
    )
    def gather_kernel(table_hbm, idx_hbm, out_hbm, idx_v, rows_v, sem):
        wid = lax.axis_index("s") * SC_CORES + lax.axis_index("c")
        base = wid * per_w

        @pl.loop(0, per_w // SC_GATHER_ROWS)
        def _(c):
            off = base + c * SC_GATHER_ROWS
            pltpu.sync_copy(idx_hbm.at[pl.ds(off, SC_GATHER_ROWS)], idx_v)
            pltpu.async_copy(table_hbm.at[idx_v], rows_v, sem).wait()
            pltpu.sync_copy(rows_v, out_hbm.at[pl.ds(off, SC_GATHER_ROWS)])

    return gather_kernel(table, idx)


def _sc_dispatch(x, dest_t, slots):
    t, d = x.shape
    per_w = t // SC_WORKERS
    assert per_w * SC_WORKERS == t and per_w % SC_GATHER_ROWS == 0
    mesh = plsc.VectorSubcoreMesh(core_axis_name="c", subcore_axis_name="s")

    @functools.partial(
        pl.kernel, mesh=mesh,
        out_type=jax.ShapeDtypeStruct((slots, d), x.dtype),
        scratch_types=[pltpu.VMEM((SC_GATHER_ROWS,), jnp.int32),
                       pltpu.VMEM((SC_GATHER_ROWS, d), x.dtype)],
        name="moe_dispatch_sc",
    )
    def scatter_kernel(x_hbm, idx_hbm, out_hbm, idx_v, rows_v):
        wid = lax.axis_index("s") * SC_CORES + lax.axis_index("c")
        base = wid * per_w

        @pl.loop(0, per_w // SC_GATHER_ROWS)
        def _(c):
            off = base + c * SC_GATHER_ROWS
            pltpu.sync_copy(x_hbm.at[pl.ds(off, SC_GATHER_ROWS)], rows_v)
            for j in range(TOP_K):
                pltpu.sync_copy(idx_hbm.at[pl.ds(j * t + off, SC_GATHER_ROWS)], idx_v)
                pltpu.sync_copy(rows_v, out_hbm.at[idx_v])

    return scatter_kernel(x, dest_t)


HALF_D = D_MODEL // 2
HI16 = -65536


def _pack_bf16_pair(a, b):
    wa = lax.bitcast_convert_type(a.astype(BF16).astype(F32), jnp.int32)
    wb = lax.bitcast_convert_type(b.astype(BF16).astype(F32), jnp.int32)
    return wa | lax.shift_right_logical(wb, 16)


def _unpack_bf16_pair(w):
    a = lax.bitcast_convert_type(w & HI16, F32)
    b = lax.bitcast_convert_type(lax.shift_left(w, 16), F32)
    return a, b


def _expert_kernel(be_ref, nu_ref, xs_ref, wgu_ref, bgu_ref, wd_ref, bd_ref, ys_ref, wgu_b, wd_b):
    i = pl.program_id(0)

    @pl.when(jnp.logical_or(i == 0, be_ref[i] != be_ref[jnp.maximum(i - 1, 0)]))
    def _():
        wgu_b[...] = wgu_ref[0].astype(BF16)
        wd_b[...] = wd_ref[0].astype(BF16)

    @pl.when(i < nu_ref[0])
    def _():
        h = jnp.dot(xs_ref[...].astype(BF16), wgu_b[...], preferred_element_type=F32) + bgu_ref[0]
        g = jnp.minimum(h[:, :D_FF], SWIGLU_LIMIT)
        u = jnp.clip(h[:, D_FF:], -SWIGLU_LIMIT, SWIGLU_LIMIT)
        act = (u + 1.0) * g * jax.nn.sigmoid(SWIGLU_ALPHA * g)
        y = jnp.dot(act.astype(BF16), wd_b[...], preferred_element_type=F32) + bd_ref[0]
        ys_ref[...] = _pack_bf16_pair(y[:, :HALF_D], y[:, HALF_D:])

    @pl.when(i >= nu_ref[0])
    def _():
        ys_ref[...] = jnp.zeros_like(ys_ref)


def _experts(block_e, n_used, xs, wgu, bgu, wd, bd):
    slots = xs.shape[0]
    nb = slots // EXPERT_BLOCK
    grid_spec = pltpu.PrefetchScalarGridSpec(
        num_scalar_prefetch=2,
        grid=(nb,),
        in_specs=[pl.BlockSpec((EXPERT_BLOCK, D_MODEL), lambda i, be, nu: (i, 0)),
                  pl.BlockSpec((1, D_MODEL, 2 * D_FF), lambda i, be, nu: (be[i], 0, 0)),
                  pl.BlockSpec((1, 1, 2 * D_FF), lambda i, be, nu: (be[i], 0, 0)),
                  pl.BlockSpec((1, D_FF, D_MODEL), lambda i, be, nu: (be[i], 0, 0)),
                  pl.BlockSpec((1, 1, D_MODEL), lambda i, be, nu: (be[i], 0, 0))],
        out_specs=pl.BlockSpec((EXPERT_BLOCK, HALF_D), lambda i, be, nu: (i, 0)),
        scratch_shapes=[pltpu.VMEM((D_MODEL, 2 * D_FF), BF16), pltpu.VMEM((D_FF, D_MODEL), BF16)],
    )
    return pl.pallas_call(
        _expert_kernel,
        out_shape=jax.ShapeDtypeStruct((slots, HALF_D), jnp.int32),
        grid_spec=grid_spec,
        compiler_params=_params(("arbitrary",)),
        name="moe_experts",
    )(block_e, n_used, xs, wgu, bgu, wd, bd)


def _weighted_rows(rg, packed):
    lo = hi = None
    for j, w in enumerate(packed):
        a, b = _unpack_bf16_pair(w)
        gj = rg[:, j:j + 1]
        lo = gj * a if lo is None else lo + gj * a
        hi = gj * b if hi is None else hi + gj * b
    return jnp.concatenate([lo, hi], axis=1)


def _combine_kernel(dest_hbm, x1_ref, rg_ref, ys_hbm, ys_tiles_hbm, l2w_ref, l2b_ref, o_ref, idx_s, buf,
                    sem_i, sem_r, *, tm):
    i = pl.program_id(0)
    nsteps = pl.num_programs(0)
    n = TOP_K * tm

    def issue(step, slot):
        cp = pltpu.make_async_copy(dest_hbm.at[pl.ds(pl.multiple_of(step * n, n), n)], idx_s, sem_i)
        cp.start()
        cp.wait()

        def body(g, carry):
            for jj in range(SUBLANES):
                for j in range(TOP_K):
                    d = idx_s[g * (SUBLANES * TOP_K) + jj * TOP_K + j]
                    pltpu.make_async_copy(ys_hbm.at[pl.ds(d, 1)], buf.at[slot, j, g, pl.ds(jj, 1)],
                                          sem_r.at[slot]).start(priority=j % 2)
            return carry

        lax.fori_loop(0, tm // SUBLANES, body, 0)

    @pl.when(i == 0)
    def _():
        issue(0, 0)

    @pl.when(i + 1 < nsteps)
    def _():
        issue(i + 1, (i + 1) % 2)

    slot = i % 2
    for j in range(TOP_K):
        pltpu.make_async_copy(ys_tiles_hbm.at[pl.ds(0, tm // SUBLANES)], buf.at[slot, j], sem_r.at[slot]).wait()
    moe = _weighted_rows(rg_ref[...], [buf[slot, j].reshape(tm, HALF_D) for j in range(TOP_K)])
    o_ref[...] = _layernorm(DN_ALPHA * x1_ref[...] + moe, l2w_ref[...], l2b_ref[...])


def _combine_rows_kernel(x1_ref, rg_ref, g0_ref, g1_ref, g2_ref, g3_ref, l2w_ref, l2b_ref, *rest):
    o_ref = rest[-1]
    moe = _weighted_rows(rg_ref[...], [g0_ref[...], g1_ref[...], g2_ref[...], g3_ref[...]])
    o_ref[...] = _layernorm(DN_ALPHA * x1_ref[...] + moe, l2w_ref[...], l2b_ref[...])


def _combine_rows(x1, rg, rows, l2w, l2b, tm, y_rest):
    t = x1.shape[0]
    nt = rows.shape[0] // TOP_K // tm
    full = lambda shape: pl.BlockSpec(shape, lambda i: (0, 0))
    choice = lambda j: pl.BlockSpec((tm, HALF_D), lambda i, j=j: (j * nt + i, 0))
    in_specs = [pl.BlockSpec((tm, D_MODEL), lambda i: (i, 0)),
                pl.BlockSpec((tm, LANES), lambda i: (i, 0)),
                choice(0), choice(1), choice(2), choice(3),
                full((1, D_MODEL)), full((1, D_MODEL))]
    args = [x1, rg, rows, rows, rows, rows, l2w, l2b]
    aliases = {}
    if y_rest is not None:
        in_specs.append(pl.BlockSpec(memory_space=pl.ANY))
        args.append(y_rest)
        aliases = {len(args) - 1: 0}
    return pl.pallas_call(
        _combine_rows_kernel,
        out_shape=jax.ShapeDtypeStruct((t, D_MODEL), F32),
        grid=(nt,),
        in_specs=in_specs,
        out_specs=pl.BlockSpec((tm, D_MODEL), lambda i: (i, 0)),
        input_output_aliases=aliases,
        compiler_params=_params(("arbitrary",)),
        name="moe_combine_rows",
    )(*args)


def _combine(dest, x1, rg, ys, l2w, l2b, tm, tok0):
    t = x1.shape[0]
    first = tok0 // tm
    full = lambda shape: pl.BlockSpec(shape, lambda i: (0, 0))
    return pl.pallas_call(
        functools.partial(_combine_kernel, tm=tm),
        out_shape=jax.ShapeDtypeStruct((t, D_MODEL), F32),
        grid=((t - tok0) // tm,),
        in_specs=[pl.BlockSpec(memory_space=pl.ANY),
                  pl.BlockSpec((tm, D_MODEL), lambda i: (first + i, 0)),
                  pl.BlockSpec((tm, LANES), lambda i: (first + i, 0)),
                  pl.BlockSpec(memory_space=pl.ANY), pl.BlockSpec(memory_space=pl.ANY),
                  full((1, D_MODEL)), full((1, D_MODEL))],
        out_specs=pl.BlockSpec((tm, D_MODEL), lambda i: (first + i, 0)),
        scratch_shapes=[pltpu.SMEM((TOP_K * tm,), jnp.int32),
                        pltpu.VMEM((2, TOP_K, tm // SUBLANES, SUBLANES, HALF_D), jnp.int32),
                        pltpu.SemaphoreType.DMA, pltpu.SemaphoreType.DMA((2,))],
        compiler_params=_params(("arbitrary",)),
        name="moe_combine",
    )(dest, x1, rg, ys, ys.reshape(ys.shape[0] // SUBLANES, SUBLANES, HALF_D), l2w, l2b)


def _prep_w_in(w):
    zeros = lambda n: jnp.zeros((D_MODEL, n), w.dtype)
    kr = w[:, 5760:5824]
    kr_sw = jnp.concatenate([kr[:, MLA_ROPE // 2:], kr[:, :MLA_ROPE // 2]], axis=1)
    parts = [w[:, 6848:9920], w[:, 0:5120], w[:, 5824:6848],
             w[:, 5120:5504], zeros(128),
             w[:, 5504:5760], kr, zeros(64), kr_sw, zeros(64)]
    return jnp.concatenate(parts, axis=1).astype(BF16)


def _prep_w_uq(w):
    w3 = w.reshape(MLA_Q_LORA, MLA_HEADS, MLA_NOPE + MLA_ROPE)
    rope = w3[:, :, MLA_NOPE:]
    rope_sw = jnp.concatenate([rope[:, :, MLA_ROPE // 2:], rope[:, :, :MLA_ROPE // 2]], axis=-1)
    pad = jnp.zeros((MLA_Q_LORA, MLA_HEADS, QK_PAD - MLA_NOPE - MLA_ROPE), w.dtype)
    main = jnp.concatenate([w3, pad], axis=-1).reshape(MLA_Q_LORA, MLA_HEADS * QK_PAD)
    swp = jnp.concatenate([rope_sw, pad], axis=-1).reshape(MLA_Q_LORA, MLA_HEADS * LANES)
    both = jnp.concatenate([main, swp], axis=1)
    return jnp.pad(both, ((0, 512 - MLA_Q_LORA), (0, 0))).astype(BF16)


def _rope_tables(seq):
    inv_freq = ROPE_THETA ** (-jnp.arange(0, MLA_ROPE, 2, dtype=F32) / MLA_ROPE)
    ang = jnp.arange(seq, dtype=F32)[:, None] * inv_freq[None, :]
    cos, sin = jnp.cos(ang), jnp.sin(ang)
    pad = jnp.zeros((seq, LANES - MLA_ROPE), F32)
    return (jnp.concatenate([cos, cos, pad], axis=1), jnp.concatenate([-sin, sin, pad], axis=1))


def _split_bf16(w):
    hi = w.astype(BF16)
    lo = (w - hi.astype(F32)).astype(BF16)
    return jnp.concatenate([hi, lo], axis=1)


def _tile(n, pref):
    return pref if n % pref == 0 else n


def _layer(x, mem, wts, sc_share):
    batch, seq, _ = x.shape
    t = batch * seq
    x2d = x.reshape(t, D_MODEL)

    proj = _matmul(x2d, wts["w_in"], BF16, _tile(t, 1024), 2048, "in_proj")
    o_a = _hgrn(proj, wts["lb"], wts["hgrn_norm_w"], batch, seq, _tile(seq, 512))

    cos, sin = _rope_tables(seq)
    q, k, v = _mla_prep(proj, wts["qnw"], wts["kvnw"], wts["w_uq"], wts["w_uk"], wts["w_uv"],
                        cos, sin, batch, seq, _tile(seq, 512))
    o_b = _flash(q, k, v, batch, seq, _tile(seq, 1024), _tile(seq, 1024))

    nmem = mem.shape[1]
    kvm = _matmul(mem.reshape(batch * nmem, D_MODEL), wts["mem_w_kv"], BF16,
                  _tile(batch * nmem, 512), 1024, "mem_kv")

    tm = _tile(seq, 512)
    x1, ri, rg, cnt = _merge(x2d, proj, o_a, o_b, kvm, wts["w_out"], wts["ln1_w"], wts["ln1_b"],
                             wts["router_w"], wts["router_b"], batch, seq, tm)

    counts = cnt[0, :N_EXPERTS].astype(jnp.int32)
    padded = (counts + EXPERT_BLOCK - 1) // EXPERT_BLOCK * EXPERT_BLOCK
    pad_end = jnp.cumsum(padded)
    pad_start = pad_end - padded
    dest_ct = (pad_start[ri[:TOP_K]] + ri[TOP_K:]).astype(jnp.int32)
    nb = t * TOP_K // EXPERT_BLOCK + N_EXPERTS
    blk_start = jnp.arange(nb, dtype=jnp.int32) * EXPERT_BLOCK
    block_e = jnp.minimum(jnp.sum((pad_end[None, :] <= blk_start[:, None]).astype(jnp.int32), axis=1),
                          N_EXPERTS - 1).astype(jnp.int32)
    n_used = (pad_end[-1:] // EXPERT_BLOCK).astype(jnp.int32)
    xs = _sc_dispatch(x1, dest_ct.reshape(TOP_K * t), nb * EXPERT_BLOCK)
    ys = _experts(block_e, n_used, xs, wts["exp_w_gu"], wts["exp_b_gu"], wts["exp_w_down"],
                        wts["exp_b_down"])
    n_sc = int(t * sc_share) // 2048 * 2048
    y_rest = None
    if n_sc < t:
        dest_tail = dest_ct[:, n_sc:].T.reshape(TOP_K * (t - n_sc))
        y_rest = _combine(dest_tail, x1, rg, ys, wts["ln2_w"], wts["ln2_b"], _tile(t, 256), n_sc)
    if n_sc > 0:
        rows = _sc_gather(ys, dest_ct[:, :n_sc].reshape(TOP_K * n_sc))
        y = _combine_rows(x1, rg, rows, wts["ln2_w"], wts["ln2_b"], 512, y_rest)
    else:
        y = y_rest
    return y.reshape(batch, seq, D_MODEL)


def kernel(x_prompt, x_sample, mem_prompt, mem_sample, w_in, hgrn_lb_logits, hgrn_norm_w,
           mla_q_norm_w, mla_w_uq, mla_kv_norm_w, mla_w_uk, mla_w_uv, mem_w_kv, w_out,
           ln1_w, ln1_b, router_w, router_b, exp_w_gu, exp_b_gu, exp_w_down, exp_b_down,
           ln2_w, ln2_b):
    depth = w_in.shape[0]
    gamma = jax.nn.softmax(hgrn_lb_logits.astype(F32), axis=1)
    cum = jnp.cumsum(gamma, axis=1)
    lb_all = cum[:, 1:] - cum[:, :1]
    y_prompt, y_sample = x_prompt, x_sample
    for l in range(depth):
        row = lambda a: a[l].reshape(1, -1).astype(F32)
        wts = {
            "w_in": _prep_w_in(w_in[l]),
            "lb": lb_all[:, l],
            "hgrn_norm_w": hgrn_norm_w[l].astype(F32),
            "qnw": jnp.pad(row(mla_q_norm_w), ((0, 0), (0, 512 - MLA_Q_LORA))),
            "kvnw": row(mla_kv_norm_w),
            "w_uq": _prep_w_uq(mla_w_uq[l]),
            "w_uk": mla_w_uk[l].astype(BF16),
            "w_uv": mla_w_uv[l].astype(BF16),
            "mem_w_kv": mem_w_kv[l].astype(BF16),
            "w_out": w_out[l].astype(BF16),
            "ln1_w": row(ln1_w), "ln1_b": row(ln1_b),
            "router_w": _split_bf16(jnp.pad(router_w[l].astype(F32), ((0, 0), (0, LANES - N_EXPERTS)))),
            "router_b": jnp.pad(row(router_b), ((0, 0), (0, LANES - N_EXPERTS)), constant_values=-jnp.inf),
            "exp_w_gu": exp_w_gu[l].astype(F32),
            "exp_b_gu": exp_b_gu[l].reshape(N_EXPERTS, 1, 2 * D_FF).astype(F32),
            "exp_w_down": exp_w_down[l].astype(F32),
            "exp_b_down": exp_b_down[l].reshape(N_EXPERTS, 1, D_MODEL).astype(F32),
            "ln2_w": row(ln2_w), "ln2_b": row(ln2_b),
        }
        y_prompt = _layer(y_prompt, mem_prompt, wts, sc_share=1.0)
        y_sample = _layer(y_sample, mem_sample, wts, sc_share=0.875)
    return (y_prompt, y_sample)
```

```python
import functools

import numpy as np
import jax
import jax.numpy as jnp
from jax import lax
from jax.experimental import pallas as pl
from jax.experimental.pallas import tpu as pltpu
from jax.experimental.pallas import tpu_sc as plsc

F32 = jnp.float32
BF16 = jnp.bfloat16

D_MODEL = 1024
HGRN_HEADS = 8
HGRN_DK = 128
MLA_HEADS = 8
MLA_Q_LORA = 384
MLA_KV_LORA = 256
MLA_NOPE = 128
MLA_ROPE = 64
MLA_V = 128
ROPE_THETA = 10000.0
MEM_HEADS = 4
MEM_HEAD_DIM = D_MODEL // MEM_HEADS
N_EXPERTS = 32
TOP_K = 4
D_FF = D_MODEL
SWIGLU_LIMIT = 7.0
SWIGLU_ALPHA = 1.702
DN_ALPHA = 2.0 ** 0.25
LN_EPS = 1e-5
RMS_EPS = 1e-6

LANES = 128
SUBLANES = 8
QK_PAD = 256
V_PAD = 256
LOG2E = 1.4426950408889634
HGRN_CHUNK = 64
HGRN_SAFE_RANGE = 160.0
EXPERT_BLOCK = 512
SC_CORES = 2
SC_SUBCORES = 16
SC_WORKERS = SC_CORES * SC_SUBCORES
SC_GATHER_ROWS = 64
VMEM_LIMIT = 56 * 1024 * 1024

COL_GA, COL_GB, COL_GM, COL_Q, COL_ZF, COL_ZB, COL_I, COL_G, COL_MQ = range(9)
COL_DQ, COL_DKV = 18, 19
IN_COLS_PAD = 10240

NT_DIMS = (((1,), (1,)), ((), ()))
TN_DIMS = (((0,), (0,)), ((), ()))


def _params(sem, vmem=VMEM_LIMIT):
    return pltpu.CompilerParams(dimension_semantics=sem, vmem_limit_bytes=vmem)


def _mm_kernel(x_ref, w_ref, o_ref, xb_ref):
    @pl.when(pl.program_id(1) == 0)
    def _():
        xb_ref[...] = x_ref[...].astype(BF16)

    o_ref[...] = jnp.dot(xb_ref[...], w_ref[...], preferred_element_type=F32).astype(o_ref.dtype)


def _matmul(x, w, out_dtype, tm, tn, name):
    m, k = x.shape
    n = w.shape[1]
    return pl.pallas_call(
        _mm_kernel,
        out_shape=jax.ShapeDtypeStruct((m, n), out_dtype),
        grid=(m // tm, n // tn),
        in_specs=[pl.BlockSpec((tm, k), lambda i, j: (i, 0)),
                  pl.BlockSpec((k, tn), lambda i, j: (0, j))],
        out_specs=pl.BlockSpec((tm, tn), lambda i, j: (i, j)),
        scratch_shapes=[pltpu.VMEM((tm, k), BF16)],
        compiler_params=_params(("arbitrary", "arbitrary")),
        name=name,
    )(x, w)


def _hgrn_kernel(*refs, reverse, epilogue, sc):
    if epilogue:
        lb_ref, q_ref, z_ref, v_ref, of_ref, g_ref, nw_ref, o_ref, st_ref, kk_s, b_s, vf_s, os_s = refs
    else:
        lb_ref, q_ref, z_ref, v_ref, o_ref, st_ref, kk_s, b_s, vf_s = refs
    C = HGRN_CHUNK
    nch = sc // C

    @pl.when(pl.program_id(1) == 0)
    def _():
        st_ref[...] = jnp.zeros_like(st_ref)

    lb = lb_ref[...]
    row = lax.broadcasted_iota(jnp.int32, (C, C), 0)
    col = lax.broadcasted_iota(jnp.int32, (C, C), 1)
    tri = (row <= col) if reverse else (row >= col)
    trib = jnp.where(tri, 1.0, 0.0).astype(BF16)

    minb = None
    for c in range(nch):
        rows = slice(c * C, (c + 1) * C)
        z = z_ref[rows, :].astype(F32)
        gate = (1.0 - lb) * jax.nn.sigmoid(z)
        lf = jnp.log(lb + gate)
        kk_s[rows, :] = (1.0 - lb) - gate
        hi = lf.astype(BF16)
        r1 = lf - hi.astype(F32)
        mid = r1.astype(BF16)
        lo = (r1 - mid.astype(F32)).astype(BF16)
        b = (jnp.dot(trib, hi, preferred_element_type=F32)
             + jnp.dot(trib, mid, preferred_element_type=F32)
             + jnp.dot(trib, lo, preferred_element_type=F32))
        b_s[rows, :] = b
        mb = jnp.min(b)
        minb = mb if minb is None else jnp.minimum(minb, mb)

    rid = lax.broadcasted_iota(jnp.int32, (C, 1), 0)

    def chunk(i, carry, fast):
        c = (nch - 1 - i) if reverse else i
        r0 = pl.multiple_of(c * C, C)
        rows = pl.ds(r0, C)
        for h in range(HGRN_HEADS):
            cols = slice(h * HGRN_DK, (h + 1) * HGRN_DK)
            q = q_ref[rows, cols].astype(F32)
            kk = kk_s[rows, cols]
            b = b_s[rows, cols]
            v = v_ref[rows, cols]
            bl = b[0:1, :] if reverse else b[C - 1:C, :]
            if fast:
                bm = 0.5 * bl
                qd = (q * jnp.exp(b - bm)).astype(BF16)
                kd = (kk * jnp.exp(bm - b)).astype(BF16)
                s = lax.dot_general(qd, kd, NT_DIMS, preferred_element_type=F32)
                s = jnp.where(tri, s, 0.0).astype(BF16)
                o = jnp.dot(s, v, preferred_element_type=F32)
            else:
                def sbody(g_i, o_acc):
                    grp = pl.ds(pl.multiple_of(r0 + g_i * SUBLANES, SUBLANES), SUBLANES)
                    b8 = b_s[grp, cols]
                    k8 = kk_s[grp, cols]
                    v8 = vf_s[grp, cols]
                    for jj in range(SUBLANES):
                        s_i = g_i * SUBLANES + jj
                        w = q * k8[jj:jj + 1, :] * jnp.exp(jnp.minimum(b - b8[jj:jj + 1, :], 0.0))
                        scol = jnp.sum(w, axis=-1, keepdims=True)
                        keep = (rid <= s_i) if reverse else (rid >= s_i)
                        o_acc = o_acc + jnp.where(keep, scol, 0.0) * v8[jj:jj + 1, :]
                    return o_acc
                o = lax.fori_loop(0, C // SUBLANES, sbody, jnp.zeros((C, HGRN_DK), F32))
            st = st_ref[h]
            qi = (q * jnp.exp(b)).astype(BF16)
            o = o + lax.dot_general(qi, st.astype(BF16), NT_DIMS, preferred_element_type=F32)
            ke = (kk * jnp.exp(bl - b)).astype(BF16)
            upd = lax.dot_general(v, ke, TN_DIMS, preferred_element_type=F32)
            st_ref[h] = st * jnp.exp(bl) + upd
            if epilogue:
                os_s[rows, cols] = o + of_ref[rows, cols].astype(F32)
            else:
                o_ref[rows, cols] = o.astype(o_ref.dtype)
        return carry

    safe = minb >= -HGRN_SAFE_RANGE

    @pl.when(safe)
    def _():
        lax.fori_loop(0, nch, functools.partial(chunk, fast=True), 0, unroll=2)

    @pl.when(jnp.logical_not(safe))
    def _():
        vf_s[...] = v_ref[...].astype(F32)
        lax.fori_loop(0, nch, functools.partial(chunk, fast=False), 0)

    if epilogue:
        nw = nw_ref[...]
        for h in range(HGRN_HEADS):
            cols = slice(h * HGRN_DK, (h + 1) * HGRN_DK)
            os = os_s[:, cols]
            ms = jnp.mean(os * os, axis=-1, keepdims=True)
            y = os * lax.rsqrt(ms + RMS_EPS) * nw
            g = g_ref[:, cols].astype(F32)
            o_ref[:, cols] = (y * (g * jax.nn.sigmoid(g))).astype(o_ref.dtype)


def _hgrn(proj, lb, norm_w, batch, seq, sc):
    t = batch * seq
    ns = seq // sc
    blk = (sc, D_MODEL)

    def spec(colblk, reverse):
        if reverse:
            return pl.BlockSpec(blk, lambda b, n: (b * ns + ns - 1 - n, colblk))
        return pl.BlockSpec(blk, lambda b, n: (b * ns + n, colblk))

    def row_spec(reverse):
        if reverse:
            return pl.BlockSpec(blk, lambda b, n: (b * ns + ns - 1 - n, 0))
        return pl.BlockSpec(blk, lambda b, n: (b * ns + n, 0))

    lb_spec = pl.BlockSpec((1, D_MODEL), lambda b, n: (0, 0))
    common_scratch = [pltpu.VMEM((HGRN_HEADS, HGRN_DK, HGRN_DK), F32),
                      pltpu.VMEM(blk, F32), pltpu.VMEM(blk, F32), pltpu.VMEM(blk, F32)]
    o_f = pl.pallas_call(
        functools.partial(_hgrn_kernel, reverse=False, epilogue=False, sc=sc),
        out_shape=jax.ShapeDtypeStruct((t, D_MODEL), BF16),
        grid=(batch, ns),
        in_specs=[lb_spec, spec(COL_Q, False), spec(COL_ZF, False), spec(COL_I, False)],
        out_specs=row_spec(False),
        scratch_shapes=common_scratch,
        compiler_params=_params(("arbitrary", "arbitrary")),
        name="hgrn_fwd",
    )(lb[0:1], proj, proj, proj)
    o_a = pl.pallas_call(
        functools.partial(_hgrn_kernel, reverse=True, epilogue=True, sc=sc),
        out_shape=jax.ShapeDtypeStruct((t, D_MODEL), BF16),
        grid=(batch, ns),
        in_specs=[lb_spec, spec(COL_Q, True), spec(COL_ZB, True), spec(COL_I, True),
                  row_spec(True), spec(COL_G, True),
                  pl.BlockSpec((1, HGRN_DK), lambda b, n: (0, 0))],
        out_specs=row_spec(True),
        scratch_shapes=common_scratch + [pltpu.VMEM(blk, F32)],
        compiler_params=_params(("arbitrary", "arbitrary")),
        name="hgrn_bwd",
    )(lb[1:2], proj, proj, proj, o_f, proj, norm_w.reshape(1, HGRN_DK))
    return o_a


def _mla_prep_kernel(dq_ref, dkv_ref, qnw_ref, kvnw_ref, wq_ref, wk_ref, wv_ref, cos_ref, sin_ref,
                     q_ref, k_ref, v_ref):
    scale = (MLA_NOPE + MLA_ROPE) ** -0.5 * LOG2E
    cos = cos_ref[...]
    sin = sin_ref[...]
    dq = dq_ref[...].astype(F32)
    ms = jnp.sum(dq * dq, axis=-1, keepdims=True) * (1.0 / MLA_Q_LORA)
    cq = (dq * lax.rsqrt(ms + RMS_EPS) * qnw_ref[...]).astype(BF16)
    qa = jnp.dot(cq, wq_ref[...], preferred_element_type=F32)
    for h in range(MLA_HEADS):
        base = h * QK_PAD
        q_ref[:, base:base + MLA_NOPE] = (qa[:, base:base + MLA_NOPE] * scale).astype(BF16)
        rp = qa[:, base + MLA_NOPE:base + QK_PAD]
        sw = qa[:, MLA_HEADS * QK_PAD + h * LANES:MLA_HEADS * QK_PAD + (h + 1) * LANES]
        q_ref[:, base + MLA_NOPE:base + QK_PAD] = ((rp * cos + sw * sin) * scale).astype(BF16)
    dkv = dkv_ref[...].astype(F32)
    ckv = dkv[:, :MLA_KV_LORA]
    msk = jnp.mean(ckv * ckv, axis=-1, keepdims=True)
    cn = (ckv * lax.rsqrt(msk + RMS_EPS) * kvnw_ref[...]).astype(BF16)
    kn = jnp.dot(cn, wk_ref[...], preferred_element_type=F32)
    vv = jnp.dot(cn, wv_ref[...], preferred_element_type=F32).astype(BF16)
    ones_col = jnp.where(lax.broadcasted_iota(jnp.int32, (vv.shape[0], LANES), 1) == 0, 1.0, 0.0).astype(BF16)
    for h in range(MLA_HEADS):
        v_ref[:, h * V_PAD:h * V_PAD + MLA_V] = vv[:, h * MLA_V:(h + 1) * MLA_V]
        v_ref[:, h * V_PAD + MLA_V:(h + 1) * V_PAD] = ones_col
    kr = (dkv[:, MLA_KV_LORA:MLA_KV_LORA + LANES] * cos
          + dkv[:, MLA_KV_LORA + LANES:MLA_KV_LORA + 2 * LANES] * sin).astype(BF16)
    for h in range(MLA_HEADS):
        base = h * QK_PAD
        k_ref[:, base:base + MLA_NOPE] = kn[:, h * MLA_NOPE:(h + 1) * MLA_NOPE].astype(BF16)
        k_ref[:, base + MLA_NOPE:base + QK_PAD] = kr


def _mla_prep(proj, qnw, kvnw, wq, wk, wv, cos, sin, batch, seq, tm):
    t = batch * seq
    npos = seq // tm
    full = lambda shape: pl.BlockSpec(shape, lambda i: (0, 0))
    return pl.pallas_call(
        _mla_prep_kernel,
        out_shape=(jax.ShapeDtypeStruct((t, MLA_HEADS * QK_PAD), BF16),
                   jax.ShapeDtypeStruct((t, MLA_HEADS * QK_PAD), BF16),
                   jax.ShapeDtypeStruct((t, MLA_HEADS * V_PAD), BF16)),
        grid=(t // tm,),
        in_specs=[pl.BlockSpec((tm, 512), lambda i: (i, COL_DQ)),
                  pl.BlockSpec((tm, 512), lambda i: (i, COL_DKV)),
                  full((1, 512)), full((1, MLA_KV_LORA)),
                  full(wq.shape), full(wk.shape), full(wv.shape),
                  pl.BlockSpec((tm, LANES), lambda i: (i % npos, 0)),
                  pl.BlockSpec((tm, LANES), lambda i: (i % npos, 0))],
        out_specs=(pl.BlockSpec((tm, MLA_HEADS * QK_PAD), lambda i: (i, 0)),
                   pl.BlockSpec((tm, MLA_HEADS * QK_PAD), lambda i: (i, 0)),
                   pl.BlockSpec((tm, MLA_HEADS * V_PAD), lambda i: (i, 0))),
        compiler_params=_params(("arbitrary",)),
        name="mla_prep",
    )(proj, proj, qnw, kvnw, wq, wk, wv, cos, sin)


def _flash_kernel(q_ref, k_ref, v_ref, o_ref, m_s, acc_s, s_buf, *, bk, nk):
    m_s[...] = jnp.full(m_s.shape, -jnp.inf, F32)
    acc_s[...] = jnp.zeros(acc_s.shape, F32)
    nt = bk // LANES
    bq = q_ref.shape[0]
    nsplit = 2 if bq % 256 == 0 else 1
    hq = bq // nsplit
    qrs = [slice(hf * hq, (hf + 1) * hq) for hf in range(nsplit)]

    def scores(j, slot):
        rows = pl.ds(pl.multiple_of(j * bk, bk), bk)
        for qr in qrs:
            s_buf[slot, qr, :] = lax.dot_general(q_ref[qr, :], k_ref[rows, :], NT_DIMS,
                                                 preferred_element_type=F32)

    def consume(j, slot):
        rows = pl.ds(pl.multiple_of(j * bk, bk), bk)
        ps, alphas = [], []
        for qr in qrs:
            tiles = [s_buf[slot, qr, t * LANES:(t + 1) * LANES] for t in range(nt)]
            tmax = tiles[0]
            for t in range(1, nt):
                tmax = jnp.maximum(tmax, tiles[t])
            m_prev = m_s[qr, :]
            m_new = jnp.maximum(m_prev, jnp.max(tmax, axis=-1, keepdims=True))
            alphas.append(jnp.exp2(m_prev - m_new))
            ps.append(jnp.concatenate([jnp.exp2(tl - m_new).astype(BF16) for tl in tiles], axis=1))
            m_s[qr, :] = m_new
        for qr, p, a in zip(qrs, ps, alphas):
            pv = jnp.dot(p, v_ref[rows, :], preferred_element_type=F32)
            acc_s[qr, :] = jnp.concatenate([a, a], axis=1) * acc_s[qr, :] + pv

    scores(0, 0)
    npairs = (nk - 1) // 2

    def body(i, carry):
        j = 2 * i
        scores(j + 1, 1)
        consume(j, 0)
        scores(j + 2, 0)
        consume(j + 1, 1)
        return carry

    lax.fori_loop(0, npairs, body, 0)
    if nk - 2 * npairs == 2:
        scores(nk - 1, 1)
        consume(nk - 2, 0)
        consume(nk - 1, 1)
    else:
        consume(nk - 1, 0)
    acc = acc_s[...]
    o_ref[...] = (acc[:, :MLA_V] / acc[:, MLA_V:MLA_V + 1]).astype(o_ref.dtype)


def _flash(q, k, v, batch, seq, bq, bk):
    t = batch * seq
    nq = seq // bq
    return pl.pallas_call(
        functools.partial(_flash_kernel, bk=bk, nk=seq // bk),
        out_shape=jax.ShapeDtypeStruct((t, MLA_HEADS * MLA_V), BF16),
        grid=(batch, MLA_HEADS, nq),
        in_specs=[pl.BlockSpec((bq, QK_PAD), lambda b, h, i: (b * nq + i, h)),
                  pl.BlockSpec((seq, QK_PAD), lambda b, h, i: (b, h)),
                  pl.BlockSpec((seq, V_PAD), lambda b, h, i: (b, h))],
        out_specs=pl.BlockSpec((bq, MLA_V), lambda b, h, i: (b * nq + i, h)),
        scratch_shapes=[pltpu.VMEM((bq, LANES), F32), pltpu.VMEM((bq, V_PAD), F32),
                        pltpu.VMEM((2, bq, bk), F32)],
        compiler_params=_params(("arbitrary", "arbitrary", "arbitrary")),
        name="mla_flash",
    )(q, k, v)


def _layernorm(y, w, b):
    mu = jnp.mean(y, axis=-1, keepdims=True)
    yc = y - mu
    var = jnp.mean(yc * yc, axis=-1, keepdims=True)
    return yc * lax.rsqrt(var + LN_EPS) * w + b


def _merge_kernel(x_ref, ga_ref, gb_ref, gm_ref, mq_ref, oa_ref, ob_ref, kvm_ref, wout_ref,
                  l1w_ref, l1b_ref, rw_ref, rb_ref,
                  x1_ref, ri_ref, rg_ref, cnt_ref, carry_s, *, tm):
    @pl.when(pl.program_id(0) == 0)
    def _():
        carry_s[...] = jnp.zeros_like(carry_s)

    parts = []
    for h in range(MEM_HEADS):
        cols = slice(h * MEM_HEAD_DIM, (h + 1) * MEM_HEAD_DIM)
        kh = kvm_ref[:, cols]
        vh = kvm_ref[:, D_MODEL + h * MEM_HEAD_DIM:D_MODEL + (h + 1) * MEM_HEAD_DIM]
        s = lax.dot_general(mq_ref[:, cols], kh, NT_DIMS, preferred_element_type=F32) * (MEM_HEAD_DIM ** -0.5)
        s = s - jnp.max(s, axis=-1, keepdims=True)
        p = jnp.exp(s)
        p = p / jnp.sum(p, axis=-1, keepdims=True)
        parts.append(jnp.dot(p.astype(BF16), vh, preferred_element_type=F32))
    om = jnp.concatenate(parts, axis=1)

    merged = (jax.nn.sigmoid(ga_ref[...].astype(F32)) * oa_ref[...].astype(F32)
              + jax.nn.sigmoid(gb_ref[...].astype(F32)) * ob_ref[...].astype(F32)
              + jax.nn.sigmoid(gm_ref[...].astype(F32)) * om)
    y = DN_ALPHA * x_ref[...] + jnp.dot(merged.astype(BF16), wout_ref[...], preferred_element_type=F32)
    x1 = _layernorm(y, l1w_ref[...], l1b_ref[...])
    x1_ref[...] = x1

    x_hi = x1.astype(BF16)
    x_lo = (x1 - x_hi.astype(F32)).astype(BF16)
    hi_part = jnp.dot(x_hi, rw_ref[...], preferred_element_type=F32)
    lo_part = jnp.dot(x_lo, rw_ref[:, :LANES], preferred_element_type=F32)
    logits = hi_part[:, :LANES] + hi_part[:, LANES:] + lo_part + rb_ref[...]
    lane_i = lax.broadcasted_iota(jnp.int32, (tm, LANES), 1)
    lane = lane_i.astype(F32)
    work = logits
    idx, val = [], []
    for _ in range(TOP_K):
        mx = jnp.max(work, axis=-1, keepdims=True)
        ix = jnp.min(jnp.where(work == mx, lane, float(LANES)), axis=-1, keepdims=True)
        idx.append(ix)
        val.append(mx)
        work = jnp.where(lane == ix, -jnp.inf, work)
    ex = [jnp.exp(v - val[0]) for v in val]
    tot = ex[0] + ex[1] + ex[2] + ex[3]
    hot = [jnp.where(lane == ix, 1.0, 0.0) for ix in idx]
    multi = hot[0] + hot[1] + hot[2] + hot[3]
    r = lax.broadcasted_iota(jnp.int32, (tm, tm), 0)
    c = lax.broadcasted_iota(jnp.int32, (tm, tm), 1)
    lower = jnp.where(r > c, 1.0, 0.0).astype(BF16)
    before = jnp.dot(lower, multi.astype(BF16), preferred_element_type=F32) + carry_s[0:1, :]
    ri = jnp.zeros((tm, LANES), F32)
    rg = jnp.zeros((tm, LANES), F32)
    for j in range(TOP_K):
        rank = jnp.sum(before * hot[j], axis=-1, keepdims=True)
        ri = ri + jnp.where(lane == float(j), idx[j], 0.0) + jnp.where(lane == float(TOP_K + j), rank, 0.0)
        rg = rg + jnp.where(lane == float(j), ex[j] / tot, 0.0)
    ri_ref[...] = ri.T[:2 * TOP_K, :].astype(jnp.int32)
    rg_ref[...] = rg
    carry_s[...] = carry_s[...] + jnp.sum(multi, axis=0, keepdims=True)
    cnt_ref[...] = carry_s[...]


def _merge(x2d, proj, o_a, o_b, kvm, wout, l1w, l1b, rw, rb, batch, seq, tm):
    t = batch * seq
    per_b = seq // tm
    nmem = kvm.shape[0] // batch
    tile = lambda colblk: pl.BlockSpec((tm, D_MODEL), lambda i: (i, colblk))
    full = lambda shape: pl.BlockSpec(shape, lambda i: (0, 0))
    return pl.pallas_call(
        functools.partial(_merge_kernel, tm=tm),
        out_shape=(jax.ShapeDtypeStruct((t, D_MODEL), F32),
                   jax.ShapeDtypeStruct((2 * TOP_K, t), jnp.int32),
                   jax.ShapeDtypeStruct((t, LANES), F32),
                   jax.ShapeDtypeStruct((8, LANES), F32)),
        grid=(t // tm,),
        in_specs=[tile(0), tile(COL_GA), tile(COL_GB), tile(COL_GM), tile(COL_MQ), tile(0), tile(0),
                  pl.BlockSpec((nmem, 2 * D_MODEL), lambda i: (i // per_b, 0)),
                  full((D_MODEL, D_MODEL)), full((1, D_MODEL)), full((1, D_MODEL)),
                  full((D_MODEL, 2 * LANES)), full((1, LANES))],
        out_specs=(tile(0), pl.BlockSpec((2 * TOP_K, tm), lambda i: (0, i)),
                   pl.BlockSpec((tm, LANES), lambda i: (i, 0)), full((8, LANES))),
        scratch_shapes=[pltpu.VMEM((8, LANES), F32)],
        compiler_params=_params(("arbitrary",)),
        name="merge_router",
    )(x2d, proj, proj, proj, proj, o_a, o_b, kvm, wout, l1w, l1b, rw, rb)


def _sc_gather(table, idx):
    p, d = idx.shape[0], table.shape[1]
    per_w = p // SC_WORKERS
    assert per_w * SC_WORKERS == p and per_w % SC_GATHER_ROWS == 0
    mesh = plsc.VectorSubcoreMesh(core_axis_name="c", subcore_axis_name="s")

    @functools.partial(
        pl.kernel, mesh=mesh,
        out_type=jax.ShapeDtypeStruct((p, d), table.dtype),
        scratch_types=[pltpu.VMEM((SC_GATHER_ROWS,), jnp.int32),
                       pltpu.VMEM((SC_GATHER_ROWS, d), table.dtype),
                       pltpu.SemaphoreType.DMA],
        name="moe_gather_sc",
    )
    def gather_kernel(table_hbm, idx_hbm, out_hbm, idx_v, rows_v, sem):
        wid = lax.axis_index("s") * SC_CORES + lax.axis_index("c")
        base = wid * per_w

        @pl.loop(0, per_w // SC_GATHER_ROWS)
        def _(c):
            off = base + c * SC_GATHER_ROWS
            pltpu.sync_copy(idx_hbm.at[pl.ds(off, SC_GATHER_ROWS)], idx_v)
            pltpu.async_copy(table_hbm.at[idx_v], rows_v, sem).wait()
            pltpu.sync_copy(rows_v, out_hbm.at[pl.ds(off, SC_GATHER_ROWS)])

    return gather_kernel(table, idx)


def _sc_dispatch(x, dest_t, slots):
    t, d = x.shape
    per_w = t // SC_WORKERS
    assert per_w * SC_WORKERS == t and per_w % SC_GATHER_ROWS == 0
    mesh = plsc.VectorSubcoreMesh(core_axis_name="c", subcore_axis_name="s")

    @functools.partial(
        pl.kernel, mesh=mesh,
        out_type=jax.ShapeDtypeStruct((slots, d), x.dtype),
        scratch_types=[pltpu.VMEM((SC_GATHER_ROWS,), jnp.int32),
                       pltpu.VMEM((SC_GATHER_ROWS, d), x.dtype)],
        name="moe_dispatch_sc",
    )
    def scatter_kernel(x_hbm, idx_hbm, out_hbm, idx_v, rows_v):
        wid = lax.axis_index("s") * SC_CORES + lax.axis_index("c")
        base = wid * per_w

        @pl.loop(0, per_w // SC_GATHER_ROWS)
        def _(c):
            off = base + c * SC_GATHER_ROWS
            pltpu.sync_copy(x_hbm.at[pl.ds(off, SC_GATHER_ROWS)], rows_v)
            for j in range(TOP_K):
                pltpu.sync_copy(idx_hbm.at[pl.ds(j * t + off, SC_GATHER_ROWS)], idx_v)
                pltpu.sync_copy(rows_v, out_hbm.at[idx_v])

    return scatter_kernel(x, dest_t)


HALF_D = D_MODEL // 2
HI16 = -65536


def _pack_bf16_pair(a, b):
    wa = lax.bitcast_convert_type(a.astype(BF16).astype(F32), jnp.int32)
    wb = lax.bitcast_convert_type(b.astype(BF16).astype(F32), jnp.int32)
    return wa | lax.shift_right_logical(wb, 16)


def _unpack_bf16_pair(w):
    a = lax.bitcast_convert_type(w & HI16, F32)
    b = lax.bitcast_convert_type(lax.shift_left(w, 16), F32)
    return a, b


def _expert_kernel(be_ref, nu_ref, xs_ref, wgu_ref, bgu_ref, wd_ref, bd_ref, ys_ref, wgu_b, wd_b):
    i = pl.program_id(0)

    @pl.when(jnp.logical_or(i == 0, be_ref[i] != be_ref[jnp.maximum(i - 1, 0)]))
    def _():
        wgu_b[...] = wgu_ref[0].astype(BF16)
        wd_b[...] = wd_ref[0].astype(BF16)

    @pl.when(i < nu_ref[0])
    def _():
        h = jnp.dot(xs_ref[...].astype(BF16), wgu_b[...], preferred_element_type=F32) + bgu_ref[0]
        g = jnp.minimum(h[:, :D_FF], SWIGLU_LIMIT)
        u = jnp.clip(h[:, D_FF:], -SWIGLU_LIMIT, SWIGLU_LIMIT)
        act = (u + 1.0) * g * jax.nn.sigmoid(SWIGLU_ALPHA * g)
        y = jnp.dot(act.astype(BF16), wd_b[...], preferred_element_type=F32) + bd_ref[0]
        ys_ref[...] = _pack_bf16_pair(y[:, :HALF_D], y[:, HALF_D:])

    @pl.when(i >= nu_ref[0])
    def _():
        ys_ref[...] = jnp.zeros_like(ys_ref)


def _experts(block_e, n_used, xs, wgu, bgu, wd, bd):
    slots = xs.shape[0]
    nb = slots // EXPERT_BLOCK
    grid_spec = pltpu.PrefetchScalarGridSpec(
        num_scalar_prefetch=2,
        grid=(nb,),
        in_specs=[pl.BlockSpec((EXPERT_BLOCK, D_MODEL), lambda i, be, nu: (i, 0)),
                  pl.BlockSpec((1, D_MODEL, 2 * D_FF), lambda i, be, nu: (be[i], 0, 0)),
                  pl.BlockSpec((1, 1, 2 * D_FF), lambda i, be, nu: (be[i], 0, 0)),
                  pl.BlockSpec((1, D_FF, D_MODEL), lambda i, be, nu: (be[i], 0, 0)),
                  pl.BlockSpec((1, 1, D_MODEL), lambda i, be, nu: (be[i], 0, 0))],
        out_specs=pl.BlockSpec((EXPERT_BLOCK, HALF_D), lambda i, be, nu: (i, 0)),
        scratch_shapes=[pltpu.VMEM((D_MODEL, 2 * D_FF), BF16), pltpu.VMEM((D_FF, D_MODEL), BF16)],
    )
    return pl.pallas_call(
        _expert_kernel,
        out_shape=jax.ShapeDtypeStruct((slots, HALF_D), jnp.int32),
        grid_spec=grid_spec,
        compiler_params=_params(("arbitrary",)),
        name="moe_experts",
    )(block_e, n_used, xs, wgu, bgu, wd, bd)


def _weighted_rows(rg, packed):
    lo = hi = None
    for j, w in enumerate(packed):
        a, b = _unpack_bf16_pair(w)
        gj = rg[:, j:j + 1]
        lo = gj * a if lo is None else lo + gj * a
        hi = gj * b if hi is None else hi + gj * b
    return jnp.concatenate([lo, hi], axis=1)


def _combine_kernel(dest_hbm, x1_ref, rg_ref, ys_hbm, ys_tiles_hbm, l2w_ref, l2b_ref, o_ref, idx_s, buf,
                    sem_i, sem_r, *, tm):
    i = pl.program_id(0)
    nsteps = pl.num_programs(0)
    n = TOP_K * tm

    def issue(step, slot):
        cp = pltpu.make_async_copy(dest_hbm.at[pl.ds(pl.multiple_of(step * n, n), n)], idx_s, sem_i)
        cp.start()
        cp.wait()

        def body(g, carry):
            for jj in range(SUBLANES):
                for j in range(TOP_K):
                    d = idx_s[g * (SUBLANES * TOP_K) + jj * TOP_K + j]
                    pltpu.make_async_copy(ys_hbm.at[pl.ds(d, 1)], buf.at[slot, j, g, pl.ds(jj, 1)],
                                          sem_r.at[slot]).start(priority=j % 2)
            return carry

        lax.fori_loop(0, tm // SUBLANES, body, 0)

    @pl.when(i == 0)
    def _():
        issue(0, 0)

    @pl.when(i + 1 < nsteps)
    def _():
        issue(i + 1, (i + 1) % 2)

    slot = i % 2
    for j in range(TOP_K):
        pltpu.make_async_copy(ys_tiles_hbm.at[pl.ds(0, tm // SUBLANES)], buf.at[slot, j], sem_r.at[slot]).wait()
    moe = _weighted_rows(rg_ref[...], [buf[slot, j].reshape(tm, HALF_D) for j in range(TOP_K)])
    o_ref[...] = _layernorm(DN_ALPHA * x1_ref[...] + moe, l2w_ref[...], l2b_ref[...])


def _combine_rows_kernel(x1_ref, rg_ref, g0_ref, g1_ref, g2_ref, g3_ref, l2w_ref, l2b_ref, *rest):
    o_ref = rest[-1]
    moe = _weighted_rows(rg_ref[...], [g0_ref[...], g1_ref[...], g2_ref[...], g3_ref[...]])
    o_ref[...] = _layernorm(DN_ALPHA * x1_ref[...] + moe, l2w_ref[...], l2b_ref[...])


def _combine_rows(x1, rg, rows, l2w, l2b, tm, y_rest):
    t = x1.shape[0]
    nt = rows.shape[0] // TOP_K // tm
    full = lambda shape: pl.BlockSpec(shape, lambda i: (0, 0))
    choice = lambda j: pl.BlockSpec((tm, HALF_D), lambda i, j=j: (j * nt + i, 0))
    in_specs = [pl.BlockSpec((tm, D_MODEL), lambda i: (i, 0)),
                pl.BlockSpec((tm, LANES), lambda i: (i, 0)),
                choice(0), choice(1), choice(2), choice(3),
                full((1, D_MODEL)), full((1, D_MODEL))]
    args = [x1, rg, rows, rows, rows, rows, l2w, l2b]
    aliases = {}
    if y_rest is not None:
        in_specs.append(pl.BlockSpec(memory_space=pl.ANY))
        args.append(y_rest)
        aliases = {len(args) - 1: 0}
    return pl.pallas_call(
        _combine_rows_kernel,
        out_shape=jax.ShapeDtypeStruct((t, D_MODEL), F32),
        grid=(nt,),
        in_specs=in_specs,
        out_specs=pl.BlockSpec((tm, D_MODEL), lambda i: (i, 0)),
        input_output_aliases=aliases,
        compiler_params=_params(("arbitrary",)),
        name="moe_combine_rows",
    )(*args)


def _combine(dest, x1, rg, ys, l2w, l2b, tm, tok0):
    t = x1.shape[0]
    first = tok0 // tm
    full = lambda shape: pl.BlockSpec(shape, lambda i: (0, 0))
    return pl.pallas_call(
        functools.partial(_combine_kernel, tm=tm),
        out_shape=jax.ShapeDtypeStruct((t, D_MODEL), F32),
        grid=((t - tok0) // tm,),
        in_specs=[pl.BlockSpec(memory_space=pl.ANY),
                  pl.BlockSpec((tm, D_MODEL), lambda i: (first + i, 0)),
                  pl.BlockSpec((tm, LANES), lambda i: (first + i, 0)),
                  pl.BlockSpec(memory_space=pl.ANY), pl.BlockSpec(memory_space=pl.ANY),
                  full((1, D_MODEL)), full((1, D_MODEL))],
        out_specs=pl.BlockSpec((tm, D_MODEL), lambda i: (first + i, 0)),
        scratch_shapes=[pltpu.SMEM((TOP_K * tm,), jnp.int32),
                        pltpu.VMEM((2, TOP_K, tm // SUBLANES, SUBLANES, HALF_D), jnp.int32),
                        pltpu.SemaphoreType.DMA, pltpu.SemaphoreType.DMA((2,))],
        compiler_params=_params(("arbitrary",)),
        name="moe_combine",
    )(dest, x1, rg, ys, ys.reshape(ys.shape[0] // SUBLANES, SUBLANES, HALF_D), l2w, l2b)


def _prep_w_in(w):
    zeros = lambda n: jnp.zeros((D_MODEL, n), w.dtype)
    kr = w[:, 5760:5824]
    kr_sw = jnp.concatenate([kr[:, MLA_ROPE // 2:], kr[:, :MLA_ROPE // 2]], axis=1)
    parts = [w[:, 6848:9920], w[:, 0:5120], w[:, 5824:6848],
             w[:, 5120:5504], zeros(128),
             w[:, 5504:5760], kr, zeros(64), kr_sw, zeros(64)]
    return jnp.concatenate(parts, axis=1).astype(BF16)


def _prep_w_uq(w):
    w3 = w.reshape(MLA_Q_LORA, MLA_HEADS, MLA_NOPE + MLA_ROPE)
    rope = w3[:, :, MLA_NOPE:]
    rope_sw = jnp.concatenate([rope[:, :, MLA_ROPE // 2:], rope[:, :, :MLA_ROPE // 2]], axis=-1)
    pad = jnp.zeros((MLA_Q_LORA, MLA_HEADS, QK_PAD - MLA_NOPE - MLA_ROPE), w.dtype)
    main = jnp.concatenate([w3, pad], axis=-1).reshape(MLA_Q_LORA, MLA_HEADS * QK_PAD)
    swp = jnp.concatenate([rope_sw, pad], axis=-1).reshape(MLA_Q_LORA, MLA_HEADS * LANES)
    both = jnp.concatenate([main, swp], axis=1)
    return jnp.pad(both, ((0, 512 - MLA_Q_LORA), (0, 0))).astype(BF16)


def _rope_tables(seq):
    inv_freq = ROPE_THETA ** (-jnp.arange(0, MLA_ROPE, 2, dtype=F32) / MLA_ROPE)
    ang = jnp.arange(seq, dtype=F32)[:, None] * inv_freq[None, :]
    cos, sin = jnp.cos(ang), jnp.sin(ang)
    pad = jnp.zeros((seq, LANES - MLA_ROPE), F32)
    return (jnp.concatenate([cos, cos, pad], axis=1), jnp.concatenate([-sin, sin, pad], axis=1))


def _split_bf16(w):
    hi = w.astype(BF16)
    lo = (w - hi.astype(F32)).astype(BF16)
    return jnp.concatenate([hi, lo], axis=1)


def _tile(n, pref):
    return pref if n % pref == 0 else n


def _layer(x, mem, wts, sc_share):
    batch, seq, _ = x.shape
    t = batch * seq
    x2d = x.reshape(t, D_MODEL)

    proj = _matmul(x2d, wts["w_in"], BF16, _tile(t, 1024), 2048, "in_proj")
    o_a = _hgrn(proj, wts["lb"], wts["hgrn_norm_w"], batch, seq, _tile(seq, 512))

    cos, sin = _rope_tables(seq)
    q, k, v = _mla_prep(proj, wts["qnw"], wts["kvnw"], wts["w_uq"], wts["w_uk"], wts["w_uv"],
                        cos, sin, batch, seq, _tile(seq, 512))
    o_b = _flash(q, k, v, batch, seq, _tile(seq, 1024), _tile(seq, 1024))

    nmem = mem.shape[1]
    kvm = _matmul(mem.reshape(batch * nmem, D_MODEL), wts["mem_w_kv"], BF16,
                  _tile(batch * nmem, 512), 1024, "mem_kv")

    tm = _tile(seq, 512)
    x1, ri, rg, cnt = _merge(x2d, proj, o_a, o_b, kvm, wts["w_out"], wts["ln1_w"], wts["ln1_b"],
                             wts["router_w"], wts["router_b"], batch, seq, tm)

    counts = cnt[0, :N_EXPERTS].astype(jnp.int32)
    padded = (counts + EXPERT_BLOCK - 1) // EXPERT_BLOCK * EXPERT_BLOCK
    pad_end = jnp.cumsum(padded)
    pad_start = pad_end - padded
    experts = jnp.arange(N_EXPERTS, dtype=jnp.int32)[:, None, None]
    start_ct = jnp.sum(jnp.where(ri[None, :TOP_K] == experts, pad_start[:, None, None], 0), axis=0)
    dest_ct = (start_ct + ri[TOP_K:]).astype(jnp.int32)
    nb = t * TOP_K // EXPERT_BLOCK + N_EXPERTS
    blk_start = jnp.arange(nb, dtype=jnp.int32) * EXPERT_BLOCK
    block_e = jnp.minimum(jnp.sum((pad_end[None, :] <= blk_start[:, None]).astype(jnp.int32), axis=1),
                          N_EXPERTS - 1).astype(jnp.int32)
    n_used = (pad_end[-1:] // EXPERT_BLOCK).astype(jnp.int32)
    xs = _sc_dispatch(x1, dest_ct.reshape(TOP_K * t), nb * EXPERT_BLOCK)
    ys = _experts(block_e, n_used, xs, wts["exp_w_gu"], wts["exp_b_gu"], wts["exp_w_down"],
                        wts["exp_b_down"])
    n_sc = int(t * sc_share) // 2048 * 2048
    y_rest = None
    if n_sc < t:
        dest_tail = dest_ct[:, n_sc:].T.reshape(TOP_K * (t - n_sc))
        y_rest = _combine(dest_tail, x1, rg, ys, wts["ln2_w"], wts["ln2_b"], _tile(t, 256), n_sc)
    if n_sc > 0:
        rows = _sc_gather(ys, dest_ct[:, :n_sc].reshape(TOP_K * n_sc))
        y = _combine_rows(x1, rg, rows, wts["ln2_w"], wts["ln2_b"], 512, y_rest)
    else:
        y = y_rest
    return y.reshape(batch, seq, D_MODEL)


def kernel(x_prompt, x_sample, mem_prompt, mem_sample, w_in, hgrn_lb_logits, hgrn_norm_w,
           mla_q_norm_w, mla_w_uq, mla_kv_norm_w, mla_w_uk, mla_w_uv, mem_w_kv, w_out,
           ln1_w, ln1_b, router_w, router_b, exp_w_gu, exp_b_gu, exp_w_down, exp_b_down,
           ln2_w, ln2_b):
    depth = w_in.shape[0]
    gamma = jax.nn.softmax(hgrn_lb_logits.astype(F32), axis=1)
    cum = jnp.cumsum(gamma, axis=1)
    lb_all = cum[:, 1:] - cum[:, :1]
    y_prompt, y_sample = x_prompt, x_sample
    for l in range(depth):
        row = lambda a: a[l].reshape(1, -1).astype(F32)
        wts = {
            "w_in": _prep_w_in(w_in[l]),
            "lb": lb_all[:, l],
            "hgrn_norm_w": hgrn_norm_w[l].astype(F32),
            "qnw": jnp.pad(row(mla_q_norm_w), ((0, 0), (0, 512 - MLA_Q_LORA))),
            "kvnw": row(mla_kv_norm_w),
            "w_uq": _prep_w_uq(mla_w_uq[l]),
            "w_uk": mla_w_uk[l].astype(BF16),
            "w_uv": mla_w_uv[l].astype(BF16),
            "mem_w_kv": mem_w_kv[l].astype(BF16),
            "w_out": w_out[l].astype(BF16),
            "ln1_w": row(ln1_w), "ln1_b": row(ln1_b),
            "router_w": _split_bf16(jnp.pad(router_w[l].astype(F32), ((0, 0), (0, LANES - N_EXPERTS)))),
            "router_b": jnp.pad(row(router_b), ((0, 0), (0, LANES - N_EXPERTS)), constant_values=-jnp.inf),
            "exp_w_gu": exp_w_gu[l].astype(F32),
            "exp_b_gu": exp_b_gu[l].reshape(N_EXPERTS, 1, 2 * D_FF).astype(F32),
            "exp_w_down": exp_w_down[l].astype(F32),
            "exp_b_down": exp_b_down[l].reshape(N_EXPERTS, 1, D_MODEL).astype(F32),
            "ln2_w": row(ln2_w), "ln2_b": row(ln2_b),
        }
        y_prompt = _layer(y_prompt, mem_prompt, wts, sc_share=1.0)
        y_sample = _layer(y_sample, mem_sample, wts, sc_share=0.875)
    return (y_prompt, y_sample)
```

```python
import functools

import numpy as np
import jax
import jax.numpy as jnp
from jax import lax
from jax.experimental import pallas as pl
from jax.experimental.pallas import tpu as pltpu
from jax.experimental.pallas import tpu_sc as plsc

F32 = jnp.float32
BF16 = jnp.bfloat16

D_MODEL = 1024
HGRN_HEADS = 8
HGRN_DK = 128
MLA_HEADS = 8
MLA_Q_LORA = 384
MLA_KV_LORA = 256
MLA_NOPE = 128
MLA_ROPE = 64
MLA_V = 128
ROPE_THETA = 10000.0
MEM_HEADS = 4
MEM_HEAD_DIM = D_MODEL // MEM_HEADS
N_EXPERTS = 32
TOP_K = 4
D_FF = D_MODEL
SWIGLU_LIMIT = 7.0
SWIGLU_ALPHA = 1.702
DN_ALPHA = 2.0 ** 0.25
LN_EPS = 1e-5
RMS_EPS = 1e-6

LANES = 128
SUBLANES = 8
QK_PAD = 256
V_PAD = 256
LOG2E = 1.4426950408889634
HGRN_CHUNK = 64
HGRN_SAFE_RANGE = 160.0
EXPERT_BLOCK = 512
SC_CORES = 2
SC_SUBCORES = 16
SC_WORKERS = SC_CORES * SC_SUBCORES
SC_GATHER_ROWS = 64
VMEM_LIMIT = 56 * 1024 * 1024

COL_GA, COL_GB, COL_GM, COL_Q, COL_ZF, COL_ZB, COL_I, COL_G, COL_MQ = range(9)
COL_DQ, COL_DKV = 18, 19
IN_COLS_PAD = 10240

NT_DIMS = (((1,), (1,)), ((), ()))
TN_DIMS = (((0,), (0,)), ((), ()))


def _params(sem, vmem=VMEM_LIMIT):
    return pltpu.CompilerParams(dimension_semantics=sem, vmem_limit_bytes=vmem)


def _mm_kernel(x_ref, w_ref, o_ref, xb_ref):
    @pl.when(pl.program_id(1) == 0)
    def _():
        xb_ref[...] = x_ref[...].astype(BF16)

    o_ref[...] = jnp.dot(xb_ref[...], w_ref[...], preferred_element_type=F32).astype(o_ref.dtype)


def _matmul(x, w, out_dtype, tm, tn, name):
    m, k = x.shape
    n = w.shape[1]
    return pl.pallas_call(
        _mm_kernel,
        out_shape=jax.ShapeDtypeStruct((m, n), out_dtype),
        grid=(m // tm, n // tn),
        in_specs=[pl.BlockSpec((tm, k), lambda i, j: (i, 0)),
                  pl.BlockSpec((k, tn), lambda i, j: (0, j))],
        out_specs=pl.BlockSpec((tm, tn), lambda i, j: (i, j)),
        scratch_shapes=[pltpu.VMEM((tm, k), BF16)],
        compiler_params=_params(("arbitrary", "arbitrary")),
        name=name,
    )(x, w)


def _hgrn_kernel(*refs, reverse, epilogue, sc):
    if epilogue:
        lb_ref, q_ref, z_ref, v_ref, of_ref, g_ref, nw_ref, o_ref, st_ref, kk_s, b_s, vf_s, os_s = refs
    else:
        lb_ref, q_ref, z_ref, v_ref, o_ref, st_ref, kk_s, b_s, vf_s = refs
    C = HGRN_CHUNK
    nch = sc // C

    @pl.when(pl.program_id(1) == 0)
    def _():
        st_ref[...] = jnp.zeros_like(st_ref)

    lb = lb_ref[...]
    row = lax.broadcasted_iota(jnp.int32, (C, C), 0)
    col = lax.broadcasted_iota(jnp.int32, (C, C), 1)
    tri = (row <= col) if reverse else (row >= col)
    trib = jnp.where(tri, 1.0, 0.0).astype(BF16)

    minb = None
    for c in range(nch):
        rows = slice(c * C, (c + 1) * C)
        z = z_ref[rows, :].astype(F32)
        gate = (1.0 - lb) * jax.nn.sigmoid(z)
        lf = jnp.log(lb + gate)
        kk_s[rows, :] = (1.0 - lb) - gate
        hi = lf.astype(BF16)
        r1 = lf - hi.astype(F32)
        mid = r1.astype(BF16)
        lo = (r1 - mid.astype(F32)).astype(BF16)
        b = (jnp.dot(trib, hi, preferred_element_type=F32)
             + jnp.dot(trib, mid, preferred_element_type=F32)
             + jnp.dot(trib, lo, preferred_element_type=F32))
        b_s[rows, :] = b
        mb = jnp.min(b)
        minb = mb if minb is None else jnp.minimum(minb, mb)

    rid = lax.broadcasted_iota(jnp.int32, (C, 1), 0)

    def chunk(i, carry, fast):
        c = (nch - 1 - i) if reverse else i
        r0 = pl.multiple_of(c * C, C)
        rows = pl.ds(r0, C)
        for h in range(HGRN_HEADS):
            cols = slice(h * HGRN_DK, (h + 1) * HGRN_DK)
            q = q_ref[rows, cols].astype(F32)
            kk = kk_s[rows, cols]
            b = b_s[rows, cols]
            v = v_ref[rows, cols]
            bl = b[0:1, :] if reverse else b[C - 1:C, :]
            if fast:
                bm = 0.5 * bl
                qd = (q * jnp.exp(b - bm)).astype(BF16)
                kd = (kk * jnp.exp(bm - b)).astype(BF16)
                s = lax.dot_general(qd, kd, NT_DIMS, preferred_element_type=F32)
                s = jnp.where(tri, s, 0.0).astype(BF16)
                o = jnp.dot(s, v, preferred_element_type=F32)
            else:
                def sbody(g_i, o_acc):
                    grp = pl.ds(pl.multiple_of(r0 + g_i * SUBLANES, SUBLANES), SUBLANES)
                    b8 = b_s[grp, cols]
                    k8 = kk_s[grp, cols]
                    v8 = vf_s[grp, cols]
                    for jj in range(SUBLANES):
                        s_i = g_i * SUBLANES + jj
                        w = q * k8[jj:jj + 1, :] * jnp.exp(jnp.minimum(b - b8[jj:jj + 1, :], 0.0))
                        scol = jnp.sum(w, axis=-1, keepdims=True)
                        keep = (rid <= s_i) if reverse else (rid >= s_i)
                        o_acc = o_acc + jnp.where(keep, scol, 0.0) * v8[jj:jj + 1, :]
                    return o_acc
                o = lax.fori_loop(0, C // SUBLANES, sbody, jnp.zeros((C, HGRN_DK), F32))
            st = st_ref[h]
            qi = (q * jnp.exp(b)).astype(BF16)
            o = o + lax.dot_general(qi, st.astype(BF16), NT_DIMS, preferred_element_type=F32)
            ke = (kk * jnp.exp(bl - b)).astype(BF16)
            upd = lax.dot_general(v, ke, TN_DIMS, preferred_element_type=F32)
            st_ref[h] = st * jnp.exp(bl) + upd
            if epilogue:
                os_s[rows, cols] = o + of_ref[rows, cols].astype(F32)
            else:
                o_ref[rows, cols] = o.astype(o_ref.dtype)
        return carry

    safe = minb >= -HGRN_SAFE_RANGE

    @pl.when(safe)
    def _():
        lax.fori_loop(0, nch, functools.partial(chunk, fast=True), 0, unroll=2)

    @pl.when(jnp.logical_not(safe))
    def _():
        vf_s[...] = v_ref[...].astype(F32)
        lax.fori_loop(0, nch, functools.partial(chunk, fast=False), 0)

    if epilogue:
        nw = nw_ref[...]
        for h in range(HGRN_HEADS):
            cols = slice(h * HGRN_DK, (h + 1) * HGRN_DK)
            os = os_s[:, cols]
            ms = jnp.mean(os * os, axis=-1, keepdims=True)
            y = os * lax.rsqrt(ms + RMS_EPS) * nw
            g = g_ref[:, cols].astype(F32)
            o_ref[:, cols] = (y * (g * jax.nn.sigmoid(g))).astype(o_ref.dtype)


def _hgrn(proj, lb, norm_w, batch, seq, sc):
    t = batch * seq
    ns = seq // sc
    blk = (sc, D_MODEL)

    def spec(colblk, reverse):
        if reverse:
            return pl.BlockSpec(blk, lambda b, n: (b * ns + ns - 1 - n, colblk))
        return pl.BlockSpec(blk, lambda b, n: (b * ns + n, colblk))

    def row_spec(reverse):
        if reverse:
            return pl.BlockSpec(blk, lambda b, n: (b * ns + ns - 1 - n, 0))
        return pl.BlockSpec(blk, lambda b, n: (b * ns + n, 0))

    lb_spec = pl.BlockSpec((1, D_MODEL), lambda b, n: (0, 0))
    common_scratch = [pltpu.VMEM((HGRN_HEADS, HGRN_DK, HGRN_DK), F32),
                      pltpu.VMEM(blk, F32), pltpu.VMEM(blk, F32), pltpu.VMEM(blk, F32)]
    o_f = pl.pallas_call(
        functools.partial(_hgrn_kernel, reverse=False, epilogue=False, sc=sc),
        out_shape=jax.ShapeDtypeStruct((t, D_MODEL), BF16),
        grid=(batch, ns),
        in_specs=[lb_spec, spec(COL_Q, False), spec(COL_ZF, False), spec(COL_I, False)],
        out_specs=row_spec(False),
        scratch_shapes=common_scratch,
        compiler_params=_params(("arbitrary", "arbitrary")),
        name="hgrn_fwd",
    )(lb[0:1], proj, proj, proj)
    o_a = pl.pallas_call(
        functools.partial(_hgrn_kernel, reverse=True, epilogue=True, sc=sc),
        out_shape=jax.ShapeDtypeStruct((t, D_MODEL), BF16),
        grid=(batch, ns),
        in_specs=[lb_spec, spec(COL_Q, True), spec(COL_ZB, True), spec(COL_I, True),
                  row_spec(True), spec(COL_G, True),
                  pl.BlockSpec((1, HGRN_DK), lambda b, n: (0, 0))],
        out_specs=row_spec(True),
        scratch_shapes=common_scratch + [pltpu.VMEM(blk, F32)],
        compiler_params=_params(("arbitrary", "arbitrary")),
        name="hgrn_bwd",
    )(lb[1:2], proj, proj, proj, o_f, proj, norm_w.reshape(1, HGRN_DK))
    return o_a


def _mla_prep_kernel(dq_ref, dkv_ref, qnw_ref, kvnw_ref, wq_ref, wk_ref, wv_ref, cos_ref, sin_ref,
                     q_ref, k_ref, v_ref):
    scale = (MLA_NOPE + MLA_ROPE) ** -0.5 * LOG2E
    cos = cos_ref[...]
    sin = sin_ref[...]
    dq = dq_ref[...].astype(F32)
    ms = jnp.sum(dq * dq, axis=-1, keepdims=True) * (1.0 / MLA_Q_LORA)
    cq = (dq * lax.rsqrt(ms + RMS_EPS) * qnw_ref[...]).astype(BF16)
    qa = jnp.dot(cq, wq_ref[...], preferred_element_type=F32)
    for h in range(MLA_HEADS):
        base = h * QK_PAD
        q_ref[:, base:base + MLA_NOPE] = (qa[:, base:base + MLA_NOPE] * scale).astype(BF16)
        rp = qa[:, base + MLA_NOPE:base + QK_PAD]
        sw = qa[:, MLA_HEADS * QK_PAD + h * LANES:MLA_HEADS * QK_PAD + (h + 1) * LANES]
        q_ref[:, base + MLA_NOPE:base + QK_PAD] = ((rp * cos + sw * sin) * scale).astype(BF16)
    dkv = dkv_ref[...].astype(F32)
    ckv = dkv[:, :MLA_KV_LORA]
    msk = jnp.mean(ckv * ckv, axis=-1, keepdims=True)
    cn = (ckv * lax.rsqrt(msk + RMS_EPS) * kvnw_ref[...]).astype(BF16)
    kn = jnp.dot(cn, wk_ref[...], preferred_element_type=F32)
    vv = jnp.dot(cn, wv_ref[...], preferred_element_type=F32).astype(BF16)
    ones_col = jnp.where(lax.broadcasted_iota(jnp.int32, (vv.shape[0], LANES), 1) == 0, 1.0, 0.0).astype(BF16)
    for h in range(MLA_HEADS):
        v_ref[:, h * V_PAD:h * V_PAD + MLA_V] = vv[:, h * MLA_V:(h + 1) * MLA_V]
        v_ref[:, h * V_PAD + MLA_V:(h + 1) * V_PAD] = ones_col
    kr = (dkv[:, MLA_KV_LORA:MLA_KV_LORA + LANES] * cos
          + dkv[:, MLA_KV_LORA + LANES:MLA_KV_LORA + 2 * LANES] * sin).astype(BF16)
    for h in range(MLA_HEADS):
        base = h * QK_PAD
        k_ref[:, base:base + MLA_NOPE] = kn[:, h * MLA_NOPE:(h + 1) * MLA_NOPE].astype(BF16)
        k_ref[:, base + MLA_NOPE:base + QK_PAD] = kr


def _mla_prep(proj, qnw, kvnw, wq, wk, wv, cos, sin, batch, seq, tm):
    t = batch * seq
    npos = seq // tm
    full = lambda shape: pl.BlockSpec(shape, lambda i: (0, 0))
    return pl.pallas_call(
        _mla_prep_kernel,
        out_shape=(jax.ShapeDtypeStruct((t, MLA_HEADS * QK_PAD), BF16),
                   jax.ShapeDtypeStruct((t, MLA_HEADS * QK_PAD), BF16),
                   jax.ShapeDtypeStruct((t, MLA_HEADS * V_PAD), BF16)),
        grid=(t // tm,),
        in_specs=[pl.BlockSpec((tm, 512), lambda i: (i, COL_DQ)),
                  pl.BlockSpec((tm, 512), lambda i: (i, COL_DKV)),
                  full((1, 512)), full((1, MLA_KV_LORA)),
                  full(wq.shape), full(wk.shape), full(wv.shape),
                  pl.BlockSpec((tm, LANES), lambda i: (i % npos, 0)),
                  pl.BlockSpec((tm, LANES), lambda i: (i % npos, 0))],
        out_specs=(pl.BlockSpec((tm, MLA_HEADS * QK_PAD), lambda i: (i, 0)),
                   pl.BlockSpec((tm, MLA_HEADS * QK_PAD), lambda i: (i, 0)),
                   pl.BlockSpec((tm, MLA_HEADS * V_PAD), lambda i: (i, 0))),
        compiler_params=_params(("arbitrary",)),
        name="mla_prep",
    )(proj, proj, qnw, kvnw, wq, wk, wv, cos, sin)


def _flash_kernel(q_ref, k_ref, v_ref, o_ref, m_s, acc_s, s_buf, *, bk, nk):
    m_s[...] = jnp.full(m_s.shape, -jnp.inf, F32)
    acc_s[...] = jnp.zeros(acc_s.shape, F32)
    nt = bk // LANES
    bq = q_ref.shape[0]
    nsplit = 2 if bq % 256 == 0 else 1
    hq = bq // nsplit
    qrs = [slice(hf * hq, (hf + 1) * hq) for hf in range(nsplit)]

    def scores(j, slot):
        rows = pl.ds(pl.multiple_of(j * bk, bk), bk)
        for qr in qrs:
            s_buf[slot, qr, :] = lax.dot_general(q_ref[qr, :], k_ref[rows, :], NT_DIMS,
                                                 preferred_element_type=F32)

    def consume(j, slot):
        rows = pl.ds(pl.multiple_of(j * bk, bk), bk)
        ps, alphas = [], []
        for qr in qrs:
            tiles = [s_buf[slot, qr, t * LANES:(t + 1) * LANES] for t in range(nt)]
            tmax = tiles[0]
            for t in range(1, nt):
                tmax = jnp.maximum(tmax, tiles[t])
            m_prev = m_s[qr, :]
            m_new = jnp.maximum(m_prev, jnp.max(tmax, axis=-1, keepdims=True))
            alphas.append(jnp.exp2(m_prev - m_new))
            ps.append(jnp.concatenate([jnp.exp2(tl - m_new).astype(BF16) for tl in tiles], axis=1))
            m_s[qr, :] = m_new
        for qr, p, a in zip(qrs, ps, alphas):
            pv = jnp.dot(p, v_ref[rows, :], preferred_element_type=F32)
            acc_s[qr, :] = jnp.concatenate([a, a], axis=1) * acc_s[qr, :] + pv

    scores(0, 0)
    npairs = (nk - 1) // 2

    def body(i, carry):
        j = 2 * i
        scores(j + 1, 1)
        consume(j, 0)
        scores(j + 2, 0)
        consume(j + 1, 1)
        return carry

    lax.fori_loop(0, npairs, body, 0)
    if nk - 2 * npairs == 2:
        scores(nk - 1, 1)
        consume(nk - 2, 0)
        consume(nk - 1, 1)
    else:
        consume(nk - 1, 0)
    acc = acc_s[...]
    o_ref[...] = (acc[:, :MLA_V] / acc[:, MLA_V:MLA_V + 1]).astype(o_ref.dtype)


def _flash(q, k, v, batch, seq, bq, bk):
    t = batch * seq
    nq = seq // bq
    return pl.pallas_call(
        functools.partial(_flash_kernel, bk=bk, nk=seq // bk),
        out_shape=jax.ShapeDtypeStruct((t, MLA_HEADS * MLA_V), BF16),
        grid=(batch, MLA_HEADS, nq),
        in_specs=[pl.BlockSpec((bq, QK_PAD), lambda b, h, i: (b * nq + i, h)),
                  pl.BlockSpec((seq, QK_PAD), lambda b, h, i: (b, h)),
                  pl.BlockSpec((seq, V_PAD), lambda b, h, i: (b, h))],
        out_specs=pl.BlockSpec((bq, MLA_V), lambda b, h, i: (b * nq + i, h)),
        scratch_shapes=[pltpu.VMEM((bq, LANES), F32), pltpu.VMEM((bq, V_PAD), F32),
                        pltpu.VMEM((2, bq, bk), F32)],
        compiler_params=_params(("arbitrary", "arbitrary", "arbitrary")),
        name="mla_flash",
    )(q, k, v)


def _layernorm(y, w, b):
    mu = jnp.mean(y, axis=-1, keepdims=True)
    yc = y - mu
    var = jnp.mean(yc * yc, axis=-1, keepdims=True)
    return yc * lax.rsqrt(var + LN_EPS) * w + b


def _merge_kernel(x_ref, ga_ref, gb_ref, gm_ref, mq_ref, oa_ref, ob_ref, kvm_ref, wout_ref,
                  l1w_ref, l1b_ref, rw_ref, rb_ref,
                  x1_ref, ri_ref, rg_ref, cnt_ref, carry_s, *, tm):
    @pl.when(pl.program_id(0) == 0)
    def _():
        carry_s[...] = jnp.zeros_like(carry_s)

    parts = []
    for h in range(MEM_HEADS):
        cols = slice(h * MEM_HEAD_DIM, (h + 1) * MEM_HEAD_DIM)
        kh = kvm_ref[:, cols]
        vh = kvm_ref[:, D_MODEL + h * MEM_HEAD_DIM:D_MODEL + (h + 1) * MEM_HEAD_DIM]
        s = lax.dot_general(mq_ref[:, cols], kh, NT_DIMS, preferred_element_type=F32) * (MEM_HEAD_DIM ** -0.5)
        s = s - jnp.max(s, axis=-1, keepdims=True)
        p = jnp.exp(s)
        p = p / jnp.sum(p, axis=-1, keepdims=True)
        parts.append(jnp.dot(p.astype(BF16), vh, preferred_element_type=F32))
    om = jnp.concatenate(parts, axis=1)

    merged = (jax.nn.sigmoid(ga_ref[...].astype(F32)) * oa_ref[...].astype(F32)
              + jax.nn.sigmoid(gb_ref[...].astype(F32)) * ob_ref[...].astype(F32)
              + jax.nn.sigmoid(gm_ref[...].astype(F32)) * om)
    y = DN_ALPHA * x_ref[...] + jnp.dot(merged.astype(BF16), wout_ref[...], preferred_element_type=F32)
    x1 = _layernorm(y, l1w_ref[...], l1b_ref[...])
    x1_ref[...] = x1

    x_hi = x1.astype(BF16)
    x_lo = (x1 - x_hi.astype(F32)).astype(BF16)
    hi_part = jnp.dot(x_hi, rw_ref[...], preferred_element_type=F32)
    lo_part = jnp.dot(x_lo, rw_ref[:, :LANES], preferred_element_type=F32)
    logits = hi_part[:, :LANES] + hi_part[:, LANES:] + lo_part + rb_ref[...]
    lane_i = lax.broadcasted_iota(jnp.int32, (tm, LANES), 1)
    lane = lane_i.astype(F32)
    work = logits
    idx, val = [], []
    for _ in range(TOP_K):
        mx = jnp.max(work, axis=-1, keepdims=True)
        ix = jnp.min(jnp.where(work == mx, lane, float(LANES)), axis=-1, keepdims=True)
        idx.append(ix)
        val.append(mx)
        work = jnp.where(lane == ix, -jnp.inf, work)
    ex = [jnp.exp(v - val[0]) for v in val]
    tot = ex[0] + ex[1] + ex[2] + ex[3]
    hot = [jnp.where(lane == ix, 1.0, 0.0) for ix in idx]
    multi = hot[0] + hot[1] + hot[2] + hot[3]
    r = lax.broadcasted_iota(jnp.int32, (tm, tm), 0)
    c = lax.broadcasted_iota(jnp.int32, (tm, tm), 1)
    lower = jnp.where(r > c, 1.0, 0.0).astype(BF16)
    before = jnp.dot(lower, multi.astype(BF16), preferred_element_type=F32) + carry_s[0:1, :]
    ri = jnp.zeros((tm, LANES), F32)
    rg = jnp.zeros((tm, LANES), F32)
    for j in range(TOP_K):
        rank = jnp.sum(before * hot[j], axis=-1, keepdims=True)
        ri = ri + jnp.where(lane == float(j), idx[j], 0.0) + jnp.where(lane == float(TOP_K + j), rank, 0.0)
        rg = rg + jnp.where(lane == float(j), ex[j] / tot, 0.0)
    ri_ref[...] = ri.T[:2 * TOP_K, :].astype(jnp.int32)
    rg_ref[...] = rg
    carry_s[...] = carry_s[...] + jnp.sum(multi, axis=0, keepdims=True)
    cnt_ref[...] = carry_s[...]


def _merge(x2d, proj, o_a, o_b, kvm, wout, l1w, l1b, rw, rb, batch, seq, tm):
    t = batch * seq
    per_b = seq // tm
    nmem = kvm.shape[0] // batch
    tile = lambda colblk: pl.BlockSpec((tm, D_MODEL), lambda i: (i, colblk))
    full = lambda shape: pl.BlockSpec(shape, lambda i: (0, 0))
    return pl.pallas_call(
        functools.partial(_merge_kernel, tm=tm),
        out_shape=(jax.ShapeDtypeStruct((t, D_MODEL), F32),
                   jax.ShapeDtypeStruct((2 * TOP_K, t), jnp.int32),
                   jax.ShapeDtypeStruct((t, LANES), F32),
                   jax.ShapeDtypeStruct((8, LANES), F32)),
        grid=(t // tm,),
        in_specs=[tile(0), tile(COL_GA), tile(COL_GB), tile(COL_GM), tile(COL_MQ), tile(0), tile(0),
                  pl.BlockSpec((nmem, 2 * D_MODEL), lambda i: (i // per_b, 0)),
                  full((D_MODEL, D_MODEL)), full((1, D_MODEL)), full((1, D_MODEL)),
                  full((D_MODEL, 2 * LANES)), full((1, LANES))],
        out_specs=(tile(0), pl.BlockSpec((2 * TOP_K, tm), lambda i: (0, i)),
                   pl.BlockSpec((tm, LANES), lambda i: (i, 0)), full((8, LANES))),
        scratch_shapes=[pltpu.VMEM((8, LANES), F32)],
        compiler_params=_params(("arbitrary",)),
        name="merge_router",
    )(x2d, proj, proj, proj, proj, o_a, o_b, kvm, wout, l1w, l1b, rw, rb)


def _sc_gather(table, idx):
    p, d = idx.shape[0], table.shape[1]
    per_w = p // SC_WORKERS
    assert per_w * SC_WORKERS == p and per_w % SC_GATHER_ROWS == 0
    mesh = plsc.VectorSubcoreMesh(core_axis_name="c", subcore_axis_name="s")

    @functools.partial(
        pl.kernel, mesh=mesh,
        out_type=jax.ShapeDtypeStruct((p, d), table.dtype),
        scratch_types=[pltpu.VMEM((SC_GATHER_ROWS,), jnp.int32),
                       pltpu.VMEM((SC_GATHER_ROWS, d), table.dtype),
                       pltpu.SemaphoreType.DMA],
        name="moe_gather_sc",
    )
    def gather_kernel(table_hbm, idx_hbm, out_hbm, idx_v, rows_v, sem):
        wid = lax.axis_index("s") * SC_CORES + lax.axis_index("c")
        base = wid * per_w

        @pl.loop(0, per_w // SC_GATHER_ROWS)
        def _(c):
            off = base + c * SC_GATHER_ROWS
            pltpu.sync_copy(idx_hbm.at[pl.ds(off, SC_GATHER_ROWS)], idx_v)
            pltpu.async_copy(table_hbm.at[idx_v], rows_v, sem).wait()
            pltpu.sync_copy(rows_v, out_hbm.at[pl.ds(off, SC_GATHER_ROWS)])

    return gather_kernel(table, idx)


def _sc_dispatch(x, dest_t, slots):
    t, d = x.shape
    per_w = t // SC_WORKERS
    assert per_w * SC_WORKERS == t and per_w % SC_GATHER_ROWS == 0
    mesh = plsc.VectorSubcoreMesh(core_axis_name="c", subcore_axis_name="s")

    @functools.partial(
        pl.kernel, mesh=mesh,
        out_type=jax.ShapeDtypeStruct((slots, d), x.dtype),
        scratch_types=[pltpu.VMEM((SC_GATHER_ROWS,), jnp.int32),
                       pltpu.VMEM((SC_GATHER_ROWS, d), x.dtype)],
        name="moe_dispatch_sc",
    )
    def scatter_kernel(x_hbm, idx_hbm, out_hbm, idx_v, rows_v):
        wid = lax.axis_index("s") * SC_CORES + lax.axis_index("c")
        base = wid * per_w

        @pl.loop(0, per_w // SC_GATHER_ROWS)
        def _(c):
            off = base + c * SC_GATHER_ROWS
            pltpu.sync_copy(x_hbm.at[pl.ds(off, SC_GATHER_ROWS)], rows_v)
            for j in range(TOP_K):
                pltpu.sync_copy(idx_hbm.at[pl.ds(j * t + off, SC_GATHER_ROWS)], idx_v)
                pltpu.sync_copy(rows_v, out_hbm.at[idx_v])

    return scatter_kernel(x, dest_t)


HALF_D = D_MODEL // 2
HI16 = -65536


def _pack_bf16_pair(a, b):
    wa = lax.bitcast_convert_type(a.astype(BF16).astype(F32), jnp.int32)
    wb = lax.bitcast_convert_type(b.astype(BF16).astype(F32), jnp.int32)
    return wa | lax.shift_right_logical(wb, 16)


def _unpack_bf16_pair(w):
    a = lax.bitcast_convert_type(w & HI16, F32)
    b = lax.bitcast_convert_type(lax.shift_left(w, 16), F32)
    return a, b


def _expert_kernel(be_ref, nu_ref, xs_ref, wgu_ref, bgu_ref, wd_ref, bd_ref, ys_ref, wgu_b, wd_b):
    i = pl.program_id(0)

    @pl.when(jnp.logical_or(i == 0, be_ref[i] != be_ref[jnp.maximum(i - 1, 0)]))
    def _():
        wgu_b[...] = wgu_ref[0].astype(BF16)
        wd_b[...] = wd_ref[0].astype(BF16)

    @pl.when(i < nu_ref[0])
    def _():
        h = jnp.dot(xs_ref[...].astype(BF16), wgu_b[...], preferred_element_type=F32) + bgu_ref[0]
        g = jnp.minimum(h[:, :D_FF], SWIGLU_LIMIT)
        u = jnp.clip(h[:, D_FF:], -SWIGLU_LIMIT, SWIGLU_LIMIT)
        act = (u + 1.0) * g * jax.nn.sigmoid(SWIGLU_ALPHA * g)
        y = jnp.dot(act.astype(BF16), wd_b[...], preferred_element_type=F32) + bd_ref[0]
        ys_ref[...] = _pack_bf16_pair(y[:, :HALF_D], y[:, HALF_D:])

    @pl.when(i >= nu_ref[0])
    def _():
        ys_ref[...] = jnp.zeros_like(ys_ref)


def _experts(block_e, n_used, xs, wgu, bgu, wd, bd):
    slots = xs.shape[0]
    nb = slots // EXPERT_BLOCK
    grid_spec = pltpu.PrefetchScalarGridSpec(
        num_scalar_prefetch=2,
        grid=(nb,),
        in_specs=[pl.BlockSpec((EXPERT_BLOCK, D_MODEL), lambda i, be, nu: (i, 0)),
                  pl.BlockSpec((1, D_MODEL, 2 * D_FF), lambda i, be, nu: (be[i], 0, 0)),
                  pl.BlockSpec((1, 1, 2 * D_FF), lambda i, be, nu: (be[i], 0, 0)),
                  pl.BlockSpec((1, D_FF, D_MODEL), lambda i, be, nu: (be[i], 0, 0)),
                  pl.BlockSpec((1, 1, D_MODEL), lambda i, be, nu: (be[i], 0, 0))],
        out_specs=pl.BlockSpec((EXPERT_BLOCK, HALF_D), lambda i, be, nu: (i, 0)),
        scratch_shapes=[pltpu.VMEM((D_MODEL, 2 * D_FF), BF16), pltpu.VMEM((D_FF, D_MODEL), BF16)],
    )
    return pl.pallas_call(
        _expert_kernel,
        out_shape=jax.ShapeDtypeStruct((slots, HALF_D), jnp.int32),
        grid_spec=grid_spec,
        compiler_params=_params(("arbitrary",)),
        name="moe_experts",
    )(block_e, n_used, xs, wgu, bgu, wd, bd)


def _weighted_rows(rg, packed):
    lo = hi = None
    for j, w in enumerate(packed):
        a, b = _unpack_bf16_pair(w)
        gj = rg[:, j:j + 1]
        lo = gj * a if lo is None else lo + gj * a
        hi = gj * b if hi is None else hi + gj * b
    return jnp.concatenate([lo, hi], axis=1)


def _combine_rows_kernel(x1_ref, rg_ref, g0_ref, g1_ref, g2_ref, g3_ref, l2w_ref, l2b_ref, o_ref):
    moe = _weighted_rows(rg_ref[...], [g0_ref[...], g1_ref[...], g2_ref[...], g3_ref[...]])
    o_ref[...] = _layernorm(DN_ALPHA * x1_ref[...] + moe, l2w_ref[...], l2b_ref[...])


def _combine_rows(x1, rg, rows, l2w, l2b, tm):
    t = x1.shape[0]
    nt = t // tm
    full = lambda shape: pl.BlockSpec(shape, lambda i: (0, 0))
    choice = lambda j: pl.BlockSpec((tm, HALF_D), lambda i, j=j: (j * nt + i, 0))
    return pl.pallas_call(
        _combine_rows_kernel,
        out_shape=jax.ShapeDtypeStruct((t, D_MODEL), F32),
        grid=(nt,),
        in_specs=[pl.BlockSpec((tm, D_MODEL), lambda i: (i, 0)),
                  pl.BlockSpec((tm, LANES), lambda i: (i, 0)),
                  choice(0), choice(1), choice(2), choice(3),
                  full((1, D_MODEL)), full((1, D_MODEL))],
        out_specs=pl.BlockSpec((tm, D_MODEL), lambda i: (i, 0)),
        compiler_params=_params(("arbitrary",)),
        name="moe_combine_rows",
    )(x1, rg, rows, rows, rows, rows, l2w, l2b)


def _prep_w_in(w):
    zeros = lambda n: jnp.zeros((D_MODEL, n), w.dtype)
    kr = w[:, 5760:5824]
    kr_sw = jnp.concatenate([kr[:, MLA_ROPE // 2:], kr[:, :MLA_ROPE // 2]], axis=1)
    parts = [w[:, 6848:9920], w[:, 0:5120], w[:, 5824:6848],
             w[:, 5120:5504], zeros(128),
             w[:, 5504:5760], kr, zeros(64), kr_sw, zeros(64)]
    return jnp.concatenate(parts, axis=1).astype(BF16)


def _prep_w_uq(w):
    w3 = w.reshape(MLA_Q_LORA, MLA_HEADS, MLA_NOPE + MLA_ROPE)
    rope = w3[:, :, MLA_NOPE:]
    rope_sw = jnp.concatenate([rope[:, :, MLA_ROPE // 2:], rope[:, :, :MLA_ROPE // 2]], axis=-1)
    pad = jnp.zeros((MLA_Q_LORA, MLA_HEADS, QK_PAD - MLA_NOPE - MLA_ROPE), w.dtype)
    main = jnp.concatenate([w3, pad], axis=-1).reshape(MLA_Q_LORA, MLA_HEADS * QK_PAD)
    swp = jnp.concatenate([rope_sw, pad], axis=-1).reshape(MLA_Q_LORA, MLA_HEADS * LANES)
    both = jnp.concatenate([main, swp], axis=1)
    return jnp.pad(both, ((0, 512 - MLA_Q_LORA), (0, 0))).astype(BF16)


def _rope_tables(seq):
    inv_freq = ROPE_THETA ** (-jnp.arange(0, MLA_ROPE, 2, dtype=F32) / MLA_ROPE)
    ang = jnp.arange(seq, dtype=F32)[:, None] * inv_freq[None, :]
    cos, sin = jnp.cos(ang), jnp.sin(ang)
    pad = jnp.zeros((seq, LANES - MLA_ROPE), F32)
    return (jnp.concatenate([cos, cos, pad], axis=1), jnp.concatenate([-sin, sin, pad], axis=1))


def _split_bf16(w):
    hi = w.astype(BF16)
    lo = (w - hi.astype(F32)).astype(BF16)
    return jnp.concatenate([hi, lo], axis=1)


def _tile(n, pref):
    return pref if n % pref == 0 else n


def _layer(x, mem, wts):
    batch, seq, _ = x.shape
    t = batch * seq
    x2d = x.reshape(t, D_MODEL)

    proj = _matmul(x2d, wts["w_in"], BF16, _tile(t, 1024), 2048, "in_proj")
    o_a = _hgrn(proj, wts["lb"], wts["hgrn_norm_w"], batch, seq, _tile(seq, 512))

    cos, sin = _rope_tables(seq)
    q, k, v = _mla_prep(proj, wts["qnw"], wts["kvnw"], wts["w_uq"], wts["w_uk"], wts["w_uv"],
                        cos, sin, batch, seq, _tile(seq, 512))
    o_b = _flash(q, k, v, batch, seq, _tile(seq, 1024), _tile(seq, 1024))

    nmem = mem.shape[1]
    kvm = _matmul(mem.reshape(batch * nmem, D_MODEL), wts["mem_w_kv"], BF16,
                  _tile(batch * nmem, 512), 1024, "mem_kv")

    tm = _tile(seq, 512)
    x1, ri, rg, cnt = _merge(x2d, proj, o_a, o_b, kvm, wts["w_out"], wts["ln1_w"], wts["ln1_b"],
                             wts["router_w"], wts["router_b"], batch, seq, tm)

    counts = cnt[0, :N_EXPERTS].astype(jnp.int32)
    padded = (counts + EXPERT_BLOCK - 1) // EXPERT_BLOCK * EXPERT_BLOCK
    pad_end = jnp.cumsum(padded)
    pad_start = pad_end - padded
    experts = jnp.arange(N_EXPERTS, dtype=jnp.int32)[:, None, None]
    start_ct = jnp.sum(jnp.where(ri[None, :TOP_K] == experts, pad_start[:, None, None], 0), axis=0)
    dest_ct = (start_ct + ri[TOP_K:]).astype(jnp.int32)
    nb = t * TOP_K // EXPERT_BLOCK + N_EXPERTS
    blk_start = jnp.arange(nb, dtype=jnp.int32) * EXPERT_BLOCK
    block_e = jnp.minimum(jnp.sum((pad_end[None, :] <= blk_start[:, None]).astype(jnp.int32), axis=1),
                          N_EXPERTS - 1).astype(jnp.int32)
    n_used = (pad_end[-1:] // EXPERT_BLOCK).astype(jnp.int32)
    dest_flat = dest_ct.reshape(TOP_K * t)
    xs = _sc_dispatch(x1, dest_flat, nb * EXPERT_BLOCK)
    ys = _experts(block_e, n_used, xs, wts["exp_w_gu"], wts["exp_b_gu"], wts["exp_w_down"], wts["exp_b_down"])
    rows = _sc_gather(ys, dest_flat)
    y = _combine_rows(x1, rg, rows, wts["ln2_w"], wts["ln2_b"], _tile(t, 512))
    return y.reshape(batch, seq, D_MODEL)


def kernel(x_prompt, x_sample, mem_prompt, mem_sample, w_in, hgrn_lb_logits, hgrn_norm_w,
           mla_q_norm_w, mla_w_uq, mla_kv_norm_w, mla_w_uk, mla_w_uv, mem_w_kv, w_out,
           ln1_w, ln1_b, router_w, router_b, exp_w_gu, exp_b_gu, exp_w_down, exp_b_down,
           ln2_w, ln2_b):
    depth = w_in.shape[0]
    gamma = jax.nn.softmax(hgrn_lb_logits.astype(F32), axis=1)
    cum = jnp.cumsum(gamma, axis=1)
    lb_all = cum[:, 1:] - cum[:, :1]
    y_prompt, y_sample = x_prompt, x_sample
    for l in range(depth):
        row = lambda a: a[l].reshape(1, -1).astype(F32)
        wts = {
            "w_in": _prep_w_in(w_in[l]),
            "lb": lb_all[:, l],
            "hgrn_norm_w": hgrn_norm_w[l].astype(F32),
            "qnw": jnp.pad(row(mla_q_norm_w), ((0, 0), (0, 512 - MLA_Q_LORA))),
            "kvnw": row(mla_kv_norm_w),
            "w_uq": _prep_w_uq(mla_w_uq[l]),
            "w_uk": mla_w_uk[l].astype(BF16),
            "w_uv": mla_w_uv[l].astype(BF16),
            "mem_w_kv": mem_w_kv[l].astype(BF16),
            "w_out": w_out[l].astype(BF16),
            "ln1_w": row(ln1_w), "ln1_b": row(ln1_b),
            "router_w": _split_bf16(jnp.pad(router_w[l].astype(F32), ((0, 0), (0, LANES - N_EXPERTS)))),
            "router_b": jnp.pad(row(router_b), ((0, 0), (0, LANES - N_EXPERTS)), constant_values=-jnp.inf),
            "exp_w_gu": exp_w_gu[l].astype(F32),
            "exp_b_gu": exp_b_gu[l].reshape(N_EXPERTS, 1, 2 * D_FF).astype(F32),
            "exp_w_down": exp_w_down[l].astype(F32),
            "exp_b_down": exp_b_down[l].reshape(N_EXPERTS, 1, D_MODEL).astype(F32),
            "ln2_w": row(ln2_w), "ln2_b": row(ln2_b),
        }
        y_prompt = _layer(y_prompt, mem_prompt, wts)
        y_sample = _layer(y_sample, mem_sample, wts)
    return (y_prompt, y_sample)
```

```python
import functools

import numpy as np
import jax
import jax.numpy as jnp
from jax import lax
from jax.experimental import pallas as pl
from jax.experimental.pallas import tpu as pltpu
from jax.experimental.pallas import tpu_sc as plsc

F32 = jnp.float32
BF16 = jnp.bfloat16

D_MODEL = 1024
HGRN_HEADS = 8
HGRN_DK = 128
MLA_HEADS = 8
MLA_Q_LORA = 384
MLA_KV_LORA = 256
MLA_NOPE = 128
MLA_ROPE = 64
MLA_V = 128
ROPE_THETA = 10000.0
MEM_HEADS = 4
MEM_HEAD_DIM = D_MODEL // MEM_HEADS
N_EXPERTS = 32
TOP_K = 4
D_FF = D_MODEL
SWIGLU_LIMIT = 7.0
SWIGLU_ALPHA = 1.702
DN_ALPHA = 2.0 ** 0.25
LN_EPS = 1e-5
RMS_EPS = 1e-6

LANES = 128
SUBLANES = 8
QK_PAD = 256
V_PAD = 256
LOG2E = 1.4426950408889634
HGRN_CHUNK = 64
HGRN_SAFE_RANGE = 160.0
EXPERT_BLOCK = 512
SC_CORES = 2
SC_SUBCORES = 16
SC_WORKERS = SC_CORES * SC_SUBCORES
SC_GATHER_ROWS = 64
VMEM_LIMIT = 56 * 1024 * 1024

COL_GA, COL_GB, COL_GM, COL_Q, COL_ZF, COL_ZB, COL_I, COL_G, COL_MQ = range(9)
COL_DQ, COL_DKV = 18, 19
IN_COLS_PAD = 10240

NT_DIMS = (((1,), (1,)), ((), ()))
TN_DIMS = (((0,), (0,)), ((), ()))


def _params(sem, vmem=VMEM_LIMIT):
    return pltpu.CompilerParams(dimension_semantics=sem, vmem_limit_bytes=vmem)


def _mm_kernel(x_ref, w_ref, o_ref, xb_ref):
    @pl.when(pl.program_id(1) == 0)
    def _():
        xb_ref[...] = x_ref[...].astype(BF16)

    o_ref[...] = jnp.dot(xb_ref[...], w_ref[...], preferred_element_type=F32).astype(o_ref.dtype)


def _matmul(x, w, out_dtype, tm, tn, name):
    m, k = x.shape
    n = w.shape[1]
    return pl.pallas_call(
        _mm_kernel,
        out_shape=jax.ShapeDtypeStruct((m, n), out_dtype),
        grid=(m // tm, n // tn),
        in_specs=[pl.BlockSpec((tm, k), lambda i, j: (i, 0)),
                  pl.BlockSpec((k, tn), lambda i, j: (0, j))],
        out_specs=pl.BlockSpec((tm, tn), lambda i, j: (i, j)),
        scratch_shapes=[pltpu.VMEM((tm, k), BF16)],
        compiler_params=_params(("arbitrary", "arbitrary")),
        name=name,
    )(x, w)


def _hgrn_kernel(*refs, reverse, epilogue, sc):
    if epilogue:
        lb_ref, q_ref, z_ref, v_ref, of_ref, g_ref, nw_ref, o_ref, st_ref, kk_s, b_s, vf_s, os_s = refs
    else:
        lb_ref, q_ref, z_ref, v_ref, o_ref, st_ref, kk_s, b_s, vf_s = refs
    C = HGRN_CHUNK
    nch = sc // C

    @pl.when(pl.program_id(1) == 0)
    def _():
        st_ref[...] = jnp.zeros_like(st_ref)

    lb = lb_ref[...]
    row = lax.broadcasted_iota(jnp.int32, (C, C), 0)
    col = lax.broadcasted_iota(jnp.int32, (C, C), 1)
    tri = (row <= col) if reverse else (row >= col)
    trib = jnp.where(tri, 1.0, 0.0).astype(BF16)

    minb = None
    for c in range(nch):
        rows = slice(c * C, (c + 1) * C)
        z = z_ref[rows, :].astype(F32)
        gate = (1.0 - lb) * jax.nn.sigmoid(z)
        lf = jnp.log(lb + gate)
        kk_s[rows, :] = (1.0 - lb) - gate
        hi = lf.astype(BF16)
        r1 = lf - hi.astype(F32)
        mid = r1.astype(BF16)
        lo = (r1 - mid.astype(F32)).astype(BF16)
        b = (jnp.dot(trib, hi, preferred_element_type=F32)
             + jnp.dot(trib, mid, preferred_element_type=F32)
             + jnp.dot(trib, lo, preferred_element_type=F32))
        b_s[rows, :] = b
        mb = jnp.min(b)
        minb = mb if minb is None else jnp.minimum(minb, mb)

    rid = lax.broadcasted_iota(jnp.int32, (C, 1), 0)

    def chunk(i, carry, fast):
        c = (nch - 1 - i) if reverse else i
        r0 = pl.multiple_of(c * C, C)
        rows = pl.ds(r0, C)
        for h in range(HGRN_HEADS):
            cols = slice(h * HGRN_DK, (h + 1) * HGRN_DK)
            q = q_ref[rows, cols].astype(F32)
            kk = kk_s[rows, cols]
            b = b_s[rows, cols]
            v = v_ref[rows, cols]
            bl = b[0:1, :] if reverse else b[C - 1:C, :]
            if fast:
                bm = 0.5 * bl
                qd = (q * jnp.exp(b - bm)).astype(BF16)
                kd = (kk * jnp.exp(bm - b)).astype(BF16)
                s = lax.dot_general(qd, kd, NT_DIMS, preferred_element_type=F32)
                s = jnp.where(tri, s, 0.0).astype(BF16)
                o = jnp.dot(s, v, preferred_element_type=F32)
            else:
                def sbody(g_i, o_acc):
                    grp = pl.ds(pl.multiple_of(r0 + g_i * SUBLANES, SUBLANES), SUBLANES)
                    b8 = b_s[grp, cols]
                    k8 = kk_s[grp, cols]
                    v8 = vf_s[grp, cols]
                    for jj in range(SUBLANES):
                        s_i = g_i * SUBLANES + jj
                        w = q * k8[jj:jj + 1, :] * jnp.exp(jnp.minimum(b - b8[jj:jj + 1, :], 0.0))
                        scol = jnp.sum(w, axis=-1, keepdims=True)
                        keep = (rid <= s_i) if reverse else (rid >= s_i)
                        o_acc = o_acc + jnp.where(keep, scol, 0.0) * v8[jj:jj + 1, :]
                    return o_acc
                o = lax.fori_loop(0, C // SUBLANES, sbody, jnp.zeros((C, HGRN_DK), F32))
            st = st_ref[h]
            qi = (q * jnp.exp(b)).astype(BF16)
            o = o + lax.dot_general(qi, st.astype(BF16), NT_DIMS, preferred_element_type=F32)
            ke = (kk * jnp.exp(bl - b)).astype(BF16)
            upd = lax.dot_general(v, ke, TN_DIMS, preferred_element_type=F32)
            st_ref[h] = st * jnp.exp(bl) + upd
            if epilogue:
                os_s[rows, cols] = o + of_ref[rows, cols].astype(F32)
            else:
                o_ref[rows, cols] = o.astype(o_ref.dtype)
        return carry

    safe = minb >= -HGRN_SAFE_RANGE

    @pl.when(safe)
    def _():
        lax.fori_loop(0, nch, functools.partial(chunk, fast=True), 0, unroll=2)

    @pl.when(jnp.logical_not(safe))
    def _():
        vf_s[...] = v_ref[...].astype(F32)
        lax.fori_loop(0, nch, functools.partial(chunk, fast=False), 0)

    if epilogue:
        nw = nw_ref[...]
        for h in range(HGRN_HEADS):
            cols = slice(h * HGRN_DK, (h + 1) * HGRN_DK)
            os = os_s[:, cols]
            ms = jnp.mean(os * os, axis=-1, keepdims=True)
            y = os * lax.rsqrt(ms + RMS_EPS) * nw
            g = g_ref[:, cols].astype(F32)
            o_ref[:, cols] = (y * (g * jax.nn.sigmoid(g))).astype(o_ref.dtype)


def _hgrn(proj, lb, norm_w, batch, seq, sc):
    t = batch * seq
    ns = seq // sc
    blk = (sc, D_MODEL)

    def spec(colblk, reverse):
        if reverse:
            return pl.BlockSpec(blk, lambda b, n: (b * ns + ns - 1 - n, colblk))
        return pl.BlockSpec(blk, lambda b, n: (b * ns + n, colblk))

    def row_spec(reverse):
        if reverse:
            return pl.BlockSpec(blk, lambda b, n: (b * ns + ns - 1 - n, 0))
        return pl.BlockSpec(blk, lambda b, n: (b * ns + n, 0))

    lb_spec = pl.BlockSpec((1, D_MODEL), lambda b, n: (0, 0))
    common_scratch = [pltpu.VMEM((HGRN_HEADS, HGRN_DK, HGRN_DK), F32),
                      pltpu.VMEM(blk, F32), pltpu.VMEM(blk, F32), pltpu.VMEM(blk, F32)]
    o_f = pl.pallas_call(
        functools.partial(_hgrn_kernel, reverse=False, epilogue=False, sc=sc),
        out_shape=jax.ShapeDtypeStruct((t, D_MODEL), BF16),
        grid=(batch, ns),
        in_specs=[lb_spec, spec(COL_Q, False), spec(COL_ZF, False), spec(COL_I, False)],
        out_specs=row_spec(False),
        scratch_shapes=common_scratch,
        compiler_params=_params(("arbitrary", "arbitrary")),
        name="hgrn_fwd",
    )(lb[0:1], proj, proj, proj)
    o_a = pl.pallas_call(
        functools.partial(_hgrn_kernel, reverse=True, epilogue=True, sc=sc),
        out_shape=jax.ShapeDtypeStruct((t, D_MODEL), BF16),
        grid=(batch, ns),
        in_specs=[lb_spec, spec(COL_Q, True), spec(COL_ZB, True), spec(COL_I, True),
                  row_spec(True), spec(COL_G, True),
                  pl.BlockSpec((1, HGRN_DK), lambda b, n: (0, 0))],
        out_specs=row_spec(True),
        scratch_shapes=common_scratch + [pltpu.VMEM(blk, F32)],
        compiler_params=_params(("arbitrary", "arbitrary")),
        name="hgrn_bwd",
    )(lb[1:2], proj, proj, proj, o_f, proj, norm_w.reshape(1, HGRN_DK))
    return o_a


def _mla_prep_kernel(dq_ref, dkv_ref, qnw_ref, kvnw_ref, wq_ref, wk_ref, wv_ref, cos_ref, sin_ref,
                     q_ref, k_ref, v_ref):
    scale = (MLA_NOPE + MLA_ROPE) ** -0.5 * LOG2E
    cos = cos_ref[...]
    sin = sin_ref[...]
    dq = dq_ref[...].astype(F32)
    ms = jnp.sum(dq * dq, axis=-1, keepdims=True) * (1.0 / MLA_Q_LORA)
    cq = (dq * lax.rsqrt(ms + RMS_EPS) * qnw_ref[...]).astype(BF16)
    qa = jnp.dot(cq, wq_ref[...], preferred_element_type=F32)
    for h in range(MLA_HEADS):
        base = h * QK_PAD
        q_ref[:, base:base + MLA_NOPE] = (qa[:, base:base + MLA_NOPE] * scale).astype(BF16)
        rp = qa[:, base + MLA_NOPE:base + QK_PAD]
        sw = qa[:, MLA_HEADS * QK_PAD + h * LANES:MLA_HEADS * QK_PAD + (h + 1) * LANES]
        q_ref[:, base + MLA_NOPE:base + QK_PAD] = ((rp * cos + sw * sin) * scale).astype(BF16)
    dkv = dkv_ref[...].astype(F32)
    ckv = dkv[:, :MLA_KV_LORA]
    msk = jnp.mean(ckv * ckv, axis=-1, keepdims=True)
    cn = (ckv * lax.rsqrt(msk + RMS_EPS) * kvnw_ref[...]).astype(BF16)
    kn = jnp.dot(cn, wk_ref[...], preferred_element_type=F32)
    vv = jnp.dot(cn, wv_ref[...], preferred_element_type=F32).astype(BF16)
    ones_col = jnp.where(lax.broadcasted_iota(jnp.int32, (vv.shape[0], LANES), 1) == 0, 1.0, 0.0).astype(BF16)
    for h in range(MLA_HEADS):
        v_ref[:, h * V_PAD:h * V_PAD + MLA_V] = vv[:, h * MLA_V:(h + 1) * MLA_V]
        v_ref[:, h * V_PAD + MLA_V:(h + 1) * V_PAD] = ones_col
    kr = (dkv[:, MLA_KV_LORA:MLA_KV_LORA + LANES] * cos
          + dkv[:, MLA_KV_LORA + LANES:MLA_KV_LORA + 2 * LANES] * sin).astype(BF16)
    for h in range(MLA_HEADS):
        base = h * QK_PAD
        k_ref[:, base:base + MLA_NOPE] = kn[:, h * MLA_NOPE:(h + 1) * MLA_NOPE].astype(BF16)
        k_ref[:, base + MLA_NOPE:base + QK_PAD] = kr


def _mla_prep(proj, qnw, kvnw, wq, wk, wv, cos, sin, batch, seq, tm):
    t = batch * seq
    npos = seq // tm
    full = lambda shape: pl.BlockSpec(shape, lambda i: (0, 0))
    return pl.pallas_call(
        _mla_prep_kernel,
        out_shape=(jax.ShapeDtypeStruct((t, MLA_HEADS * QK_PAD), BF16),
                   jax.ShapeDtypeStruct((t, MLA_HEADS * QK_PAD), BF16),
                   jax.ShapeDtypeStruct((t, MLA_HEADS * V_PAD), BF16)),
        grid=(t // tm,),
        in_specs=[pl.BlockSpec((tm, 512), lambda i: (i, COL_DQ)),
                  pl.BlockSpec((tm, 512), lambda i: (i, COL_DKV)),
                  full((1, 512)), full((1, MLA_KV_LORA)),
                  full(wq.shape), full(wk.shape), full(wv.shape),
                  pl.BlockSpec((tm, LANES), lambda i: (i % npos, 0)),
                  pl.BlockSpec((tm, LANES), lambda i: (i % npos, 0))],
        out_specs=(pl.BlockSpec((tm, MLA_HEADS * QK_PAD), lambda i: (i, 0)),
                   pl.BlockSpec((tm, MLA_HEADS * QK_PAD), lambda i: (i, 0)),
                   pl.BlockSpec((tm, MLA_HEADS * V_PAD), lambda i: (i, 0))),
        compiler_params=_params(("arbitrary",)),
        name="mla_prep",
    )(proj, proj, qnw, kvnw, wq, wk, wv, cos, sin)


def _flash_kernel(q_ref, k_ref, v_ref, o_ref, m_s, acc_s, s_buf, *, bk, nk):
    m_s[...] = jnp.full(m_s.shape, -jnp.inf, F32)
    acc_s[...] = jnp.zeros(acc_s.shape, F32)
    nt = bk // LANES
    bq = q_ref.shape[0]
    nsplit = 2 if bq % 256 == 0 else 1
    hq = bq // nsplit
    qrs = [slice(hf * hq, (hf + 1) * hq) for hf in range(nsplit)]

    def scores(j, slot):
        rows = pl.ds(pl.multiple_of(j * bk, bk), bk)
        for qr in qrs:
            s_buf[slot, qr, :] = lax.dot_general(q_ref[qr, :], k_ref[rows, :], NT_DIMS,
                                                 preferred_element_type=F32)

    def consume(j, slot):
        rows = pl.ds(pl.multiple_of(j * bk, bk), bk)
        ps, alphas = [], []
        for qr in qrs:
            tiles = [s_buf[slot, qr, t * LANES:(t + 1) * LANES] for t in range(nt)]
            tmax = tiles[0]
            for t in range(1, nt):
                tmax = jnp.maximum(tmax, tiles[t])
            m_prev = m_s[qr, :]
            m_new = jnp.maximum(m_prev, jnp.max(tmax, axis=-1, keepdims=True))
            alphas.append(jnp.exp2(m_prev - m_new))
            ps.append(jnp.concatenate([jnp.exp2(tl - m_new).astype(BF16) for tl in tiles], axis=1))
            m_s[qr, :] = m_new
        for qr, p, a in zip(qrs, ps, alphas):
            pv = jnp.dot(p, v_ref[rows, :], preferred_element_type=F32)
            acc_s[qr, :] = jnp.concatenate([a, a], axis=1) * acc_s[qr, :] + pv

    scores(0, 0)
    npairs = (nk - 1) // 2

    def body(i, carry):
        j = 2 * i
        scores(j + 1, 1)
        consume(j, 0)
        scores(j + 2, 0)
        consume(j + 1, 1)
        return carry

    lax.fori_loop(0, npairs, body, 0)
    if nk - 2 * npairs == 2:
        scores(nk - 1, 1)
        consume(nk - 2, 0)
        consume(nk - 1, 1)
    else:
        consume(nk - 1, 0)
    acc = acc_s[...]
    o_ref[...] = (acc[:, :MLA_V] / acc[:, MLA_V:MLA_V + 1]).astype(o_ref.dtype)


def _flash(q, k, v, batch, seq, bq, bk):
    t = batch * seq
    nq = seq // bq
    return pl.pallas_call(
        functools.partial(_flash_kernel, bk=bk, nk=seq // bk),
        out_shape=jax.ShapeDtypeStruct((t, MLA_HEADS * MLA_V), BF16),
        grid=(batch, MLA_HEADS, nq),
        in_specs=[pl.BlockSpec((bq, QK_PAD), lambda b, h, i: (b * nq + i, h)),
                  pl.BlockSpec((seq, QK_PAD), lambda b, h, i: (b, h)),
                  pl.BlockSpec((seq, V_PAD), lambda b, h, i: (b, h))],
        out_specs=pl.BlockSpec((bq, MLA_V), lambda b, h, i: (b * nq + i, h)),
        scratch_shapes=[pltpu.VMEM((bq, LANES), F32), pltpu.VMEM((bq, V_PAD), F32),
                        pltpu.VMEM((2, bq, bk), F32)],
        compiler_params=_params(("arbitrary", "arbitrary", "arbitrary")),
        name="mla_flash",
    )(q, k, v)


def _layernorm(y, w, b):
    mu = jnp.mean(y, axis=-1, keepdims=True)
    yc = y - mu
    var = jnp.mean(yc * yc, axis=-1, keepdims=True)
    return yc * lax.rsqrt(var + LN_EPS) * w + b


def _merge_kernel(x_ref, ga_ref, gb_ref, gm_ref, mq_ref, oa_ref, ob_ref, kvm_ref, wout_ref,
                  l1w_ref, l1b_ref, rw_ref, rb_ref,
                  x1_ref, x1p_ref, ri_ref, rg_ref, cnt_ref, carry_s, *, tm):
    @pl.when(pl.program_id(0) == 0)
    def _():
        carry_s[...] = jnp.zeros_like(carry_s)

    parts = []
    for h in range(MEM_HEADS):
        cols = slice(h * MEM_HEAD_DIM, (h + 1) * MEM_HEAD_DIM)
        kh = kvm_ref[:, cols]
        vh = kvm_ref[:, D_MODEL + h * MEM_HEAD_DIM:D_MODEL + (h + 1) * MEM_HEAD_DIM]
        s = lax.dot_general(mq_ref[:, cols], kh, NT_DIMS, preferred_element_type=F32) * (MEM_HEAD_DIM ** -0.5)
        s = s - jnp.max(s, axis=-1, keepdims=True)
        p = jnp.exp(s)
        p = p / jnp.sum(p, axis=-1, keepdims=True)
        parts.append(jnp.dot(p.astype(BF16), vh, preferred_element_type=F32))
    om = jnp.concatenate(parts, axis=1)

    merged = (jax.nn.sigmoid(ga_ref[...].astype(F32)) * oa_ref[...].astype(F32)
              + jax.nn.sigmoid(gb_ref[...].astype(F32)) * ob_ref[...].astype(F32)
              + jax.nn.sigmoid(gm_ref[...].astype(F32)) * om)
    y = DN_ALPHA * x_ref[...] + jnp.dot(merged.astype(BF16), wout_ref[...], preferred_element_type=F32)
    x1 = _layernorm(y, l1w_ref[...], l1b_ref[...])
    x1_ref[...] = x1
    x1p_ref[...] = _pack_bf16_pair(x1[:, :HALF_D], x1[:, HALF_D:])

    x_hi = x1.astype(BF16)
    x_lo = (x1 - x_hi.astype(F32)).astype(BF16)
    hi_part = jnp.dot(x_hi, rw_ref[...], preferred_element_type=F32)
    lo_part = jnp.dot(x_lo, rw_ref[:, :LANES], preferred_element_type=F32)
    logits = hi_part[:, :LANES] + hi_part[:, LANES:] + lo_part + rb_ref[...]
    lane_i = lax.broadcasted_iota(jnp.int32, (tm, LANES), 1)
    lane = lane_i.astype(F32)
    work = logits
    idx, val = [], []
    for _ in range(TOP_K):
        mx = jnp.max(work, axis=-1, keepdims=True)
        ix = jnp.min(jnp.where(work == mx, lane, float(LANES)), axis=-1, keepdims=True)
        idx.append(ix)
        val.append(mx)
        work = jnp.where(lane == ix, -jnp.inf, work)
    ex = [jnp.exp(v - val[0]) for v in val]
    tot = ex[0] + ex[1] + ex[2] + ex[3]
    hot = [jnp.where(lane == ix, 1.0, 0.0) for ix in idx]
    multi = hot[0] + hot[1] + hot[2] + hot[3]
    r = lax.broadcasted_iota(jnp.int32, (tm, tm), 0)
    c = lax.broadcasted_iota(jnp.int32, (tm, tm), 1)
    lower = jnp.where(r > c, 1.0, 0.0).astype(BF16)
    before = jnp.dot(lower, multi.astype(BF16), preferred_element_type=F32) + carry_s[0:1, :]
    ri = jnp.zeros((tm, LANES), F32)
    rg = jnp.zeros((tm, LANES), F32)
    for j in range(TOP_K):
        rank = jnp.sum(before * hot[j], axis=-1, keepdims=True)
        ri = ri + jnp.where(lane == float(j), idx[j], 0.0) + jnp.where(lane == float(TOP_K + j), rank, 0.0)
        rg = rg + jnp.where(lane == float(j), ex[j] / tot, 0.0)
    ri_ref[...] = ri.T[:2 * TOP_K, :].astype(jnp.int32)
    rg_ref[...] = rg
    carry_s[...] = carry_s[...] + jnp.sum(multi, axis=0, keepdims=True)
    cnt_ref[...] = carry_s[...]


def _merge(x2d, proj, o_a, o_b, kvm, wout, l1w, l1b, rw, rb, batch, seq, tm):
    t = batch * seq
    per_b = seq // tm
    nmem = kvm.shape[0] // batch
    tile = lambda colblk: pl.BlockSpec((tm, D_MODEL), lambda i: (i, colblk))
    full = lambda shape: pl.BlockSpec(shape, lambda i: (0, 0))
    return pl.pallas_call(
        functools.partial(_merge_kernel, tm=tm),
        out_shape=(jax.ShapeDtypeStruct((t, D_MODEL), F32),
                   jax.ShapeDtypeStruct((t, HALF_D), jnp.int32),
                   jax.ShapeDtypeStruct((2 * TOP_K, t), jnp.int32),
                   jax.ShapeDtypeStruct((t, LANES), F32),
                   jax.ShapeDtypeStruct((8, LANES), F32)),
        grid=(t // tm,),
        in_specs=[tile(0), tile(COL_GA), tile(COL_GB), tile(COL_GM), tile(COL_MQ), tile(0), tile(0),
                  pl.BlockSpec((nmem, 2 * D_MODEL), lambda i: (i // per_b, 0)),
                  full((D_MODEL, D_MODEL)), full((1, D_MODEL)), full((1, D_MODEL)),
                  full((D_MODEL, 2 * LANES)), full((1, LANES))],
        out_specs=(tile(0), pl.BlockSpec((tm, HALF_D), lambda i: (i, 0)),
                   pl.BlockSpec((2 * TOP_K, tm), lambda i: (0, i)),
                   pl.BlockSpec((tm, LANES), lambda i: (i, 0)), full((8, LANES))),
        scratch_shapes=[pltpu.VMEM((8, LANES), F32)],
        compiler_params=_params(("arbitrary",)),
        name="merge_router",
    )(x2d, proj, proj, proj, proj, o_a, o_b, kvm, wout, l1w, l1b, rw, rb)


def _sc_gather(table, idx):
    p, d = idx.shape[0], table.shape[1]
    per_w = p // SC_WORKERS
    assert per_w * SC_WORKERS == p and per_w % SC_GATHER_ROWS == 0
    mesh = plsc.VectorSubcoreMesh(core_axis_name="c", subcore_axis_name="s")

    @functools.partial(
        pl.kernel, mesh=mesh,
        out_type=jax.ShapeDtypeStruct((p, d), table.dtype),
        scratch_types=[pltpu.VMEM((SC_GATHER_ROWS,), jnp.int32),
                       pltpu.VMEM((SC_GATHER_ROWS, d), table.dtype),
                       pltpu.SemaphoreType.DMA],
        name="moe_gather_sc",
    )
    def gather_kernel(table_hbm, idx_hbm, out_hbm, idx_v, rows_v, sem):
        wid = lax.axis_index("s") * SC_CORES + lax.axis_index("c")
        base = wid * per_w

        @pl.loop(0, per_w // SC_GATHER_ROWS)
        def _(c):
            off = base + c * SC_GATHER_ROWS
            pltpu.sync_copy(idx_hbm.at[pl.ds(off, SC_GATHER_ROWS)], idx_v)
            pltpu.async_copy(table_hbm.at[idx_v], rows_v, sem).wait()
            pltpu.sync_copy(rows_v, out_hbm.at[pl.ds(off, SC_GATHER_ROWS)])

    return gather_kernel(table, idx)


def _sc_dispatch(x, dest_t, slots):
    t, d = x.shape
    per_w = t // SC_WORKERS
    assert per_w * SC_WORKERS == t and per_w % SC_GATHER_ROWS == 0
    mesh = plsc.VectorSubcoreMesh(core_axis_name="c", subcore_axis_name="s")

    @functools.partial(
        pl.kernel, mesh=mesh,
        out_type=jax.ShapeDtypeStruct((slots, d), x.dtype),
        scratch_types=[pltpu.VMEM((SC_GATHER_ROWS,), jnp.int32),
                       pltpu.VMEM((SC_GATHER_ROWS, d), x.dtype)],
        name="moe_dispatch_sc",
    )
    def scatter_kernel(x_hbm, idx_hbm, out_hbm, idx_v, rows_v):
        wid = lax.axis_index("s") * SC_CORES + lax.axis_index("c")
        base = wid * per_w

        @pl.loop(0, per_w // SC_GATHER_ROWS)
        def _(c):
            off = base + c * SC_GATHER_ROWS
            pltpu.sync_copy(x_hbm.at[pl.ds(off, SC_GATHER_ROWS)], rows_v)
            for j in range(TOP_K):
                pltpu.sync_copy(idx_hbm.at[pl.ds(j * t + off, SC_GATHER_ROWS)], idx_v)
                pltpu.sync_copy(rows_v, out_hbm.at[idx_v])

    return scatter_kernel(x, dest_t)


HALF_D = D_MODEL // 2
HI16 = -65536


def _pack_bf16_pair(a, b):
    wa = lax.bitcast_convert_type(a.astype(BF16).astype(F32), jnp.int32)
    wb = lax.bitcast_convert_type(b.astype(BF16).astype(F32), jnp.int32)
    return wa | lax.shift_right_logical(wb, 16)


def _unpack_bf16_pair(w):
    a = lax.bitcast_convert_type(w & HI16, F32)
    b = lax.bitcast_convert_type(lax.shift_left(w, 16), F32)
    return a, b


def _expert_kernel(be_ref, nu_ref, xs_ref, wgu_ref, bgu_ref, wd_ref, bd_ref, ys_ref, wgu_b, wd_b):
    i = pl.program_id(0)

    @pl.when(jnp.logical_or(i == 0, be_ref[i] != be_ref[jnp.maximum(i - 1, 0)]))
    def _():
        wgu_b[...] = wgu_ref[0].astype(BF16)
        wd_b[...] = wd_ref[0].astype(BF16)

    @pl.when(i < nu_ref[0])
    def _():
        xa, xb = _unpack_bf16_pair(xs_ref[...])
        x = jnp.concatenate([xa, xb], axis=1).astype(BF16)
        h = jnp.dot(x, wgu_b[...], preferred_element_type=F32) + bgu_ref[0]
        g = jnp.minimum(h[:, :D_FF], SWIGLU_LIMIT)
        u = jnp.clip(h[:, D_FF:], -SWIGLU_LIMIT, SWIGLU_LIMIT)
        act = (u + 1.0) * g * jax.nn.sigmoid(SWIGLU_ALPHA * g)
        y = jnp.dot(act.astype(BF16), wd_b[...], preferred_element_type=F32) + bd_ref[0]
        ys_ref[...] = _pack_bf16_pair(y[:, :HALF_D], y[:, HALF_D:])

    @pl.when(i >= nu_ref[0])
    def _():
        ys_ref[...] = jnp.zeros_like(ys_ref)


def _experts(block_e, n_used, xs, wgu, bgu, wd, bd):
    slots = xs.shape[0]
    nb = slots // EXPERT_BLOCK
    grid_spec = pltpu.PrefetchScalarGridSpec(
        num_scalar_prefetch=2,
        grid=(nb,),
        in_specs=[pl.BlockSpec((EXPERT_BLOCK, HALF_D), lambda i, be, nu: (i, 0)),
                  pl.BlockSpec((1, D_MODEL, 2 * D_FF), lambda i, be, nu: (be[i], 0, 0)),
                  pl.BlockSpec((1, 1, 2 * D_FF), lambda i, be, nu: (be[i], 0, 0)),
                  pl.BlockSpec((1, D_FF, D_MODEL), lambda i, be, nu: (be[i], 0, 0)),
                  pl.BlockSpec((1, 1, D_MODEL), lambda i, be, nu: (be[i], 0, 0))],
        out_specs=pl.BlockSpec((EXPERT_BLOCK, HALF_D), lambda i, be, nu: (i, 0)),
        scratch_shapes=[pltpu.VMEM((D_MODEL, 2 * D_FF), BF16), pltpu.VMEM((D_FF, D_MODEL), BF16)],
    )
    return pl.pallas_call(
        _expert_kernel,
        out_shape=jax.ShapeDtypeStruct((slots, HALF_D), jnp.int32),
        grid_spec=grid_spec,
        compiler_params=_params(("arbitrary",)),
        name="moe_experts",
    )(block_e, n_used, xs, wgu, bgu, wd, bd)


def _weighted_rows(rg, packed):
    lo = hi = None
    for j, w in enumerate(packed):
        a, b = _unpack_bf16_pair(w)
        gj = rg[:, j:j + 1]
        lo = gj * a if lo is None else lo + gj * a
        hi = gj * b if hi is None else hi + gj * b
    return jnp.concatenate([lo, hi], axis=1)


def _combine_rows_kernel(x1_ref, rg_ref, g0_ref, g1_ref, g2_ref, g3_ref, l2w_ref, l2b_ref, o_ref):
    moe = _weighted_rows(rg_ref[...], [g0_ref[...], g1_ref[...], g2_ref[...], g3_ref[...]])
    o_ref[...] = _layernorm(DN_ALPHA * x1_ref[...] + moe, l2w_ref[...], l2b_ref[...])


def _combine_rows(x1, rg, rows, l2w, l2b, tm):
    t = x1.shape[0]
    nt = t // tm
    full = lambda shape: pl.BlockSpec(shape, lambda i: (0, 0))
    choice = lambda j: pl.BlockSpec((tm, HALF_D), lambda i, j=j: (j * nt + i, 0))
    return pl.pallas_call(
        _combine_rows_kernel,
        out_shape=jax.ShapeDtypeStruct((t, D_MODEL), F32),
        grid=(nt,),
        in_specs=[pl.BlockSpec((tm, D_MODEL), lambda i: (i, 0)),
                  pl.BlockSpec((tm, LANES), lambda i: (i, 0)),
                  choice(0), choice(1), choice(2), choice(3),
                  full((1, D_MODEL)), full((1, D_MODEL))],
        out_specs=pl.BlockSpec((tm, D_MODEL), lambda i: (i, 0)),
        compiler_params=_params(("arbitrary",)),
        name="moe_combine_rows",
    )(x1, rg, rows, rows, rows, rows, l2w, l2b)


def _prep_w_in(w):
    zeros = lambda n: jnp.zeros((D_MODEL, n), w.dtype)
    kr = w[:, 5760:5824]
    kr_sw = jnp.concatenate([kr[:, MLA_ROPE // 2:], kr[:, :MLA_ROPE // 2]], axis=1)
    parts = [w[:, 6848:9920], w[:, 0:5120], w[:, 5824:6848],
             w[:, 5120:5504], zeros(128),
             w[:, 5504:5760], kr, zeros(64), kr_sw, zeros(64)]
    return jnp.concatenate(parts, axis=1).astype(BF16)


def _prep_w_uq(w):
    w3 = w.reshape(MLA_Q_LORA, MLA_HEADS, MLA_NOPE + MLA_ROPE)
    rope = w3[:, :, MLA_NOPE:]
    rope_sw = jnp.concatenate([rope[:, :, MLA_ROPE // 2:], rope[:, :, :MLA_ROPE // 2]], axis=-1)
    pad = jnp.zeros((MLA_Q_LORA, MLA_HEADS, QK_PAD - MLA_NOPE - MLA_ROPE), w.dtype)
    main = jnp.concatenate([w3, pad], axis=-1).reshape(MLA_Q_LORA, MLA_HEADS * QK_PAD)
    swp = jnp.concatenate([rope_sw, pad], axis=-1).reshape(MLA_Q_LORA, MLA_HEADS * LANES)
    both = jnp.concatenate([main, swp], axis=1)
    return jnp.pad(both, ((0, 512 - MLA_Q_LORA), (0, 0))).astype(BF16)


def _rope_tables(seq):
    inv_freq = ROPE_THETA ** (-jnp.arange(0, MLA_ROPE, 2, dtype=F32) / MLA_ROPE)
    ang = jnp.arange(seq, dtype=F32)[:, None] * inv_freq[None, :]
    cos, sin = jnp.cos(ang), jnp.sin(ang)
    pad = jnp.zeros((seq, LANES - MLA_ROPE), F32)
    return (jnp.concatenate([cos, cos, pad], axis=1), jnp.concatenate([-sin, sin, pad], axis=1))


def _split_bf16(w):
    hi = w.astype(BF16)
    lo = (w - hi.astype(F32)).astype(BF16)
    return jnp.concatenate([hi, lo], axis=1)


def _tile(n, pref):
    return pref if n % pref == 0 else n


def _layer(x, mem, wts):
    batch, seq, _ = x.shape
    t = batch * seq
    x2d = x.reshape(t, D_MODEL)

    proj = _matmul(x2d, wts["w_in"], BF16, _tile(t, 1024), 2048, "in_proj")
    o_a = _hgrn(proj, wts["lb"], wts["hgrn_norm_w"], batch, seq, _tile(seq, 512))

    cos, sin = _rope_tables(seq)
    q, k, v = _mla_prep(proj, wts["qnw"], wts["kvnw"], wts["w_uq"], wts["w_uk"], wts["w_uv"],
                        cos, sin, batch, seq, _tile(seq, 512))
    o_b = _flash(q, k, v, batch, seq, _tile(seq, 1024), _tile(seq, 1024))

    nmem = mem.shape[1]
    kvm = _matmul(mem.reshape(batch * nmem, D_MODEL), wts["mem_w_kv"], BF16,
                  _tile(batch * nmem, 512), 1024, "mem_kv")

    tm = _tile(seq, 512)
    x1, x1p, ri, rg, cnt = _merge(x2d, proj, o_a, o_b, kvm, wts["w_out"], wts["ln1_w"], wts["ln1_b"],
                             wts["router_w"], wts["router_b"], batch, seq, tm)

    counts = cnt[0, :N_EXPERTS].astype(jnp.int32)
    padded = (counts + EXPERT_BLOCK - 1) // EXPERT_BLOCK * EXPERT_BLOCK
    pad_end = jnp.cumsum(padded)
    pad_start = pad_end - padded
    experts = jnp.arange(N_EXPERTS, dtype=jnp.int32)[:, None, None]
    start_ct = jnp.sum(jnp.where(ri[None, :TOP_K] == experts, pad_start[:, None, None], 0), axis=0)
    dest_ct = (start_ct + ri[TOP_K:]).astype(jnp.int32)
    nb = t * TOP_K // EXPERT_BLOCK + N_EXPERTS
    blk_start = jnp.arange(nb, dtype=jnp.int32) * EXPERT_BLOCK
    block_e = jnp.minimum(jnp.sum((pad_end[None, :] <= blk_start[:, None]).astype(jnp.int32), axis=1),
                          N_EXPERTS - 1).astype(jnp.int32)
    n_used = (pad_end[-1:] // EXPERT_BLOCK).astype(jnp.int32)
    dest_flat = dest_ct.reshape(TOP_K * t)
    xs = _sc_dispatch(x1p, dest_flat, nb * EXPERT_BLOCK)
    ys = _experts(block_e, n_used, xs, wts["exp_w_gu"], wts["exp_b_gu"], wts["exp_w_down"], wts["exp_b_down"])
    rows = _sc_gather(ys, dest_flat)
    y = _combine_rows(x1, rg, rows, wts["ln2_w"], wts["ln2_b"], _tile(t, 512))
    return y.reshape(batch, seq, D_MODEL)


def kernel(x_prompt, x_sample, mem_prompt, mem_sample, w_in, hgrn_lb_logits, hgrn_norm_w,
           mla_q_norm_w, mla_w_uq, mla_kv_norm_w, mla_w_uk, mla_w_uv, mem_w_kv, w_out,
           ln1_w, ln1_b, router_w, router_b, exp_w_gu, exp_b_gu, exp_w_down, exp_b_down,
           ln2_w, ln2_b):
    depth = w_in.shape[0]
    gamma = jax.nn.softmax(hgrn_lb_logits.astype(F32), axis=1)
    cum = jnp.cumsum(gamma, axis=1)
    lb_all = cum[:, 1:] - cum[:, :1]
    y_prompt, y_sample = x_prompt, x_sample
    for l in range(depth):
        row = lambda a: a[l].reshape(1, -1).astype(F32)
        wts = {
            "w_in": _prep_w_in(w_in[l]),
            "lb": lb_all[:, l],
            "hgrn_norm_w": hgrn_norm_w[l].astype(F32),
            "qnw": jnp.pad(row(mla_q_norm_w), ((0, 0), (0, 512 - MLA_Q_LORA))),
            "kvnw": row(mla_kv_norm_w),
            "w_uq": _prep_w_uq(mla_w_uq[l]),
            "w_uk": mla_w_uk[l].astype(BF16),
            "w_uv": mla_w_uv[l].astype(BF16),
            "mem_w_kv": mem_w_kv[l].astype(BF16),
            "w_out": w_out[l].astype(BF16),
            "ln1_w": row(ln1_w), "ln1_b": row(ln1_b),
            "router_w": _split_bf16(jnp.pad(router_w[l].astype(F32), ((0, 0), (0, LANES - N_EXPERTS)))),
            "router_b": jnp.pad(row(router_b), ((0, 0), (0, LANES - N_EXPERTS)), constant_values=-jnp.inf),
            "exp_w_gu": exp_w_gu[l].astype(F32),
            "exp_b_gu": exp_b_gu[l].reshape(N_EXPERTS, 1, 2 * D_FF).astype(F32),
            "exp_w_down": exp_w_down[l].astype(F32),
            "exp_b_down": exp_b_down[l].reshape(N_EXPERTS, 1, D_MODEL).astype(F32),
            "ln2_w": row(ln2_w), "ln2_b": row(ln2_b),
        }
        y_prompt = _layer(y_prompt, mem_prompt, wts)
        y_sample = _layer(y_sample, mem_sample, wts)
    return (y_prompt, y_sample)
```

```python
import functools

import numpy as np
import jax
import jax.numpy as jnp
from jax import lax
from jax.experimental import pallas as pl
from jax.experimental.pallas import tpu as pltpu
from jax.experimental.pallas import tpu_sc as plsc

F32 = jnp.float32
BF16 = jnp.bfloat16

D_MODEL = 1024
HGRN_HEADS = 8
HGRN_DK = 128
MLA_HEADS = 8
MLA_Q_LORA = 384
MLA_KV_LORA = 256
MLA_NOPE = 128
MLA_ROPE = 64
MLA_V = 128
ROPE_THETA = 10000.0
MEM_HEADS = 4
MEM_HEAD_DIM = D_MODEL // MEM_HEADS
N_EXPERTS = 32
TOP_K = 4
D_FF = D_MODEL
SWIGLU_LIMIT = 7.0
SWIGLU_ALPHA = 1.702
DN_ALPHA = 2.0 ** 0.25
LN_EPS = 1e-5
RMS_EPS = 1e-6

LANES = 128
SUBLANES = 8
QK_PAD = 256
V_PAD = 256
LOG2E = 1.4426950408889634
HGRN_CHUNK = 64
HGRN_SAFE_RANGE = 160.0
EXPERT_BLOCK = 512
SC_CORES = 2
SC_SUBCORES = 16
SC_WORKERS = SC_CORES * SC_SUBCORES
SC_GATHER_ROWS = 128
VMEM_LIMIT = 56 * 1024 * 1024

COL_GA, COL_GB, COL_GM, COL_Q, COL_ZF, COL_ZB, COL_I, COL_G, COL_MQ = range(9)
COL_DQ, COL_DKV = 18, 19
IN_COLS_PAD = 10240

NT_DIMS = (((1,), (1,)), ((), ()))
TN_DIMS = (((0,), (0,)), ((), ()))


def _params(sem, vmem=VMEM_LIMIT):
    return pltpu.CompilerParams(dimension_semantics=sem, vmem_limit_bytes=vmem)


def _mm_kernel(x_ref, w_ref, o_ref, xb_ref):
    @pl.when(pl.program_id(1) == 0)
    def _():
        xb_ref[...] = x_ref[...].astype(BF16)

    o_ref[...] = jnp.dot(xb_ref[...], w_ref[...], preferred_element_type=F32).astype(o_ref.dtype)


def _matmul(x, w, out_dtype, tm, tn, name):
    m, k = x.shape
    n = w.shape[1]
    return pl.pallas_call(
        _mm_kernel,
        out_shape=jax.ShapeDtypeStruct((m, n), out_dtype),
        grid=(m // tm, n // tn),
        in_specs=[pl.BlockSpec((tm, k), lambda i, j: (i, 0)),
                  pl.BlockSpec((k, tn), lambda i, j: (0, j))],
        out_specs=pl.BlockSpec((tm, tn), lambda i, j: (i, j)),
        scratch_shapes=[pltpu.VMEM((tm, k), BF16)],
        compiler_params=_params(("arbitrary", "arbitrary")),
        name=name,
    )(x, w)


def _hgrn_kernel(*refs, reverse, epilogue, sc):
    if epilogue:
        lb_ref, q_ref, z_ref, v_ref, of_ref, g_ref, nw_ref, o_ref, st_ref, kk_s, b_s, vf_s, os_s = refs
    else:
        lb_ref, q_ref, z_ref, v_ref, o_ref, st_ref, kk_s, b_s, vf_s = refs
    C = HGRN_CHUNK
    nch = sc // C

    @pl.when(pl.program_id(1) == 0)
    def _():
        st_ref[...] = jnp.zeros_like(st_ref)

    lb = lb_ref[...]
    row = lax.broadcasted_iota(jnp.int32, (C, C), 0)
    col = lax.broadcasted_iota(jnp.int32, (C, C), 1)
    tri = (row <= col) if reverse else (row >= col)
    trib = jnp.where(tri, 1.0, 0.0).astype(BF16)

    minb = None
    for c in range(nch):
        rows = slice(c * C, (c + 1) * C)
        z = z_ref[rows, :].astype(F32)
        gate = (1.0 - lb) * jax.nn.sigmoid(z)
        lf = jnp.log(lb + gate)
        kk_s[rows, :] = (1.0 - lb) - gate
        hi = lf.astype(BF16)
        r1 = lf - hi.astype(F32)
        mid = r1.astype(BF16)
        lo = (r1 - mid.astype(F32)).astype(BF16)
        b = (jnp.dot(trib, hi, preferred_element_type=F32)
             + jnp.dot(trib, mid, preferred_element_type=F32)
             + jnp.dot(trib, lo, preferred_element_type=F32))
        b_s[rows, :] = b
        mb = jnp.min(b)
        minb = mb if minb is None else jnp.minimum(minb, mb)

    rid = lax.broadcasted_iota(jnp.int32, (C, 1), 0)

    def chunk(i, carry, fast):
        c = (nch - 1 - i) if reverse else i
        r0 = pl.multiple_of(c * C, C)
        rows = pl.ds(r0, C)
        for h in range(HGRN_HEADS):
            cols = slice(h * HGRN_DK, (h + 1) * HGRN_DK)
            q = q_ref[rows, cols].astype(F32)
            kk = kk_s[rows, cols]
            b = b_s[rows, cols]
            v = v_ref[rows, cols]
            bl = b[0:1, :] if reverse else b[C - 1:C, :]
            if fast:
                bm = 0.5 * bl
                qd = (q * jnp.exp(b - bm)).astype(BF16)
                kd = (kk * jnp.exp(bm - b)).astype(BF16)
                s = lax.dot_general(qd, kd, NT_DIMS, preferred_element_type=F32)
                s = jnp.where(tri, s, 0.0).astype(BF16)
                o = jnp.dot(s, v, preferred_element_type=F32)
            else:
                def sbody(g_i, o_acc):
                    grp = pl.ds(pl.multiple_of(r0 + g_i * SUBLANES, SUBLANES), SUBLANES)
                    b8 = b_s[grp, cols]
                    k8 = kk_s[grp, cols]
                    v8 = vf_s[grp, cols]
                    for jj in range(SUBLANES):
                        s_i = g_i * SUBLANES + jj
                        w = q * k8[jj:jj + 1, :] * jnp.exp(jnp.minimum(b - b8[jj:jj + 1, :], 0.0))
                        scol = jnp.sum(w, axis=-1, keepdims=True)
                        keep = (rid <= s_i) if reverse else (rid >= s_i)
                        o_acc = o_acc + jnp.where(keep, scol, 0.0) * v8[jj:jj + 1, :]
                    return o_acc
                o = lax.fori_loop(0, C // SUBLANES, sbody, jnp.zeros((C, HGRN_DK), F32))
            st = st_ref[h]
            qi = (q * jnp.exp(b)).astype(BF16)
            o = o + lax.dot_general(qi, st.astype(BF16), NT_DIMS, preferred_element_type=F32)
            ke = (kk * jnp.exp(bl - b)).astype(BF16)
            upd = lax.dot_general(v, ke, TN_DIMS, preferred_element_type=F32)
            st_ref[h] = st * jnp.exp(bl) + upd
            if epilogue:
                os_s[rows, cols] = o + of_ref[rows, cols].astype(F32)
            else:
                o_ref[rows, cols] = o.astype(o_ref.dtype)
        return carry

    safe = minb >= -HGRN_SAFE_RANGE

    @pl.when(safe)
    def _():
        lax.fori_loop(0, nch, functools.partial(chunk, fast=True), 0, unroll=4)

    @pl.when(jnp.logical_not(safe))
    def _():
        vf_s[...] = v_ref[...].astype(F32)
        lax.fori_loop(0, nch, functools.partial(chunk, fast=False), 0)

    if epilogue:
        nw = nw_ref[...]
        for h in range(HGRN_HEADS):
            cols = slice(h * HGRN_DK, (h + 1) * HGRN_DK)
            os = os_s[:, cols]
            ms = jnp.mean(os * os, axis=-1, keepdims=True)
            y = os * lax.rsqrt(ms + RMS_EPS) * nw
            g = g_ref[:, cols].astype(F32)
            o_ref[:, cols] = (y * (g * jax.nn.sigmoid(g))).astype(o_ref.dtype)


def _hgrn(proj, lb, norm_w, batch, seq, sc):
    t = batch * seq
    ns = seq // sc
    blk = (sc, D_MODEL)

    def spec(colblk, reverse):
        if reverse:
            return pl.BlockSpec(blk, lambda b, n: (b * ns + ns - 1 - n, colblk))
        return pl.BlockSpec(blk, lambda b, n: (b * ns + n, colblk))

    def row_spec(reverse):
        if reverse:
            return pl.BlockSpec(blk, lambda b, n: (b * ns + ns - 1 - n, 0))
        return pl.BlockSpec(blk, lambda b, n: (b * ns + n, 0))

    lb_spec = pl.BlockSpec((1, D_MODEL), lambda b, n: (0, 0))
    common_scratch = [pltpu.VMEM((HGRN_HEADS, HGRN_DK, HGRN_DK), F32),
                      pltpu.VMEM(blk, F32), pltpu.VMEM(blk, F32), pltpu.VMEM(blk, F32)]
    o_f = pl.pallas_call(
        functools.partial(_hgrn_kernel, reverse=False, epilogue=False, sc=sc),
        out_shape=jax.ShapeDtypeStruct((t, D_MODEL), BF16),
        grid=(batch, ns),
        in_specs=[lb_spec, spec(COL_Q, False), spec(COL_ZF, False), spec(COL_I, False)],
        out_specs=row_spec(False),
        scratch_shapes=common_scratch,
        compiler_params=_params(("arbitrary", "arbitrary")),
        name="hgrn_fwd",
    )(lb[0:1], proj, proj, proj)
    o_a = pl.pallas_call(
        functools.partial(_hgrn_kernel, reverse=True, epilogue=True, sc=sc),
        out_shape=jax.ShapeDtypeStruct((t, D_MODEL), BF16),
        grid=(batch, ns),
        in_specs=[lb_spec, spec(COL_Q, True), spec(COL_ZB, True), spec(COL_I, True),
                  row_spec(True), spec(COL_G, True),
                  pl.BlockSpec((1, HGRN_DK), lambda b, n: (0, 0))],
        out_specs=row_spec(True),
        scratch_shapes=common_scratch + [pltpu.VMEM(blk, F32)],
        compiler_params=_params(("arbitrary", "arbitrary")),
        name="hgrn_bwd",
    )(lb[1:2], proj, proj, proj, o_f, proj, norm_w.reshape(1, HGRN_DK))
    return o_a


def _mla_prep_kernel(dq_ref, dkv_ref, qnw_ref, kvnw_ref, wq_ref, wk_ref, wv_ref, cos_ref, sin_ref,
                     q_ref, k_ref, v_ref):
    scale = (MLA_NOPE + MLA_ROPE) ** -0.5 * LOG2E
    cos = cos_ref[...]
    sin = sin_ref[...]
    dq = dq_ref[...].astype(F32)
    ms = jnp.sum(dq * dq, axis=-1, keepdims=True) * (1.0 / MLA_Q_LORA)
    cq = (dq * lax.rsqrt(ms + RMS_EPS) * qnw_ref[...]).astype(BF16)
    qa = jnp.dot(cq, wq_ref[...], preferred_element_type=F32)
    for h in range(MLA_HEADS):
        base = h * QK_PAD
        q_ref[:, base:base + MLA_NOPE] = (qa[:, base:base + MLA_NOPE] * scale).astype(BF16)
        rp = qa[:, base + MLA_NOPE:base + QK_PAD]
        sw = qa[:, MLA_HEADS * QK_PAD + h * LANES:MLA_HEADS * QK_PAD + (h + 1) * LANES]
        q_ref[:, base + MLA_NOPE:base + QK_PAD] = ((rp * cos + sw * sin) * scale).astype(BF16)
    dkv = dkv_ref[...].astype(F32)
    ckv = dkv[:, :MLA_KV_LORA]
    msk = jnp.mean(ckv * ckv, axis=-1, keepdims=True)
    cn = (ckv * lax.rsqrt(msk + RMS_EPS) * kvnw_ref[...]).astype(BF16)
    kn = jnp.dot(cn, wk_ref[...], preferred_element_type=F32)
    vv = jnp.dot(cn, wv_ref[...], preferred_element_type=F32).astype(BF16)
    ones_col = jnp.where(lax.broadcasted_iota(jnp.int32, (vv.shape[0], LANES), 1) == 0, 1.0, 0.0).astype(BF16)
    for h in range(MLA_HEADS):
        v_ref[:, h * V_PAD:h * V_PAD + MLA_V] = vv[:, h * MLA_V:(h + 1) * MLA_V]
        v_ref[:, h * V_PAD + MLA_V:(h + 1) * V_PAD] = ones_col
    kr = (dkv[:, MLA_KV_LORA:MLA_KV_LORA + LANES] * cos
          + dkv[:, MLA_KV_LORA + LANES:MLA_KV_LORA + 2 * LANES] * sin).astype(BF16)
    for h in range(MLA_HEADS):
        base = h * QK_PAD
        k_ref[:, base:base + MLA_NOPE] = kn[:, h * MLA_NOPE:(h + 1) * MLA_NOPE].astype(BF16)
        k_ref[:, base + MLA_NOPE:base + QK_PAD] = kr


def _mla_prep(proj, qnw, kvnw, wq, wk, wv, cos, sin, batch, seq, tm):
    t = batch * seq
    npos = seq // tm
    full = lambda shape: pl.BlockSpec(shape, lambda i: (0, 0))
    return pl.pallas_call(
        _mla_prep_kernel,
        out_shape=(jax.ShapeDtypeStruct((t, MLA_HEADS * QK_PAD), BF16),
                   jax.ShapeDtypeStruct((t, MLA_HEADS * QK_PAD), BF16),
                   jax.ShapeDtypeStruct((t, MLA_HEADS * V_PAD), BF16)),
        grid=(t // tm,),
        in_specs=[pl.BlockSpec((tm, 512), lambda i: (i, COL_DQ)),
                  pl.BlockSpec((tm, 512), lambda i: (i, COL_DKV)),
                  full((1, 512)), full((1, MLA_KV_LORA)),
                  full(wq.shape), full(wk.shape), full(wv.shape),
                  pl.BlockSpec((tm, LANES), lambda i: (i % npos, 0)),
                  pl.BlockSpec((tm, LANES), lambda i: (i % npos, 0))],
        out_specs=(pl.BlockSpec((tm, MLA_HEADS * QK_PAD), lambda i: (i, 0)),
                   pl.BlockSpec((tm, MLA_HEADS * QK_PAD), lambda i: (i, 0)),
                   pl.BlockSpec((tm, MLA_HEADS * V_PAD), lambda i: (i, 0))),
        compiler_params=_params(("arbitrary",)),
        name="mla_prep",
    )(proj, proj, qnw, kvnw, wq, wk, wv, cos, sin)


def _flash_kernel(q_ref, k_ref, v_ref, o_ref, m_s, acc_s, s_buf, *, bk, nk):
    m_s[...] = jnp.full(m_s.shape, -jnp.inf, F32)
    acc_s[...] = jnp.zeros(acc_s.shape, F32)
    nt = bk // LANES
    bq = q_ref.shape[0]
    nsplit = 2 if bq % 256 == 0 else 1
    hq = bq // nsplit
    qrs = [slice(hf * hq, (hf + 1) * hq) for hf in range(nsplit)]

    def scores(j, slot):
        rows = pl.ds(pl.multiple_of(j * bk, bk), bk)
        for qr in qrs:
            s_buf[slot, qr, :] = lax.dot_general(q_ref[qr, :], k_ref[rows, :], NT_DIMS,
                                                 preferred_element_type=F32)

    def consume(j, slot):
        rows = pl.ds(pl.multiple_of(j * bk, bk), bk)
        ps, alphas = [], []
        for qr in qrs:
            tiles = [s_buf[slot, qr, t * LANES:(t + 1) * LANES] for t in range(nt)]
            tmax = tiles[0]
            for t in range(1, nt):
                tmax = jnp.maximum(tmax, tiles[t])
            m_prev = m_s[qr, :]
            m_new = jnp.maximum(m_prev, jnp.max(tmax, axis=-1, keepdims=True))
            alphas.append(jnp.exp2(m_prev - m_new))
            ps.append(jnp.concatenate([jnp.exp2(tl - m_new).astype(BF16) for tl in tiles], axis=1))
            m_s[qr, :] = m_new
        for qr, p, a in zip(qrs, ps, alphas):
            pv = jnp.dot(p, v_ref[rows, :], preferred_element_type=F32)
            acc_s[qr, :] = jnp.concatenate([a, a], axis=1) * acc_s[qr, :] + pv

    scores(0, 0)
    npairs = (nk - 1) // 2

    def body(i, carry):
        j = 2 * i
        scores(j + 1, 1)
        consume(j, 0)
        scores(j + 2, 0)
        consume(j + 1, 1)
        return carry

    lax.fori_loop(0, npairs, body, 0)
    if nk - 2 * npairs == 2:
        scores(nk - 1, 1)
        consume(nk - 2, 0)
        consume(nk - 1, 1)
    else:
        consume(nk - 1, 0)
    acc = acc_s[...]
    o_ref[...] = (acc[:, :MLA_V] / acc[:, MLA_V:MLA_V + 1]).astype(o_ref.dtype)


def _flash(q, k, v, batch, seq, bq, bk):
    t = batch * seq
    nq = seq // bq
    return pl.pallas_call(
        functools.partial(_flash_kernel, bk=bk, nk=seq // bk),
        out_shape=jax.ShapeDtypeStruct((t, MLA_HEADS * MLA_V), BF16),
        grid=(batch, MLA_HEADS, nq),
        in_specs=[pl.BlockSpec((bq, QK_PAD), lambda b, h, i: (b * nq + i, h)),
                  pl.BlockSpec((seq, QK_PAD), lambda b, h, i: (b, h)),
                  pl.BlockSpec((seq, V_PAD), lambda b, h, i: (b, h))],
        out_specs=pl.BlockSpec((bq, MLA_V), lambda b, h, i: (b * nq + i, h)),
        scratch_shapes=[pltpu.VMEM((bq, LANES), F32), pltpu.VMEM((bq, V_PAD), F32),
                        pltpu.VMEM((2, bq, bk), F32)],
        compiler_params=_params(("arbitrary", "arbitrary", "arbitrary")),
        name="mla_flash",
    )(q, k, v)


def _layernorm(y, w, b):
    mu = jnp.mean(y, axis=-1, keepdims=True)
    yc = y - mu
    var = jnp.mean(yc * yc, axis=-1, keepdims=True)
    return yc * lax.rsqrt(var + LN_EPS) * w + b


def _merge_kernel(x_ref, ga_ref, gb_ref, gm_ref, mq_ref, oa_ref, ob_ref, kvm_ref, wout_ref,
                  l1w_ref, l1b_ref, rw_ref, rb_ref,
                  x1_ref, x1p_ref, ri_ref, rg_ref, cnt_ref, carry_s, *, tm):
    @pl.when(pl.program_id(0) == 0)
    def _():
        carry_s[...] = jnp.zeros_like(carry_s)

    parts = []
    for h in range(MEM_HEADS):
        cols = slice(h * MEM_HEAD_DIM, (h + 1) * MEM_HEAD_DIM)
        kh = kvm_ref[:, cols]
        vh = kvm_ref[:, D_MODEL + h * MEM_HEAD_DIM:D_MODEL + (h + 1) * MEM_HEAD_DIM]
        s = lax.dot_general(mq_ref[:, cols], kh, NT_DIMS, preferred_element_type=F32) * (MEM_HEAD_DIM ** -0.5)
        s = s - jnp.max(s, axis=-1, keepdims=True)
        p = jnp.exp(s)
        p = p / jnp.sum(p, axis=-1, keepdims=True)
        parts.append(jnp.dot(p.astype(BF16), vh, preferred_element_type=F32))
    om = jnp.concatenate(parts, axis=1)

    merged = (jax.nn.sigmoid(ga_ref[...].astype(F32)) * oa_ref[...].astype(F32)
              + jax.nn.sigmoid(gb_ref[...].astype(F32)) * ob_ref[...].astype(F32)
              + jax.nn.sigmoid(gm_ref[...].astype(F32)) * om)
    y = DN_ALPHA * x_ref[...] + jnp.dot(merged.astype(BF16), wout_ref[...], preferred_element_type=F32)
    x1 = _layernorm(y, l1w_ref[...], l1b_ref[...])
    x1_ref[...] = x1
    x1p_ref[...] = _pack_bf16_pair(x1[:, :HALF_D], x1[:, HALF_D:])

    x_hi = x1.astype(BF16)
    x_lo = (x1 - x_hi.astype(F32)).astype(BF16)
    hi_part = jnp.dot(x_hi, rw_ref[...], preferred_element_type=F32)
    lo_part = jnp.dot(x_lo, rw_ref[:, :LANES], preferred_element_type=F32)
    logits = hi_part[:, :LANES] + hi_part[:, LANES:] + lo_part + rb_ref[...]
    lane_i = lax.broadcasted_iota(jnp.int32, (tm, LANES), 1)
    lane = lane_i.astype(F32)
    work = logits
    idx, val = [], []
    for _ in range(TOP_K):
        mx = jnp.max(work, axis=-1, keepdims=True)
        ix = jnp.min(jnp.where(work == mx, lane, float(LANES)), axis=-1, keepdims=True)
        idx.append(ix)
        val.append(mx)
        work = jnp.where(lane == ix, -jnp.inf, work)
    ex = [jnp.exp(v - val[0]) for v in val]
    tot = ex[0] + ex[1] + ex[2] + ex[3]
    hot = [jnp.where(lane == ix, 1.0, 0.0) for ix in idx]
    multi = hot[0] + hot[1] + hot[2] + hot[3]
    r = lax.broadcasted_iota(jnp.int32, (tm, tm), 0)
    c = lax.broadcasted_iota(jnp.int32, (tm, tm), 1)
    lower = jnp.where(r > c, 1.0, 0.0).astype(BF16)
    before = jnp.dot(lower, multi.astype(BF16), preferred_element_type=F32) + carry_s[0:1, :]
    ri = jnp.zeros((tm, LANES), F32)
    rg = jnp.zeros((tm, LANES), F32)
    for j in range(TOP_K):
        rank = jnp.sum(before * hot[j], axis=-1, keepdims=True)
        ri = ri + jnp.where(lane == float(j), idx[j], 0.0) + jnp.where(lane == float(TOP_K + j), rank, 0.0)
        rg = rg + jnp.where(lane == float(j), ex[j] / tot, 0.0)
    ri_ref[...] = ri.T[:2 * TOP_K, :].astype(jnp.int32)
    rg_ref[...] = rg
    carry_s[...] = carry_s[...] + jnp.sum(multi, axis=0, keepdims=True)
    cnt_ref[...] = carry_s[...]


def _merge(x2d, proj, o_a, o_b, kvm, wout, l1w, l1b, rw, rb, batch, seq, tm):
    t = batch * seq
    per_b = seq // tm
    nmem = kvm.shape[0] // batch
    tile = lambda colblk: pl.BlockSpec((tm, D_MODEL), lambda i: (i, colblk))
    full = lambda shape: pl.BlockSpec(shape, lambda i: (0, 0))
    return pl.pallas_call(
        functools.partial(_merge_kernel, tm=tm),
        out_shape=(jax.ShapeDtypeStruct((t, D_MODEL), F32),
                   jax.ShapeDtypeStruct((t, HALF_D), jnp.int32),
                   jax.ShapeDtypeStruct((2 * TOP_K, t), jnp.int32),
                   jax.ShapeDtypeStruct((t, LANES), F32),
                   jax.ShapeDtypeStruct((8, LANES), F32)),
        grid=(t // tm,),
        in_specs=[tile(0), tile(COL_GA), tile(COL_GB), tile(COL_GM), tile(COL_MQ), tile(0), tile(0),
                  pl.BlockSpec((nmem, 2 * D_MODEL), lambda i: (i // per_b, 0)),
                  full((D_MODEL, D_MODEL)), full((1, D_MODEL)), full((1, D_MODEL)),
                  full((D_MODEL, 2 * LANES)), full((1, LANES))],
        out_specs=(tile(0), pl.BlockSpec((tm, HALF_D), lambda i: (i, 0)),
                   pl.BlockSpec((2 * TOP_K, tm), lambda i: (0, i)),
                   pl.BlockSpec((tm, LANES), lambda i: (i, 0)), full((8, LANES))),
        scratch_shapes=[pltpu.VMEM((8, LANES), F32)],
        compiler_params=_params(("arbitrary",)),
        name="merge_router",
    )(x2d, proj, proj, proj, proj, o_a, o_b, kvm, wout, l1w, l1b, rw, rb)


def _sc_gather(table, idx):
    p, d = idx.shape[0], table.shape[1]
    per_w = p // SC_WORKERS
    assert per_w * SC_WORKERS == p and per_w % SC_GATHER_ROWS == 0
    mesh = plsc.VectorSubcoreMesh(core_axis_name="c", subcore_axis_name="s")

    @functools.partial(
        pl.kernel, mesh=mesh,
        out_type=jax.ShapeDtypeStruct((p, d), table.dtype),
        scratch_types=[pltpu.VMEM((SC_GATHER_ROWS,), jnp.int32),
                       pltpu.VMEM((SC_GATHER_ROWS, d), table.dtype),
                       pltpu.SemaphoreType.DMA],
        name="moe_gather_sc",
    )
    def gather_kernel(table_hbm, idx_hbm, out_hbm, idx_v, rows_v, sem):
        wid = lax.axis_index("s") * SC_CORES + lax.axis_index("c")
        base = wid * per_w

        @pl.loop(0, per_w // SC_GATHER_ROWS)
        def _(c):
            off = base + c * SC_GATHER_ROWS
            pltpu.sync_copy(idx_hbm.at[pl.ds(off, SC_GATHER_ROWS)], idx_v)
            pltpu.async_copy(table_hbm.at[idx_v], rows_v, sem).wait()
            pltpu.sync_copy(rows_v, out_hbm.at[pl.ds(off, SC_GATHER_ROWS)])

    return gather_kernel(table, idx)


def _sc_dispatch(x, dest_t, slots):
    t, d = x.shape
    per_w = t // SC_WORKERS
    assert per_w * SC_WORKERS == t and per_w % SC_GATHER_ROWS == 0
    mesh = plsc.VectorSubcoreMesh(core_axis_name="c", subcore_axis_name="s")

    @functools.partial(
        pl.kernel, mesh=mesh,
        out_type=jax.ShapeDtypeStruct((slots, d), x.dtype),
        scratch_types=[pltpu.VMEM((SC_GATHER_ROWS,), jnp.int32),
                       pltpu.VMEM((SC_GATHER_ROWS, d), x.dtype)],
        name="moe_dispatch_sc",
    )
    def scatter_kernel(x_hbm, idx_hbm, out_hbm, idx_v, rows_v):
        wid = lax.axis_index("s") * SC_CORES + lax.axis_index("c")
        base = wid * per_w

        @pl.loop(0, per_w // SC_GATHER_ROWS)
        def _(c):
            off = base + c * SC_GATHER_ROWS
            pltpu.sync_copy(x_hbm.at[pl.ds(off, SC_GATHER_ROWS)], rows_v)
            for j in range(TOP_K):
                pltpu.sync_copy(idx_hbm.at[pl.ds(j * t + off, SC_GATHER_ROWS)], idx_v)
                pltpu.sync_copy(rows_v, out_hbm.at[idx_v])

    return scatter_kernel(x, dest_t)


HALF_D = D_MODEL // 2
HI16 = -65536


def _pack_bf16_pair(a, b):
    wa = lax.bitcast_convert_type(a.astype(BF16).astype(F32), jnp.int32)
    wb = lax.bitcast_convert_type(b.astype(BF16).astype(F32), jnp.int32)
    return wa | lax.shift_right_logical(wb, 16)


def _unpack_bf16_pair(w):
    a = lax.bitcast_convert_type(w & HI16, F32)
    b = lax.bitcast_convert_type(lax.shift_left(w, 16), F32)
    return a, b


def _expert_kernel(be_ref, nu_ref, xs_ref, wgu_ref, bgu_ref, wd_ref, bd_ref, ys_ref, wgu_b, wd_b):
    i = pl.program_id(0)

    @pl.when(jnp.logical_or(i == 0, be_ref[i] != be_ref[jnp.maximum(i - 1, 0)]))
    def _():
        wgu_b[...] = wgu_ref[0].astype(BF16)
        wd_b[...] = wd_ref[0].astype(BF16)

    @pl.when(i < nu_ref[0])
    def _():
        xa, xb = _unpack_bf16_pair(xs_ref[...])
        x = jnp.concatenate([xa, xb], axis=1).astype(BF16)
        h = jnp.dot(x, wgu_b[...], preferred_element_type=F32) + bgu_ref[0]
        g = jnp.minimum(h[:, :D_FF], SWIGLU_LIMIT)
        u = jnp.clip(h[:, D_FF:], -SWIGLU_LIMIT, SWIGLU_LIMIT)
        act = (u + 1.0) * g * jax.nn.sigmoid(SWIGLU_ALPHA * g)
        y = jnp.dot(act.astype(BF16), wd_b[...], preferred_element_type=F32) + bd_ref[0]
        ys_ref[...] = _pack_bf16_pair(y[:, :HALF_D], y[:, HALF_D:])

    @pl.when(i >= nu_ref[0])
    def _():
        ys_ref[...] = jnp.zeros_like(ys_ref)


def _experts(block_e, n_used, xs, wgu, bgu, wd, bd):
    slots = xs.shape[0]
    nb = slots // EXPERT_BLOCK
    grid_spec = pltpu.PrefetchScalarGridSpec(
        num_scalar_prefetch=2,
        grid=(nb,),
        in_specs=[pl.BlockSpec((EXPERT_BLOCK, HALF_D), lambda i, be, nu: (i, 0)),
                  pl.BlockSpec((1, D_MODEL, 2 * D_FF), lambda i, be, nu: (be[i], 0, 0)),
                  pl.BlockSpec((1, 1, 2 * D_FF), lambda i, be, nu: (be[i], 0, 0)),
                  pl.BlockSpec((1, D_FF, D_MODEL), lambda i, be, nu: (be[i], 0, 0)),
                  pl.BlockSpec((1, 1, D_MODEL), lambda i, be, nu: (be[i], 0, 0))],
        out_specs=pl.BlockSpec((EXPERT_BLOCK, HALF_D), lambda i, be, nu: (i, 0)),
        scratch_shapes=[pltpu.VMEM((D_MODEL, 2 * D_FF), BF16), pltpu.VMEM((D_FF, D_MODEL), BF16)],
    )
    return pl.pallas_call(
        _expert_kernel,
        out_shape=jax.ShapeDtypeStruct((slots, HALF_D), jnp.int32),
        grid_spec=grid_spec,
        compiler_params=_params(("arbitrary",)),
        name="moe_experts",
    )(block_e, n_used, xs, wgu, bgu, wd, bd)


def _weighted_rows(rg, packed):
    lo = hi = None
    for j, w in enumerate(packed):
        a, b = _unpack_bf16_pair(w)
        gj = rg[:, j:j + 1]
        lo = gj * a if lo is None else lo + gj * a
        hi = gj * b if hi is None else hi + gj * b
    return jnp.concatenate([lo, hi], axis=1)


def _combine_rows_kernel(x1_ref, rg_ref, g0_ref, g1_ref, g2_ref, g3_ref, l2w_ref, l2b_ref, o_ref):
    moe = _weighted_rows(rg_ref[...], [g0_ref[...], g1_ref[...], g2_ref[...], g3_ref[...]])
    o_ref[...] = _layernorm(DN_ALPHA * x1_ref[...] + moe, l2w_ref[...], l2b_ref[...])


def _combine_rows(x1, rg, rows, l2w, l2b, tm):
    t = x1.shape[0]
    nt = t // tm
    full = lambda shape: pl.BlockSpec(shape, lambda i: (0, 0))
    choice = lambda j: pl.BlockSpec((tm, HALF_D), lambda i, j=j: (j * nt + i, 0))
    return pl.pallas_call(
        _combine_rows_kernel,
        out_shape=jax.ShapeDtypeStruct((t, D_MODEL), F32),
        grid=(nt,),
        in_specs=[pl.BlockSpec((tm, D_MODEL), lambda i: (i, 0)),
                  pl.BlockSpec((tm, LANES), lambda i: (i, 0)),
                  choice(0), choice(1), choice(2), choice(3),
                  full((1, D_MODEL)), full((1, D_MODEL))],
        out_specs=pl.BlockSpec((tm, D_MODEL), lambda i: (i, 0)),
        compiler_params=_params(("arbitrary",)),
        name="moe_combine_rows",
    )(x1, rg, rows, rows, rows, rows, l2w, l2b)


def _prep_w_in(w):
    zeros = lambda n: jnp.zeros((D_MODEL, n), w.dtype)
    kr = w[:, 5760:5824]
    kr_sw = jnp.concatenate([kr[:, MLA_ROPE // 2:], kr[:, :MLA_ROPE // 2]], axis=1)
    parts = [w[:, 6848:9920], w[:, 0:5120], w[:, 5824:6848],
             w[:, 5120:5504], zeros(128),
             w[:, 5504:5760], kr, zeros(64), kr_sw, zeros(64)]
    return jnp.concatenate(parts, axis=1).astype(BF16)


def _prep_w_uq(w):
    w3 = w.reshape(MLA_Q_LORA, MLA_HEADS, MLA_NOPE + MLA_ROPE)
    rope = w3[:, :, MLA_NOPE:]
    rope_sw = jnp.concatenate([rope[:, :, MLA_ROPE // 2:], rope[:, :, :MLA_ROPE // 2]], axis=-1)
    pad = jnp.zeros((MLA_Q_LORA, MLA_HEADS, QK_PAD - MLA_NOPE - MLA_ROPE), w.dtype)
    main = jnp.concatenate([w3, pad], axis=-1).reshape(MLA_Q_LORA, MLA_HEADS * QK_PAD)
    swp = jnp.concatenate([rope_sw, pad], axis=-1).reshape(MLA_Q_LORA, MLA_HEADS * LANES)
    both = jnp.concatenate([main, swp], axis=1)
    return jnp.pad(both, ((0, 512 - MLA_Q_LORA), (0, 0))).astype(BF16)


def _rope_tables(seq):
    inv_freq = ROPE_THETA ** (-jnp.arange(0, MLA_ROPE, 2, dtype=F32) / MLA_ROPE)
    ang = jnp.arange(seq, dtype=F32)[:, None] * inv_freq[None, :]
    cos, sin = jnp.cos(ang), jnp.sin(ang)
    pad = jnp.zeros((seq, LANES - MLA_ROPE), F32)
    return (jnp.concatenate([cos, cos, pad], axis=1), jnp.concatenate([-sin, sin, pad], axis=1))


def _split_bf16(w):
    hi = w.astype(BF16)
    lo = (w - hi.astype(F32)).astype(BF16)
    return jnp.concatenate([hi, lo], axis=1)


def _tile(n, pref):
    return pref if n % pref == 0 else n


def _layer(x, mem, wts):
    batch, seq, _ = x.shape
    t = batch * seq
    x2d = x.reshape(t, D_MODEL)

    proj = _matmul(x2d, wts["w_in"], BF16, _tile(t, 1024), 2048, "in_proj")
    o_a = _hgrn(proj, wts["lb"], wts["hgrn_norm_w"], batch, seq, _tile(seq, 512))

    cos, sin = _rope_tables(seq)
    q, k, v = _mla_prep(proj, wts["qnw"], wts["kvnw"], wts["w_uq"], wts["w_uk"], wts["w_uv"],
                        cos, sin, batch, seq, _tile(seq, 512))
    o_b = _flash(q, k, v, batch, seq, _tile(seq, 1024), _tile(seq, 1024))

    nmem = mem.shape[1]
    kvm = _matmul(mem.reshape(batch * nmem, D_MODEL), wts["mem_w_kv"], BF16,
                  _tile(batch * nmem, 512), 1024, "mem_kv")

    tm = _tile(seq, 512)
    x1, x1p, ri, rg, cnt = _merge(x2d, proj, o_a, o_b, kvm, wts["w_out"], wts["ln1_w"], wts["ln1_b"],
                             wts["router_w"], wts["router_b"], batch, seq, tm)

    counts = cnt[0, :N_EXPERTS].astype(jnp.int32)
    padded = (counts + EXPERT_BLOCK - 1) // EXPERT_BLOCK * EXPERT_BLOCK
    pad_end = jnp.cumsum(padded)
    pad_start = pad_end - padded
    experts = jnp.arange(N_EXPERTS, dtype=jnp.int32)[:, None, None]
    start_ct = jnp.sum(jnp.where(ri[None, :TOP_K] == experts, pad_start[:, None, None], 0), axis=0)
    dest_ct = (start_ct + ri[TOP_K:]).astype(jnp.int32)
    nb = t * TOP_K // EXPERT_BLOCK + N_EXPERTS
    blk_start = jnp.arange(nb, dtype=jnp.int32) * EXPERT_BLOCK
    block_e = jnp.minimum(jnp.sum((pad_end[None, :] <= blk_start[:, None]).astype(jnp.int32), axis=1),
                          N_EXPERTS - 1).astype(jnp.int32)
    n_used = (pad_end[-1:] // EXPERT_BLOCK).astype(jnp.int32)
    dest_flat = dest_ct.reshape(TOP_K * t)
    xs = _sc_dispatch(x1p, dest_flat, nb * EXPERT_BLOCK)
    ys = _experts(block_e, n_used, xs, wts["exp_w_gu"], wts["exp_b_gu"], wts["exp_w_down"], wts["exp_b_down"])
    rows = _sc_gather(ys, dest_flat)
    y = _combine_rows(x1, rg, rows, wts["ln2_w"], wts["ln2_b"], _tile(t, 512))
    return y.reshape(batch, seq, D_MODEL)


def kernel(x_prompt, x_sample, mem_prompt, mem_sample, w_in, hgrn_lb_logits, hgrn_norm_w,
           mla_q_norm_w, mla_w_uq, mla_kv_norm_w, mla_w_uk, mla_w_uv, mem_w_kv, w_out,
           ln1_w, ln1_b, router_w, router_b, exp_w_gu, exp_b_gu, exp_w_down, exp_b_down,
           ln2_w, ln2_b):
    depth = w_in.shape[0]
    gamma = jax.nn.softmax(hgrn_lb_logits.astype(F32), axis=1)
    cum = jnp.cumsum(gamma, axis=1)
    lb_all = cum[:, 1:] - cum[:, :1]
    y_prompt, y_sample = x_prompt, x_sample
    for l in range(depth):
        row = lambda a: a[l].reshape(1, -1).astype(F32)
        wts = {
            "w_in": _prep_w_in(w_in[l]),
            "lb": lb_all[:, l],
            "hgrn_norm_w": hgrn_norm_w[l].astype(F32),
            "qnw": jnp.pad(row(mla_q_norm_w), ((0, 0), (0, 512 - MLA_Q_LORA))),
            "kvnw": row(mla_kv_norm_w),
            "w_uq": _prep_w_uq(mla_w_uq[l]),
            "w_uk": mla_w_uk[l].astype(BF16),
            "w_uv": mla_w_uv[l].astype(BF16),
            "mem_w_kv": mem_w_kv[l].astype(BF16),
            "w_out": w_out[l].astype(BF16),
            "ln1_w": row(ln1_w), "ln1_b": row(ln1_b),
            "router_w": _split_bf16(jnp.pad(router_w[l].astype(F32), ((0, 0), (0, LANES - N_EXPERTS)))),
            "router_b": jnp.pad(row(router_b), ((0, 0), (0, LANES - N_EXPERTS)), constant_values=-jnp.inf),
            "exp_w_gu": exp_w_gu[l].astype(F32),
            "exp_b_gu": exp_b_gu[l].reshape(N_EXPERTS, 1, 2 * D_FF).astype(F32),
            "exp_w_down": exp_w_down[l].astype(F32),
            "exp_b_down": exp_b_down[l].reshape(N_EXPERTS, 1, D_MODEL).astype(F32),
            "ln2_w": row(ln2_w), "ln2_b": row(ln2_b),
        }
        y_prompt = _layer(y_prompt, mem_prompt, wts)
        y_sample = _layer(y_sample, mem_sample, wts)
    return (y_prompt, y_sample)
```

```python
import functools

import numpy as np
import jax
import jax.numpy as jnp
from jax import lax
from jax.experimental import pallas as pl
from jax.experimental.pallas import tpu as pltpu
from jax.experimental.pallas import tpu_sc as plsc

F32 = jnp.float32
BF16 = jnp.bfloat16

D_MODEL = 1024
HGRN_HEADS = 8
HGRN_DK = 128
MLA_HEADS = 8
MLA_Q_LORA = 384
MLA_KV_LORA = 256
MLA_NOPE = 128
MLA_ROPE = 64
MLA_V = 128
ROPE_THETA = 10000.0
MEM_HEADS = 4
MEM_HEAD_DIM = D_MODEL // MEM_HEADS
N_EXPERTS = 32
TOP_K = 4
D_FF = D_MODEL
SWIGLU_LIMIT = 7.0
SWIGLU_ALPHA = 1.702
DN_ALPHA = 2.0 ** 0.25
LN_EPS = 1e-5
RMS_EPS = 1e-6

LANES = 128
SUBLANES = 8
QK_PAD = 256
V_PAD = 256
LOG2E = 1.4426950408889634
HGRN_CHUNK = 64
HGRN_SAFE_RANGE = 160.0
EXPERT_BLOCK = 512
SC_CORES = 2
SC_SUBCORES = 16
SC_WORKERS = SC_CORES * SC_SUBCORES
SC_GATHER_ROWS = 128
VMEM_LIMIT = 56 * 1024 * 1024

COL_GA, COL_GB, COL_GM, COL_Q, COL_ZF, COL_ZB, COL_I, COL_G, COL_MQ = range(9)
COL_DQ, COL_DKV = 18, 19
IN_COLS_PAD = 10240

NT_DIMS = (((1,), (1,)), ((), ()))
TN_DIMS = (((0,), (0,)), ((), ()))


def _params(sem, vmem=VMEM_LIMIT):
    return pltpu.CompilerParams(dimension_semantics=sem, vmem_limit_bytes=vmem)


def _mm_kernel(x_ref, w_ref, o_ref, xb_ref):
    @pl.when(pl.program_id(1) == 0)
    def _():
        xb_ref[...] = x_ref[...].astype(BF16)

    o_ref[...] = jnp.dot(xb_ref[...], w_ref[...], preferred_element_type=F32).astype(o_ref.dtype)


def _matmul(x, w, out_dtype, tm, tn, name):
    m, k = x.shape
    n = w.shape[1]
    return pl.pallas_call(
        _mm_kernel,
        out_shape=jax.ShapeDtypeStruct((m, n), out_dtype),
        grid=(m // tm, n // tn),
        in_specs=[pl.BlockSpec((tm, k), lambda i, j: (i, 0)),
                  pl.BlockSpec((k, tn), lambda i, j: (0, j))],
        out_specs=pl.BlockSpec((tm, tn), lambda i, j: (i, j)),
        scratch_shapes=[pltpu.VMEM((tm, k), BF16)],
        compiler_params=_params(("arbitrary", "arbitrary")),
        name=name,
    )(x, w)


def _hgrn_kernel(*refs, reverse, epilogue, sc):
    if epilogue:
        lb_ref, q_ref, z_ref, v_ref, of_ref, g_ref, nw_ref, o_ref, st_ref, kk_s, b_s, vf_s, os_s = refs
    else:
        lb_ref, q_ref, z_ref, v_ref, o_ref, st_ref, kk_s, b_s, vf_s = refs
    C = HGRN_CHUNK
    nch = sc // C

    @pl.when(pl.program_id(1) == 0)
    def _():
        st_ref[...] = jnp.zeros_like(st_ref)

    lb = lb_ref[...]
    row = lax.broadcasted_iota(jnp.int32, (C, C), 0)
    col = lax.broadcasted_iota(jnp.int32, (C, C), 1)
    tri = (row <= col) if reverse else (row >= col)
    trib = jnp.where(tri, 1.0, 0.0).astype(BF16)

    minb = None
    for c in range(nch):
        rows = slice(c * C, (c + 1) * C)
        z = z_ref[rows, :].astype(F32)
        gate = (1.0 - lb) * jax.nn.sigmoid(z)
        lf = jnp.log(lb + gate)
        kk_s[rows, :] = (1.0 - lb) - gate
        hi = lf.astype(BF16)
        r1 = lf - hi.astype(F32)
        mid = r1.astype(BF16)
        lo = (r1 - mid.astype(F32)).astype(BF16)
        b = (jnp.dot(trib, hi, preferred_element_type=F32)
             + jnp.dot(trib, mid, preferred_element_type=F32)
             + jnp.dot(trib, lo, preferred_element_type=F32))
        b_s[rows, :] = b
        mb = jnp.min(b)
        minb = mb if minb is None else jnp.minimum(minb, mb)

    rid = lax.broadcasted_iota(jnp.int32, (C, 1), 0)

    def chunk(i, carry, fast):
        c = (nch - 1 - i) if reverse else i
        r0 = pl.multiple_of(c * C, C)
        rows = pl.ds(r0, C)
        for h in range(HGRN_HEADS):
            cols = slice(h * HGRN_DK, (h + 1) * HGRN_DK)
            q = q_ref[rows, cols].astype(F32)
            kk = kk_s[rows, cols]
            b = b_s[rows, cols]
            v = v_ref[rows, cols]
            bl = b[0:1, :] if reverse else b[C - 1:C, :]
            if fast:
                bm = 0.5 * bl
                qd = (q * jnp.exp(b - bm)).astype(BF16)
                kd = (kk * jnp.exp(bm - b)).astype(BF16)
                s = lax.dot_general(qd, kd, NT_DIMS, preferred_element_type=F32)
                s = jnp.where(tri, s, 0.0).astype(BF16)
                o = jnp.dot(s, v, preferred_element_type=F32)
            else:
                def sbody(g_i, o_acc):
                    grp = pl.ds(pl.multiple_of(r0 + g_i * SUBLANES, SUBLANES), SUBLANES)
                    b8 = b_s[grp, cols]
                    k8 = kk_s[grp, cols]
                    v8 = vf_s[grp, cols]
                    for jj in range(SUBLANES):
                        s_i = g_i * SUBLANES + jj
                        w = q * k8[jj:jj + 1, :] * jnp.exp(jnp.minimum(b - b8[jj:jj + 1, :], 0.0))
                        scol = jnp.sum(w, axis=-1, keepdims=True)
                        keep = (rid <= s_i) if reverse else (rid >= s_i)
                        o_acc = o_acc + jnp.where(keep, scol, 0.0) * v8[jj:jj + 1, :]
                    return o_acc
                o = lax.fori_loop(0, C // SUBLANES, sbody, jnp.zeros((C, HGRN_DK), F32))
            st = st_ref[h]
            qi = (q * jnp.exp(b)).astype(BF16)
            o = o + lax.dot_general(qi, st.astype(BF16), NT_DIMS, preferred_element_type=F32)
            ke = (kk * jnp.exp(bl - b)).astype(BF16)
            upd = lax.dot_general(v, ke, TN_DIMS, preferred_element_type=F32)
            st_ref[h] = st * jnp.exp(bl) + upd
            if epilogue:
                os_s[rows, cols] = o + of_ref[rows, cols].astype(F32)
            else:
                o_ref[rows, cols] = o.astype(o_ref.dtype)
        return carry

    safe = minb >= -HGRN_SAFE_RANGE

    @pl.when(safe)
    def _():
        lax.fori_loop(0, nch, functools.partial(chunk, fast=True), 0, unroll=8)

    @pl.when(jnp.logical_not(safe))
    def _():
        vf_s[...] = v_ref[...].astype(F32)
        lax.fori_loop(0, nch, functools.partial(chunk, fast=False), 0)

    if epilogue:
        nw = nw_ref[...]
        for h in range(HGRN_HEADS):
            cols = slice(h * HGRN_DK, (h + 1) * HGRN_DK)
            os = os_s[:, cols]
            ms = jnp.mean(os * os, axis=-1, keepdims=True)
            y = os * lax.rsqrt(ms + RMS_EPS) * nw
            g = g_ref[:, cols].astype(F32)
            o_ref[:, cols] = (y * (g * jax.nn.sigmoid(g))).astype(o_ref.dtype)


def _hgrn(proj, lb, norm_w, batch, seq, sc):
    t = batch * seq
    ns = seq // sc
    blk = (sc, D_MODEL)

    def spec(colblk, reverse):
        if reverse:
            return pl.BlockSpec(blk, lambda b, n: (b * ns + ns - 1 - n, colblk))
        return pl.BlockSpec(blk, lambda b, n: (b * ns + n, colblk))

    def row_spec(reverse):
        if reverse:
            return pl.BlockSpec(blk, lambda b, n: (b * ns + ns - 1 - n, 0))
        return pl.BlockSpec(blk, lambda b, n: (b * ns + n, 0))

    lb_spec = pl.BlockSpec((1, D_MODEL), lambda b, n: (0, 0))
    common_scratch = [pltpu.VMEM((HGRN_HEADS, HGRN_DK, HGRN_DK), F32),
                      pltpu.VMEM(blk, F32), pltpu.VMEM(blk, F32), pltpu.VMEM(blk, F32)]
    o_f = pl.pallas_call(
        functools.partial(_hgrn_kernel, reverse=False, epilogue=False, sc=sc),
        out_shape=jax.ShapeDtypeStruct((t, D_MODEL), BF16),
        grid=(batch, ns),
        in_specs=[lb_spec, spec(COL_Q, False), spec(COL_ZF, False), spec(COL_I, False)],
        out_specs=row_spec(False),
        scratch_shapes=common_scratch,
        compiler_params=_params(("arbitrary", "arbitrary")),
        name="hgrn_fwd",
    )(lb[0:1], proj, proj, proj)
    o_a = pl.pallas_call(
        functools.partial(_hgrn_kernel, reverse=True, epilogue=True, sc=sc),
        out_shape=jax.ShapeDtypeStruct((t, D_MODEL), BF16),
        grid=(batch, ns),
        in_specs=[lb_spec, spec(COL_Q, True), spec(COL_ZB, True), spec(COL_I, True),
                  row_spec(True), spec(COL_G, True),
                  pl.BlockSpec((1, HGRN_DK), lambda b, n: (0, 0))],
        out_specs=row_spec(True),
        scratch_shapes=common_scratch + [pltpu.VMEM(blk, F32)],
        compiler_params=_params(("arbitrary", "arbitrary")),
        name="hgrn_bwd",
    )(lb[1:2], proj, proj, proj, o_f, proj, norm_w.reshape(1, HGRN_DK))
    return o_a


def _mla_prep_kernel(dq_ref, dkv_ref, qnw_ref, kvnw_ref, wq_ref, wk_ref, wv_ref, cos_ref, sin_ref,
                     q_ref, k_ref, v_ref):
    scale = (MLA_NOPE + MLA_ROPE) ** -0.5 * LOG2E
    cos = cos_ref[...]
    sin = sin_ref[...]
    dq = dq_ref[...].astype(F32)
    ms = jnp.sum(dq * dq, axis=-1, keepdims=True) * (1.0 / MLA_Q_LORA)
    cq = (dq * lax.rsqrt(ms + RMS_EPS) * qnw_ref[...]).astype(BF16)
    qa = jnp.dot(cq, wq_ref[...], preferred_element_type=F32)
    for h in range(MLA_HEADS):
        base = h * QK_PAD
        q_ref[:, base:base + MLA_NOPE] = (qa[:, base:base + MLA_NOPE] * scale).astype(BF16)
        rp = qa[:, base + MLA_NOPE:base + QK_PAD]
        sw = qa[:, MLA_HEADS * QK_PAD + h * LANES:MLA_HEADS * QK_PAD + (h + 1) * LANES]
        q_ref[:, base + MLA_NOPE:base + QK_PAD] = ((rp * cos + sw * sin) * scale).astype(BF16)
    dkv = dkv_ref[...].astype(F32)
    ckv = dkv[:, :MLA_KV_LORA]
    msk = jnp.mean(ckv * ckv, axis=-1, keepdims=True)
    cn = (ckv * lax.rsqrt(msk + RMS_EPS) * kvnw_ref[...]).astype(BF16)
    kn = jnp.dot(cn, wk_ref[...], preferred_element_type=F32)
    vv = jnp.dot(cn, wv_ref[...], preferred_element_type=F32).astype(BF16)
    ones_col = jnp.where(lax.broadcasted_iota(jnp.int32, (vv.shape[0], LANES), 1) == 0, 1.0, 0.0).astype(BF16)
    for h in range(MLA_HEADS):
        v_ref[:, h * V_PAD:h * V_PAD + MLA_V] = vv[:, h * MLA_V:(h + 1) * MLA_V]
        v_ref[:, h * V_PAD + MLA_V:(h + 1) * V_PAD] = ones_col
    kr = (dkv[:, MLA_KV_LORA:MLA_KV_LORA + LANES] * cos
          + dkv[:, MLA_KV_LORA + LANES:MLA_KV_LORA + 2 * LANES] * sin).astype(BF16)
    for h in range(MLA_HEADS):
        base = h * QK_PAD
        k_ref[:, base:base + MLA_NOPE] = kn[:, h * MLA_NOPE:(h + 1) * MLA_NOPE].astype(BF16)
        k_ref[:, base + MLA_NOPE:base + QK_PAD] = kr


def _mla_prep(proj, qnw, kvnw, wq, wk, wv, cos, sin, batch, seq, tm):
    t = batch * seq
    npos = seq // tm
    full = lambda shape: pl.BlockSpec(shape, lambda i: (0, 0))
    return pl.pallas_call(
        _mla_prep_kernel,
        out_shape=(jax.ShapeDtypeStruct((t, MLA_HEADS * QK_PAD), BF16),
                   jax.ShapeDtypeStruct((t, MLA_HEADS * QK_PAD), BF16),
                   jax.ShapeDtypeStruct((t, MLA_HEADS * V_PAD), BF16)),
        grid=(t // tm,),
        in_specs=[pl.BlockSpec((tm, 512), lambda i: (i, COL_DQ)),
                  pl.BlockSpec((tm, 512), lambda i: (i, COL_DKV)),
                  full((1, 512)), full((1, MLA_KV_LORA)),
                  full(wq.shape), full(wk.shape), full(wv.shape),
                  pl.BlockSpec((tm, LANES), lambda i: (i % npos, 0)),
                  pl.BlockSpec((tm, LANES), lambda i: (i % npos, 0))],
        out_specs=(pl.BlockSpec((tm, MLA_HEADS * QK_PAD), lambda i: (i, 0)),
                   pl.BlockSpec((tm, MLA_HEADS * QK_PAD), lambda i: (i, 0)),
                   pl.BlockSpec((tm, MLA_HEADS * V_PAD), lambda i: (i, 0))),
        compiler_params=_params(("arbitrary",)),
        name="mla_prep",
    )(proj, proj, qnw, kvnw, wq, wk, wv, cos, sin)


def _flash_kernel(q_ref, k_ref, v_ref, o_ref, m_s, acc_s, s_buf, *, bk, nk):
    m_s[...] = jnp.full(m_s.shape, -jnp.inf, F32)
    acc_s[...] = jnp.zeros(acc_s.shape, F32)
    nt = bk // LANES
    bq = q_ref.shape[0]
    nsplit = 2 if bq % 256 == 0 else 1
    hq = bq // nsplit
    qrs = [slice(hf * hq, (hf + 1) * hq) for hf in range(nsplit)]

    def scores(j, slot):
        rows = pl.ds(pl.multiple_of(j * bk, bk), bk)
        for qr in qrs:
            s_buf[slot, qr, :] = lax.dot_general(q_ref[qr, :], k_ref[rows, :], NT_DIMS,
                                                 preferred_element_type=F32)

    def consume(j, slot):
        rows = pl.ds(pl.multiple_of(j * bk, bk), bk)
        ps, alphas = [], []
        for qr in qrs:
            tiles = [s_buf[slot, qr, t * LANES:(t + 1) * LANES] for t in range(nt)]
            tmax = tiles[0]
            for t in range(1, nt):
                tmax = jnp.maximum(tmax, tiles[t])
            m_prev = m_s[qr, :]
            m_new = jnp.maximum(m_prev, jnp.max(tmax, axis=-1, keepdims=True))
            alphas.append(jnp.exp2(m_prev - m_new))
            ps.append(jnp.concatenate([jnp.exp2(tl - m_new).astype(BF16) for tl in tiles], axis=1))
            m_s[qr, :] = m_new
        for qr, p, a in zip(qrs, ps, alphas):
            pv = jnp.dot(p, v_ref[rows, :], preferred_element_type=F32)
            acc_s[qr, :] = jnp.concatenate([a, a], axis=1) * acc_s[qr, :] + pv

    scores(0, 0)
    npairs = (nk - 1) // 2

    def body(i, carry):
        j = 2 * i
        scores(j + 1, 1)
        consume(j, 0)
        scores(j + 2, 0)
        consume(j + 1, 1)
        return carry

    lax.fori_loop(0, npairs, body, 0)
    if nk - 2 * npairs == 2:
        scores(nk - 1, 1)
        consume(nk - 2, 0)
        consume(nk - 1, 1)
    else:
        consume(nk - 1, 0)
    acc = acc_s[...]
    o_ref[...] = (acc[:, :MLA_V] / acc[:, MLA_V:MLA_V + 1]).astype(o_ref.dtype)


def _flash(q, k, v, batch, seq, bq, bk):
    t = batch * seq
    nq = seq // bq
    return pl.pallas_call(
        functools.partial(_flash_kernel, bk=bk, nk=seq // bk),
        out_shape=jax.ShapeDtypeStruct((t, MLA_HEADS * MLA_V), BF16),
        grid=(batch, MLA_HEADS, nq),
        in_specs=[pl.BlockSpec((bq, QK_PAD), lambda b, h, i: (b * nq + i, h)),
                  pl.BlockSpec((seq, QK_PAD), lambda b, h, i: (b, h)),
                  pl.BlockSpec((seq, V_PAD), lambda b, h, i: (b, h))],
        out_specs=pl.BlockSpec((bq, MLA_V), lambda b, h, i: (b * nq + i, h)),
        scratch_shapes=[pltpu.VMEM((bq, LANES), F32), pltpu.VMEM((bq, V_PAD), F32),
                        pltpu.VMEM((2, bq, bk), F32)],
        compiler_params=_params(("arbitrary", "arbitrary", "arbitrary")),
        name="mla_flash",
    )(q, k, v)


def _layernorm(y, w, b):
    mu = jnp.mean(y, axis=-1, keepdims=True)
    yc = y - mu
    var = jnp.mean(yc * yc, axis=-1, keepdims=True)
    return yc * lax.rsqrt(var + LN_EPS) * w + b


def _merge_kernel(x_ref, ga_ref, gb_ref, gm_ref, mq_ref, oa_ref, ob_ref, kvm_ref, wout_ref,
                  l1w_ref, l1b_ref, rw_ref, rb_ref,
                  x1_ref, x1p_ref, ri_ref, rg_ref, cnt_ref, carry_s, *, tm):
    @pl.when(pl.program_id(0) == 0)
    def _():
        carry_s[...] = jnp.zeros_like(carry_s)

    parts = []
    for h in range(MEM_HEADS):
        cols = slice(h * MEM_HEAD_DIM, (h + 1) * MEM_HEAD_DIM)
        kh = kvm_ref[:, cols]
        vh = kvm_ref[:, D_MODEL + h * MEM_HEAD_DIM:D_MODEL + (h + 1) * MEM_HEAD_DIM]
        s = lax.dot_general(mq_ref[:, cols], kh, NT_DIMS, preferred_element_type=F32) * (MEM_HEAD_DIM ** -0.5)
        s = s - jnp.max(s, axis=-1, keepdims=True)
        p = jnp.exp(s)
        p = p / jnp.sum(p, axis=-1, keepdims=True)
        parts.append(jnp.dot(p.astype(BF16), vh, preferred_element_type=F32))
    om = jnp.concatenate(parts, axis=1)

    merged = (jax.nn.sigmoid(ga_ref[...].astype(F32)) * oa_ref[...].astype(F32)
              + jax.nn.sigmoid(gb_ref[...].astype(F32)) * ob_ref[...].astype(F32)
              + jax.nn.sigmoid(gm_ref[...].astype(F32)) * om)
    y = DN_ALPHA * x_ref[...] + jnp.dot(merged.astype(BF16), wout_ref[...], preferred_element_type=F32)
    x1 = _layernorm(y, l1w_ref[...], l1b_ref[...])
    x1_ref[...] = x1
    x1p_ref[...] = _pack_bf16_pair(x1[:, :HALF_D], x1[:, HALF_D:])

    x_hi = x1.astype(BF16)
    x_lo = (x1 - x_hi.astype(F32)).astype(BF16)
    hi_part = jnp.dot(x_hi, rw_ref[...], preferred_element_type=F32)
    lo_part = jnp.dot(x_lo, rw_ref[:, :LANES], preferred_element_type=F32)
    logits = hi_part[:, :LANES] + hi_part[:, LANES:] + lo_part + rb_ref[...]
    lane_i = lax.broadcasted_iota(jnp.int32, (tm, LANES), 1)
    lane = lane_i.astype(F32)
    work = logits
    idx, val = [], []
    for _ in range(TOP_K):
        mx = jnp.max(work, axis=-1, keepdims=True)
        ix = jnp.min(jnp.where(work == mx, lane, float(LANES)), axis=-1, keepdims=True)
        idx.append(ix)
        val.append(mx)
        work = jnp.where(lane == ix, -jnp.inf, work)
    ex = [jnp.exp(v - val[0]) for v in val]
    tot = ex[0] + ex[1] + ex[2] + ex[3]
    hot = [jnp.where(lane == ix, 1.0, 0.0) for ix in idx]
    multi = hot[0] + hot[1] + hot[2] + hot[3]
    r = lax.broadcasted_iota(jnp.int32, (tm, tm), 0)
    c = lax.broadcasted_iota(jnp.int32, (tm, tm), 1)
    lower = jnp.where(r > c, 1.0, 0.0).astype(BF16)
    before = jnp.dot(lower, multi.astype(BF16), preferred_element_type=F32) + carry_s[0:1, :]
    ri = jnp.zeros((tm, LANES), F32)
    rg = jnp.zeros((tm, LANES), F32)
    for j in range(TOP_K):
        rank = jnp.sum(before * hot[j], axis=-1, keepdims=True)
        ri = ri + jnp.where(lane == float(j), idx[j], 0.0) + jnp.where(lane == float(TOP_K + j), rank, 0.0)
        rg = rg + jnp.where(lane == float(j), ex[j] / tot, 0.0)
    ri_ref[...] = ri.T[:2 * TOP_K, :].astype(jnp.int32)
    rg_ref[...] = rg
    carry_s[...] = carry_s[...] + jnp.sum(multi, axis=0, keepdims=True)
    cnt_ref[...] = carry_s[...]


def _merge(x2d, proj, o_a, o_b, kvm, wout, l1w, l1b, rw, rb, batch, seq, tm):
    t = batch * seq
    per_b = seq // tm
    nmem = kvm.shape[0] // batch
    tile = lambda colblk: pl.BlockSpec((tm, D_MODEL), lambda i: (i, colblk))
    full = lambda shape: pl.BlockSpec(shape, lambda i: (0, 0))
    return pl.pallas_call(
        functools.partial(_merge_kernel, tm=tm),
        out_shape=(jax.ShapeDtypeStruct((t, D_MODEL), F32),
                   jax.ShapeDtypeStruct((t, HALF_D), jnp.int32),
                   jax.ShapeDtypeStruct((2 * TOP_K, t), jnp.int32),
                   jax.ShapeDtypeStruct((t, LANES), F32),
                   jax.ShapeDtypeStruct((8, LANES), F32)),
        grid=(t // tm,),
        in_specs=[tile(0), tile(COL_GA), tile(COL_GB), tile(COL_GM), tile(COL_MQ), tile(0), tile(0),
                  pl.BlockSpec((nmem, 2 * D_MODEL), lambda i: (i // per_b, 0)),
                  full((D_MODEL, D_MODEL)), full((1, D_MODEL)), full((1, D_MODEL)),
                  full((D_MODEL, 2 * LANES)), full((1, LANES))],
        out_specs=(tile(0), pl.BlockSpec((tm, HALF_D), lambda i: (i, 0)),
                   pl.BlockSpec((2 * TOP_K, tm), lambda i: (0, i)),
                   pl.BlockSpec((tm, LANES), lambda i: (i, 0)), full((8, LANES))),
        scratch_shapes=[pltpu.VMEM((8, LANES), F32)],
        compiler_params=_params(("arbitrary",)),
        name="merge_router",
    )(x2d, proj, proj, proj, proj, o_a, o_b, kvm, wout, l1w, l1b, rw, rb)


def _sc_gather(table, idx):
    p, d = idx.shape[0], table.shape[1]
    per_w = p // SC_WORKERS
    assert per_w * SC_WORKERS == p and per_w % SC_GATHER_ROWS == 0
    mesh = plsc.VectorSubcoreMesh(core_axis_name="c", subcore_axis_name="s")

    @functools.partial(
        pl.kernel, mesh=mesh,
        out_type=jax.ShapeDtypeStruct((p, d), table.dtype),
        scratch_types=[pltpu.VMEM((SC_GATHER_ROWS,), jnp.int32),
                       pltpu.VMEM((SC_GATHER_ROWS, d), table.dtype),
                       pltpu.SemaphoreType.DMA],
        name="moe_gather_sc",
    )
    def gather_kernel(table_hbm, idx_hbm, out_hbm, idx_v, rows_v, sem):
        wid = lax.axis_index("s") * SC_CORES + lax.axis_index("c")
        base = wid * per_w

        @pl.loop(0, per_w // SC_GATHER_ROWS)
        def _(c):
            off = base + c * SC_GATHER_ROWS
            pltpu.sync_copy(idx_hbm.at[pl.ds(off, SC_GATHER_ROWS)], idx_v)
            pltpu.async_copy(table_hbm.at[idx_v], rows_v, sem).wait()
            pltpu.sync_copy(rows_v, out_hbm.at[pl.ds(off, SC_GATHER_ROWS)])

    return gather_kernel(table, idx)


def _sc_dispatch(x, dest_t, slots):
    t, d = x.shape
    per_w = t // SC_WORKERS
    assert per_w * SC_WORKERS == t and per_w % SC_GATHER_ROWS == 0
    mesh = plsc.VectorSubcoreMesh(core_axis_name="c", subcore_axis_name="s")

    @functools.partial(
        pl.kernel, mesh=mesh,
        out_type=jax.ShapeDtypeStruct((slots, d), x.dtype),
        scratch_types=[pltpu.VMEM((SC_GATHER_ROWS,), jnp.int32),
                       pltpu.VMEM((SC_GATHER_ROWS, d), x.dtype)],
        name="moe_dispatch_sc",
    )
    def scatter_kernel(x_hbm, idx_hbm, out_hbm, idx_v, rows_v):
        wid = lax.axis_index("s") * SC_CORES + lax.axis_index("c")
        base = wid * per_w

        @pl.loop(0, per_w // SC_GATHER_ROWS)
        def _(c):
            off = base + c * SC_GATHER_ROWS
            pltpu.sync_copy(x_hbm.at[pl.ds(off, SC_GATHER_ROWS)], rows_v)
            for j in range(TOP_K):
                pltpu.sync_copy(idx_hbm.at[pl.ds(j * t + off, SC_GATHER_ROWS)], idx_v)
                pltpu.sync_copy(rows_v, out_hbm.at[idx_v])

    return scatter_kernel(x, dest_t)


HALF_D = D_MODEL // 2
HI16 = -65536


def _pack_bf16_pair(a, b):
    wa = lax.bitcast_convert_type(a.astype(BF16).astype(F32), jnp.int32)
    wb = lax.bitcast_convert_type(b.astype(BF16).astype(F32), jnp.int32)
    return wa | lax.shift_right_logical(wb, 16)


def _unpack_bf16_pair(w):
    a = lax.bitcast_convert_type(w & HI16, F32)
    b = lax.bitcast_convert_type(lax.shift_left(w, 16), F32)
    return a, b


def _expert_kernel(be_ref, nu_ref, xs_ref, wgu_ref, bgu_ref, wd_ref, bd_ref, ys_ref, wgu_b, wd_b):
    i = pl.program_id(0)

    @pl.when(jnp.logical_or(i == 0, be_ref[i] != be_ref[jnp.maximum(i - 1, 0)]))
    def _():
        wgu_b[...] = wgu_ref[0].astype(BF16)
        wd_b[...] = wd_ref[0].astype(BF16)

    @pl.when(i < nu_ref[0])
    def _():
        xa, xb = _unpack_bf16_pair(xs_ref[...])
        x = jnp.concatenate([xa, xb], axis=1).astype(BF16)
        h = jnp.dot(x, wgu_b[...], preferred_element_type=F32) + bgu_ref[0]
        g = jnp.minimum(h[:, :D_FF], SWIGLU_LIMIT)
        u = jnp.clip(h[:, D_FF:], -SWIGLU_LIMIT, SWIGLU_LIMIT)
        act = (u + 1.0) * g * jax.nn.sigmoid(SWIGLU_ALPHA * g)
        y = jnp.dot(act.astype(BF16), wd_b[...], preferred_element_type=F32) + bd_ref[0]
        ys_ref[...] = _pack_bf16_pair(y[:, :HALF_D], y[:, HALF_D:])

    @pl.when(i >= nu_ref[0])
    def _():
        ys_ref[...] = jnp.zeros_like(ys_ref)


def _experts(block_e, n_used, xs, wgu, bgu, wd, bd):
    slots = xs.shape[0]
    nb = slots // EXPERT_BLOCK
    grid_spec = pltpu.PrefetchScalarGridSpec(
        num_scalar_prefetch=2,
        grid=(nb,),
        in_specs=[pl.BlockSpec((EXPERT_BLOCK, HALF_D), lambda i, be, nu: (i, 0)),
                  pl.BlockSpec((1, D_MODEL, 2 * D_FF), lambda i, be, nu: (be[i], 0, 0)),
                  pl.BlockSpec((1, 1, 2 * D_FF), lambda i, be, nu: (be[i], 0, 0)),
                  pl.BlockSpec((1, D_FF, D_MODEL), lambda i, be, nu: (be[i], 0, 0)),
                  pl.BlockSpec((1, 1, D_MODEL), lambda i, be, nu: (be[i], 0, 0))],
        out_specs=pl.BlockSpec((EXPERT_BLOCK, HALF_D), lambda i, be, nu: (i, 0)),
        scratch_shapes=[pltpu.VMEM((D_MODEL, 2 * D_FF), BF16), pltpu.VMEM((D_FF, D_MODEL), BF16)],
    )
    return pl.pallas_call(
        _expert_kernel,
        out_shape=jax.ShapeDtypeStruct((slots, HALF_D), jnp.int32),
        grid_spec=grid_spec,
        compiler_params=_params(("arbitrary",)),
        name="moe_experts",
    )(block_e, n_used, xs, wgu, bgu, wd, bd)


def _weighted_rows(rg, packed):
    lo = hi = None
    for j, w in enumerate(packed):
        a, b = _unpack_bf16_pair(w)
        gj = rg[:, j:j + 1]
        lo = gj * a if lo is None else lo + gj * a
        hi = gj * b if hi is None else hi + gj * b
    return jnp.concatenate([lo, hi], axis=1)


def _combine_rows_kernel(x1_ref, rg_ref, g0_ref, g1_ref, g2_ref, g3_ref, l2w_ref, l2b_ref, o_ref):
    moe = _weighted_rows(rg_ref[...], [g0_ref[...], g1_ref[...], g2_ref[...], g3_ref[...]])
    o_ref[...] = _layernorm(DN_ALPHA * x1_ref[...] + moe, l2w_ref[...], l2b_ref[...])


def _combine_rows(x1, rg, rows, l2w, l2b, tm):
    t = x1.shape[0]
    nt = t // tm
    full = lambda shape: pl.BlockSpec(shape, lambda i: (0, 0))
    choice = lambda j: pl.BlockSpec((tm, HALF_D), lambda i, j=j: (j * nt + i, 0))
    return pl.pallas_call(
        _combine_rows_kernel,
        out_shape=jax.ShapeDtypeStruct((t, D_MODEL), F32),
        grid=(nt,),
        in_specs=[pl.BlockSpec((tm, D_MODEL), lambda i: (i, 0)),
                  pl.BlockSpec((tm, LANES), lambda i: (i, 0)),
                  choice(0), choice(1), choice(2), choice(3),
                  full((1, D_MODEL)), full((1, D_MODEL))],
        out_specs=pl.BlockSpec((tm, D_MODEL), lambda i: (i, 0)),
        compiler_params=_params(("arbitrary",)),
        name="moe_combine_rows",
    )(x1, rg, rows, rows, rows, rows, l2w, l2b)


def _prep_w_in(w):
    zeros = lambda n: jnp.zeros((D_MODEL, n), w.dtype)
    kr = w[:, 5760:5824]
    kr_sw = jnp.concatenate([kr[:, MLA_ROPE // 2:], kr[:, :MLA_ROPE // 2]], axis=1)
    parts = [w[:, 6848:9920], w[:, 0:5120], w[:, 5824:6848],
             w[:, 5120:5504], zeros(128),
             w[:, 5504:5760], kr, zeros(64), kr_sw, zeros(64)]
    return jnp.concatenate(parts, axis=1).astype(BF16)


def _prep_w_uq(w):
    w3 = w.reshape(MLA_Q_LORA, MLA_HEADS, MLA_NOPE + MLA_ROPE)
    rope = w3[:, :, MLA_NOPE:]
    rope_sw = jnp.concatenate([rope[:, :, MLA_ROPE // 2:], rope[:, :, :MLA_ROPE // 2]], axis=-1)
    pad = jnp.zeros((MLA_Q_LORA, MLA_HEADS, QK_PAD - MLA_NOPE - MLA_ROPE), w.dtype)
    main = jnp.concatenate([w3, pad], axis=-1).reshape(MLA_Q_LORA, MLA_HEADS * QK_PAD)
    swp = jnp.concatenate([rope_sw, pad], axis=-1).reshape(MLA_Q_LORA, MLA_HEADS * LANES)
    both = jnp.concatenate([main, swp], axis=1)
    return jnp.pad(both, ((0, 512 - MLA_Q_LORA), (0, 0))).astype(BF16)


def _rope_tables(seq):
    inv_freq = ROPE_THETA ** (-jnp.arange(0, MLA_ROPE, 2, dtype=F32) / MLA_ROPE)
    ang = jnp.arange(seq, dtype=F32)[:, None] * inv_freq[None, :]
    cos, sin = jnp.cos(ang), jnp.sin(ang)
    pad = jnp.zeros((seq, LANES - MLA_ROPE), F32)
    return (jnp.concatenate([cos, cos, pad], axis=1), jnp.concatenate([-sin, sin, pad], axis=1))


def _split_bf16(w):
    hi = w.astype(BF16)
    lo = (w - hi.astype(F32)).astype(BF16)
    return jnp.concatenate([hi, lo], axis=1)


def _tile(n, pref):
    return pref if n % pref == 0 else n


def _layer(x, mem, wts):
    batch, seq, _ = x.shape
    t = batch * seq
    x2d = x.reshape(t, D_MODEL)

    proj = _matmul(x2d, wts["w_in"], BF16, _tile(t, 2048), 2048, "in_proj")
    o_a = _hgrn(proj, wts["lb"], wts["hgrn_norm_w"], batch, seq, _tile(seq, 512))

    cos, sin = _rope_tables(seq)
    q, k, v = _mla_prep(proj, wts["qnw"], wts["kvnw"], wts["w_uq"], wts["w_uk"], wts["w_uv"],
                        cos, sin, batch, seq, _tile(seq, 512))
    o_b = _flash(q, k, v, batch, seq, _tile(seq, 1024), _tile(seq, 1024))

    nmem = mem.shape[1]
    kvm = _matmul(mem.reshape(batch * nmem, D_MODEL), wts["mem_w_kv"], BF16,
                  _tile(batch * nmem, 512), 1024, "mem_kv")

    tm = _tile(seq, 512)
    x1, x1p, ri, rg, cnt = _merge(x2d, proj, o_a, o_b, kvm, wts["w_out"], wts["ln1_w"], wts["ln1_b"],
                             wts["router_w"], wts["router_b"], batch, seq, tm)

    counts = cnt[0, :N_EXPERTS].astype(jnp.int32)
    padded = (counts + EXPERT_BLOCK - 1) // EXPERT_BLOCK * EXPERT_BLOCK
    pad_end = jnp.cumsum(padded)
    pad_start = pad_end - padded
    experts = jnp.arange(N_EXPERTS, dtype=jnp.int32)[:, None, None]
    start_ct = jnp.sum(jnp.where(ri[None, :TOP_K] == experts, pad_start[:, None, None], 0), axis=0)
    dest_ct = (start_ct + ri[TOP_K:]).astype(jnp.int32)
    nb = t * TOP_K // EXPERT_BLOCK + N_EXPERTS
    blk_start = jnp.arange(nb, dtype=jnp.int32) * EXPERT_BLOCK
    block_e = jnp.minimum(jnp.sum((pad_end[None, :] <= blk_start[:, None]).astype(jnp.int32), axis=1),
                          N_EXPERTS - 1).astype(jnp.int32)
    n_used = (pad_end[-1:] // EXPERT_BLOCK).astype(jnp.int32)
    dest_flat = dest_ct.reshape(TOP_K * t)
    xs = _sc_dispatch(x1p, dest_flat, nb * EXPERT_BLOCK)
    ys = _experts(block_e, n_used, xs, wts["exp_w_gu"], wts["exp_b_gu"], wts["exp_w_down"], wts["exp_b_down"])
    rows = _sc_gather(ys, dest_flat)
    y = _combine_rows(x1, rg, rows, wts["ln2_w"], wts["ln2_b"], _tile(t, 512))
    return y.reshape(batch, seq, D_MODEL)


def kernel(x_prompt, x_sample, mem_prompt, mem_sample, w_in, hgrn_lb_logits, hgrn_norm_w,
           mla_q_norm_w, mla_w_uq, mla_kv_norm_w, mla_w_uk, mla_w_uv, mem_w_kv, w_out,
           ln1_w, ln1_b, router_w, router_b, exp_w_gu, exp_b_gu, exp_w_down, exp_b_down,
           ln2_w, ln2_b):
    depth = w_in.shape[0]
    gamma = jax.nn.softmax(hgrn_lb_logits.astype(F32), axis=1)
    cum = jnp.cumsum(gamma, axis=1)
    lb_all = cum[:, 1:] - cum[:, :1]
    y_prompt, y_sample = x_prompt, x_sample
    for l in range(depth):
        row = lambda a: a[l].reshape(1, -1).astype(F32)
        wts = {
            "w_in": _prep_w_in(w_in[l]),
            "lb": lb_all[:, l],
            "hgrn_norm_w": hgrn_norm_w[l].astype(F32),
            "qnw": jnp.pad(row(mla_q_norm_w), ((0, 0), (0, 512 - MLA_Q_LORA))),
            "kvnw": row(mla_kv_norm_w),
            "w_uq": _prep_w_uq(mla_w_uq[l]),
            "w_uk": mla_w_uk[l].astype(BF16),
            "w_uv": mla_w_uv[l].astype(BF16),
            "mem_w_kv": mem_w_kv[l].astype(BF16),
            "w_out": w_out[l].astype(BF16),
            "ln1_w": row(ln1_w), "ln1_b": row(ln1_b),
            "router_w": _split_bf16(jnp.pad(router_w[l].astype(F32), ((0, 0), (0, LANES - N_EXPERTS)))),
            "router_b": jnp.pad(row(router_b), ((0, 0), (0, LANES - N_EXPERTS)), constant_values=-jnp.inf),
            "exp_w_gu": exp_w_gu[l].astype(F32),
            "exp_b_gu": exp_b_gu[l].reshape(N_EXPERTS, 1, 2 * D_FF).astype(F32),
            "exp_w_down": exp_w_down[l].astype(F32),
            "exp_b_down": exp_b_down[l].reshape(N_EXPERTS, 1, D_MODEL).astype(F32),
            "ln2_w": row(ln2_w), "ln2_b": row(ln2_b),
        }
        y_prompt = _layer(y_prompt, mem_prompt, wts)
        y_sample = _layer(y_sample, mem_sample, wts)
    return (y_prompt, y_sample)
```

```python
import functools

import jax
import jax.numpy as jnp
from jax import lax
from jax.experimental import pallas as pl
from jax.experimental.pallas import tpu as pltpu
from jax.experimental.pallas import tpu_sc as plsc

F32 = jnp.float32
BF16 = jnp.bfloat16

D_MODEL = 1024
HGRN_HEADS = 8
HGRN_DK = 128
MLA_HEADS = 8
MLA_Q_LORA = 384
MLA_KV_LORA = 256
MLA_NOPE = 128
MLA_ROPE = 64
MLA_V = 128
ROPE_THETA = 10000.0
MEM_HEADS = 4
MEM_HEAD_DIM = D_MODEL // MEM_HEADS
N_EXPERTS = 32
TOP_K = 4
D_FF = D_MODEL
SWIGLU_LIMIT = 7.0
SWIGLU_ALPHA = 1.702
DN_ALPHA = 2.0 ** 0.25
LN_EPS = 1e-5
RMS_EPS = 1e-6

LANES = 128
SUBLANES = 8
QK_PAD = 256
V_PAD = 256
LOG2E = 1.4426950408889634
HGRN_CHUNK = 64
HGRN_SAFE_RANGE = 160.0
EXPERT_BLOCK = 512
SC_CORES = 2
SC_SUBCORES = 16
SC_WORKERS = SC_CORES * SC_SUBCORES
SC_GATHER_ROWS = 128
VMEM_LIMIT = 56 * 1024 * 1024

COL_GA, COL_GB, COL_GM, COL_Q, COL_ZF, COL_ZB, COL_I, COL_G, COL_MQ = range(9)
COL_DQ, COL_DKV = 18, 19

NT_DIMS = (((1,), (1,)), ((), ()))
TN_DIMS = (((0,), (0,)), ((), ()))


def _params(sem, vmem=VMEM_LIMIT):
    return pltpu.CompilerParams(dimension_semantics=sem, vmem_limit_bytes=vmem)


def _mm_kernel(x_ref, w_ref, o_ref, xb_ref):
    @pl.when(pl.program_id(1) == 0)
    def _():
        xb_ref[...] = x_ref[...].astype(BF16)

    o_ref[...] = jnp.dot(xb_ref[...], w_ref[...], preferred_element_type=F32).astype(o_ref.dtype)


def _matmul(x, w, out_dtype, tm, tn, name):
    m, k = x.shape
    n = w.shape[1]
    return pl.pallas_call(
        _mm_kernel,
        out_shape=jax.ShapeDtypeStruct((m, n), out_dtype),
        grid=(m // tm, n // tn),
        in_specs=[pl.BlockSpec((tm, k), lambda i, j: (i, 0)),
                  pl.BlockSpec((k, tn), lambda i, j: (0, j))],
        out_specs=pl.BlockSpec((tm, tn), lambda i, j: (i, j)),
        scratch_shapes=[pltpu.VMEM((tm, k), BF16)],
        compiler_params=_params(("arbitrary", "arbitrary")),
        name=name,
    )(x, w)


def _hgrn_kernel(*refs, reverse, epilogue, sc):
    if epilogue:
        lb_ref, q_ref, z_ref, v_ref, of_ref, g_ref, nw_ref, o_ref, st_ref, kk_s, b_s, vf_s, os_s = refs
    else:
        lb_ref, q_ref, z_ref, v_ref, o_ref, st_ref, kk_s, b_s, vf_s = refs
    C = HGRN_CHUNK
    nch = sc // C

    @pl.when(pl.program_id(1) == 0)
    def _():
        st_ref[...] = jnp.zeros_like(st_ref)

    lb = lb_ref[...]
    row = lax.broadcasted_iota(jnp.int32, (C, C), 0)
    col = lax.broadcasted_iota(jnp.int32, (C, C), 1)
    tri = (row <= col) if reverse else (row >= col)
    trib = jnp.where(tri, 1.0, 0.0).astype(BF16)

    minb = None
    for c in range(nch):
        rows = slice(c * C, (c + 1) * C)
        z = z_ref[rows, :].astype(F32)
        gate = (1.0 - lb) * jax.nn.sigmoid(z)
        lf = jnp.log(lb + gate)
        kk_s[rows, :] = (1.0 - lb) - gate
        hi = lf.astype(BF16)
        r1 = lf - hi.astype(F32)
        mid = r1.astype(BF16)
        lo = (r1 - mid.astype(F32)).astype(BF16)
        b = (jnp.dot(trib, hi, preferred_element_type=F32)
             + jnp.dot(trib, mid, preferred_element_type=F32)
             + jnp.dot(trib, lo, preferred_element_type=F32))
        b_s[rows, :] = b
        mb = jnp.min(b)
        minb = mb if minb is None else jnp.minimum(minb, mb)

    rid = lax.broadcasted_iota(jnp.int32, (C, 1), 0)

    def chunk(i, carry, fast):
        c = (nch - 1 - i) if reverse else i
        r0 = pl.multiple_of(c * C, C)
        rows = pl.ds(r0, C)
        for h in range(HGRN_HEADS):
            cols = slice(h * HGRN_DK, (h + 1) * HGRN_DK)
            q = q_ref[rows, cols].astype(F32)
            kk = kk_s[rows, cols]
            b = b_s[rows, cols]
            v = v_ref[rows, cols]
            bl = b[0:1, :] if reverse else b[C - 1:C, :]
            if fast:
                bm = 0.5 * bl
                qd = (q * jnp.exp(b - bm)).astype(BF16)
                kd = (kk * jnp.exp(bm - b)).astype(BF16)
                s = lax.dot_general(qd, kd, NT_DIMS, preferred_element_type=F32)
                s = jnp.where(tri, s, 0.0).astype(BF16)
                o = jnp.dot(s, v, preferred_element_type=F32)
            else:
                def sbody(g_i, o_acc):
                    grp = pl.ds(pl.multiple_of(r0 + g_i * SUBLANES, SUBLANES), SUBLANES)
                    b8 = b_s[grp, cols]
                    k8 = kk_s[grp, cols]
                    v8 = vf_s[grp, cols]
                    for jj in range(SUBLANES):
                        s_i = g_i * SUBLANES + jj
                        w = q * k8[jj:jj + 1, :] * jnp.exp(jnp.minimum(b - b8[jj:jj + 1, :], 0.0))
                        scol = jnp.sum(w, axis=-1, keepdims=True)
                        keep = (rid <= s_i) if reverse else (rid >= s_i)
                        o_acc = o_acc + jnp.where(keep, scol, 0.0) * v8[jj:jj + 1, :]
                    return o_acc
                o = lax.fori_loop(0, C // SUBLANES, sbody, jnp.zeros((C, HGRN_DK), F32))
            st = st_ref[h]
            qi = (q * jnp.exp(b)).astype(BF16)
            o = o + lax.dot_general(qi, st.astype(BF16), NT_DIMS, preferred_element_type=F32)
            ke = (kk * jnp.exp(bl - b)).astype(BF16)
            upd = lax.dot_general(v, ke, TN_DIMS, preferred_element_type=F32)
            st_ref[h] = st * jnp.exp(bl) + upd
            if epilogue:
                os_s[rows, cols] = o + of_ref[rows, cols].astype(F32)
            else:
                o_ref[rows, cols] = o.astype(o_ref.dtype)
        return carry

    safe = minb >= -HGRN_SAFE_RANGE

    @pl.when(safe)
    def _():
        lax.fori_loop(0, nch, functools.partial(chunk, fast=True), 0, unroll=8)

    @pl.when(jnp.logical_not(safe))
    def _():
        vf_s[...] = v_ref[...].astype(F32)
        lax.fori_loop(0, nch, functools.partial(chunk, fast=False), 0)

    if epilogue:
        nw = nw_ref[...]
        for h in range(HGRN_HEADS):
            cols = slice(h * HGRN_DK, (h + 1) * HGRN_DK)
            os = os_s[:, cols]
            ms = jnp.mean(os * os, axis=-1, keepdims=True)
            y = os * lax.rsqrt(ms + RMS_EPS) * nw
            g = g_ref[:, cols].astype(F32)
            o_ref[:, cols] = (y * (g * jax.nn.sigmoid(g))).astype(o_ref.dtype)


def _hgrn(proj, lb, norm_w, batch, seq, sc):
    t = batch * seq
    ns = seq // sc
    blk = (sc, D_MODEL)

    def spec(colblk, reverse):
        if reverse:
            return pl.BlockSpec(blk, lambda b, n: (b * ns + ns - 1 - n, colblk))
        return pl.BlockSpec(blk, lambda b, n: (b * ns + n, colblk))

    def row_spec(reverse):
        if reverse:
            return pl.BlockSpec(blk, lambda b, n: (b * ns + ns - 1 - n, 0))
        return pl.BlockSpec(blk, lambda b, n: (b * ns + n, 0))

    lb_spec = pl.BlockSpec((1, D_MODEL), lambda b, n: (0, 0))
    common_scratch = [pltpu.VMEM((HGRN_HEADS, HGRN_DK, HGRN_DK), F32),
                      pltpu.VMEM(blk, F32), pltpu.VMEM(blk, F32), pltpu.VMEM(blk, F32)]
    o_f = pl.pallas_call(
        functools.partial(_hgrn_kernel, reverse=False, epilogue=False, sc=sc),
        out_shape=jax.ShapeDtypeStruct((t, D_MODEL), BF16),
        grid=(batch, ns),
        in_specs=[lb_spec, spec(COL_Q, False), spec(COL_ZF, False), spec(COL_I, False)],
        out_specs=row_spec(False),
        scratch_shapes=common_scratch,
        compiler_params=_params(("arbitrary", "arbitrary")),
        name="hgrn_fwd",
    )(lb[0:1], proj, proj, proj)
    o_a = pl.pallas_call(
        functools.partial(_hgrn_kernel, reverse=True, epilogue=True, sc=sc),
        out_shape=jax.ShapeDtypeStruct((t, D_MODEL), BF16),
        grid=(batch, ns),
        in_specs=[lb_spec, spec(COL_Q, True), spec(COL_ZB, True), spec(COL_I, True),
                  row_spec(True), spec(COL_G, True),
                  pl.BlockSpec((1, HGRN_DK), lambda b, n: (0, 0))],
        out_specs=row_spec(True),
        scratch_shapes=common_scratch + [pltpu.VMEM(blk, F32)],
        compiler_params=_params(("arbitrary", "arbitrary")),
        name="hgrn_bwd",
    )(lb[1:2], proj, proj, proj, o_f, proj, norm_w.reshape(1, HGRN_DK))
    return o_a


def _mla_prep_kernel(dq_ref, dkv_ref, qnw_ref, kvnw_ref, wq_ref, wk_ref, wv_ref, cos_ref, sin_ref,
                     q_ref, k_ref, v_ref):
    scale = (MLA_NOPE + MLA_ROPE) ** -0.5 * LOG2E
    cos = cos_ref[...]
    sin = sin_ref[...]
    dq = dq_ref[...].astype(F32)
    ms = jnp.sum(dq * dq, axis=-1, keepdims=True) * (1.0 / MLA_Q_LORA)
    cq = (dq * lax.rsqrt(ms + RMS_EPS) * qnw_ref[...]).astype(BF16)
    qa = jnp.dot(cq, wq_ref[...], preferred_element_type=F32)
    for h in range(MLA_HEADS):
        base = h * QK_PAD
        q_ref[:, base:base + MLA_NOPE] = (qa[:, base:base + MLA_NOPE] * scale).astype(BF16)
        rp = qa[:, base + MLA_NOPE:base + QK_PAD]
        sw = qa[:, MLA_HEADS * QK_PAD + h * LANES:MLA_HEADS * QK_PAD + (h + 1) * LANES]
        q_ref[:, base + MLA_NOPE:base + QK_PAD] = ((rp * cos + sw * sin) * scale).astype(BF16)
    dkv = dkv_ref[...].astype(F32)
    ckv = dkv[:, :MLA_KV_LORA]
    msk = jnp.mean(ckv * ckv, axis=-1, keepdims=True)
    cn = (ckv * lax.rsqrt(msk + RMS_EPS) * kvnw_ref[...]).astype(BF16)
    kn = jnp.dot(cn, wk_ref[...], preferred_element_type=F32)
    vv = jnp.dot(cn, wv_ref[...], preferred_element_type=F32).astype(BF16)
    ones_col = jnp.where(lax.broadcasted_iota(jnp.int32, (vv.shape[0], LANES), 1) == 0, 1.0, 0.0).astype(BF16)
    for h in range(MLA_HEADS):
        v_ref[:, h * V_PAD:h * V_PAD + MLA_V] = vv[:, h * MLA_V:(h + 1) * MLA_V]
        v_ref[:, h * V_PAD + MLA_V:(h + 1) * V_PAD] = ones_col
    kr = (dkv[:, MLA_KV_LORA:MLA_KV_LORA + LANES] * cos
          + dkv[:, MLA_KV_LORA + LANES:MLA_KV_LORA + 2 * LANES] * sin).astype(BF16)
    for h in range(MLA_HEADS):
        base = h * QK_PAD
        k_ref[:, base:base + MLA_NOPE] = kn[:, h * MLA_NOPE:(h + 1) * MLA_NOPE].astype(BF16)
        k_ref[:, base + MLA_NOPE:base + QK_PAD] = kr


def _mla_prep(proj, qnw, kvnw, wq, wk, wv, cos, sin, batch, seq, tm):
    t = batch * seq
    npos = seq // tm
    full = lambda shape: pl.BlockSpec(shape, lambda i: (0, 0))
    return pl.pallas_call(
        _mla_prep_kernel,
        out_shape=(jax.ShapeDtypeStruct((t, MLA_HEADS * QK_PAD), BF16),
                   jax.ShapeDtypeStruct((t, MLA_HEADS * QK_PAD), BF16),
                   jax.ShapeDtypeStruct((t, MLA_HEADS * V_PAD), BF16)),
        grid=(t // tm,),
        in_specs=[pl.BlockSpec((tm, 512), lambda i: (i, COL_DQ)),
                  pl.BlockSpec((tm, 512), lambda i: (i, COL_DKV)),
                  full((1, 512)), full((1, MLA_KV_LORA)),
                  full(wq.shape), full(wk.shape), full(wv.shape),
                  pl.BlockSpec((tm, LANES), lambda i: (i % npos, 0)),
                  pl.BlockSpec((tm, LANES), lambda i: (i % npos, 0))],
        out_specs=(pl.BlockSpec((tm, MLA_HEADS * QK_PAD), lambda i: (i, 0)),
                   pl.BlockSpec((tm, MLA_HEADS * QK_PAD), lambda i: (i, 0)),
                   pl.BlockSpec((tm, MLA_HEADS * V_PAD), lambda i: (i, 0))),
        compiler_params=_params(("arbitrary",)),
        name="mla_prep",
    )(proj, proj, qnw, kvnw, wq, wk, wv, cos, sin)


def _flash_kernel(q_ref, k_ref, v_ref, o_ref, m_s, acc_s, s_buf, *, bk, nk):
    m_s[...] = jnp.full(m_s.shape, -jnp.inf, F32)
    acc_s[...] = jnp.zeros(acc_s.shape, F32)
    nt = bk // LANES
    bq = q_ref.shape[0]
    nsplit = 2 if bq % 256 == 0 else 1
    hq = bq // nsplit
    qrs = [slice(hf * hq, (hf + 1) * hq) for hf in range(nsplit)]

    def scores(j, slot):
        rows = pl.ds(pl.multiple_of(j * bk, bk), bk)
        for qr in qrs:
            s_buf[slot, qr, :] = lax.dot_general(q_ref[qr, :], k_ref[rows, :], NT_DIMS,
                                                 preferred_element_type=F32)

    def consume(j, slot):
        rows = pl.ds(pl.multiple_of(j * bk, bk), bk)
        ps, alphas = [], []
        for qr in qrs:
            tiles = [s_buf[slot, qr, t * LANES:(t + 1) * LANES] for t in range(nt)]
            tmax = tiles[0]
            for t in range(1, nt):
                tmax = jnp.maximum(tmax, tiles[t])
            m_prev = m_s[qr, :]
            m_new = jnp.maximum(m_prev, jnp.max(tmax, axis=-1, keepdims=True))
            alphas.append(jnp.exp2(m_prev - m_new))
            ps.append(jnp.concatenate([jnp.exp2(tl - m_new).astype(BF16) for tl in tiles], axis=1))
            m_s[qr, :] = m_new
        for qr, p, a in zip(qrs, ps, alphas):
            pv = jnp.dot(p, v_ref[rows, :], preferred_element_type=F32)
            acc_s[qr, :] = jnp.concatenate([a, a], axis=1) * acc_s[qr, :] + pv

    scores(0, 0)
    npairs = (nk - 1) // 2

    def body(i, carry):
        j = 2 * i
        scores(j + 1, 1)
        consume(j, 0)
        scores(j + 2, 0)
        consume(j + 1, 1)
        return carry

    lax.fori_loop(0, npairs, body, 0)
    if nk - 2 * npairs == 2:
        scores(nk - 1, 1)
        consume(nk - 2, 0)
        consume(nk - 1, 1)
    else:
        consume(nk - 1, 0)
    acc = acc_s[...]
    o_ref[...] = (acc[:, :MLA_V] / acc[:, MLA_V:MLA_V + 1]).astype(o_ref.dtype)


def _flash(q, k, v, batch, seq, bq, bk):
    t = batch * seq
    nq = seq // bq
    return pl.pallas_call(
        functools.partial(_flash_kernel, bk=bk, nk=seq // bk),
        out_shape=jax.ShapeDtypeStruct((t, MLA_HEADS * MLA_V), BF16),
        grid=(batch, MLA_HEADS, nq),
        in_specs=[pl.BlockSpec((bq, QK_PAD), lambda b, h, i: (b * nq + i, h)),
                  pl.BlockSpec((seq, QK_PAD), lambda b, h, i: (b, h)),
                  pl.BlockSpec((seq, V_PAD), lambda b, h, i: (b, h))],
        out_specs=pl.BlockSpec((bq, MLA_V), lambda b, h, i: (b * nq + i, h)),
        scratch_shapes=[pltpu.VMEM((bq, LANES), F32), pltpu.VMEM((bq, V_PAD), F32),
                        pltpu.VMEM((2, bq, bk), F32)],
        compiler_params=_params(("arbitrary", "arbitrary", "arbitrary")),
        name="mla_flash",
    )(q, k, v)


def _layernorm(y, w, b):
    mu = jnp.mean(y, axis=-1, keepdims=True)
    yc = y - mu
    var = jnp.mean(yc * yc, axis=-1, keepdims=True)
    return yc * lax.rsqrt(var + LN_EPS) * w + b


def _merge_kernel(x_ref, ga_ref, gb_ref, gm_ref, mq_ref, oa_ref, ob_ref, kvm_ref, wout_ref,
                  l1w_ref, l1b_ref, rw_ref, rb_ref,
                  x1_ref, x1p_ref, ri_ref, rg_ref, cnt_ref, carry_s, *, tm):
    @pl.when(pl.program_id(0) == 0)
    def _():
        carry_s[...] = jnp.zeros_like(carry_s)

    parts = []
    for h in range(MEM_HEADS):
        cols = slice(h * MEM_HEAD_DIM, (h + 1) * MEM_HEAD_DIM)
        kh = kvm_ref[:, cols]
        vh = kvm_ref[:, D_MODEL + h * MEM_HEAD_DIM:D_MODEL + (h + 1) * MEM_HEAD_DIM]
        s = lax.dot_general(mq_ref[:, cols], kh, NT_DIMS, preferred_element_type=F32) * (MEM_HEAD_DIM ** -0.5)
        s = s - jnp.max(s, axis=-1, keepdims=True)
        p = jnp.exp(s)
        p = p / jnp.sum(p, axis=-1, keepdims=True)
        parts.append(jnp.dot(p.astype(BF16), vh, preferred_element_type=F32))
    om = jnp.concatenate(parts, axis=1)

    merged = (jax.nn.sigmoid(ga_ref[...].astype(F32)) * oa_ref[...].astype(F32)
              + jax.nn.sigmoid(gb_ref[...].astype(F32)) * ob_ref[...].astype(F32)
              + jax.nn.sigmoid(gm_ref[...].astype(F32)) * om)
    y = DN_ALPHA * x_ref[...] + jnp.dot(merged.astype(BF16), wout_ref[...], preferred_element_type=F32)
    x1 = _layernorm(y, l1w_ref[...], l1b_ref[...])
    x1_ref[...] = x1
    x1p_ref[...] = _pack_bf16_pair(x1[:, :HALF_D], x1[:, HALF_D:])

    x_hi = x1.astype(BF16)
    x_lo = (x1 - x_hi.astype(F32)).astype(BF16)
    hi_part = jnp.dot(x_hi, rw_ref[...], preferred_element_type=F32)
    lo_part = jnp.dot(x_lo, rw_ref[:, :LANES], preferred_element_type=F32)
    logits = hi_part[:, :LANES] + hi_part[:, LANES:] + lo_part + rb_ref[...]
    lane_i = lax.broadcasted_iota(jnp.int32, (tm, LANES), 1)
    lane = lane_i.astype(F32)
    work = logits
    idx, val = [], []
    for _ in range(TOP_K):
        mx = jnp.max(work, axis=-1, keepdims=True)
        ix = jnp.min(jnp.where(work == mx, lane, float(LANES)), axis=-1, keepdims=True)
        idx.append(ix)
        val.append(mx)
        work = jnp.where(lane == ix, -jnp.inf, work)
    ex = [jnp.exp(v - val[0]) for v in val]
    tot = ex[0] + ex[1] + ex[2] + ex[3]
    hot = [jnp.where(lane == ix, 1.0, 0.0) for ix in idx]
    multi = hot[0] + hot[1] + hot[2] + hot[3]
    r = lax.broadcasted_iota(jnp.int32, (tm, tm), 0)
    c = lax.broadcasted_iota(jnp.int32, (tm, tm), 1)
    lower = jnp.where(r > c, 1.0, 0.0).astype(BF16)
    before = jnp.dot(lower, multi.astype(BF16), preferred_element_type=F32) + carry_s[0:1, :]
    ri = jnp.zeros((tm, LANES), F32)
    rg = jnp.zeros((tm, LANES), F32)
    for j in range(TOP_K):
        rank = jnp.sum(before * hot[j], axis=-1, keepdims=True)
        ri = ri + jnp.where(lane == float(j), idx[j], 0.0) + jnp.where(lane == float(TOP_K + j), rank, 0.0)
        rg = rg + jnp.where(lane == float(j), ex[j] / tot, 0.0)
    ri_ref[...] = ri.T[:2 * TOP_K, :].astype(jnp.int32)
    rg_ref[...] = rg
    carry_s[...] = carry_s[...] + jnp.sum(multi, axis=0, keepdims=True)
    cnt_ref[...] = carry_s[...]


def _merge(x2d, proj, o_a, o_b, kvm, wout, l1w, l1b, rw, rb, batch, seq, tm):
    t = batch * seq
    per_b = seq // tm
    nmem = kvm.shape[0] // batch
    tile = lambda colblk: pl.BlockSpec((tm, D_MODEL), lambda i: (i, colblk))
    full = lambda shape: pl.BlockSpec(shape, lambda i: (0, 0))
    return pl.pallas_call(
        functools.partial(_merge_kernel, tm=tm),
        out_shape=(jax.ShapeDtypeStruct((t, D_MODEL), F32),
                   jax.ShapeDtypeStruct((t, HALF_D), jnp.int32),
                   jax.ShapeDtypeStruct((2 * TOP_K, t), jnp.int32),
                   jax.ShapeDtypeStruct((t, LANES), F32),
                   jax.ShapeDtypeStruct((8, LANES), F32)),
        grid=(t // tm,),
        in_specs=[tile(0), tile(COL_GA), tile(COL_GB), tile(COL_GM), tile(COL_MQ), tile(0), tile(0),
                  pl.BlockSpec((nmem, 2 * D_MODEL), lambda i: (i // per_b, 0)),
                  full((D_MODEL, D_MODEL)), full((1, D_MODEL)), full((1, D_MODEL)),
                  full((D_MODEL, 2 * LANES)), full((1, LANES))],
        out_specs=(tile(0), pl.BlockSpec((tm, HALF_D), lambda i: (i, 0)),
                   pl.BlockSpec((2 * TOP_K, tm), lambda i: (0, i)),
                   pl.BlockSpec((tm, LANES), lambda i: (i, 0)), full((8, LANES))),
        scratch_shapes=[pltpu.VMEM((8, LANES), F32)],
        compiler_params=_params(("arbitrary",)),
        name="merge_router",
    )(x2d, proj, proj, proj, proj, o_a, o_b, kvm, wout, l1w, l1b, rw, rb)


def _sc_gather(table, idx):
    p, d = idx.shape[0], table.shape[1]
    per_w = p // SC_WORKERS
    assert per_w * SC_WORKERS == p and per_w % SC_GATHER_ROWS == 0
    mesh = plsc.VectorSubcoreMesh(core_axis_name="c", subcore_axis_name="s")

    @functools.partial(
        pl.kernel, mesh=mesh,
        out_type=jax.ShapeDtypeStruct((p, d), table.dtype),
        scratch_types=[pltpu.VMEM((SC_GATHER_ROWS,), jnp.int32),
                       pltpu.VMEM((SC_GATHER_ROWS, d), table.dtype),
                       pltpu.SemaphoreType.DMA],
        name="moe_gather_sc",
    )
    def gather_kernel(table_hbm, idx_hbm, out_hbm, idx_v, rows_v, sem):
        wid = lax.axis_index("s") * SC_CORES + lax.axis_index("c")
        base = wid * per_w

        @pl.loop(0, per_w // SC_GATHER_ROWS)
        def _(c):
            off = base + c * SC_GATHER_ROWS
            pltpu.sync_copy(idx_hbm.at[pl.ds(off, SC_GATHER_ROWS)], idx_v)
            pltpu.async_copy(table_hbm.at[idx_v], rows_v, sem).wait()
            pltpu.sync_copy(rows_v, out_hbm.at[pl.ds(off, SC_GATHER_ROWS)])

    return gather_kernel(table, idx)


def _sc_dispatch(x, dest_t, slots):
    t, d = x.shape
    per_w = t // SC_WORKERS
    assert per_w * SC_WORKERS == t and per_w % SC_GATHER_ROWS == 0
    mesh = plsc.VectorSubcoreMesh(core_axis_name="c", subcore_axis_name="s")

    @functools.partial(
        pl.kernel, mesh=mesh,
        out_type=jax.ShapeDtypeStruct((slots, d), x.dtype),
        scratch_types=[pltpu.VMEM((SC_GATHER_ROWS,), jnp.int32),
                       pltpu.VMEM((SC_GATHER_ROWS, d), x.dtype)],
        name="moe_dispatch_sc",
    )
    def scatter_kernel(x_hbm, idx_hbm, out_hbm, idx_v, rows_v):
        wid = lax.axis_index("s") * SC_CORES + lax.axis_index("c")
        base = wid * per_w

        @pl.loop(0, per_w // SC_GATHER_ROWS)
        def _(c):
            off = base + c * SC_GATHER_ROWS
            pltpu.sync_copy(x_hbm.at[pl.ds(off, SC_GATHER_ROWS)], rows_v)
            for j in range(TOP_K):
                pltpu.sync_copy(idx_hbm.at[pl.ds(j * t + off, SC_GATHER_ROWS)], idx_v)
                pltpu.sync_copy(rows_v, out_hbm.at[idx_v])

    return scatter_kernel(x, dest_t)


HALF_D = D_MODEL // 2
HI16 = -65536


def _pack_bf16_pair(a, b):
    wa = lax.bitcast_convert_type(a.astype(BF16).astype(F32), jnp.int32)
    wb = lax.bitcast_convert_type(b.astype(BF16).astype(F32), jnp.int32)
    return wa | lax.shift_right_logical(wb, 16)


def _unpack_bf16_pair(w):
    a = lax.bitcast_convert_type(w & HI16, F32)
    b = lax.bitcast_convert_type(lax.shift_left(w, 16), F32)
    return a, b


def _expert_kernel(be_ref, nu_ref, xs_ref, wgu_ref, bgu_ref, wd_ref, bd_ref, ys_ref, wgu_b, wd_b):
    i = pl.program_id(0)

    @pl.when(jnp.logical_or(i == 0, be_ref[i] != be_ref[jnp.maximum(i - 1, 0)]))
    def _():
        wgu_b[...] = wgu_ref[0].astype(BF16)
        wd_b[...] = wd_ref[0].astype(BF16)

    @pl.when(i < nu_ref[0])
    def _():
        xa, xb = _unpack_bf16_pair(xs_ref[...])
        x = jnp.concatenate([xa, xb], axis=1).astype(BF16)
        h = jnp.dot(x, wgu_b[...], preferred_element_type=F32) + bgu_ref[0]
        g = jnp.minimum(h[:, :D_FF], SWIGLU_LIMIT)
        u = jnp.clip(h[:, D_FF:], -SWIGLU_LIMIT, SWIGLU_LIMIT)
        act = (u + 1.0) * g * jax.nn.sigmoid(SWIGLU_ALPHA * g)
        y = jnp.dot(act.astype(BF16), wd_b[...], preferred_element_type=F32) + bd_ref[0]
        ys_ref[...] = _pack_bf16_pair(y[:, :HALF_D], y[:, HALF_D:])

    @pl.when(i >= nu_ref[0])
    def _():
        ys_ref[...] = jnp.zeros_like(ys_ref)


def _experts(block_e, n_used, xs, wgu, bgu, wd, bd):
    slots = xs.shape[0]
    nb = slots // EXPERT_BLOCK
    grid_spec = pltpu.PrefetchScalarGridSpec(
        num_scalar_prefetch=2,
        grid=(nb,),
        in_specs=[pl.BlockSpec((EXPERT_BLOCK, HALF_D), lambda i, be, nu: (i, 0)),
                  pl.BlockSpec((1, D_MODEL, 2 * D_FF), lambda i, be, nu: (be[i], 0, 0)),
                  pl.BlockSpec((1, 1, 2 * D_FF), lambda i, be, nu: (be[i], 0, 0)),
                  pl.BlockSpec((1, D_FF, D_MODEL), lambda i, be, nu: (be[i], 0, 0)),
                  pl.BlockSpec((1, 1, D_MODEL), lambda i, be, nu: (be[i], 0, 0))],
        out_specs=pl.BlockSpec((EXPERT_BLOCK, HALF_D), lambda i, be, nu: (i, 0)),
        scratch_shapes=[pltpu.VMEM((D_MODEL, 2 * D_FF), BF16), pltpu.VMEM((D_FF, D_MODEL), BF16)],
    )
    return pl.pallas_call(
        _expert_kernel,
        out_shape=jax.ShapeDtypeStruct((slots, HALF_D), jnp.int32),
        grid_spec=grid_spec,
        compiler_params=_params(("arbitrary",)),
        name="moe_experts",
    )(block_e, n_used, xs, wgu, bgu, wd, bd)


def _weighted_rows(rg, packed):
    lo = hi = None
    for j, w in enumerate(packed):
        a, b = _unpack_bf16_pair(w)
        gj = rg[:, j:j + 1]
        lo = gj * a if lo is None else lo + gj * a
        hi = gj * b if hi is None else hi + gj * b
    return jnp.concatenate([lo, hi], axis=1)


def _combine_rows_kernel(x1_ref, rg_ref, g0_ref, g1_ref, g2_ref, g3_ref, l2w_ref, l2b_ref, o_ref):
    moe = _weighted_rows(rg_ref[...], [g0_ref[...], g1_ref[...], g2_ref[...], g3_ref[...]])
    o_ref[...] = _layernorm(DN_ALPHA * x1_ref[...] + moe, l2w_ref[...], l2b_ref[...])


def _combine_rows(x1, rg, rows, l2w, l2b, tm):
    t = x1.shape[0]
    nt = t // tm
    full = lambda shape: pl.BlockSpec(shape, lambda i: (0, 0))
    choice = lambda j: pl.BlockSpec((tm, HALF_D), lambda i, j=j: (j * nt + i, 0))
    return pl.pallas_call(
        _combine_rows_kernel,
        out_shape=jax.ShapeDtypeStruct((t, D_MODEL), F32),
        grid=(nt,),
        in_specs=[pl.BlockSpec((tm, D_MODEL), lambda i: (i, 0)),
                  pl.BlockSpec((tm, LANES), lambda i: (i, 0)),
                  choice(0), choice(1), choice(2), choice(3),
                  full((1, D_MODEL)), full((1, D_MODEL))],
        out_specs=pl.BlockSpec((tm, D_MODEL), lambda i: (i, 0)),
        compiler_params=_params(("arbitrary",)),
        name="moe_combine_rows",
    )(x1, rg, rows, rows, rows, rows, l2w, l2b)


def _prep_w_in(w):
    zeros = lambda n: jnp.zeros((D_MODEL, n), w.dtype)
    kr = w[:, 5760:5824]
    kr_sw = jnp.concatenate([kr[:, MLA_ROPE // 2:], kr[:, :MLA_ROPE // 2]], axis=1)
    parts = [w[:, 6848:9920], w[:, 0:5120], w[:, 5824:6848],
             w[:, 5120:5504], zeros(128),
             w[:, 5504:5760], kr, zeros(64), kr_sw, zeros(64)]
    return jnp.concatenate(parts, axis=1).astype(BF16)


def _prep_w_uq(w):
    w3 = w.reshape(MLA_Q_LORA, MLA_HEADS, MLA_NOPE + MLA_ROPE)
    rope = w3[:, :, MLA_NOPE:]
    rope_sw = jnp.concatenate([rope[:, :, MLA_ROPE // 2:], rope[:, :, :MLA_ROPE // 2]], axis=-1)
    pad = jnp.zeros((MLA_Q_LORA, MLA_HEADS, QK_PAD - MLA_NOPE - MLA_ROPE), w.dtype)
    main = jnp.concatenate([w3, pad], axis=-1).reshape(MLA_Q_LORA, MLA_HEADS * QK_PAD)
    swp = jnp.concatenate([rope_sw, pad], axis=-1).reshape(MLA_Q_LORA, MLA_HEADS * LANES)
    both = jnp.concatenate([main, swp], axis=1)
    return jnp.pad(both, ((0, 512 - MLA_Q_LORA), (0, 0))).astype(BF16)


def _rope_tables(seq):
    inv_freq = ROPE_THETA ** (-jnp.arange(0, MLA_ROPE, 2, dtype=F32) / MLA_ROPE)
    ang = jnp.arange(seq, dtype=F32)[:, None] * inv_freq[None, :]
    cos, sin = jnp.cos(ang), jnp.sin(ang)
    pad = jnp.zeros((seq, LANES - MLA_ROPE), F32)
    return (jnp.concatenate([cos, cos, pad], axis=1), jnp.concatenate([-sin, sin, pad], axis=1))


def _split_bf16(w):
    hi = w.astype(BF16)
    lo = (w - hi.astype(F32)).astype(BF16)
    return jnp.concatenate([hi, lo], axis=1)


def _tile(n, pref):
    return pref if n % pref == 0 else n


def _layer(x, mem, wts):
    batch, seq, _ = x.shape
    t = batch * seq
    x2d = x.reshape(t, D_MODEL)

    proj = _matmul(x2d, wts["w_in"], BF16, _tile(t, 2048), 2048, "in_proj")
    o_a = _hgrn(proj, wts["lb"], wts["hgrn_norm_w"], batch, seq, _tile(seq, 512))

    cos, sin = _rope_tables(seq)
    q, k, v = _mla_prep(proj, wts["qnw"], wts["kvnw"], wts["w_uq"], wts["w_uk"], wts["w_uv"],
                        cos, sin, batch, seq, _tile(seq, 512))
    o_b = _flash(q, k, v, batch, seq, _tile(seq, 1024), _tile(seq, 1024))

    nmem = mem.shape[1]
    kvm = _matmul(mem.reshape(batch * nmem, D_MODEL), wts["mem_w_kv"], BF16,
                  _tile(batch * nmem, 512), 1024, "mem_kv")

    tm = _tile(seq, 512)
    x1, x1p, ri, rg, cnt = _merge(x2d, proj, o_a, o_b, kvm, wts["w_out"], wts["ln1_w"], wts["ln1_b"],
                             wts["router_w"], wts["router_b"], batch, seq, tm)

    counts = cnt[0, :N_EXPERTS].astype(jnp.int32)
    padded = (counts + EXPERT_BLOCK - 1) // EXPERT_BLOCK * EXPERT_BLOCK
    pad_end = jnp.cumsum(padded)
    pad_start = pad_end - padded
    experts = jnp.arange(N_EXPERTS, dtype=jnp.int32)[:, None, None]
    start_ct = jnp.sum(jnp.where(ri[None, :TOP_K] == experts, pad_start[:, None, None], 0), axis=0)
    dest_ct = (start_ct + ri[TOP_K:]).astype(jnp.int32)
    nb = t * TOP_K // EXPERT_BLOCK + N_EXPERTS
    blk_start = jnp.arange(nb, dtype=jnp.int32) * EXPERT_BLOCK
    block_e = jnp.minimum(jnp.sum((pad_end[None, :] <= blk_start[:, None]).astype(jnp.int32), axis=1),
                          N_EXPERTS - 1).astype(jnp.int32)
    n_used = (pad_end[-1:] // EXPERT_BLOCK).astype(jnp.int32)
    dest_flat = dest_ct.reshape(TOP_K * t)
    xs = _sc_dispatch(x1p, dest_flat, nb * EXPERT_BLOCK)
    ys = _experts(block_e, n_used, xs, wts["exp_w_gu"], wts["exp_b_gu"], wts["exp_w_down"], wts["exp_b_down"])
    rows = _sc_gather(ys, dest_flat)
    y = _combine_rows(x1, rg, rows, wts["ln2_w"], wts["ln2_b"], _tile(t, 1024))
    return y.reshape(batch, seq, D_MODEL)


def kernel(x_prompt, x_sample, mem_prompt, mem_sample, w_in, hgrn_lb_logits, hgrn_norm_w,
           mla_q_norm_w, mla_w_uq, mla_kv_norm_w, mla_w_uk, mla_w_uv, mem_w_kv, w_out,
           ln1_w, ln1_b, router_w, router_b, exp_w_gu, exp_b_gu, exp_w_down, exp_b_down,
           ln2_w, ln2_b):
    depth = w_in.shape[0]
    gamma = jax.nn.softmax(hgrn_lb_logits.astype(F32), axis=1)
    cum = jnp.cumsum(gamma, axis=1)
    lb_all = cum[:, 1:] - cum[:, :1]
    y_prompt, y_sample = x_prompt, x_sample
    for l in range(depth):
        row = lambda a: a[l].reshape(1, -1).astype(F32)
        wts = {
            "w_in": _prep_w_in(w_in[l]),
            "lb": lb_all[:, l],
            "hgrn_norm_w": hgrn_norm_w[l].astype(F32),
            "qnw": jnp.pad(row(mla_q_norm_w), ((0, 0), (0, 512 - MLA_Q_LORA))),
            "kvnw": row(mla_kv_norm_w),
            "w_uq": _prep_w_uq(mla_w_uq[l]),
            "w_uk": mla_w_uk[l].astype(BF16),
            "w_uv": mla_w_uv[l].astype(BF16),
            "mem_w_kv": mem_w_kv[l].astype(BF16),
            "w_out": w_out[l].astype(BF16),
            "ln1_w": row(ln1_w), "ln1_b": row(ln1_b),
            "router_w": _split_bf16(jnp.pad(router_w[l].astype(F32), ((0, 0), (0, LANES - N_EXPERTS)))),
            "router_b": jnp.pad(row(router_b), ((0, 0), (0, LANES - N_EXPERTS)), constant_values=-jnp.inf),
            "exp_w_gu": exp_w_gu[l].astype(F32),
            "exp_b_gu": exp_b_gu[l].reshape(N_EXPERTS, 1, 2 * D_FF).astype(F32),
            "exp_w_down": exp_w_down[l].astype(F32),
            "exp_b_down": exp_b_down[l].reshape(N_EXPERTS, 1, D_MODEL).astype(F32),
            "ln2_w": row(ln2_w), "ln2_b": row(ln2_b),
        }
        y_prompt = _layer(y_prompt, mem_prompt, wts)
        y_sample = _layer(y_sample, mem_sample, wts)
    return (y_prompt, y_sample)
```

```python
import functools

import jax
import jax.numpy as jnp
from jax import lax
from jax.experimental import pallas as pl
from jax.experimental.pallas import tpu as pltpu
from jax.experimental.pallas import tpu_sc as plsc

F32 = jnp.float32
BF16 = jnp.bfloat16

D_MODEL = 1024
HGRN_HEADS = 8
HGRN_DK = 128
MLA_HEADS = 8
MLA_Q_LORA = 384
MLA_KV_LORA = 256
MLA_NOPE = 128
MLA_ROPE = 64
MLA_V = 128
ROPE_THETA = 10000.0
MEM_HEADS = 4
MEM_HEAD_DIM = D_MODEL // MEM_HEADS
N_EXPERTS = 32
TOP_K = 4
D_FF = D_MODEL
SWIGLU_LIMIT = 7.0
SWIGLU_ALPHA = 1.702
DN_ALPHA = 2.0 ** 0.25
LN_EPS = 1e-5
RMS_EPS = 1e-6

LANES = 128
SUBLANES = 8
QK_PAD = 256
V_PAD = 256
LOG2E = 1.4426950408889634
HGRN_CHUNK = 64
HGRN_SAFE_RANGE = 160.0
EXPERT_BLOCK = 512
SC_CORES = 2
SC_SUBCORES = 16
SC_WORKERS = SC_CORES * SC_SUBCORES
SC_GATHER_ROWS = 128
VMEM_LIMIT = 56 * 1024 * 1024

COL_GA, COL_GB, COL_GM, COL_Q, COL_ZF, COL_ZB, COL_I, COL_G, COL_MQ = range(9)
COL_DQ, COL_DKV = 18, 19

NT_DIMS = (((1,), (1,)), ((), ()))
TN_DIMS = (((0,), (0,)), ((), ()))


def _params(sem, vmem=VMEM_LIMIT):
    return pltpu.CompilerParams(dimension_semantics=sem, vmem_limit_bytes=vmem)


def _mm_kernel(x_ref, w_ref, o_ref, xb_ref):
    @pl.when(pl.program_id(1) == 0)
    def _():
        xb_ref[...] = x_ref[...].astype(BF16)

    o_ref[...] = jnp.dot(xb_ref[...], w_ref[...], preferred_element_type=F32).astype(o_ref.dtype)


def _matmul(x, w, out_dtype, tm, tn, name):
    m, k = x.shape
    n = w.shape[1]
    return pl.pallas_call(
        _mm_kernel,
        out_shape=jax.ShapeDtypeStruct((m, n), out_dtype),
        grid=(m // tm, n // tn),
        in_specs=[pl.BlockSpec((tm, k), lambda i, j: (i, 0)),
                  pl.BlockSpec((k, tn), lambda i, j: (0, j))],
        out_specs=pl.BlockSpec((tm, tn), lambda i, j: (i, j)),
        scratch_shapes=[pltpu.VMEM((tm, k), BF16)],
        compiler_params=_params(("arbitrary", "arbitrary")),
        name=name,
    )(x, w)


def _hgrn_kernel(*refs, reverse, epilogue, sc):
    if epilogue:
        lb_ref, q_ref, z_ref, v_ref, of_ref, g_ref, nw_ref, o_ref, st_ref, kk_s, b_s, vf_s, os_s = refs
    else:
        lb_ref, q_ref, z_ref, v_ref, o_ref, st_ref, kk_s, b_s, vf_s = refs
    C = HGRN_CHUNK
    nch = sc // C

    @pl.when(pl.program_id(1) == 0)
    def _():
        st_ref[...] = jnp.zeros_like(st_ref)

    lb = lb_ref[...]
    row = lax.broadcasted_iota(jnp.int32, (C, C), 0)
    col = lax.broadcasted_iota(jnp.int32, (C, C), 1)
    tri = (row <= col) if reverse else (row >= col)
    trib = jnp.where(tri, 1.0, 0.0).astype(BF16)

    minb = None
    for c in range(nch):
        rows = slice(c * C, (c + 1) * C)
        z = z_ref[rows, :].astype(F32)
        gate = (1.0 - lb) * jax.nn.sigmoid(z)
        lf = jnp.log(lb + gate)
        kk_s[rows, :] = (1.0 - lb) - gate
        hi = lf.astype(BF16)
        r1 = lf - hi.astype(F32)
        mid = r1.astype(BF16)
        lo = (r1 - mid.astype(F32)).astype(BF16)
        b = (jnp.dot(trib, hi, preferred_element_type=F32)
             + jnp.dot(trib, mid, preferred_element_type=F32)
             + jnp.dot(trib, lo, preferred_element_type=F32))
        b_s[rows, :] = b
        mb = jnp.min(b)
        minb = mb if minb is None else jnp.minimum(minb, mb)

    rid = lax.broadcasted_iota(jnp.int32, (C, 1), 0)

    def chunk(i, carry, fast):
        c = (nch - 1 - i) if reverse else i
        r0 = pl.multiple_of(c * C, C)
        rows = pl.ds(r0, C)
        for h in range(HGRN_HEADS):
            cols = slice(h * HGRN_DK, (h + 1) * HGRN_DK)
            q = q_ref[rows, cols].astype(F32)
            kk = kk_s[rows, cols]
            b = b_s[rows, cols]
            v = v_ref[rows, cols]
            bl = b[0:1, :] if reverse else b[C - 1:C, :]
            if fast:
                bm = 0.5 * bl
                qd = (q * jnp.exp(b - bm)).astype(BF16)
                kd = (kk * jnp.exp(bm - b)).astype(BF16)
                s = lax.dot_general(qd, kd, NT_DIMS, preferred_element_type=F32)
                s = jnp.where(tri, s, 0.0).astype(BF16)
                o = jnp.dot(s, v, preferred_element_type=F32)
            else:
                def sbody(g_i, o_acc):
                    grp = pl.ds(pl.multiple_of(r0 + g_i * SUBLANES, SUBLANES), SUBLANES)
                    b8 = b_s[grp, cols]
                    k8 = kk_s[grp, cols]
                    v8 = vf_s[grp, cols]
                    for jj in range(SUBLANES):
                        s_i = g_i * SUBLANES + jj
                        w = q * k8[jj:jj + 1, :] * jnp.exp(jnp.minimum(b - b8[jj:jj + 1, :], 0.0))
                        scol = jnp.sum(w, axis=-1, keepdims=True)
                        keep = (rid <= s_i) if reverse else (rid >= s_i)
                        o_acc = o_acc + jnp.where(keep, scol, 0.0) * v8[jj:jj + 1, :]
                    return o_acc
                o = lax.fori_loop(0, C // SUBLANES, sbody, jnp.zeros((C, HGRN_DK), F32))
            st = st_ref[h]
            qi = (q * jnp.exp(b)).astype(BF16)
            o = o + lax.dot_general(qi, st.astype(BF16), NT_DIMS, preferred_element_type=F32)
            ke = (kk * jnp.exp(bl - b)).astype(BF16)
            upd = lax.dot_general(v, ke, TN_DIMS, preferred_element_type=F32)
            st_ref[h] = st * jnp.exp(bl) + upd
            if epilogue:
                os_s[rows, cols] = o + of_ref[rows, cols].astype(F32)
            else:
                o_ref[rows, cols] = o.astype(o_ref.dtype)
        return carry

    safe = minb >= -HGRN_SAFE_RANGE

    @pl.when(safe)
    def _():
        lax.fori_loop(0, nch, functools.partial(chunk, fast=True), 0, unroll=8)

    @pl.when(jnp.logical_not(safe))
    def _():
        vf_s[...] = v_ref[...].astype(F32)
        lax.fori_loop(0, nch, functools.partial(chunk, fast=False), 0)

    if epilogue:
        nw = nw_ref[...]
        for h in range(HGRN_HEADS):
            cols = slice(h * HGRN_DK, (h + 1) * HGRN_DK)
            os = os_s[:, cols]
            ms = jnp.mean(os * os, axis=-1, keepdims=True)
            y = os * lax.rsqrt(ms + RMS_EPS) * nw
            g = g_ref[:, cols].astype(F32)
            o_ref[:, cols] = (y * (g * jax.nn.sigmoid(g))).astype(o_ref.dtype)


def _hgrn(proj, lb, norm_w, batch, seq, sc):
    t = batch * seq
    ns = seq // sc
    blk = (sc, D_MODEL)

    def spec(colblk, reverse):
        if reverse:
            return pl.BlockSpec(blk, lambda b, n: (b * ns + ns - 1 - n, colblk))
        return pl.BlockSpec(blk, lambda b, n: (b * ns + n, colblk))

    def row_spec(reverse):
        if reverse:
            return pl.BlockSpec(blk, lambda b, n: (b * ns + ns - 1 - n, 0))
        return pl.BlockSpec(blk, lambda b, n: (b * ns + n, 0))

    lb_spec = pl.BlockSpec((1, D_MODEL), lambda b, n: (0, 0))
    common_scratch = [pltpu.VMEM((HGRN_HEADS, HGRN_DK, HGRN_DK), F32),
                      pltpu.VMEM(blk, F32), pltpu.VMEM(blk, F32), pltpu.VMEM(blk, F32)]
    o_f = pl.pallas_call(
        functools.partial(_hgrn_kernel, reverse=False, epilogue=False, sc=sc),
        out_shape=jax.ShapeDtypeStruct((t, D_MODEL), BF16),
        grid=(batch, ns),
        in_specs=[lb_spec, spec(COL_Q, False), spec(COL_ZF, False), spec(COL_I, False)],
        out_specs=row_spec(False),
        scratch_shapes=common_scratch,
        compiler_params=_params(("arbitrary", "arbitrary")),
        name="hgrn_fwd",
    )(lb[0:1], proj, proj, proj)
    o_a = pl.pallas_call(
        functools.partial(_hgrn_kernel, reverse=True, epilogue=True, sc=sc),
        out_shape=jax.ShapeDtypeStruct((t, D_MODEL), BF16),
        grid=(batch, ns),
        in_specs=[lb_spec, spec(COL_Q, True), spec(COL_ZB, True), spec(COL_I, True),
                  row_spec(True), spec(COL_G, True),
                  pl.BlockSpec((1, HGRN_DK), lambda b, n: (0, 0))],
        out_specs=row_spec(True),
        scratch_shapes=common_scratch + [pltpu.VMEM(blk, F32)],
        compiler_params=_params(("arbitrary", "arbitrary")),
        name="hgrn_bwd",
    )(lb[1:2], proj, proj, proj, o_f, proj, norm_w.reshape(1, HGRN_DK))
    return o_a


def _mla_prep_kernel(dq_ref, dkv_ref, qnw_ref, kvnw_ref, wq_ref, wk_ref, wv_ref, cos_ref, sin_ref,
                     q_ref, k_ref, v_ref):
    scale = (MLA_NOPE + MLA_ROPE) ** -0.5 * LOG2E
    cos = cos_ref[...]
    sin = sin_ref[...]
    dq = dq_ref[...].astype(F32)
    ms = jnp.sum(dq * dq, axis=-1, keepdims=True) * (1.0 / MLA_Q_LORA)
    cq = (dq * lax.rsqrt(ms + RMS_EPS) * qnw_ref[...]).astype(BF16)
    qa = jnp.dot(cq, wq_ref[...], preferred_element_type=F32)
    for h in range(MLA_HEADS):
        base = h * QK_PAD
        q_ref[:, base:base + MLA_NOPE] = (qa[:, base:base + MLA_NOPE] * scale).astype(BF16)
        rp = qa[:, base + MLA_NOPE:base + QK_PAD]
        sw = qa[:, MLA_HEADS * QK_PAD + h * LANES:MLA_HEADS * QK_PAD + (h + 1) * LANES]
        q_ref[:, base + MLA_NOPE:base + QK_PAD] = ((rp * cos + sw * sin) * scale).astype(BF16)
    dkv = dkv_ref[...].astype(F32)
    ckv = dkv[:, :MLA_KV_LORA]
    msk = jnp.mean(ckv * ckv, axis=-1, keepdims=True)
    cn = (ckv * lax.rsqrt(msk + RMS_EPS) * kvnw_ref[...]).astype(BF16)
    kn = jnp.dot(cn, wk_ref[...], preferred_element_type=F32)
    vv = jnp.dot(cn, wv_ref[...], preferred_element_type=F32).astype(BF16)
    ones_col = jnp.where(lax.broadcasted_iota(jnp.int32, (vv.shape[0], LANES), 1) == 0, 1.0, 0.0).astype(BF16)
    for h in range(MLA_HEADS):
        v_ref[:, h * V_PAD:h * V_PAD + MLA_V] = vv[:, h * MLA_V:(h + 1) * MLA_V]
        v_ref[:, h * V_PAD + MLA_V:(h + 1) * V_PAD] = ones_col
    kr = (dkv[:, MLA_KV_LORA:MLA_KV_LORA + LANES] * cos
          + dkv[:, MLA_KV_LORA + LANES:MLA_KV_LORA + 2 * LANES] * sin).astype(BF16)
    for h in range(MLA_HEADS):
        base = h * QK_PAD
        k_ref[:, base:base + MLA_NOPE] = kn[:, h * MLA_NOPE:(h + 1) * MLA_NOPE].astype(BF16)
        k_ref[:, base + MLA_NOPE:base + QK_PAD] = kr


def _mla_prep(proj, qnw, kvnw, wq, wk, wv, cos, sin, batch, seq, tm):
    t = batch * seq
    npos = seq // tm
    full = lambda shape: pl.BlockSpec(shape, lambda i: (0, 0))
    return pl.pallas_call(
        _mla_prep_kernel,
        out_shape=(jax.ShapeDtypeStruct((t, MLA_HEADS * QK_PAD), BF16),
                   jax.ShapeDtypeStruct((t, MLA_HEADS * QK_PAD), BF16),
                   jax.ShapeDtypeStruct((t, MLA_HEADS * V_PAD), BF16)),
        grid=(t // tm,),
        in_specs=[pl.BlockSpec((tm, 512), lambda i: (i, COL_DQ)),
                  pl.BlockSpec((tm, 512), lambda i: (i, COL_DKV)),
                  full((1, 512)), full((1, MLA_KV_LORA)),
                  full(wq.shape), full(wk.shape), full(wv.shape),
                  pl.BlockSpec((tm, LANES), lambda i: (i % npos, 0)),
                  pl.BlockSpec((tm, LANES), lambda i: (i % npos, 0))],
        out_specs=(pl.BlockSpec((tm, MLA_HEADS * QK_PAD), lambda i: (i, 0)),
                   pl.BlockSpec((tm, MLA_HEADS * QK_PAD), lambda i: (i, 0)),
                   pl.BlockSpec((tm, MLA_HEADS * V_PAD), lambda i: (i, 0))),
        compiler_params=_params(("arbitrary",)),
        name="mla_prep",
    )(proj, proj, qnw, kvnw, wq, wk, wv, cos, sin)


def _flash_kernel(q_ref, k_ref, v_ref, o_ref, m_s, acc_s, s_buf, *, bk, nk):
    m_s[...] = jnp.full(m_s.shape, -jnp.inf, F32)
    acc_s[...] = jnp.zeros(acc_s.shape, F32)
    nt = bk // LANES
    bq = q_ref.shape[0]
    nsplit = 2 if bq % 256 == 0 else 1
    hq = bq // nsplit
    qrs = [slice(hf * hq, (hf + 1) * hq) for hf in range(nsplit)]

    def scores(j, slot):
        rows = pl.ds(pl.multiple_of(j * bk, bk), bk)
        for qr in qrs:
            s_buf[slot, qr, :] = lax.dot_general(q_ref[qr, :], k_ref[rows, :], NT_DIMS,
                                                 preferred_element_type=F32)

    def consume(j, slot):
        rows = pl.ds(pl.multiple_of(j * bk, bk), bk)
        ps, alphas = [], []
        for qr in qrs:
            tiles = [s_buf[slot, qr, t * LANES:(t + 1) * LANES] for t in range(nt)]
            tmax = tiles[0]
            for t in range(1, nt):
                tmax = jnp.maximum(tmax, tiles[t])
            m_prev = m_s[qr, :]
            m_new = jnp.maximum(m_prev, jnp.max(tmax, axis=-1, keepdims=True))
            alphas.append(jnp.exp2(m_prev - m_new))
            ps.append(jnp.concatenate([jnp.exp2(tl - m_new).astype(BF16) for tl in tiles], axis=1))
            m_s[qr, :] = m_new
        for qr, p, a in zip(qrs, ps, alphas):
            pv = jnp.dot(p, v_ref[rows, :], preferred_element_type=F32)
            acc_s[qr, :] = jnp.concatenate([a, a], axis=1) * acc_s[qr, :] + pv

    scores(0, 0)
    npairs = (nk - 1) // 2

    def body(i, carry):
        j = 2 * i
        scores(j + 1, 1)
        consume(j, 0)
        scores(j + 2, 0)
        consume(j + 1, 1)
        return carry

    lax.fori_loop(0, npairs, body, 0)
    if nk - 2 * npairs == 2:
        scores(nk - 1, 1)
        consume(nk - 2, 0)
        consume(nk - 1, 1)
    else:
        consume(nk - 1, 0)
    acc = acc_s[...]
    o_ref[...] = (acc[:, :MLA_V] / acc[:, MLA_V:MLA_V + 1]).astype(o_ref.dtype)


def _flash(q, k, v, batch, seq, bq, bk):
    t = batch * seq
    nq = seq // bq
    return pl.pallas_call(
        functools.partial(_flash_kernel, bk=bk, nk=seq // bk),
        out_shape=jax.ShapeDtypeStruct((t, MLA_HEADS * MLA_V), BF16),
        grid=(batch, MLA_HEADS, nq),
        in_specs=[pl.BlockSpec((bq, QK_PAD), lambda b, h, i: (b * nq + i, h)),
                  pl.BlockSpec((seq, QK_PAD), lambda b, h, i: (b, h)),
                  pl.BlockSpec((seq, V_PAD), lambda b, h, i: (b, h))],
        out_specs=pl.BlockSpec((bq, MLA_V), lambda b, h, i: (b * nq + i, h)),
        scratch_shapes=[pltpu.VMEM((bq, LANES), F32), pltpu.VMEM((bq, V_PAD), F32),
                        pltpu.VMEM((2, bq, bk), F32)],
        compiler_params=_params(("arbitrary", "arbitrary", "arbitrary")),
        name="mla_flash",
    )(q, k, v)


def _layernorm(y, w, b):
    mu = jnp.mean(y, axis=-1, keepdims=True)
    yc = y - mu
    var = jnp.mean(yc * yc, axis=-1, keepdims=True)
    return yc * lax.rsqrt(var + LN_EPS) * w + b


def _merge_kernel(x_ref, ga_ref, gb_ref, gm_ref, mq_ref, oa_ref, ob_ref, kvm_ref, wout_ref,
                  l1w_ref, l1b_ref, rw_ref, rb_ref,
                  x1_ref, x1p_ref, ri_ref, rg_ref, cnt_ref, carry_s, *, tm):
    @pl.when(pl.program_id(0) == 0)
    def _():
        carry_s[...] = jnp.zeros_like(carry_s)

    parts = []
    for h in range(MEM_HEADS):
        cols = slice(h * MEM_HEAD_DIM, (h + 1) * MEM_HEAD_DIM)
        kh = kvm_ref[:, cols]
        vh = kvm_ref[:, D_MODEL + h * MEM_HEAD_DIM:D_MODEL + (h + 1) * MEM_HEAD_DIM]
        s = lax.dot_general(mq_ref[:, cols], kh, NT_DIMS, preferred_element_type=F32) * (MEM_HEAD_DIM ** -0.5)
        s = s - jnp.max(s, axis=-1, keepdims=True)
        p = jnp.exp(s)
        p = p / jnp.sum(p, axis=-1, keepdims=True)
        parts.append(jnp.dot(p.astype(BF16), vh, preferred_element_type=F32))
    om = jnp.concatenate(parts, axis=1)

    gated_a = jax.nn.sigmoid(ga_ref[...]) * oa_ref[...]
    gated_b = jax.nn.sigmoid(gb_ref[...]) * ob_ref[...]
    merged = (gated_a.astype(F32) + gated_b.astype(F32)
              + jax.nn.sigmoid(gm_ref[...]).astype(F32) * om)
    y = DN_ALPHA * x_ref[...] + jnp.dot(merged.astype(BF16), wout_ref[...], preferred_element_type=F32)
    x1 = _layernorm(y, l1w_ref[...], l1b_ref[...])
    x1_ref[...] = x1
    x1p_ref[...] = _pack_bf16_pair(x1[:, :HALF_D], x1[:, HALF_D:])

    x_hi = x1.astype(BF16)
    x_lo = (x1 - x_hi.astype(F32)).astype(BF16)
    hi_part = jnp.dot(x_hi, rw_ref[...], preferred_element_type=F32)
    lo_part = jnp.dot(x_lo, rw_ref[:, :LANES], preferred_element_type=F32)
    logits = hi_part[:, :LANES] + hi_part[:, LANES:] + lo_part + rb_ref[...]
    lane_i = lax.broadcasted_iota(jnp.int32, (tm, LANES), 1)
    lane = lane_i.astype(F32)
    work = logits
    idx, val = [], []
    for _ in range(TOP_K):
        mx = jnp.max(work, axis=-1, keepdims=True)
        ix = jnp.min(jnp.where(work == mx, lane, float(LANES)), axis=-1, keepdims=True)
        idx.append(ix)
        val.append(mx)
        work = jnp.where(lane == ix, -jnp.inf, work)
    ex = [jnp.exp(v - val[0]) for v in val]
    tot = ex[0] + ex[1] + ex[2] + ex[3]
    hot = [jnp.where(lane == ix, 1.0, 0.0) for ix in idx]
    multi = hot[0] + hot[1] + hot[2] + hot[3]
    r = lax.broadcasted_iota(jnp.int32, (tm, tm), 0)
    c = lax.broadcasted_iota(jnp.int32, (tm, tm), 1)
    lower = jnp.where(r > c, 1.0, 0.0).astype(BF16)
    before = jnp.dot(lower, multi.astype(BF16), preferred_element_type=F32) + carry_s[0:1, :]
    ri = jnp.zeros((tm, LANES), F32)
    rg = jnp.zeros((tm, LANES), F32)
    for j in range(TOP_K):
        rank = jnp.sum(before * hot[j], axis=-1, keepdims=True)
        ri = ri + jnp.where(lane == float(j), idx[j], 0.0) + jnp.where(lane == float(TOP_K + j), rank, 0.0)
        rg = rg + jnp.where(lane == float(j), ex[j] / tot, 0.0)
    ri_ref[...] = ri.T[:2 * TOP_K, :].astype(jnp.int32)
    rg_ref[...] = rg
    carry_s[...] = carry_s[...] + jnp.sum(multi, axis=0, keepdims=True)
    cnt_ref[...] = carry_s[...]


def _merge(x2d, proj, o_a, o_b, kvm, wout, l1w, l1b, rw, rb, batch, seq, tm):
    t = batch * seq
    per_b = seq // tm
    nmem = kvm.shape[0] // batch
    tile = lambda colblk: pl.BlockSpec((tm, D_MODEL), lambda i: (i, colblk))
    full = lambda shape: pl.BlockSpec(shape, lambda i: (0, 0))
    return pl.pallas_call(
        functools.partial(_merge_kernel, tm=tm),
        out_shape=(jax.ShapeDtypeStruct((t, D_MODEL), F32),
                   jax.ShapeDtypeStruct((t, HALF_D), jnp.int32),
                   jax.ShapeDtypeStruct((2 * TOP_K, t), jnp.int32),
                   jax.ShapeDtypeStruct((t, LANES), F32),
                   jax.ShapeDtypeStruct((8, LANES), F32)),
        grid=(t // tm,),
        in_specs=[tile(0), tile(COL_GA), tile(COL_GB), tile(COL_GM), tile(COL_MQ), tile(0), tile(0),
                  pl.BlockSpec((nmem, 2 * D_MODEL), lambda i: (i // per_b, 0)),
                  full((D_MODEL, D_MODEL)), full((1, D_MODEL)), full((1, D_MODEL)),
                  full((D_MODEL, 2 * LANES)), full((1, LANES))],
        out_specs=(tile(0), pl.BlockSpec((tm, HALF_D), lambda i: (i, 0)),
                   pl.BlockSpec((2 * TOP_K, tm), lambda i: (0, i)),
                   pl.BlockSpec((tm, LANES), lambda i: (i, 0)), full((8, LANES))),
        scratch_shapes=[pltpu.VMEM((8, LANES), F32)],
        compiler_params=_params(("arbitrary",)),
        name="merge_router",
    )(x2d, proj, proj, proj, proj, o_a, o_b, kvm, wout, l1w, l1b, rw, rb)


def _sc_gather(table, idx):
    p, d = idx.shape[0], table.shape[1]
    per_w = p // SC_WORKERS
    assert per_w * SC_WORKERS == p and per_w % SC_GATHER_ROWS == 0
    mesh = plsc.VectorSubcoreMesh(core_axis_name="c", subcore_axis_name="s")

    @functools.partial(
        pl.kernel, mesh=mesh,
        out_type=jax.ShapeDtypeStruct((p, d), table.dtype),
        scratch_types=[pltpu.VMEM((SC_GATHER_ROWS,), jnp.int32),
                       pltpu.VMEM((SC_GATHER_ROWS, d), table.dtype),
                       pltpu.SemaphoreType.DMA],
        name="moe_gather_sc",
    )
    def gather_kernel(table_hbm, idx_hbm, out_hbm, idx_v, rows_v, sem):
        wid = lax.axis_index("s") * SC_CORES + lax.axis_index("c")
        base = wid * per_w

        @pl.loop(0, per_w // SC_GATHER_ROWS)
        def _(c):
            off = base + c * SC_GATHER_ROWS
            pltpu.sync_copy(idx_hbm.at[pl.ds(off, SC_GATHER_ROWS)], idx_v)
            pltpu.async_copy(table_hbm.at[idx_v], rows_v, sem).wait()
            pltpu.sync_copy(rows_v, out_hbm.at[pl.ds(off, SC_GATHER_ROWS)])

    return gather_kernel(table, idx)


def _sc_dispatch(x, dest_t, slots):
    t, d = x.shape
    per_w = t // SC_WORKERS
    assert per_w * SC_WORKERS == t and per_w % SC_GATHER_ROWS == 0
    mesh = plsc.VectorSubcoreMesh(core_axis_name="c", subcore_axis_name="s")

    @functools.partial(
        pl.kernel, mesh=mesh,
        out_type=jax.ShapeDtypeStruct((slots, d), x.dtype),
        scratch_types=[pltpu.VMEM((SC_GATHER_ROWS,), jnp.int32),
                       pltpu.VMEM((SC_GATHER_ROWS, d), x.dtype)],
        name="moe_dispatch_sc",
    )
    def scatter_kernel(x_hbm, idx_hbm, out_hbm, idx_v, rows_v):
        wid = lax.axis_index("s") * SC_CORES + lax.axis_index("c")
        base = wid * per_w

        @pl.loop(0, per_w // SC_GATHER_ROWS)
        def _(c):
            off = base + c * SC_GATHER_ROWS
            pltpu.sync_copy(x_hbm.at[pl.ds(off, SC_GATHER_ROWS)], rows_v)
            for j in range(TOP_K):
                pltpu.sync_copy(idx_hbm.at[pl.ds(j * t + off, SC_GATHER_ROWS)], idx_v)
                pltpu.sync_copy(rows_v, out_hbm.at[idx_v])

    return scatter_kernel(x, dest_t)


HALF_D = D_MODEL // 2
HI16 = -65536


def _pack_bf16_pair(a, b):
    wa = lax.bitcast_convert_type(a.astype(BF16).astype(F32), jnp.int32)
    wb = lax.bitcast_convert_type(b.astype(BF16).astype(F32), jnp.int32)
    return wa | lax.shift_right_logical(wb, 16)


def _unpack_bf16_pair(w):
    a = lax.bitcast_convert_type(w & HI16, F32)
    b = lax.bitcast_convert_type(lax.shift_left(w, 16), F32)
    return a, b


def _expert_kernel(be_ref, nu_ref, xs_ref, wgu_ref, bgu_ref, wd_ref, bd_ref, ys_ref, wgu_b, wd_b):
    i = pl.program_id(0)

    @pl.when(jnp.logical_or(i == 0, be_ref[i] != be_ref[jnp.maximum(i - 1, 0)]))
    def _():
        wgu_b[...] = wgu_ref[0].astype(BF16)
        wd_b[...] = wd_ref[0].astype(BF16)

    @pl.when(i < nu_ref[0])
    def _():
        xa, xb = _unpack_bf16_pair(xs_ref[...])
        x = jnp.concatenate([xa, xb], axis=1).astype(BF16)
        h = jnp.dot(x, wgu_b[...], preferred_element_type=F32) + bgu_ref[0]
        g = jnp.minimum(h[:, :D_FF], SWIGLU_LIMIT)
        u = jnp.clip(h[:, D_FF:], -SWIGLU_LIMIT, SWIGLU_LIMIT)
        act = (u + 1.0) * g * jax.nn.sigmoid(SWIGLU_ALPHA * g)
        y = jnp.dot(act.astype(BF16), wd_b[...], preferred_element_type=F32) + bd_ref[0]
        ys_ref[...] = _pack_bf16_pair(y[:, :HALF_D], y[:, HALF_D:])

    @pl.when(i >= nu_ref[0])
    def _():
        ys_ref[...] = jnp.zeros_like(ys_ref)


def _experts(block_e, n_used, xs, wgu, bgu, wd, bd):
    slots = xs.shape[0]
    nb = slots // EXPERT_BLOCK
    grid_spec = pltpu.PrefetchScalarGridSpec(
        num_scalar_prefetch=2,
        grid=(nb,),
        in_specs=[pl.BlockSpec((EXPERT_BLOCK, HALF_D), lambda i, be, nu: (i, 0)),
                  pl.BlockSpec((1, D_MODEL, 2 * D_FF), lambda i, be, nu: (be[i], 0, 0)),
                  pl.BlockSpec((1, 1, 2 * D_FF), lambda i, be, nu: (be[i], 0, 0)),
                  pl.BlockSpec((1, D_FF, D_MODEL), lambda i, be, nu: (be[i], 0, 0)),
                  pl.BlockSpec((1, 1, D_MODEL), lambda i, be, nu: (be[i], 0, 0))],
        out_specs=pl.BlockSpec((EXPERT_BLOCK, HALF_D), lambda i, be, nu: (i, 0)),
        scratch_shapes=[pltpu.VMEM((D_MODEL, 2 * D_FF), BF16), pltpu.VMEM((D_FF, D_MODEL), BF16)],
    )
    return pl.pallas_call(
        _expert_kernel,
        out_shape=jax.ShapeDtypeStruct((slots, HALF_D), jnp.int32),
        grid_spec=grid_spec,
        compiler_params=_params(("arbitrary",)),
        name="moe_experts",
    )(block_e, n_used, xs, wgu, bgu, wd, bd)


def _weighted_rows(rg, packed):
    lo = hi = None
    for j, w in enumerate(packed):
        a, b = _unpack_bf16_pair(w)
        gj = rg[:, j:j + 1]
        lo = gj * a if lo is None else lo + gj * a
        hi = gj * b if hi is None else hi + gj * b
    return jnp.concatenate([lo, hi], axis=1)


def _combine_rows_kernel(x1_ref, rg_ref, g0_ref, g1_ref, g2_ref, g3_ref, l2w_ref, l2b_ref, o_ref):
    moe = _weighted_rows(rg_ref[...], [g0_ref[...], g1_ref[...], g2_ref[...], g3_ref[...]])
    o_ref[...] = _layernorm(DN_ALPHA * x1_ref[...] + moe, l2w_ref[...], l2b_ref[...])


def _combine_rows(x1, rg, rows, l2w, l2b, tm):
    t = x1.shape[0]
    nt = t // tm
    full = lambda shape: pl.BlockSpec(shape, lambda i: (0, 0))
    choice = lambda j: pl.BlockSpec((tm, HALF_D), lambda i, j=j: (j * nt + i, 0))
    return pl.pallas_call(
        _combine_rows_kernel,
        out_shape=jax.ShapeDtypeStruct((t, D_MODEL), F32),
        grid=(nt,),
        in_specs=[pl.BlockSpec((tm, D_MODEL), lambda i: (i, 0)),
                  pl.BlockSpec((tm, LANES), lambda i: (i, 0)),
                  choice(0), choice(1), choice(2), choice(3),
                  full((1, D_MODEL)), full((1, D_MODEL))],
        out_specs=pl.BlockSpec((tm, D_MODEL), lambda i: (i, 0)),
        compiler_params=_params(("arbitrary",)),
        name="moe_combine_rows",
    )(x1, rg, rows, rows, rows, rows, l2w, l2b)


def _prep_w_in(w):
    zeros = lambda n: jnp.zeros((D_MODEL, n), w.dtype)
    kr = w[:, 5760:5824]
    kr_sw = jnp.concatenate([kr[:, MLA_ROPE // 2:], kr[:, :MLA_ROPE // 2]], axis=1)
    parts = [w[:, 6848:9920], w[:, 0:5120], w[:, 5824:6848],
             w[:, 5120:5504], zeros(128),
             w[:, 5504:5760], kr, zeros(64), kr_sw, zeros(64)]
    return jnp.concatenate(parts, axis=1).astype(BF16)


def _prep_w_uq(w):
    w3 = w.reshape(MLA_Q_LORA, MLA_HEADS, MLA_NOPE + MLA_ROPE)
    rope = w3[:, :, MLA_NOPE:]
    rope_sw = jnp.concatenate([rope[:, :, MLA_ROPE // 2:], rope[:, :, :MLA_ROPE // 2]], axis=-1)
    pad = jnp.zeros((MLA_Q_LORA, MLA_HEADS, QK_PAD - MLA_NOPE - MLA_ROPE), w.dtype)
    main = jnp.concatenate([w3, pad], axis=-1).reshape(MLA_Q_LORA, MLA_HEADS * QK_PAD)
    swp = jnp.concatenate([rope_sw, pad], axis=-1).reshape(MLA_Q_LORA, MLA_HEADS * LANES)
    both = jnp.concatenate([main, swp], axis=1)
    return jnp.pad(both, ((0, 512 - MLA_Q_LORA), (0, 0))).astype(BF16)


def _rope_tables(seq):
    inv_freq = ROPE_THETA ** (-jnp.arange(0, MLA_ROPE, 2, dtype=F32) / MLA_ROPE)
    ang = jnp.arange(seq, dtype=F32)[:, None] * inv_freq[None, :]
    cos, sin = jnp.cos(ang), jnp.sin(ang)
    pad = jnp.zeros((seq, LANES - MLA_ROPE), F32)
    return (jnp.concatenate([cos, cos, pad], axis=1), jnp.concatenate([-sin, sin, pad], axis=1))


def _split_bf16(w):
    hi = w.astype(BF16)
    lo = (w - hi.astype(F32)).astype(BF16)
    return jnp.concatenate([hi, lo], axis=1)


def _tile(n, pref):
    return pref if n % pref == 0 else n


def _layer(x, mem, wts):
    batch, seq, _ = x.shape
    t = batch * seq
    x2d = x.reshape(t, D_MODEL)

    proj = _matmul(x2d, wts["w_in"], BF16, _tile(t, 2048), 2048, "in_proj")
    o_a = _hgrn(proj, wts["lb"], wts["hgrn_norm_w"], batch, seq, _tile(seq, 512))

    cos, sin = _rope_tables(seq)
    q, k, v = _mla_prep(proj, wts["qnw"], wts["kvnw"], wts["w_uq"], wts["w_uk"], wts["w_uv"],
                        cos, sin, batch, seq, _tile(seq, 512))
    o_b = _flash(q, k, v, batch, seq, _tile(seq, 1024), _tile(seq, 1024))

    nmem = mem.shape[1]
    kvm = _matmul(mem.reshape(batch * nmem, D_MODEL), wts["mem_w_kv"], BF16,
                  _tile(batch * nmem, 512), 1024, "mem_kv")

    tm = _tile(seq, 512)
    x1, x1p, ri, rg, cnt = _merge(x2d, proj, o_a, o_b, kvm, wts["w_out"], wts["ln1_w"], wts["ln1_b"],
                             wts["router_w"], wts["router_b"], batch, seq, tm)

    counts = cnt[0, :N_EXPERTS].astype(jnp.int32)
    padded = (counts + EXPERT_BLOCK - 1) // EXPERT_BLOCK * EXPERT_BLOCK
    pad_end = jnp.cumsum(padded)
    pad_start = pad_end - padded
    experts = jnp.arange(N_EXPERTS, dtype=jnp.int32)[:, None, None]
    start_ct = jnp.sum(jnp.where(ri[None, :TOP_K] == experts, pad_start[:, None, None], 0), axis=0)
    dest_ct = (start_ct + ri[TOP_K:]).astype(jnp.int32)
    nb = t * TOP_K // EXPERT_BLOCK + N_EXPERTS
    blk_start = jnp.arange(nb, dtype=jnp.int32) * EXPERT_BLOCK
    block_e = jnp.minimum(jnp.sum((pad_end[None, :] <= blk_start[:, None]).astype(jnp.int32), axis=1),
                          N_EXPERTS - 1).astype(jnp.int32)
    n_used = (pad_end[-1:] // EXPERT_BLOCK).astype(jnp.int32)
    dest_flat = dest_ct.reshape(TOP_K * t)
    xs = _sc_dispatch(x1p, dest_flat, nb * EXPERT_BLOCK)
    ys = _experts(block_e, n_used, xs, wts["exp_w_gu"], wts["exp_b_gu"], wts["exp_w_down"], wts["exp_b_down"])
    rows = _sc_gather(ys, dest_flat)
    y = _combine_rows(x1, rg, rows, wts["ln2_w"], wts["ln2_b"], _tile(t, 1024))
    return y.reshape(batch, seq, D_MODEL)


def kernel(x_prompt, x_sample, mem_prompt, mem_sample, w_in, hgrn_lb_logits, hgrn_norm_w,
           mla_q_norm_w, mla_w_uq, mla_kv_norm_w, mla_w_uk, mla_w_uv, mem_w_kv, w_out,
           ln1_w, ln1_b, router_w, router_b, exp_w_gu, exp_b_gu, exp_w_down, exp_b_down,
           ln2_w, ln2_b):
    depth = w_in.shape[0]
    gamma = jax.nn.softmax(hgrn_lb_logits.astype(F32), axis=1)
    cum = jnp.cumsum(gamma, axis=1)
    lb_all = cum[:, 1:] - cum[:, :1]
    y_prompt, y_sample = x_prompt, x_sample
    for l in range(depth):
        row = lambda a: a[l].reshape(1, -1).astype(F32)
        wts = {
            "w_in": _prep_w_in(w_in[l]),
            "lb": lb_all[:, l],
            "hgrn_norm_w": hgrn_norm_w[l].astype(F32),
            "qnw": jnp.pad(row(mla_q_norm_w), ((0, 0), (0, 512 - MLA_Q_LORA))),
            "kvnw": row(mla_kv_norm_w),
            "w_uq": _prep_w_uq(mla_w_uq[l]),
            "w_uk": mla_w_uk[l].astype(BF16),
            "w_uv": mla_w_uv[l].astype(BF16),
            "mem_w_kv": mem_w_kv[l].astype(BF16),
            "w_out": w_out[l].astype(BF16),
            "ln1_w": row(ln1_w), "ln1_b": row(ln1_b),
            "router_w": _split_bf16(jnp.pad(router_w[l].astype(F32), ((0, 0), (0, LANES - N_EXPERTS)))),
            "router_b": jnp.pad(row(router_b), ((0, 0), (0, LANES - N_EXPERTS)), constant_values=-jnp.inf),
            "exp_w_gu": exp_w_gu[l].astype(F32),
            "exp_b_gu": exp_b_gu[l].reshape(N_EXPERTS, 1, 2 * D_FF).astype(F32),
            "exp_w_down": exp_w_down[l].astype(F32),
            "exp_b_down": exp_b_down[l].reshape(N_EXPERTS, 1, D_MODEL).astype(F32),
            "ln2_w": row(ln2_w), "ln2_b": row(ln2_b),
        }
        y_prompt = _layer(y_prompt, mem_prompt, wts)
        y_sample = _layer(y_sample, mem_sample, wts)
    return (y_prompt, y_sample)
```
